```python
import jax, jax.numpy as jnp
from jax import lax
import numpy as np

D_MODEL = 2048
BATCH = 8
SEQ = 8192
DEPTH = 2

N_META = 16
BLOCK = 128
PAD = BLOCK - N_META
EPS = 1e-6
NEG = -1e30

FOX_HEADS = 8
FOX_HEAD_DIM = D_MODEL // 16
FOX_WIDTH = FOX_HEADS * FOX_HEAD_DIM

CONV_CH = D_MODEL // 2
CONV_K = 3

GLA_HEADS = 4
GLA_DK = D_MODEL // 16
GLA_DV = D_MODEL // 8
GLA_RANK = 16
GLA_TAU = 16.0

D_FF = D_MODEL * 11 // 4
MLP_CONV_K = 3

SPLIT_SIZES = (FOX_WIDTH, FOX_WIDTH, FOX_WIDTH, FOX_HEADS,
               CONV_CH, CONV_CH, CONV_CH,
               GLA_HEADS * GLA_DK, GLA_HEADS * GLA_DK, GLA_HEADS * GLA_DV, GLA_HEADS * GLA_DV, GLA_RANK,
               D_MODEL, D_MODEL, D_MODEL)
N_IN = sum(SPLIT_SIZES)
SPLIT_POINTS = tuple(sum(SPLIT_SIZES[:i + 1]) for i in range(len(SPLIT_SIZES) - 1))
FOX_F_START = 3 * FOX_WIDTH

kernel_name = 'hybrid_fox_shortconv_gla_block'


def _rmsnorm(x, g):
    xf = x.astype(jnp.float32)
    y = xf * lax.rsqrt(jnp.mean(xf * xf, axis=-1, keepdims=True) + EPS)
    return (y * g.astype(jnp.float32)).astype(x.dtype)


def _causal_dwconv(x, w):
    k_width, ch = w.shape
    return lax.conv_general_dilated(x, w[:, None, :].astype(x.dtype), window_strides=(1,),
                                    padding=[(k_width - 1, 0)],
                                    dimension_numbers=('NWC', 'WIO', 'NWC'),
                                    feature_group_count=ch)


def _to_heads_padded(t, n_heads):
    b, l, w = t.shape
    t = t.reshape(b, l, n_heads, w // n_heads).transpose(0, 2, 1, 3)
    return jnp.pad(t, ((0, 0), (0, 0), (PAD, 0), (0, 0)))


def _from_heads_padded(t):
    b, h, lp, dh = t.shape
    return t[:, :, PAD:].transpose(0, 2, 1, 3).reshape(b, lp - PAD, h * dh)


def _fox_attention(q, k, v, logf):
    bn, nh, lp, dh = q.shape
    nb = lp // BLOCK
    c = jnp.cumsum(logf, axis=-1)
    pos = jnp.arange(lp)
    key_valid = pos >= PAD
    qb = jnp.moveaxis(q.reshape(bn, nh, nb, BLOCK, dh), 2, 0)
    cb = jnp.moveaxis(c.reshape(bn, nh, nb, BLOCK), 2, 0)
    scale = dh ** -0.5

    def one_block(args):
        qi, ci, i = args
        s = jnp.einsum('bhqd,bhkd->bhqk', qi, k, preferred_element_type=jnp.float32) * scale
        s = s + ci[..., None] - c[:, :, None, :]
        qpos = i * BLOCK + jnp.arange(BLOCK)
        mask = (pos[None, :] <= qpos[:, None]) & key_valid[None, :]
        p = jax.nn.softmax(jnp.where(mask, s, NEG), axis=-1)
        return jnp.einsum('bhqk,bhkd->bhqd', p.astype(v.dtype), v)

    out = lax.map(one_block, (qb, cb, jnp.arange(nb)))
    return jnp.moveaxis(out, 0, 2).reshape(bn, nh, lp, dh)


def _gla_chunked(q, k, v, logg):
    bn, nh, lp, dk = q.shape
    dv = v.shape[-1]
    nc = lp // BLOCK
    f32 = jnp.float32

    def to_chunks(t):
        return jnp.moveaxis(t.astype(f32).reshape(bn, nh, nc, BLOCK, t.shape[-1]), 2, 0)

    causal = jnp.tril(jnp.ones((BLOCK, BLOCK), dtype=bool))

    def step(state, inp):
        qc, kc, vc, gc = inp
        b = jnp.cumsum(gc, axis=2)
        diff = b[:, :, :, None, :] - b[:, :, None, :, :]
        decay = jnp.exp(jnp.where(causal[None, None, :, :, None], diff, -jnp.inf))
        att = jnp.einsum('bhtd,bhsd,bhtsd->bhts', qc, kc, decay)
        o = (jnp.einsum('bhts,bhsv->bhtv', att, vc)
             + jnp.einsum('bhtd,bhdv->bhtv', qc * jnp.exp(b), state))
        b_last = b[:, :, -1:, :]
        state = (jnp.exp(b_last[:, :, 0, :])[..., None] * state
                 + jnp.einsum('bhsd,bhsv->bhdv', kc * jnp.exp(b_last - b), vc))
        return state, o

    s0 = jnp.zeros((bn, nh, dk, dv), f32)
    _, o = lax.scan(step, s0, (to_chunks(q), to_chunks(k), to_chunks(v), to_chunks(logg)))
    return jnp.moveaxis(o, 0, 2).reshape(bn, nh, lp, dv)


def _fwd_setup_inputs(seed: int = 0) -> dict:
    key = jax.random.key(seed)
    ks = jax.random.split(key, 20)

    def nrm(k, shape, scale):
        return jax.random.normal(k, shape, jnp.float32) * scale

    w_in = nrm(ks[3], (DEPTH, D_MODEL, N_IN), D_MODEL ** -0.5)
    w_in = w_in.at[:, :, FOX_F_START:FOX_F_START + FOX_HEADS].multiply(0.1)
    return {
        'x': nrm(ks[0], (BATCH, SEQ, D_MODEL), 1.0),
        'meta_tokens': nrm(ks[1], (N_META, D_MODEL), 1.0),
        'norm1_g': 1.0 + nrm(ks[2], (DEPTH, D_MODEL), 0.1),
        'w_in': w_in,
        'fox_b_f': 3.0 + nrm(ks[4], (DEPTH, FOX_HEADS), 0.1),
        'gate_b': nrm(ks[5], (DEPTH, 3 * D_MODEL), 0.01),
        'conv_w': nrm(ks[6], (DEPTH, CONV_K, CONV_CH), CONV_K ** -0.5),
        'gla_w_g2': nrm(ks[7], (DEPTH, GLA_RANK, GLA_HEADS * GLA_DK), GLA_RANK ** -0.5),
        'gla_b_g': nrm(ks[8], (DEPTH, GLA_HEADS * GLA_DK), 0.1),
        'gla_norm_g': 1.0 + nrm(ks[9], (DEPTH, GLA_HEADS * GLA_DV), 0.1),
        'w_a_o': nrm(ks[10], (DEPTH, FOX_WIDTH, D_MODEL), FOX_WIDTH ** -0.5),
        'w_b_o': nrm(ks[11], (DEPTH, CONV_CH, D_MODEL), CONV_CH ** -0.5),
        'w_c_o': nrm(ks[12], (DEPTH, GLA_HEADS * GLA_DV, D_MODEL), (GLA_HEADS * GLA_DV) ** -0.5),
        'w_o': nrm(ks[13], (DEPTH, D_MODEL, D_MODEL), D_MODEL ** -0.5),
        'norm2_g': 1.0 + nrm(ks[14], (DEPTH, D_MODEL), 0.1),
        'w_up': nrm(ks[15], (DEPTH, D_MODEL, 2 * D_FF), D_MODEL ** -0.5),
        'mlp_conv_w': nrm(ks[16], (DEPTH, MLP_CONV_K, 2 * D_FF), MLP_CONV_K ** -0.5),
        'w_down': nrm(ks[17], (DEPTH, D_FF, D_MODEL), D_FF ** -0.5),
        'final_norm_g': 1.0 + nrm(ks[18], (D_MODEL,), 0.1),
    }


def _fwd_reference(x, meta_tokens, norm1_g, w_in, fox_b_f, gate_b, conv_w, gla_w_g2, gla_b_g,
              gla_norm_g, w_a_o, w_b_o, w_c_o, w_o, norm2_g, w_up, mlp_conv_w, w_down,
              final_norm_g):
    f32 = jnp.float32
    bn = x.shape[0]
    meta = jnp.broadcast_to(meta_tokens[None].astype(x.dtype), (bn, N_META, D_MODEL))
    h = jnp.concatenate([meta, x], axis=1)
    for l in range(DEPTH):
        xn = _rmsnorm(h, norm1_g[l])
        proj = xn @ w_in[l]
        (qa, ka, va, fa, sc_b, sc_c, sc_h, qc, kc, vc, rc, glr,
         g_a, g_b, g_c) = jnp.split(proj, SPLIT_POINTS, axis=-1)

        logf = jax.nn.log_sigmoid(fa.astype(f32) + fox_b_f[l].astype(f32))
        logf = jnp.pad(logf.transpose(0, 2, 1), ((0, 0), (0, 0), (PAD, 0)))
        oa = _fox_attention(_to_heads_padded(qa, FOX_HEADS), _to_heads_padded(ka, FOX_HEADS),
                            _to_heads_padded(va, FOX_HEADS), logf)
        ya = _from_heads_padded(oa) @ w_a_o[l]

        yb = (sc_b * _causal_dwconv(sc_c * sc_h, conv_w[l])) @ w_b_o[l]

        logg = jax.nn.log_sigmoid((glr @ gla_w_g2[l]).astype(f32) + gla_b_g[l].astype(f32)) / GLA_TAU
        oc = _gla_chunked(_to_heads_padded(qc * (GLA_DK ** -0.5), GLA_HEADS),
                          _to_heads_padded(kc, GLA_HEADS), _to_heads_padded(vc, GLA_HEADS),
                          _to_heads_padded(logg, GLA_HEADS))
        oc = _from_heads_padded(oc).astype(h.dtype)
        oc = _rmsnorm(oc.reshape(oc.shape[0], oc.shape[1], GLA_HEADS, GLA_DV),
                      gla_norm_g[l].reshape(GLA_HEADS, GLA_DV)).reshape(oc.shape)
        yc = (jax.nn.silu(rc) * oc) @ w_c_o[l]

        gbias = gate_b[l]
        mix = (jax.nn.sigmoid(g_a + gbias[:D_MODEL]) * ya
               + jax.nn.sigmoid(g_b + gbias[D_MODEL:2 * D_MODEL]) * yb
               + jax.nn.sigmoid(g_c + gbias[2 * D_MODEL:]) * yc)
        h = h + mix @ w_o[l]

        u = _causal_dwconv(_rmsnorm(h, norm2_g[l]) @ w_up[l], mlp_conv_w[l])
        u_gate, u_up = jnp.split(u, 2, axis=-1)
        h = h + (jax.nn.silu(u_gate) * u_up) @ w_down[l]
    return _rmsnorm(h, final_norm_g)[:, N_META:]


import jax as _jax
import jax.numpy as _jnp

TWIN_FORMAT = 'train_step'
FWD_PARAMS = ['x', 'meta_tokens', 'norm1_g', 'w_in', 'fox_b_f', 'gate_b', 'conv_w', 'gla_w_g2', 'gla_b_g', 'gla_norm_g', 'w_a_o', 'w_b_o', 'w_c_o', 'w_o', 'norm2_g', 'w_up', 'mlp_conv_w', 'w_down', 'final_norm_g']
TWIN_WEIGHTS = ['meta_tokens', 'norm1_g', 'w_in', 'fox_b_f', 'gate_b', 'conv_w', 'gla_w_g2', 'gla_b_g', 'gla_norm_g', 'w_a_o', 'w_b_o', 'w_c_o', 'w_o', 'norm2_g', 'w_up', 'mlp_conv_w', 'w_down', 'final_norm_g']
TWIN_DIFF_INPUT = 'x'
TWIN_INPUTS = ['x', 'meta_tokens', 'norm1_g', 'w_in', 'fox_b_f', 'gate_b', 'conv_w', 'gla_w_g2', 'gla_b_g', 'gla_norm_g', 'w_a_o', 'w_b_o', 'w_c_o', 'w_o', 'norm2_g', 'w_up', 'mlp_conv_w', 'w_down', 'final_norm_g', 'loss_target', 'm_meta_tokens', 'm_norm1_g', 'm_w_in', 'm_fox_b_f', 'm_gate_b', 'm_conv_w', 'm_gla_w_g2', 'm_gla_b_g', 'm_gla_norm_g', 'm_w_a_o', 'm_w_b_o', 'm_w_c_o', 'm_w_o', 'm_norm2_g', 'm_w_up', 'm_mlp_conv_w', 'm_w_down', 'm_final_norm_g', 'v_meta_tokens', 'v_norm1_g', 'v_w_in', 'v_fox_b_f', 'v_gate_b', 'v_conv_w', 'v_gla_w_g2', 'v_gla_b_g', 'v_gla_norm_g', 'v_w_a_o', 'v_w_b_o', 'v_w_c_o', 'v_w_o', 'v_norm2_g', 'v_w_up', 'v_mlp_conv_w', 'v_w_down', 'v_final_norm_g']
TWIN_OUTPUTS = ['loss', 'grad_x', 'grad_meta_tokens', 'grad_norm1_g', 'grad_w_in', 'grad_fox_b_f', 'grad_gate_b', 'grad_conv_w', 'grad_gla_w_g2', 'grad_gla_b_g', 'grad_gla_norm_g', 'grad_w_a_o', 'grad_w_b_o', 'grad_w_c_o', 'grad_w_o', 'grad_norm2_g', 'grad_w_up', 'grad_mlp_conv_w', 'grad_w_down', 'grad_final_norm_g', 'delta_meta_tokens', 'delta_norm1_g', 'delta_w_in', 'delta_fox_b_f', 'delta_gate_b', 'delta_conv_w', 'delta_gla_w_g2', 'delta_gla_b_g', 'delta_gla_norm_g', 'delta_w_a_o', 'delta_w_b_o', 'delta_w_c_o', 'delta_w_o', 'delta_norm2_g', 'delta_w_up', 'delta_mlp_conv_w', 'delta_w_down', 'delta_final_norm_g', 'new_m_meta_tokens', 'new_m_norm1_g', 'new_m_w_in', 'new_m_fox_b_f', 'new_m_gate_b', 'new_m_conv_w', 'new_m_gla_w_g2', 'new_m_gla_b_g', 'new_m_gla_norm_g', 'new_m_w_a_o', 'new_m_w_b_o', 'new_m_w_c_o', 'new_m_w_o', 'new_m_norm2_g', 'new_m_w_up', 'new_m_mlp_conv_w', 'new_m_w_down', 'new_m_final_norm_g', 'new_v_meta_tokens', 'new_v_norm1_g', 'new_v_w_in', 'new_v_fox_b_f', 'new_v_gate_b', 'new_v_conv_w', 'new_v_gla_w_g2', 'new_v_gla_b_g', 'new_v_gla_norm_g', 'new_v_w_a_o', 'new_v_w_b_o', 'new_v_w_c_o', 'new_v_w_o', 'new_v_norm2_g', 'new_v_w_up', 'new_v_mlp_conv_w', 'new_v_w_down', 'new_v_final_norm_g']
TWIN_LEAF_KINDS = {'loss': 'loss', 'grad_x': 'grad_x', 'grad_meta_tokens': 'grad_w', 'grad_norm1_g': 'grad_w', 'grad_w_in': 'grad_w', 'grad_fox_b_f': 'grad_w', 'grad_gate_b': 'grad_w', 'grad_conv_w': 'grad_w', 'grad_gla_w_g2': 'grad_w', 'grad_gla_b_g': 'grad_w', 'grad_gla_norm_g': 'grad_w', 'grad_w_a_o': 'grad_w', 'grad_w_b_o': 'grad_w', 'grad_w_c_o': 'grad_w', 'grad_w_o': 'grad_w', 'grad_norm2_g': 'grad_w', 'grad_w_up': 'grad_w', 'grad_mlp_conv_w': 'grad_w', 'grad_w_down': 'grad_w', 'grad_final_norm_g': 'grad_w', 'delta_meta_tokens': 'delta_w', 'delta_norm1_g': 'delta_w', 'delta_w_in': 'delta_w', 'delta_fox_b_f': 'delta_w', 'delta_gate_b': 'delta_w', 'delta_conv_w': 'delta_w', 'delta_gla_w_g2': 'delta_w', 'delta_gla_b_g': 'delta_w', 'delta_gla_norm_g': 'delta_w', 'delta_w_a_o': 'delta_w', 'delta_w_b_o': 'delta_w', 'delta_w_c_o': 'delta_w', 'delta_w_o': 'delta_w', 'delta_norm2_g': 'delta_w', 'delta_w_up': 'delta_w', 'delta_mlp_conv_w': 'delta_w', 'delta_w_down': 'delta_w', 'delta_final_norm_g': 'delta_w', 'new_m_meta_tokens': 'new_m', 'new_m_norm1_g': 'new_m', 'new_m_w_in': 'new_m', 'new_m_fox_b_f': 'new_m', 'new_m_gate_b': 'new_m', 'new_m_conv_w': 'new_m', 'new_m_gla_w_g2': 'new_m', 'new_m_gla_b_g': 'new_m', 'new_m_gla_norm_g': 'new_m', 'new_m_w_a_o': 'new_m', 'new_m_w_b_o': 'new_m', 'new_m_w_c_o': 'new_m', 'new_m_w_o': 'new_m', 'new_m_norm2_g': 'new_m', 'new_m_w_up': 'new_m', 'new_m_mlp_conv_w': 'new_m', 'new_m_w_down': 'new_m', 'new_m_final_norm_g': 'new_m', 'new_v_meta_tokens': 'new_v', 'new_v_norm1_g': 'new_v', 'new_v_w_in': 'new_v', 'new_v_fox_b_f': 'new_v', 'new_v_gate_b': 'new_v', 'new_v_conv_w': 'new_v', 'new_v_gla_w_g2': 'new_v', 'new_v_gla_b_g': 'new_v', 'new_v_gla_norm_g': 'new_v', 'new_v_w_a_o': 'new_v', 'new_v_w_b_o': 'new_v', 'new_v_w_c_o': 'new_v', 'new_v_w_o': 'new_v', 'new_v_norm2_g': 'new_v', 'new_v_w_up': 'new_v', 'new_v_mlp_conv_w': 'new_v', 'new_v_w_down': 'new_v', 'new_v_final_norm_g': 'new_v'}


def _forward(args):
    return _fwd_reference(*[args[k] for k in FWD_PARAMS])


def _output_shape():
    def fwd():
        inp = _fwd_setup_inputs(0)
        return _fwd_reference(*[inp[k] for k in FWD_PARAMS])
    out = _jax.eval_shape(fwd)
    return out.shape, out.dtype

N_MICROBATCH = 1
ADAM_LR = 0.001
ADAM_B1 = 0.9
ADAM_B2 = 0.999
ADAM_EPS = 1e-08
ADAM_WD = 0.01
ADAM_STEP = 10
PER_EXAMPLE_BATCH_AXIS = {'x': 0, 'loss_target': 0}
SHARED_INPUTS = []
_WEIGHT_DTYPES = {'meta_tokens': _jnp.float32, 'norm1_g': _jnp.float32, 'w_in': _jnp.float32, 'fox_b_f': _jnp.float32, 'gate_b': _jnp.float32, 'conv_w': _jnp.float32, 'gla_w_g2': _jnp.float32, 'gla_b_g': _jnp.float32, 'gla_norm_g': _jnp.float32, 'w_a_o': _jnp.float32, 'w_b_o': _jnp.float32, 'w_c_o': _jnp.float32, 'w_o': _jnp.float32, 'norm2_g': _jnp.float32, 'w_up': _jnp.float32, 'mlp_conv_w': _jnp.float32, 'w_down': _jnp.float32, 'final_norm_g': _jnp.float32}
MOMENT_SCALE = {'meta_tokens': 6.325985e-03, 'norm1_g': 1.607173e-01, 'w_in': 5.674608e-02, 'fox_b_f': 3.227082e-01, 'gate_b': 1.878786e-02, 'conv_w': 1.007050e-01, 'gla_w_g2': 9.766707e-03, 'gla_b_g': 3.990662e-02, 'gla_norm_g': 5.990781e-02, 'w_a_o': 1.773684e-02, 'w_b_o': 7.194566e-02, 'w_c_o': 4.202109e-02, 'w_o': 8.563714e-02, 'norm2_g': 9.862211e-02, 'w_up': 3.968002e-02, 'mlp_conv_w': 4.012490e-02, 'w_down': 6.589738e-02, 'final_norm_g': 3.220794e+01}


def _to_microbatches(a, axis):
    t = _jnp.moveaxis(a, axis, 0)
    t = t.reshape((N_MICROBATCH, t.shape[0] // N_MICROBATCH) + t.shape[1:])
    return _jnp.moveaxis(t, 1, axis + 1)


def setup_inputs(seed: int = 0) -> dict:
    inp = _fwd_setup_inputs(seed)
    key = _jax.random.fold_in(_jax.random.key(seed), 7919)
    shape, _ = _output_shape()
    out = dict(inp)
    out["loss_target"] = _jax.random.normal(_jax.random.fold_in(key, 0), shape, _jnp.float32)
    for i, name in enumerate(TWIN_WEIGHTS):
        w = inp[name].astype(_jnp.float32)
        if MOMENT_SCALE is None:
            s = _jnp.sqrt(_jnp.mean(_jnp.square(w)) + 1e-30)
        else:
            s = MOMENT_SCALE[name]
        km, kv = _jax.random.split(_jax.random.fold_in(key, i + 1))
        out[name] = w
        out["m_" + name] = s * _jax.random.normal(km, w.shape, _jnp.float32)
        out["v_" + name] = (s * s) * _jax.random.uniform(kv, w.shape, _jnp.float32, 0.5, 1.5)
    if N_MICROBATCH > 1:
        for name, axis in PER_EXAMPLE_BATCH_AXIS.items():
            out[name] = _to_microbatches(out[name], axis)
    return {'x': out['x'], 'meta_tokens': out['meta_tokens'], 'norm1_g': out['norm1_g'], 'w_in': out['w_in'], 'fox_b_f': out['fox_b_f'], 'gate_b': out['gate_b'], 'conv_w': out['conv_w'], 'gla_w_g2': out['gla_w_g2'], 'gla_b_g': out['gla_b_g'], 'gla_norm_g': out['gla_norm_g'], 'w_a_o': out['w_a_o'], 'w_b_o': out['w_b_o'], 'w_c_o': out['w_c_o'], 'w_o': out['w_o'], 'norm2_g': out['norm2_g'], 'w_up': out['w_up'], 'mlp_conv_w': out['mlp_conv_w'], 'w_down': out['w_down'], 'final_norm_g': out['final_norm_g'], 'loss_target': out['loss_target'], 'm_meta_tokens': out['m_meta_tokens'], 'm_norm1_g': out['m_norm1_g'], 'm_w_in': out['m_w_in'], 'm_fox_b_f': out['m_fox_b_f'], 'm_gate_b': out['m_gate_b'], 'm_conv_w': out['m_conv_w'], 'm_gla_w_g2': out['m_gla_w_g2'], 'm_gla_b_g': out['m_gla_b_g'], 'm_gla_norm_g': out['m_gla_norm_g'], 'm_w_a_o': out['m_w_a_o'], 'm_w_b_o': out['m_w_b_o'], 'm_w_c_o': out['m_w_c_o'], 'm_w_o': out['m_w_o'], 'm_norm2_g': out['m_norm2_g'], 'm_w_up': out['m_w_up'], 'm_mlp_conv_w': out['m_mlp_conv_w'], 'm_w_down': out['m_w_down'], 'm_final_norm_g': out['m_final_norm_g'], 'v_meta_tokens': out['v_meta_tokens'], 'v_norm1_g': out['v_norm1_g'], 'v_w_in': out['v_w_in'], 'v_fox_b_f': out['v_fox_b_f'], 'v_gate_b': out['v_gate_b'], 'v_conv_w': out['v_conv_w'], 'v_gla_w_g2': out['v_gla_w_g2'], 'v_gla_b_g': out['v_gla_b_g'], 'v_gla_norm_g': out['v_gla_norm_g'], 'v_w_a_o': out['v_w_a_o'], 'v_w_b_o': out['v_w_b_o'], 'v_w_c_o': out['v_w_c_o'], 'v_w_o': out['v_w_o'], 'v_norm2_g': out['v_norm2_g'], 'v_w_up': out['v_w_up'], 'v_mlp_conv_w': out['v_mlp_conv_w'], 'v_w_down': out['v_w_down'], 'v_final_norm_g': out['v_final_norm_g']}


def _loss(weights, diff, rest, loss_target):
    with _jax.named_scope("forward"):
        args = {**rest, TWIN_DIFF_INPUT: diff, **{k: w.astype(_WEIGHT_DTYPES[k]) for k, w in weights.items()}}
        y = _forward(args)
    with _jax.named_scope("loss_head"):
        err = _jnp.square(y.astype(_jnp.float32) - loss_target)
        return 0.5 * _jnp.sum(_jnp.mean(err, axis=-1)) if err.ndim else 0.5 * err


def _adamw(w, g, m, v):
    m = ADAM_B1 * m + (1.0 - ADAM_B1) * g
    v = ADAM_B2 * v + (1.0 - ADAM_B2) * _jnp.square(g)
    m_hat = m / (1.0 - ADAM_B1 ** ADAM_STEP)
    v_hat = v / (1.0 - ADAM_B2 ** ADAM_STEP)
    delta = -ADAM_LR * (m_hat / (_jnp.sqrt(v_hat) + ADAM_EPS) + ADAM_WD * w)
    return delta, m, v


def reference(x, meta_tokens, norm1_g, w_in, fox_b_f, gate_b, conv_w, gla_w_g2, gla_b_g, gla_norm_g, w_a_o, w_b_o, w_c_o, w_o, norm2_g, w_up, mlp_conv_w, w_down, final_norm_g, loss_target, m_meta_tokens, m_norm1_g, m_w_in, m_fox_b_f, m_gate_b, m_conv_w, m_gla_w_g2, m_gla_b_g, m_gla_norm_g, m_w_a_o, m_w_b_o, m_w_c_o, m_w_o, m_norm2_g, m_w_up, m_mlp_conv_w, m_w_down, m_final_norm_g, v_meta_tokens, v_norm1_g, v_w_in, v_fox_b_f, v_gate_b, v_conv_w, v_gla_w_g2, v_gla_b_g, v_gla_norm_g, v_w_a_o, v_w_b_o, v_w_c_o, v_w_o, v_norm2_g, v_w_up, v_mlp_conv_w, v_w_down, v_final_norm_g):
    given = dict(x=x, meta_tokens=meta_tokens, norm1_g=norm1_g, w_in=w_in, fox_b_f=fox_b_f, gate_b=gate_b, conv_w=conv_w, gla_w_g2=gla_w_g2, gla_b_g=gla_b_g, gla_norm_g=gla_norm_g, w_a_o=w_a_o, w_b_o=w_b_o, w_c_o=w_c_o, w_o=w_o, norm2_g=norm2_g, w_up=w_up, mlp_conv_w=mlp_conv_w, w_down=w_down, final_norm_g=final_norm_g, loss_target=loss_target, m_meta_tokens=m_meta_tokens, m_norm1_g=m_norm1_g, m_w_in=m_w_in, m_fox_b_f=m_fox_b_f, m_gate_b=m_gate_b, m_conv_w=m_conv_w, m_gla_w_g2=m_gla_w_g2, m_gla_b_g=m_gla_b_g, m_gla_norm_g=m_gla_norm_g, m_w_a_o=m_w_a_o, m_w_b_o=m_w_b_o, m_w_c_o=m_w_c_o, m_w_o=m_w_o, m_norm2_g=m_norm2_g, m_w_up=m_w_up, m_mlp_conv_w=m_mlp_conv_w, m_w_down=m_w_down, m_final_norm_g=m_final_norm_g, v_meta_tokens=v_meta_tokens, v_norm1_g=v_norm1_g, v_w_in=v_w_in, v_fox_b_f=v_fox_b_f, v_gate_b=v_gate_b, v_conv_w=v_conv_w, v_gla_w_g2=v_gla_w_g2, v_gla_b_g=v_gla_b_g, v_gla_norm_g=v_gla_norm_g, v_w_a_o=v_w_a_o, v_w_b_o=v_w_b_o, v_w_c_o=v_w_c_o, v_w_o=v_w_o, v_norm2_g=v_norm2_g, v_w_up=v_w_up, v_mlp_conv_w=v_mlp_conv_w, v_w_down=v_w_down, v_final_norm_g=v_final_norm_g)
    weights = {n: given[n] for n in TWIN_WEIGHTS}
    shared = {n: given[n] for n in SHARED_INPUTS}
    per_example = {n: given[n] for n in ['x']}
    grad_fn = _jax.value_and_grad(_loss, argnums=(0, 1))

    def one_microbatch(ex, loss_target):
        ex = dict(ex)
        diff = ex.pop(TWIN_DIFF_INPUT)
        return grad_fn(weights, diff, {**shared, **ex}, loss_target)

    if N_MICROBATCH == 1:
        loss, (grad_w, grad_x) = one_microbatch(per_example, given["loss_target"])
    else:
        def body(carry, xs):
            loss_sum, grad_sum = carry
            l_k, (gw_k, gx_k) = one_microbatch(xs[0], xs[1])
            with _jax.named_scope("update"):
                return (loss_sum + l_k, _jax.tree.map(_jnp.add, grad_sum, gw_k)), gx_k

        init = (_jnp.zeros((), _jnp.float32), _jax.tree.map(_jnp.zeros_like, weights))
        (loss, grad_w), grad_x = _jax.lax.scan(body, init, (per_example, given["loss_target"]))
    with _jax.named_scope("update"):
        delta_w, new_m, new_v = {}, {}, {}
        for n in TWIN_WEIGHTS:
            delta_w[n], new_m[n], new_v[n] = _adamw(weights[n], grad_w[n], given["m_" + n], given["v_" + n])
    return (loss, grad_x, *[grad_w[n] for n in TWIN_WEIGHTS], *[delta_w[n] for n in TWIN_WEIGHTS],
            *[new_m[n] for n in TWIN_WEIGHTS], *[new_v[n] for n in TWIN_WEIGHTS])
```

```python
import functools

import jax
import jax.numpy as jnp
from jax import lax
from jax.experimental import pallas as pl
from jax.experimental.pallas import tpu as pltpu

F32, BF16 = jnp.float32, jnp.bfloat16
HIGHEST = lax.Precision.HIGHEST
MESH = pl.DeviceIdType.MESH

N_META = 16
BLOCK = 128
LANES = 128
PAD = BLOCK - N_META
EPS = 1e-6
NEG = -1e30
HALO = 16
VMEM_LIMIT = 56 * 1024 * 1024
ADAM_BLOCK_BYTES = 1 << 20

D_MODEL = 2048
FOX_HEADS, FOX_HD = 8, 128
FOX_WIDTH = FOX_HEADS * FOX_HD
CONV_CH = 1024
GLA_HEADS, GLA_DK, GLA_DV, GLA_RANK, GLA_TAU = 4, 128, 256, 16, 16.0
GLA_SUB = 32
D_FF = 5632
N_IN = 15384
DEPTH = 2

_R = dict(qa=0, ka=1024, va=2048, fa=3072, scb=3080, scc=4104, sch=5128, qc=6152, kc=6664,
          vc=7176, rc=8200, glr=9224, ga=9240, gb=11288, gc=13336)
QA, KA, VA, SCB, SCC, SCH, QC, KC, VC, RC, GA, GB, GC = (
    0, 1024, 2048, 3072, 4096, 5120, 6144, 6656, 7168, 8192, 9216, 11264, 13312)
N_MAIN = 15360
N_SIDE = 256

ADAM_LR, ADAM_B1, ADAM_B2, ADAM_EPS, ADAM_WD, ADAM_STEP = 0.001, 0.9, 0.999, 1e-08, 0.01, 10

SHARDED = (("w_in", (D_MODEL, N_IN), 1), ("conv_w", (3, CONV_CH), 1), ("gla_w_g2", (GLA_RANK, 512), 1),
           ("w_a_o", (1024, D_MODEL), 1), ("w_b_o", (1024, D_MODEL), 1), ("w_c_o", (1024, D_MODEL), 1),
           ("w_o", (D_MODEL, D_MODEL), 0), ("w_up", (D_MODEL, 2 * D_FF), 1), ("mlp_conv_w", (3, 2 * D_FF), 1),
           ("w_down", (D_FF, D_MODEL), 0), ("meta_tokens", (N_META // 2, D_MODEL), 1))
REPLICATED = (("norm1_g", (2, D_MODEL)), ("fox_b_f", (2, 8)), ("gate_b", (2, 3 * D_MODEL)), ("gla_b_g", (2, 512)),
              ("gla_norm_g", (2, 1024)), ("norm2_g", (2, D_MODEL)), ("final_norm_g", (D_MODEL,)))
WEIGHT_ORDER = ("meta_tokens", "norm1_g", "w_in", "fox_b_f", "gate_b", "conv_w", "gla_w_g2", "gla_b_g",
                "gla_norm_g", "w_a_o", "w_b_o", "w_c_o", "w_o", "norm2_g", "w_up", "mlp_conv_w", "w_down",
                "final_norm_g")
PACK_COLS = 1024
GATHER_F32 = ("conv_w", "mlp_conv_w", "meta_tokens")


def _pick(n, cands):
    for c in cands:
        if n % c == 0:
            return c
    return n


def _params(sem):
    return pltpu.CompilerParams(dimension_semantics=sem, vmem_limit_bytes=VMEM_LIMIT)


def _sigmoid(x):
    return jax.nn.sigmoid(x)


def _log_sigmoid(x):
    return jnp.minimum(x, 0.0) - jnp.log(1.0 + jnp.exp(-jnp.abs(x)))


def _mm(a, b, mode, out_dtype, name, add=None):
    if mode == "nn":
        (M, K), (K2, N) = a.shape, b.shape
    elif mode == "nt":
        (M, K), (N, K2) = a.shape, b.shape
    else:
        (K, M), (K2, N) = a.shape, b.shape
    assert K == K2, (name, a.shape, b.shape)
    if mode == "tn":
        tm = _pick(M, (2048, 1408, 1024, 512, 256, 128))
        tn = _pick(N, (512, 256, 128))
        tk = _pick(K, (640, 512, 384, 256, 128))
    else:
        tm = _pick(M, (1664, 640, 384, 128))
        tn = _pick(N, (512, 256, 128))
        tk = K if K <= 2048 else _pick(K, (1408, 1024, 512, 256, 128))
    nk = K // tk
    dims = {"nn": (((1,), (0,)), ((), ())), "nt": (((1,), (1,)), ((), ())), "tn": (((0,), (0,)), ((), ()))}[mode]

    def body(*refs):
        if add is None:
            a_ref, b_ref, o_ref, acc = refs
        else:
            a_ref, b_ref, add_ref, o_ref, acc = refs
        k = pl.program_id(2)

        @pl.when(k == 0)
        def _():
            acc[...] = jnp.zeros_like(acc)

        acc[...] += lax.dot_general(a_ref[...].astype(BF16), b_ref[...].astype(BF16), dims,
                                    preferred_element_type=F32)

        @pl.when(k == nk - 1)
        def _():
            r = acc[...]
            if add is not None:
                r = r + add_ref[...].astype(F32)
            o_ref[...] = r.astype(o_ref.dtype)

    a_spec = {"nn": pl.BlockSpec((tm, tk), lambda i, j, k: (i, k)),
              "nt": pl.BlockSpec((tm, tk), lambda i, j, k: (i, k)),
              "tn": pl.BlockSpec((tk, tm), lambda i, j, k: (k, i))}[mode]
    b_spec = {"nn": pl.BlockSpec((tk, tn), lambda i, j, k: (k, j)),
              "nt": pl.BlockSpec((tn, tk), lambda i, j, k: (j, k)),
              "tn": pl.BlockSpec((tk, tn), lambda i, j, k: (k, j))}[mode]
    o_spec = pl.BlockSpec((tm, tn), lambda i, j, k: (i, j))
    ins, specs = [a, b], [a_spec, b_spec]
    if add is not None:
        ins.append(add)
        specs.append(o_spec)
    return pl.pallas_call(
        body, name=name, grid=(M // tm, N // tn, nk), in_specs=specs, out_specs=o_spec,
        out_shape=jax.ShapeDtypeStruct((M, N), out_dtype),
        scratch_shapes=[pltpu.VMEM((tm, tn), F32)],
        compiler_params=_params(("parallel", "parallel", "arbitrary")),
    )(*ins)


class Row:
    def __init__(self, arr, w, cb=None):
        self.arr, self.w, self.cb = arr, w, (cb if cb is not None else (lambda g: g))


class Const:
    def __init__(self, arr, shape=None, idx=None):
        self.arr = arr
        self.shape = shape if shape is not None else arr.shape
        self.idx = idx if idx is not None else (lambda g: (0,) * arr.ndim)


def _row_spec(r, tm):
    return pl.BlockSpec((tm, r.w), lambda g, i, r=r: (i, r.cb(g)))


def _const_spec(c):
    return pl.BlockSpec(c.shape, lambda g, i, c=c: c.idx(g))


def _valid_rows(i, tm):
    return (i * tm + lax.broadcasted_iota(jnp.int32, (tm, 1), 0)) >= PAD


def _rw_fwd(name, f, rows, consts, outs, Lp, tm, G=1):
    nr, nc = len(rows), len(consts)

    def body(*refs):
        i = pl.program_id(1)
        rv = [r[...].astype(F32) for r in refs[:nr]]
        cv = [r[...].astype(F32) for r in refs[nr:nr + nc]]
        res = f(_valid_rows(i, tm), *rv, *cv)
        for o_ref, v in zip(refs[nr + nc:], res):
            o_ref[...] = v.astype(o_ref.dtype)

    return pl.pallas_call(
        body, name=name, grid=(G, Lp // tm),
        in_specs=[_row_spec(r, tm) for r in rows] + [_const_spec(c) for c in consts],
        out_specs=[pl.BlockSpec((tm, w), lambda g, i: (i, g)) for w, _ in outs],
        out_shape=[jax.ShapeDtypeStruct((Lp, w * G), dt) for w, dt in outs],
        compiler_params=_params(("parallel", "arbitrary")),
    )(*[r.arr for r in rows], *[c.arr for c in consts])


def _rw_bwd(name, f, rows, consts, cts, drow_dtypes, adds, Lp, tm, G=1):
    nr, nc, nt = len(rows), len(consts), len(cts)
    want = [k for k, dt in enumerate(drow_dtypes) if dt is not None]
    add_k = [k for k in want if adds[k] is not None]

    def body(*refs):
        i = pl.program_id(1)
        pos = 0
        rv = [r[...].astype(F32) for r in refs[pos:pos + nr]]
        pos += nr
        cv = [r[...].astype(F32) for r in refs[pos:pos + nc]]
        pos += nc
        tv = [r[...].astype(F32) for r in refs[pos:pos + nt]]
        pos += nt
        av = {k: refs[pos + n][...].astype(F32) for n, k in enumerate(add_k)}
        pos += len(add_k)
        drow_refs = refs[pos:pos + len(want)]
        pos += len(want)
        dconst_refs = refs[pos:pos + nc]
        valid = _valid_rows(i, tm)
        _, vjp = jax.vjp(lambda *a: tuple(f(valid, *a)), *rv, *cv)
        grads = vjp(tuple(tv))
        for o_ref, k in zip(drow_refs, want):
            gk = grads[k]
            if k in av:
                gk = gk + av[k]
            o_ref[...] = gk.astype(o_ref.dtype)
        for n, o_ref in enumerate(dconst_refs):
            gc = grads[nr + n]

            @pl.when(i == 0)
            def _(o_ref=o_ref, gc=gc):
                o_ref[...] = gc

            @pl.when(i > 0)
            def _(o_ref=o_ref, gc=gc):
                o_ref[...] += gc

    out_row = lambda w: pl.BlockSpec((tm, w), lambda g, i: (i, g))
    res = pl.pallas_call(
        body, name=name, grid=(G, Lp // tm),
        in_specs=([_row_spec(r, tm) for r in rows] + [_const_spec(c) for c in consts]
                  + [_row_spec(r, tm) for r in cts] + [out_row(rows[k].w) for k in add_k]),
        out_specs=[out_row(rows[k].w) for k in want] + [_const_spec(c) for c in consts],
        out_shape=([jax.ShapeDtypeStruct((Lp, rows[k].w * G), drow_dtypes[k]) for k in want]
                   + [jax.ShapeDtypeStruct(c.arr.shape, F32) for c in consts]),
        compiler_params=_params(("parallel", "arbitrary")),
    )(*[r.arr for r in rows], *[c.arr for c in consts], *[r.arr for r in cts], *[adds[k] for k in add_k])
    drows = [None] * nr
    for n, k in enumerate(want):
        drows[k] = res[n]
    return drows, list(res[len(want):])


def _f_rms(valid, h, g):
    r = lax.rsqrt(jnp.mean(h * h, axis=-1, keepdims=True) + EPS)
    return (jnp.where(valid, h * r * g, 0.0),)


def _f_logg(valid, glr, w, b):
    pre = jnp.dot(glr.astype(BF16), w.astype(BF16), preferred_element_type=F32) + b
    return (jnp.where(valid, _log_sigmoid(pre) / GLA_TAU, 0.0),)


def _f_gla_post(valid, oc, rc, g):
    y = oc * lax.rsqrt(jnp.mean(oc * oc, axis=-1, keepdims=True) + EPS) * g
    return (jnp.where(valid, rc * _sigmoid(rc) * y, 0.0),)


def _f_merge(valid, ya, yb, yc, ga, gb, gc, ba, bb, bc):
    mix = _sigmoid(ga + ba) * ya + _sigmoid(gb + bb) * yb + _sigmoid(gc + bc) * yc
    return (jnp.where(valid, mix, 0.0),)


def _fox_gate_fwd(side, bf, Lp):
    t = BLOCK
    n = Lp // t

    def body(s_ref, b_ref, c_ref, carry):
        i = pl.program_id(0)

        @pl.when(i == 0)
        def _():
            carry[...] = jnp.zeros_like(carry)

        lane = lax.broadcasted_iota(jnp.int32, (t, LANES), 1)
        ok = _valid_rows(i, t) & (lane < FOX_HEADS)
        logf = jnp.where(ok, _log_sigmoid(s_ref[...] + b_ref[...]), 0.0)
        tril = (lax.broadcasted_iota(jnp.int32, (t, t), 1) <= lax.broadcasted_iota(jnp.int32, (t, t), 0)).astype(F32)
        c = jnp.dot(tril, logf, precision=HIGHEST, preferred_element_type=F32) + carry[...]
        c_ref[...] = c
        carry[...] = c[t - 1:t, :]

    return pl.pallas_call(
        body, name="fox_gate_fwd", grid=(n,),
        in_specs=[pl.BlockSpec((t, LANES), lambda i: (i, 0)), pl.BlockSpec((1, LANES), lambda i: (0, 0))],
        out_specs=pl.BlockSpec((t, LANES), lambda i: (i, 0)),
        out_shape=jax.ShapeDtypeStruct((Lp, LANES), F32),
        scratch_shapes=[pltpu.VMEM((1, LANES), F32)],
        compiler_params=_params(("arbitrary",)),
    )(side, bf)


def _fox_gate_bwd(side, bf, dc, Lp):
    t = BLOCK
    n = Lp // t

    def body(s_ref, b_ref, dc_ref, dfa_ref, db_ref, carry):
        i = pl.program_id(0)

        @pl.when(i == 0)
        def _():
            carry[...] = jnp.zeros_like(carry)

        lane = lax.broadcasted_iota(jnp.int32, (t, LANES), 1)
        ok = _valid_rows(n - 1 - i, t) & (lane < FOX_HEADS)
        triu = (lax.broadcasted_iota(jnp.int32, (t, t), 1) >= lax.broadcasted_iota(jnp.int32, (t, t), 0)).astype(F32)
        dlogf = jnp.dot(triu, dc_ref[...], precision=HIGHEST, preferred_element_type=F32) + carry[...]
        carry[...] = dlogf[0:1, :]
        dpre = jnp.where(ok, dlogf * _sigmoid(-(s_ref[...] + b_ref[...])), 0.0)
        dfa_ref[...] = dpre
        part = jnp.sum(dpre, axis=0, keepdims=True)

        @pl.when(i == 0)
        def _():
            db_ref[...] = part

        @pl.when(i > 0)
        def _():
            db_ref[...] += part

    rev = lambda i: (n - 1 - i, 0)
    return pl.pallas_call(
        body, name="fox_gate_bwd", grid=(n,),
        in_specs=[pl.BlockSpec((t, LANES), rev), pl.BlockSpec((1, LANES), lambda i: (0, 0)),
                  pl.BlockSpec((t, LANES), rev)],
        out_specs=[pl.BlockSpec((t, LANES), rev), pl.BlockSpec((1, LANES), lambda i: (0, 0))],
        out_shape=[jax.ShapeDtypeStruct((Lp, LANES), F32), jax.ShapeDtypeStruct((1, LANES), F32)],
        scratch_shapes=[pltpu.VMEM((1, LANES), F32)],
        compiler_params=_params(("arbitrary",)),
    )(side, bf, dc)


def _fox_scores(q, k, cq, ck, i, j, t):
    s = lax.dot_general(q, k, (((1,), (1,)), ((), ())), preferred_element_type=F32) * (FOX_HD ** -0.5)
    s = s + (cq - ck)
    row = i * t + lax.broadcasted_iota(jnp.int32, (t, t), 0)
    col = j * t + lax.broadcasted_iota(jnp.int32, (t, t), 1)
    mask = (col <= row) & (col >= PAD)
    return jnp.where(mask, s, NEG), mask


def _fox_fwd(main, c_col, c_row, Lp, t):
    n = Lp // t
    qb, kb, vb = QA // FOX_HD, KA // FOX_HD, VA // FOX_HD

    def body(q_ref, k_ref, v_ref, cq_ref, ck_ref, o_ref, lse_ref, m_s, l_s, acc):
        i, j = pl.program_id(1), pl.program_id(2)

        @pl.when(j == 0)
        def _():
            m_s[...] = jnp.full_like(m_s, NEG)
            l_s[...] = jnp.zeros_like(l_s)
            acc[...] = jnp.zeros_like(acc)

        @pl.when(j <= i)
        def _():
            s, _ = _fox_scores(q_ref[...], k_ref[...], cq_ref[...], ck_ref[...], i, j, t)
            m_new = jnp.maximum(m_s[...], jnp.max(s, axis=1, keepdims=True))
            alpha = jnp.exp(m_s[...] - m_new)
            p = jnp.exp(s - m_new)
            l_s[...] = alpha * l_s[...] + jnp.sum(p, axis=1, keepdims=True)
            acc[...] = alpha * acc[...] + jnp.dot(p.astype(BF16), v_ref[...], preferred_element_type=F32)
            m_s[...] = m_new

        @pl.when(j == i)
        def _():
            o_ref[...] = jnp.where(_valid_rows(i, t), acc[...] / l_s[...], 0.0).astype(o_ref.dtype)
            lse_ref[...] = m_s[...] + jnp.log(l_s[...])

    kv = lambda base: pl.BlockSpec((t, FOX_HD), lambda h, i, j: (jnp.minimum(j, i), base + h))
    return pl.pallas_call(
        body, name="fox_fwd", grid=(FOX_HEADS, n, n),
        in_specs=[pl.BlockSpec((t, FOX_HD), lambda h, i, j: (i, qb + h)), kv(kb), kv(vb),
                  pl.BlockSpec((None, t, 1), lambda h, i, j: (h, i, 0)),
                  pl.BlockSpec((None, 1, t), lambda h, i, j: (h, 0, jnp.minimum(j, i)))],
        out_specs=[pl.BlockSpec((t, FOX_HD), lambda h, i, j: (i, h)),
                   pl.BlockSpec((None, t, 1), lambda h, i, j: (h, i, 0))],
        out_shape=[jax.ShapeDtypeStruct((Lp, FOX_WIDTH), BF16), jax.ShapeDtypeStruct((FOX_HEADS, Lp, 1), F32)],
        scratch_shapes=[pltpu.VMEM((t, 1), F32), pltpu.VMEM((t, 1), F32), pltpu.VMEM((t, FOX_HD), F32)],
        compiler_params=_params(("parallel", "parallel", "arbitrary")),
    )(main, main, main, c_col, c_row)


def _fox_bwd_dq(main, c_col, c_row, lse, doa, Lp, t):
    n = Lp // t
    qb, kb, vb = QA // FOX_HD, KA // FOX_HD, VA // FOX_HD

    def body(q_ref, k_ref, v_ref, cq_ref, ck_ref, lse_ref, do_ref, dq_ref, dl_ref, acc, dl_s):
        i, jj = pl.program_id(1), pl.program_id(2)
        second = jj >= n
        j = jnp.where(second, jj - n, jj)

        @pl.when(jj == 0)
        def _():
            acc[...] = jnp.zeros_like(acc)
            dl_s[...] = jnp.zeros_like(dl_s)

        def p_and_dp():
            s, mask = _fox_scores(q_ref[...], k_ref[...], cq_ref[...], ck_ref[...], i, j, t)
            p = jnp.where(mask, jnp.exp(s - lse_ref[...]), 0.0)
            dp = lax.dot_general(do_ref[...], v_ref[...], (((1,), (1,)), ((), ())), preferred_element_type=F32)
            return p, dp

        @pl.when(jnp.logical_and(jnp.logical_not(second), j <= i))
        def _():
            p, dp = p_and_dp()
            dl_s[...] += jnp.sum(p * dp, axis=1, keepdims=True)

        @pl.when(jnp.logical_and(second, j <= i))
        def _():
            p, dp = p_and_dp()
            ds = p * (dp - dl_s[...])
            acc[...] += jnp.dot(ds.astype(BF16), k_ref[...], preferred_element_type=F32)

        @pl.when(jnp.logical_and(second, j == i))
        def _():
            dq_ref[...] = (acc[...] * (FOX_HD ** -0.5)).astype(dq_ref.dtype)
            dl_ref[...] = dl_s[...]

    blk = lambda i, jj: jnp.minimum(jnp.where(jj >= n, jj - n, jj), i)
    kv = lambda base: pl.BlockSpec((t, FOX_HD), lambda h, i, jj: (blk(i, jj), base + h))
    qrow = lambda base: pl.BlockSpec((t, FOX_HD), lambda h, i, jj: (i, base + h))
    col = pl.BlockSpec((None, t, 1), lambda h, i, jj: (h, i, 0))
    return pl.pallas_call(
        body, name="fox_bwd_dq", grid=(FOX_HEADS, n, 2 * n),
        in_specs=[qrow(qb), kv(kb), kv(vb), col,
                  pl.BlockSpec((None, 1, t), lambda h, i, jj: (h, 0, blk(i, jj))), col, qrow(0)],
        out_specs=[qrow(0), col],
        out_shape=[jax.ShapeDtypeStruct((Lp, FOX_WIDTH), BF16), jax.ShapeDtypeStruct((FOX_HEADS, Lp, 1), F32)],
        scratch_shapes=[pltpu.VMEM((t, FOX_HD), F32), pltpu.VMEM((t, 1), F32)],
        compiler_params=_params(("parallel", "parallel", "arbitrary")),
    )(main, main, main, c_col, c_row, lse, doa)


def _fox_bwd_dkv(main, c_col, c_row, lse, delta, doa, Lp, t):
    n = Lp // t
    qb, kb, vb = QA // FOX_HD, KA // FOX_HD, VA // FOX_HD

    def body(q_ref, k_ref, v_ref, cq_ref, ck_ref, lse_ref, dl_ref, do_ref, dk_ref, dv_ref, dck_ref, dk_s, dv_s, dc_s):
        j, i = pl.program_id(1), pl.program_id(2)

        @pl.when(i == 0)
        def _():
            dk_s[...] = jnp.zeros_like(dk_s)
            dv_s[...] = jnp.zeros_like(dv_s)
            dc_s[...] = jnp.zeros_like(dc_s)

        @pl.when(i >= j)
        def _():
            s, mask = _fox_scores(q_ref[...], k_ref[...], cq_ref[...], ck_ref[...], i, j, t)
            p = jnp.where(mask, jnp.exp(s - lse_ref[...]), 0.0)
            do = do_ref[...]
            dv_s[...] += lax.dot_general(p.astype(BF16), do, (((0,), (0,)), ((), ())), preferred_element_type=F32)
            dp = lax.dot_general(do, v_ref[...], (((1,), (1,)), ((), ())), preferred_element_type=F32)
            ds = p * (dp - dl_ref[...])
            dk_s[...] += lax.dot_general(ds.astype(BF16), q_ref[...], (((0,), (0,)), ((), ())),
                                         preferred_element_type=F32)
            dc_s[...] -= jnp.sum(ds, axis=0, keepdims=True)

        @pl.when(i == n - 1)
        def _():
            dk_ref[...] = (dk_s[...] * (FOX_HD ** -0.5)).astype(dk_ref.dtype)
            dv_ref[...] = dv_s[...].astype(dv_ref.dtype)
            dck_ref[...] = dc_s[...]

    qrow = lambda base: pl.BlockSpec((t, FOX_HD), lambda h, j, i: (jnp.maximum(i, j), base + h))
    kv = lambda base: pl.BlockSpec((t, FOX_HD), lambda h, j, i: (j, base + h))
    col = pl.BlockSpec((None, t, 1), lambda h, j, i: (h, jnp.maximum(i, j), 0))
    row = pl.BlockSpec((None, 1, t), lambda h, j, i: (h, 0, j))
    return pl.pallas_call(
        body, name="fox_bwd_dkv", grid=(FOX_HEADS, n, n),
        in_specs=[qrow(qb), kv(kb), kv(vb), col, row, col, col, qrow(0)],
        out_specs=[kv(0), kv(0), row],
        out_shape=[jax.ShapeDtypeStruct((Lp, FOX_WIDTH), BF16), jax.ShapeDtypeStruct((Lp, FOX_WIDTH), BF16),
                   jax.ShapeDtypeStruct((FOX_HEADS, 1, Lp), F32)],
        scratch_shapes=[pltpu.VMEM((t, FOX_HD), F32), pltpu.VMEM((t, FOX_HD), F32), pltpu.VMEM((1, t), F32)],
        compiler_params=_params(("parallel", "parallel", "arbitrary")),
    )(main, main, main, c_col, c_row, lse, delta, doa)


def _shift_down(x, n):
    return pltpu.roll(x, n, 0)


def _shift_up(x, n):
    return pltpu.roll(x, x.shape[0] - n, 0)


def _prev_spec(tm, ct, cb):
    return pl.BlockSpec((HALO, ct), lambda g, i: (jnp.maximum(i * (tm // HALO) - 1, 0), cb(g)))


def _next_spec(tm, ct, cb, nrows):
    last = nrows // HALO - 1
    return pl.BlockSpec((HALO, ct), lambda g, i: (jnp.minimum((i + 1) * (tm // HALO), last), cb(g)))


def _cur_spec(tm, ct, cb):
    return pl.BlockSpec((tm, ct), lambda g, i: (i, cb(g)))


def _wrow(w_ref, k):
    return w_ref[k:k + 1, :]


def _rows3(s0, s1, s2, ct):
    r = lax.broadcasted_iota(jnp.int32, (8, ct), 0)
    return jnp.where(r == 0, s0, jnp.where(r == 1, s1, jnp.where(r == 2, s2, 0.0)))


def _acc_out(ref, i, val):
    @pl.when(i == 0)
    def _():
        ref[...] = val

    @pl.when(i > 0)
    def _():
        ref[...] += val


def _sconv_fwd(main, w8, Lp, tm):
    ct = 256
    G = CONV_CH // ct
    bb, cb, hb = (lambda g: SCB // ct + g), (lambda g: SCC // ct + g), (lambda g: SCH // ct + g)

    def body(b_ref, c_ref, h_ref, cp_ref, hp_ref, w_ref, o_ref):
        i = pl.program_id(1)
        z = c_ref[...].astype(F32) * h_ref[...].astype(F32)
        zp = jnp.where(i > 0, cp_ref[...].astype(F32) * hp_ref[...].astype(F32), 0.0)
        zz = jnp.concatenate([zp, z], axis=0)
        cz = (_wrow(w_ref, 0) * _shift_down(zz, 2)[HALO:] + _wrow(w_ref, 1) * _shift_down(zz, 1)[HALO:]
              + _wrow(w_ref, 2) * z)
        o_ref[...] = (b_ref[...].astype(F32) * cz).astype(o_ref.dtype)

    return pl.pallas_call(
        body, name="sconv_fwd", grid=(G, Lp // tm),
        in_specs=[_cur_spec(tm, ct, bb), _cur_spec(tm, ct, cb), _cur_spec(tm, ct, hb),
                  _prev_spec(tm, ct, cb), _prev_spec(tm, ct, hb), pl.BlockSpec((8, ct), lambda g, i: (0, g))],
        out_specs=pl.BlockSpec((tm, ct), lambda g, i: (i, g)),
        out_shape=jax.ShapeDtypeStruct((Lp, CONV_CH), BF16),
        compiler_params=_params(("parallel", "arbitrary")),
    )(main, main, main, main, main, w8)


def _sconv_bwd(main, w8, dub, Lp, tm):
    ct = 256
    G = CONV_CH // ct
    n = Lp // tm
    bb, cb, hb, ob = (lambda g: SCB // ct + g), (lambda g: SCC // ct + g), (lambda g: SCH // ct + g), (lambda g: g)

    def body(b_ref, c_ref, h_ref, cp_ref, hp_ref, bn_ref, d_ref, dn_ref, w_ref, db_ref, dc_ref, dh_ref, dw_ref):
        i = pl.program_id(1)
        b, c, h = b_ref[...].astype(F32), c_ref[...].astype(F32), h_ref[...].astype(F32)
        z = c * h
        zp = jnp.where(i > 0, cp_ref[...].astype(F32) * hp_ref[...].astype(F32), 0.0)
        zz = jnp.concatenate([zp, z], axis=0)
        z1, z2 = _shift_down(zz, 1)[HALO:], _shift_down(zz, 2)[HALO:]
        w0, w1, w2 = _wrow(w_ref, 0), _wrow(w_ref, 1), _wrow(w_ref, 2)
        cz = w0 * z2 + w1 * z1 + w2 * z
        dub_c = d_ref[...].astype(F32)
        db_ref[...] = (dub_c * cz).astype(db_ref.dtype)
        dcz = dub_c * b
        dcz_n = jnp.where(i < n - 1, dn_ref[...].astype(F32) * bn_ref[...].astype(F32), 0.0)
        dd = jnp.concatenate([dcz, dcz_n], axis=0)
        dz = w2 * dcz + w1 * _shift_up(dd, 1)[:tm] + w0 * _shift_up(dd, 2)[:tm]
        dc_ref[...] = (dz * h).astype(dc_ref.dtype)
        dh_ref[...] = (dz * c).astype(dh_ref.dtype)
        s = lambda x: jnp.sum(dcz * x, axis=0, keepdims=True)
        _acc_out(dw_ref, i, _rows3(s(z2), s(z1), s(z), ct))

    out = pl.BlockSpec((tm, ct), lambda g, i: (i, g))
    return pl.pallas_call(
        body, name="sconv_bwd", grid=(G, n),
        in_specs=[_cur_spec(tm, ct, bb), _cur_spec(tm, ct, cb), _cur_spec(tm, ct, hb),
                  _prev_spec(tm, ct, cb), _prev_spec(tm, ct, hb), _next_spec(tm, ct, bb, Lp),
                  _cur_spec(tm, ct, ob), _next_spec(tm, ct, ob, Lp), pl.BlockSpec((8, ct), lambda g, i: (0, g))],
        out_specs=[out, out, out, pl.BlockSpec((8, ct), lambda g, i: (0, g))],
        out_shape=[jax.ShapeDtypeStruct((Lp, CONV_CH), BF16)] * 3 + [jax.ShapeDtypeStruct((8, CONV_CH), F32)],
        compiler_params=_params(("parallel", "arbitrary")),
    )(main, main, main, main, main, main, dub, dub, w8)


def _conv3(w_ref, ext):
    return _wrow(w_ref, 0) * _shift_down(ext, 2) + _wrow(w_ref, 1) * _shift_down(ext, 1) + _wrow(w_ref, 2) * ext


def _mlp_act_fwd(up, w8, Lp, tm):
    ct = 256
    G = D_FF // ct
    gb, ub = (lambda g: g), (lambda g: G + g)

    def body(g_ref, u_ref, gp_ref, up_ref, wg_ref, wu_ref, o_ref):
        i = pl.program_id(1)

        def conv(cur, prev, w_ref):
            ext = jnp.concatenate([jnp.where(i > 0, prev[...].astype(F32), 0.0), cur[...].astype(F32)], axis=0)
            return _conv3(w_ref, ext)[HALO:]

        ug, uu = conv(g_ref, gp_ref, wg_ref), conv(u_ref, up_ref, wu_ref)
        o_ref[...] = (ug * _sigmoid(ug) * uu).astype(o_ref.dtype)

    wspec = lambda cb: pl.BlockSpec((8, ct), lambda g, i: (0, cb(g)))
    return pl.pallas_call(
        body, name="mlp_act_fwd", grid=(G, Lp // tm),
        in_specs=[_cur_spec(tm, ct, gb), _cur_spec(tm, ct, ub), _prev_spec(tm, ct, gb), _prev_spec(tm, ct, ub),
                  wspec(gb), wspec(ub)],
        out_specs=pl.BlockSpec((tm, ct), lambda g, i: (i, g)),
        out_shape=jax.ShapeDtypeStruct((Lp, D_FF), BF16),
        compiler_params=_params(("parallel", "arbitrary")),
    )(up, up, up, up, w8, w8)


def _mlp_act_bwd(up, w8, da, Lp, tm):
    ct = 256
    G = D_FF // ct
    n = Lp // tm
    gb, ub, ob = (lambda g: g), (lambda g: G + g), (lambda g: g)

    def body(g_ref, u_ref, gp_ref, up_ref, gn_ref, un_ref, d_ref, dn_ref, wg_ref, wu_ref,
             dg_ref, du_ref, dwg_ref, dwu_ref):
        i = pl.program_id(1)

        def ext_of(prev, cur, nxt):
            return jnp.concatenate([jnp.where(i > 0, prev[...].astype(F32), 0.0), cur[...].astype(F32),
                                    jnp.where(i < n - 1, nxt[...].astype(F32), 0.0)], axis=0)

        eg, eu = ext_of(gp_ref, g_ref, gn_ref), ext_of(up_ref, u_ref, un_ref)
        da_e = jnp.concatenate([jnp.zeros((HALO, ct), F32), d_ref[...].astype(F32),
                                jnp.where(i < n - 1, dn_ref[...].astype(F32), 0.0)], axis=0)
        ug, uu = _conv3(wg_ref, eg), _conv3(wu_ref, eu)
        sg = _sigmoid(ug)
        dug = da_e * uu * (sg * (1.0 + ug * (1.0 - sg)))
        duu = da_e * (ug * sg)
        cur = slice(HALO, HALO + tm)

        def back(w_ref, dx, e, dx_ref, dw_ref):
            d_in = _wrow(w_ref, 2) * dx + _wrow(w_ref, 1) * _shift_up(dx, 1) + _wrow(w_ref, 0) * _shift_up(dx, 2)
            dx_ref[...] = d_in[cur].astype(dx_ref.dtype)
            s = lambda x: jnp.sum(dx[cur] * x[cur], axis=0, keepdims=True)
            _acc_out(dw_ref, i, _rows3(s(_shift_down(e, 2)), s(_shift_down(e, 1)), s(e), ct))

        back(wg_ref, dug, eg, dg_ref, dwg_ref)
        back(wu_ref, duu, eu, du_ref, dwu_ref)

    wspec = lambda cb: pl.BlockSpec((8, ct), lambda g, i: (0, cb(g)))
    out = pl.BlockSpec((tm, ct), lambda g, i: (i, g))
    return pl.pallas_call(
        body, name="mlp_act_bwd", grid=(G, n),
        in_specs=[_cur_spec(tm, ct, gb), _cur_spec(tm, ct, ub), _prev_spec(tm, ct, gb), _prev_spec(tm, ct, ub),
                  _next_spec(tm, ct, gb, Lp), _next_spec(tm, ct, ub, Lp), _cur_spec(tm, ct, ob),
                  _next_spec(tm, ct, ob, Lp), wspec(gb), wspec(ub)],
        out_specs=[out, out, wspec(ob), wspec(ob)],
        out_shape=[jax.ShapeDtypeStruct((Lp, D_FF), BF16)] * 2 + [jax.ShapeDtypeStruct((8, D_FF), F32)] * 2,
        compiler_params=_params(("parallel", "arbitrary")),
    )(up, up, up, up, up, up, da, da, w8, w8)


def _gla_chunk(q, k, v, g, s0):
    C = BLOCK
    r_i = lax.broadcasted_iota(jnp.int32, (C, C), 0)
    c_i = lax.broadcasted_iota(jnp.int32, (C, C), 1)
    hdot = functools.partial(jnp.dot, precision=HIGHEST, preferred_element_type=F32)
    b = hdot((c_i <= r_i).astype(F32), g)
    sub_start = jnp.bitwise_and(r_i, -GLA_SUB)
    ref_all = hdot((c_i == sub_start).astype(F32), b)
    qs = q * (GLA_DK ** -0.5)
    qt = (qs * jnp.exp(b - ref_all)).astype(BF16)
    att = jnp.zeros((C, C), F32)
    for n in range(C // GLA_SUB):
        ref_n = hdot((c_i == n * GLA_SUB).astype(F32), b)
        kt = (k * jnp.exp(jnp.minimum(ref_n - b, 60.0))).astype(BF16)
        a_n = lax.dot_general(qt, kt, (((1,), (1,)), ((), ())), preferred_element_type=F32)
        att = att + jnp.where((sub_start == n * GLA_SUB) & (c_i <= r_i), a_n, 0.0)
    o = (jnp.dot(att.astype(BF16), v.astype(BF16), preferred_element_type=F32)
         + jnp.dot((qs * jnp.exp(b)).astype(BF16), s0.astype(BF16), preferred_element_type=F32))
    b_last = hdot((c_i == C - 1).astype(F32), b)
    kd = (k * jnp.exp(b_last - b)).astype(BF16)
    last_rows = (lax.broadcasted_iota(jnp.int32, (C, GLA_DV), 0) == C - 1).astype(F32)
    decay = lax.dot_general(b, last_rows, (((0,), (0,)), ((), ())), precision=HIGHEST,
                            preferred_element_type=F32)
    s1 = jnp.exp(decay) * s0 + lax.dot_general(kd, v.astype(BF16), (((0,), (0,)), ((), ())),
                                               preferred_element_type=F32)
    return o, s1


def _gla_fwd(main, logg, Lp):
    n = Lp // BLOCK
    qb, kb, vb = QC // GLA_DK, KC // GLA_DK, VC // GLA_DV

    def body(q_ref, k_ref, v_ref, g_ref, o_ref, st_ref, s_s):
        c = pl.program_id(1)

        @pl.when(c == 0)
        def _():
            s_s[...] = jnp.zeros_like(s_s)

        s0 = s_s[...]
        st_ref[...] = s0
        o, s1 = _gla_chunk(q_ref[...].astype(F32), k_ref[...].astype(F32), v_ref[...].astype(F32), g_ref[...], s0)
        o_ref[...] = o
        s_s[...] = s1

    return pl.pallas_call(
        body, name="gla_fwd", grid=(GLA_HEADS, n),
        in_specs=[pl.BlockSpec((BLOCK, GLA_DK), lambda h, c: (c, qb + h)),
                  pl.BlockSpec((BLOCK, GLA_DK), lambda h, c: (c, kb + h)),
                  pl.BlockSpec((BLOCK, GLA_DV), lambda h, c: (c, vb + h)),
                  pl.BlockSpec((BLOCK, GLA_DK), lambda h, c: (c, h))],
        out_specs=[pl.BlockSpec((BLOCK, GLA_DV), lambda h, c: (c, h)),
                   pl.BlockSpec((None, None, GLA_DK, GLA_DV), lambda h, c: (h, c, 0, 0))],
        out_shape=[jax.ShapeDtypeStruct((Lp, GLA_HEADS * GLA_DV), F32),
                   jax.ShapeDtypeStruct((GLA_HEADS, n, GLA_DK, GLA_DV), F32)],
        scratch_shapes=[pltpu.VMEM((GLA_DK, GLA_DV), F32)],
        compiler_params=_params(("parallel", "arbitrary")),
    )(main, main, main, logg)


def _gla_bwd(main, logg, states, do, Lp):
    n = Lp // BLOCK
    qb, kb, vb = QC // GLA_DK, KC // GLA_DK, VC // GLA_DV

    def body(q_ref, k_ref, v_ref, g_ref, st_ref, do_ref, dq_ref, dk_ref, dv_ref, dg_ref, ds_s):
        c = pl.program_id(1)

        @pl.when(c == 0)
        def _():
            ds_s[...] = jnp.zeros_like(ds_s)

        _, vjp = jax.vjp(_gla_chunk, q_ref[...].astype(F32), k_ref[...].astype(F32), v_ref[...].astype(F32),
                         g_ref[...], st_ref[...])
        dq, dk, dv, dg, ds0 = vjp((do_ref[...], ds_s[...]))
        dq_ref[...] = dq.astype(dq_ref.dtype)
        dk_ref[...] = dk.astype(dk_ref.dtype)
        dv_ref[...] = dv.astype(dv_ref.dtype)
        dg_ref[...] = dg
        ds_s[...] = ds0

    rk = lambda base: pl.BlockSpec((BLOCK, GLA_DK), lambda h, c: (n - 1 - c, base + h))
    rv = lambda base: pl.BlockSpec((BLOCK, GLA_DV), lambda h, c: (n - 1 - c, base + h))
    return pl.pallas_call(
        body, name="gla_bwd", grid=(GLA_HEADS, n),
        in_specs=[rk(qb), rk(kb), rv(vb), rk(0),
                  pl.BlockSpec((None, None, GLA_DK, GLA_DV), lambda h, c: (h, n - 1 - c, 0, 0)), rv(0)],
        out_specs=[rk(0), rk(0), rv(0), rk(0)],
        out_shape=[jax.ShapeDtypeStruct((Lp, GLA_HEADS * GLA_DK), BF16), jax.ShapeDtypeStruct((Lp, GLA_HEADS * GLA_DK), BF16),
                   jax.ShapeDtypeStruct((Lp, GLA_HEADS * GLA_DV), BF16), jax.ShapeDtypeStruct((Lp, GLA_HEADS * GLA_DK), F32)],
        scratch_shapes=[pltpu.VMEM((GLA_DK, GLA_DV), F32)],
        compiler_params=_params(("parallel", "arbitrary")),
    )(main, main, main, logg, states, do)


def _loss_head(h, g, target, Lp):
    t = BLOCK
    D = D_MODEL

    def body(h_ref, g_ref, t_ref, loss_ref, dh_ref, dg_ref):
        i = pl.program_id(0)
        x = h_ref[...]
        tok = (i * t + lax.broadcasted_iota(jnp.int32, (t, 1), 0)) >= BLOCK
        r = lax.rsqrt(jnp.mean(x * x, axis=-1, keepdims=True) + EPS)
        nrm = x * r
        e = jnp.where(tok, nrm * g_ref[...] - t_ref[...], 0.0)
        part = 0.5 * jnp.sum(jnp.sum(e * e, axis=1, keepdims=True), axis=0, keepdims=True) / D
        dy = e / D
        dn = dy * g_ref[...]
        dh_ref[...] = r * (dn - nrm * jnp.mean(dn * nrm, axis=-1, keepdims=True))
        _acc_out(dg_ref, i, jnp.sum(dy * nrm, axis=0, keepdims=True))
        _acc_out(loss_ref, i, jnp.broadcast_to(part, (1, LANES)))

    return pl.pallas_call(
        body, name="loss_head", grid=(Lp // t,),
        in_specs=[pl.BlockSpec((t, D), lambda i: (i, 0)), pl.BlockSpec((1, D), lambda i: (0, 0)),
                  pl.BlockSpec((t, D), lambda i: (jnp.maximum(i - 1, 0), 0))],
        out_specs=[pl.BlockSpec((1, LANES), lambda i: (0, 0)), pl.BlockSpec((t, D), lambda i: (i, 0)),
                   pl.BlockSpec((1, D), lambda i: (0, 0))],
        out_shape=[jax.ShapeDtypeStruct((1, LANES), F32), jax.ShapeDtypeStruct((Lp, D), F32),
                   jax.ShapeDtypeStruct((1, D), F32)],
        compiler_params=_params(("arbitrary",)),
    )(h, g, target)


def _adamw(w, g, m, v, name):
    shape = w.shape
    cols = shape[-1]
    rows = w.size // cols
    w2, g2, m2, v2 = (a.reshape(rows, cols) for a in (w, g, m, v))
    budget_rows = max(8, ADAM_BLOCK_BYTES // (4 * cols))
    tr = rows if rows <= budget_rows else _pick(rows, tuple(t for t in (512, 256, 128, 64, 32, 16, 8) if t <= budget_rows))

    def body(w_ref, g_ref, m_ref, v_ref, d_ref, nm_ref, nv_ref):
        gg = g_ref[...]
        mm = ADAM_B1 * m_ref[...] + (1.0 - ADAM_B1) * gg
        vv = ADAM_B2 * v_ref[...] + (1.0 - ADAM_B2) * jnp.square(gg)
        m_hat = mm / (1.0 - ADAM_B1 ** ADAM_STEP)
        v_hat = vv / (1.0 - ADAM_B2 ** ADAM_STEP)
        d_ref[...] = -ADAM_LR * (m_hat / (jnp.sqrt(v_hat) + ADAM_EPS) + ADAM_WD * w_ref[...])
        nm_ref[...] = mm
        nv_ref[...] = vv

    spec = pl.BlockSpec((tr, cols), lambda i: (i, 0))
    d, nm, nv = pl.pallas_call(
        body, name=name, grid=(rows // tr,), in_specs=[spec] * 4, out_specs=[spec] * 3,
        out_shape=[jax.ShapeDtypeStruct((rows, cols), F32)] * 3,
        compiler_params=_params(("parallel",)),
    )(w2, g2, m2, v2)
    return d.reshape(shape), nm.reshape(shape), nv.reshape(shape)


def _place():
    x, y, c = lax.axis_index("x"), lax.axis_index("y"), lax.axis_index("c")
    chips = [(1 - x, y), (x, 1 - y), (1 - x, 1 - y)]
    return x, y, c, chips


def _rcopy(src, dst, send_sems, recv_sems, k, to):
    return pltpu.make_async_remote_copy(src_ref=src, dst_ref=dst, send_sem=send_sems.at[k], recv_sem=recv_sems.at[k],
                                        device_id=to, device_id_type=MESH)


def _any_spec():
    return pl.BlockSpec(memory_space=pl.ANY)


def _gather_weights(mine):
    _, R, C = mine.shape

    def body(m_ref, o_ref, send_sems, recv_sems, local_sem):
        x, y, c, chips = _place()
        me = 2 * x + y
        own = pltpu.make_async_copy(m_ref, o_ref.at[me], local_sem)
        own.start()
        first = [_rcopy(m_ref.at[c], o_ref.at[me, c], send_sems, recv_sems, j, (*chip, c))
                 for j, chip in enumerate(chips)]
        for cp in first:
            cp.start()
        passed = []
        for j, (px, py) in enumerate(chips):
            blk = o_ref.at[2 * px + py, c]
            _rcopy(blk, blk, send_sems, recv_sems, j, (x, y, c)).wait_recv()
            fwd = _rcopy(blk, blk, send_sems, recv_sems, 3 + j, (x, y, 1 - c))
            fwd.start()
            passed.append(fwd)
        for j, (px, py) in enumerate(chips):
            blk = o_ref.at[2 * px + py, 1 - c]
            _rcopy(blk, blk, send_sems, recv_sems, 3 + j, (x, y, c)).wait_recv()
        for cp in first + passed:
            cp.wait_send()
        own.wait()

    return pl.pallas_call(
        body, name="gather_weights", in_specs=[_any_spec()], out_specs=_any_spec(),
        out_shape=jax.ShapeDtypeStruct((4, 2, R, C), mine.dtype),
        scratch_shapes=[pltpu.SemaphoreType.DMA((6,)), pltpu.SemaphoreType.DMA((6,)), pltpu.SemaphoreType.DMA],
    )(mine)


def _swap_halves(g):
    _, _, R, C = g.shape

    def body(g_ref, o_ref, send_sems, recv_sems):
        x, y, c, _ = _place()
        cps = [_rcopy(g_ref.at[t, 1 - c], o_ref.at[t], send_sems, recv_sems, t, (x, y, 1 - c)) for t in range(4)]
        for cp in cps:
            cp.start()
        for cp in cps:
            cp.wait()

    return pl.pallas_call(
        body, name="rs_swap_halves", in_specs=[_any_spec()], out_specs=_any_spec(),
        out_shape=jax.ShapeDtypeStruct((4, R, C), g.dtype),
        scratch_shapes=[pltpu.SemaphoreType.DMA((4,)), pltpu.SemaphoreType.DMA((4,))],
    )(g)


def _scatter_chips(hsum):
    _, R, C = hsum.shape

    def body(h_ref, o_ref, send_sems, recv_sems, local_sem):
        x, y, c, chips = _place()
        me = 2 * x + y
        own = pltpu.make_async_copy(h_ref.at[me], o_ref.at[me], local_sem)
        own.start()
        cps = [_rcopy(h_ref.at[2 * px + py], o_ref.at[me], send_sems, recv_sems, j, (px, py, c))
               for j, (px, py) in enumerate(chips)]
        for cp in cps:
            cp.start()
        for j, (px, py) in enumerate(chips):
            blk = o_ref.at[2 * px + py]
            _rcopy(blk, blk, send_sems, recv_sems, j, (x, y, c)).wait_recv()
        for cp in cps:
            cp.wait_send()
        own.wait()

    return pl.pallas_call(
        body, name="rs_scatter_chips", in_specs=[_any_spec()], out_specs=_any_spec(),
        out_shape=jax.ShapeDtypeStruct((4, R, C), hsum.dtype),
        scratch_shapes=[pltpu.SemaphoreType.DMA((3,)), pltpu.SemaphoreType.DMA((3,)), pltpu.SemaphoreType.DMA],
    )(hsum)


def _join_halves(f):
    R, C = f.shape

    def body(f_ref, o_ref, send_sems, recv_sems, local_sem):
        x, y, c, _ = _place()
        own = pltpu.make_async_copy(f_ref, o_ref.at[c], local_sem)
        own.start()
        cp = _rcopy(f_ref, o_ref.at[c], send_sems, recv_sems, 0, (x, y, 1 - c))
        cp.start()
        blk = o_ref.at[1 - c]
        _rcopy(blk, blk, send_sems, recv_sems, 0, (x, y, c)).wait_recv()
        cp.wait_send()
        own.wait()

    return pl.pallas_call(
        body, name="rs_join_halves", in_specs=[_any_spec()], out_specs=_any_spec(),
        out_shape=jax.ShapeDtypeStruct((2, R, C), f.dtype),
        scratch_shapes=[pltpu.SemaphoreType.DMA((1,)), pltpu.SemaphoreType.DMA((1,)), pltpu.SemaphoreType.DMA],
    )(f)


def _add_halves(g, other):
    _, _, R, C = g.shape
    tr = _pick(R, (512, 256, 128, 64, 32, 16, 8))
    c = lax.axis_index("c")

    def body(c_ref, g_ref, o_ref, out_ref):
        out_ref[...] = g_ref[...] + o_ref[...]

    return pl.pallas_call(
        body, name="rs_add_halves",
        grid_spec=pltpu.PrefetchScalarGridSpec(
            num_scalar_prefetch=1, grid=(4, R // tr),
            in_specs=[pl.BlockSpec((None, None, tr, C), lambda t, i, cr: (t, cr[0], i, 0)),
                      pl.BlockSpec((None, tr, C), lambda t, i, cr: (t, i, 0))],
            out_specs=pl.BlockSpec((None, tr, C), lambda t, i, cr: (t, i, 0))),
        out_shape=jax.ShapeDtypeStruct((4, R, C), F32),
        compiler_params=_params(("parallel", "parallel")),
    )(jnp.reshape(c, (1,)).astype(jnp.int32), g, other)


def _sum_chips(q):
    _, R, C = q.shape
    tr = _pick(R, (512, 256, 128, 64, 32, 16, 8))

    def body(q_ref, out_ref):
        out_ref[...] = ((q_ref[0] + q_ref[1]) + q_ref[2]) + q_ref[3]

    return pl.pallas_call(
        body, name="rs_sum_chips", grid=(R // tr,),
        in_specs=[pl.BlockSpec((4, tr, C), lambda i: (0, i, 0))],
        out_specs=pl.BlockSpec((tr, C), lambda i: (i, 0)),
        out_shape=jax.ShapeDtypeStruct((R, C), F32),
        compiler_params=_params(("parallel",)),
    )(q)


def _allreduce_small(v):
    R, C = v.shape

    def body(v_ref, sum_ref, all_ref, send_sems, recv_sems):
        x, y, c, _ = _place()
        me = 4 * x + 2 * y + c
        rows = lambda d: all_ref.at[pl.ds(pl.multiple_of(d * R, 8), R), :]

        def peer(k):
            flip = lambda bit, v: (1 - v) if ((k + 1) >> bit) & 1 else v
            return flip(2, x), flip(1, y), flip(0, c)

        outs = [_rcopy(v_ref, rows(me), send_sems, recv_sems, k, peer(k)) for k in range(7)]
        for cp in outs:
            cp.start()
        all_ref[pl.ds(pl.multiple_of(me * R, 8), R), :] = v_ref[...]
        for k in range(7):
            px, py, pc = peer(k)
            blk = rows(4 * px + 2 * py + pc)
            _rcopy(blk, blk, send_sems, recv_sems, k, (x, y, c)).wait_recv()
        for cp in outs:
            cp.wait_send()
        tot = all_ref[0:R, :]
        for d in range(1, 8):
            tot = tot + all_ref[d * R:(d + 1) * R, :]
        sum_ref[...] = tot

    vm = pl.BlockSpec(memory_space=pltpu.VMEM)
    return pl.pallas_call(
        body, name="allreduce_small", in_specs=[vm], out_specs=[vm, vm],
        out_shape=[jax.ShapeDtypeStruct((R, C), F32), jax.ShapeDtypeStruct((8 * R, C), F32)],
        scratch_shapes=[pltpu.SemaphoreType.DMA((7,)), pltpu.SemaphoreType.DMA((7,))],
    )(v)[0]


def _shard_shape(shape, axis):
    s = list(shape)
    s[axis] //= 4
    return tuple(s)


def _size(shape):
    n = 1
    for d in shape:
        n *= d
    return n


def _pack(pieces, dtype):
    flat = jnp.concatenate([p.astype(dtype).reshape(-1) for p in pieces])
    rows = -(-flat.shape[0] // (PACK_COLS * 16)) * 16
    return jnp.pad(flat, (0, rows * PACK_COLS - flat.shape[0])).reshape(rows, PACK_COLS)


def _unpack(buf, shapes):
    flat = buf.reshape(-1)
    out, pos = [], 0
    for s in shapes:
        n = _size(s)
        out.append(flat[pos:pos + n].reshape(s))
        pos += n
    return out


def _layer_piece(name, arr, l):
    if name == "meta_tokens":
        return arr[l * (N_META // 2):(l + 1) * (N_META // 2)]
    return arr[l]


def _prep_layer(w):
    w_in = w["w_in"]
    col = lambda a, n: w_in[:, _R[a]:_R[a] + n]
    main = jnp.concatenate([col("qa", 3072), col("scb", 3072), col("qc", 3072), col("ga", 6144)], axis=1).astype(BF16)
    zpad = lambda n: jnp.zeros((D_MODEL, n), w_in.dtype)
    side = jnp.concatenate([col("fa", 8), zpad(LANES - 8), col("glr", GLA_RANK), zpad(LANES - GLA_RANK)],
                           axis=1).astype(BF16)
    pad8 = lambda a: jnp.pad(a.astype(F32), ((0, 8 - a.shape[0]), (0, 0)))
    return dict(
        main=main, side=side,
        w_g2=jnp.pad(w["gla_w_g2"].astype(F32), ((0, LANES - GLA_RANK), (0, 0))),
        conv_w=pad8(w["conv_w"]), mlp_conv_w=pad8(w["mlp_conv_w"]),
        w_a_o=w["w_a_o"].astype(BF16), w_b_o=w["w_b_o"].astype(BF16), w_c_o=w["w_c_o"].astype(BF16),
        w_o=w["w_o"].astype(BF16), w_up=w["w_up"].astype(BF16), w_down=w["w_down"].astype(BF16))


def _row2(v):
    return v.reshape(1, -1).astype(F32)


def _layer_fwd(h, p, rep, l, Lp, tm, ta):
    tag = lambda s: f"{s}_l{l}"
    g1, g2 = _row2(rep["norm1_g"][l]), _row2(rep["norm2_g"][l])
    bf = jnp.pad(_row2(rep["fox_b_f"][l]), ((0, 0), (0, LANES - FOX_HEADS)))
    gate_b, b_g, gnorm = _row2(rep["gate_b"][l]), _row2(rep["gla_b_g"][l]), _row2(rep["gla_norm_g"][l])
    (xn,) = _rw_fwd(tag("rms1_fwd"), _f_rms, [Row(h, D_MODEL)], [Const(g1)], [(D_MODEL, BF16)], Lp, BLOCK)
    main = _mm(xn, p["main"], "nn", BF16, tag("proj_main"))
    side = _mm(xn, p["side"], "nn", F32, tag("proj_side"))
    c = _fox_gate_fwd(side, bf, Lp)
    c_t = c[:, :FOX_HEADS].T
    c_col, c_row = c_t[:, :, None], c_t[:, None, :]
    oa, lse = _fox_fwd(main, c_col, c_row, Lp, ta)
    ya = _mm(oa, p["w_a_o"], "nn", BF16, tag("ya"))
    ub = _sconv_fwd(main, p["conv_w"], Lp, tm)
    yb = _mm(ub, p["w_b_o"], "nn", BF16, tag("yb"))
    glr = Row(side, LANES, lambda g: 1)
    (logg,) = _rw_fwd(tag("logg_fwd"), _f_logg, [glr], [Const(p["w_g2"]), Const(b_g)], [(512, F32)], Lp, tm)
    oc, states = _gla_fwd(main, logg, Lp)
    rc = Row(main, GLA_DV, lambda g: RC // GLA_DV + g)
    gn = Const(gnorm, (1, GLA_DV), lambda g: (0, g))
    (uc,) = _rw_fwd(tag("gla_post_fwd"), _f_gla_post, [Row(oc, GLA_DV), rc], [gn], [(GLA_DV, BF16)], Lp, tm,
                    G=GLA_HEADS)
    yc = _mm(uc, p["w_c_o"], "nn", BF16, tag("yc"))
    cw = 512
    G = D_MODEL // cw
    mrows = [Row(ya, cw), Row(yb, cw), Row(yc, cw), Row(main, cw, lambda g: GA // cw + g),
             Row(main, cw, lambda g: GB // cw + g), Row(main, cw, lambda g: GC // cw + g)]
    mconsts = [Const(gate_b, (1, cw), lambda g, k=k: (0, k * G + g)) for k in range(3)]
    (mix,) = _rw_fwd(tag("merge_fwd"), _f_merge, mrows, mconsts, [(cw, BF16)], Lp, tm, G=G)
    h1 = _mm(mix, p["w_o"], "nn", F32, tag("h1"), add=h)
    (xn2,) = _rw_fwd(tag("rms2_fwd"), _f_rms, [Row(h1, D_MODEL)], [Const(g2)], [(D_MODEL, BF16)], Lp, BLOCK)
    up = _mm(xn2, p["w_up"], "nn", BF16, tag("up"))
    act = _mlp_act_fwd(up, p["mlp_conv_w"], Lp, tm)
    h2 = _mm(act, p["w_down"], "nn", F32, tag("h2"), add=h1)
    res = dict(h=h, xn=xn, main=main, side=side, c_col=c_col, c_row=c_row, oa=oa, lse=lse, ya=ya, ub=ub, yb=yb,
               logg=logg, oc=oc, states=states, uc=uc, yc=yc, mix=mix, h1=h1, xn2=xn2, up=up, act=act,
               g1=g1, g2=g2, bf=bf, gate_b=gate_b, b_g=b_g, gnorm=gnorm)
    return h2, res


def _layer_bwd(dh2, p, r, l, Lp, tm, ta):
    tag = lambda s: f"{s}_l{l}"
    g = {}
    g["w_down"] = _mm(r["act"], dh2, "tn", F32, tag("d_w_down"))
    dact = _mm(dh2, p["w_down"], "nt", BF16, tag("d_act"))
    dgate, dval, dwg, dwu = _mlp_act_bwd(r["up"], p["mlp_conv_w"], dact, Lp, tm)
    g["mlp_conv_w"] = jnp.concatenate([dwg[:3], dwu[:3]], axis=1)
    dup = jnp.concatenate([dgate, dval], axis=1)
    g["w_up"] = _mm(r["xn2"], dup, "tn", F32, tag("d_w_up"))
    dxn2 = _mm(dup, p["w_up"], "nt", F32, tag("d_xn2"))
    (dh1,), (dg2,) = _rw_bwd(tag("rms2_bwd"), _f_rms, [Row(r["h1"], D_MODEL)], [Const(r["g2"])],
                             [Row(dxn2, D_MODEL)], [F32], [dh2], Lp, BLOCK)
    g["norm2_g"] = dg2[0]
    g["w_o"] = _mm(r["mix"], dh1, "tn", F32, tag("d_w_o"))
    dmix = _mm(dh1, p["w_o"], "nt", BF16, tag("d_mix"))
    cw = 512
    G = D_MODEL // cw
    main = r["main"]
    mrows = [Row(r["ya"], cw), Row(r["yb"], cw), Row(r["yc"], cw), Row(main, cw, lambda g_: GA // cw + g_),
             Row(main, cw, lambda g_: GB // cw + g_), Row(main, cw, lambda g_: GC // cw + g_)]
    mconsts = [Const(r["gate_b"], (1, cw), lambda g_, k=k: (0, k * G + g_)) for k in range(3)]
    (dya, dyb, dyc, dga, dgb, dgc), dbs = _rw_bwd(tag("merge_bwd"), _f_merge, mrows, mconsts, [Row(dmix, cw)],
                                                  [BF16] * 6, [None] * 6, Lp, tm, G=G)
    g["gate_b"] = jnp.concatenate([dbs[k][0, k * D_MODEL:(k + 1) * D_MODEL] for k in range(3)])
    g["w_a_o"] = _mm(r["oa"], dya, "tn", F32, tag("d_w_a_o"))
    doa = _mm(dya, p["w_a_o"], "nt", BF16, tag("d_oa"))
    g["w_b_o"] = _mm(r["ub"], dyb, "tn", F32, tag("d_w_b_o"))
    dub = _mm(dyb, p["w_b_o"], "nt", BF16, tag("d_ub"))
    g["w_c_o"] = _mm(r["uc"], dyc, "tn", F32, tag("d_w_c_o"))
    duc = _mm(dyc, p["w_c_o"], "nt", BF16, tag("d_uc"))
    dq, delta = _fox_bwd_dq(main, r["c_col"], r["c_row"], r["lse"], doa, Lp, ta)
    dk, dv, dck = _fox_bwd_dkv(main, r["c_col"], r["c_row"], r["lse"], delta, doa, Lp, ta)
    dc = jnp.pad(dck[:, 0, :].T, ((0, 0), (0, LANES - FOX_HEADS)))
    dfa, dbf = _fox_gate_bwd(r["side"], r["bf"], dc, Lp)
    g["fox_b_f"] = dbf[0, :FOX_HEADS]
    dscb, dscc, dsch, dcw = _sconv_bwd(main, p["conv_w"], dub, Lp, tm)
    g["conv_w"] = dcw[:3]
    rc = Row(main, GLA_DV, lambda g_: RC // GLA_DV + g_)
    gn = Const(r["gnorm"], (1, GLA_DV), lambda g_: (0, g_))
    (doc, drc), (dgn,) = _rw_bwd(tag("gla_post_bwd"), _f_gla_post, [Row(r["oc"], GLA_DV), rc], [gn],
                                 [Row(duc, GLA_DV)], [F32, BF16], [None, None], Lp, tm, G=GLA_HEADS)
    g["gla_norm_g"] = dgn[0]
    dqc, dkc, dvc, dlogg = _gla_bwd(main, r["logg"], r["states"], doc, Lp)
    glr = Row(r["side"], LANES, lambda g_: 1)
    (dglr,), (dwg2, dbg) = _rw_bwd(tag("logg_bwd"), _f_logg, [glr], [Const(p["w_g2"]), Const(r["b_g"])],
                                   [Row(dlogg, 512)], [F32], [None], Lp, tm)
    g["gla_w_g2"] = dwg2[:GLA_RANK]
    g["gla_b_g"] = dbg[0]
    dmain = jnp.concatenate([dq, dk, dv, dscb, dscc, dsch, dqc, dkc, dvc, drc, dga, dgb, dgc], axis=1)
    dside = jnp.concatenate([dfa, dglr], axis=1)
    dwm = _mm(r["xn"], dmain, "tn", F32, tag("d_w_main"))
    dws = _mm(r["xn"], dside, "tn", F32, tag("d_w_side"))
    g["w_in"] = jnp.concatenate([dwm[:, :3072], dws[:, :8], dwm[:, 3072:9216], dws[:, LANES:LANES + GLA_RANK],
                                 dwm[:, 9216:]], axis=1)
    dxn = _mm(dmain, p["main"], "nt", F32, tag("d_xn_main"))
    dxn = _mm(dside, p["side"], "nt", F32, tag("d_xn_side"), add=dxn)
    (dh,), (dg1,) = _rw_bwd(tag("rms1_bwd"), _f_rms, [Row(r["h"], D_MODEL)], [Const(r["g1"])], [Row(dxn, D_MODEL)],
                            [F32], [dh1], Lp, BLOCK)
    g["norm1_g"] = dg1[0]
    return dh, g


def _local_step(x, target, meta, layers, rep):
    seq = x.shape[0]
    Lp = PAD + N_META + seq
    tm = _pick(Lp, (640, 384, 128))
    ta = tm
    h = jnp.concatenate([jnp.zeros((PAD, D_MODEL), F32), meta.astype(F32), x], axis=0)
    prepped = [_prep_layer(w) for w in layers]
    saved = []
    for l in range(DEPTH):
        h, res = _layer_fwd(h, prepped[l], rep, l, Lp, tm, ta)
        saved.append(res)
    loss, dh, dgf = _loss_head(h, _row2(rep["final_norm_g"]), target, Lp)
    grads = [None] * DEPTH
    for l in reversed(range(DEPTH)):
        dh, grads[l] = _layer_bwd(dh, prepped[l], saved[l], l, Lp, tm, ta)
    return loss[0, 0], dh[BLOCK:], dh[PAD:BLOCK], grads, dgf[0]


def kernel(x, meta_tokens, norm1_g, w_in, fox_b_f, gate_b, conv_w, gla_w_g2, gla_b_g, gla_norm_g, w_a_o, w_b_o, w_c_o, w_o, norm2_g, w_up, mlp_conv_w, w_down, final_norm_g, loss_target, m_meta_tokens, m_norm1_g, m_w_in, m_fox_b_f, m_gate_b, m_conv_w, m_gla_w_g2, m_gla_b_g, m_gla_norm_g, m_w_a_o, m_w_b_o, m_w_c_o, m_w_o, m_norm2_g, m_w_up, m_mlp_conv_w, m_w_down, m_final_norm_g, v_meta_tokens, v_norm1_g, v_w_in, v_fox_b_f, v_gate_b, v_conv_w, v_gla_w_g2, v_gla_b_g, v_gla_norm_g, v_w_a_o, v_w_b_o, v_w_c_o, v_w_o, v_norm2_g, v_w_up, v_mlp_conv_w, v_w_down, v_final_norm_g):
    given = dict(locals())
    weights = {n: given[n] for n in WEIGHT_ORDER}
    rep = {n: weights[n] for n, _ in REPLICATED}

    shard_shapes = [_shard_shape(shape, axis) for _, shape, axis in SHARDED]
    exact = [k for k, (n, _, _) in enumerate(SHARDED) if n in GATHER_F32]

    def wire_pieces(l):
        ws = [_layer_piece(n, weights[n], l) for n, _, _ in SHARDED]
        his = [w.astype(BF16) for w in ws]
        return his + [(ws[k] - his[k].astype(F32)).astype(BF16) for k in exact]

    mine = jnp.stack([_pack(wire_pieces(l), BF16) for l in range(DEPTH)])
    gathered = _gather_weights(mine)
    full = [dict() for _ in range(DEPTH)]
    for l in range(DEPTH):
        per_chip = [_unpack(gathered[t, l], shard_shapes + [shard_shapes[k] for k in exact]) for t in range(4)]
        for k, (n, _, axis) in enumerate(SHARDED):
            full[l][n] = jnp.concatenate([per_chip[t][k] for t in range(4)], axis=axis)
        for e, k in enumerate(exact):
            lo = jnp.concatenate([per_chip[t][len(SHARDED) + e] for t in range(4)], axis=SHARDED[k][2])
            full[l][SHARDED[k][0]] = full[l][SHARDED[k][0]].astype(F32) + lo.astype(F32)
    meta_full = jnp.concatenate([full[l].pop("meta_tokens") for l in range(DEPTH)], axis=0)

    loss, grad_x, grad_meta, grads, d_final = _local_step(x[0], loss_target[0], meta_full, full, rep)
    loss = lax.psum(loss, ("x", "y", "c"))

    for l in range(DEPTH):
        grads[l]["meta_tokens"] = grad_meta[l * (N_META // 2):(l + 1) * (N_META // 2)]

    def shard_of(a, axis, t):
        n = a.shape[axis] // 4
        return lax.slice_in_dim(a, t * n, (t + 1) * n, axis=axis)

    packed = jnp.stack([jnp.stack([_pack([shard_of(grads[l][n], axis, t) for n, _, axis in SHARDED], F32)
                                   for l in range(DEPTH)]) for t in range(4)])
    chip_sum = _add_halves(packed, _swap_halves(packed))
    half = _sum_chips(_scatter_chips(chip_sum))
    summed = _join_halves(half)
    gshard = {}
    for l in range(DEPTH):
        for k, piece in enumerate(_unpack(summed[l], shard_shapes)):
            gshard.setdefault(SHARDED[k][0], []).append(piece)
    gout = {n: (jnp.concatenate(ps, axis=0) if n == "meta_tokens" else jnp.stack(ps)) for n, ps in gshard.items()}

    rep_g = {n: (d_final if n == "final_norm_g" else jnp.stack([grads[l][n] for l in range(DEPTH)])) for n, _ in REPLICATED}
    flat = jnp.concatenate([rep_g[n].astype(F32).reshape(-1) for n, _ in REPLICATED])
    rrows = -(-flat.shape[0] // (PACK_COLS * 8)) * 8
    small = jnp.pad(flat, (0, rrows * PACK_COLS - flat.shape[0])).reshape(rrows, PACK_COLS)
    small = _allreduce_small(small).reshape(-1)
    pos = 0
    for n, shape in REPLICATED:
        gout[n] = small[pos:pos + _size(shape)].reshape(shape)
        pos += _size(shape)

    deltas, new_m, new_v = {}, {}, {}
    for n in WEIGHT_ORDER:
        deltas[n], new_m[n], new_v[n] = _adamw(weights[n], gout[n], given["m_" + n], given["v_" + n], "adamw_" + n)
    return (loss, grad_x[None], *[gout[n] for n in WEIGHT_ORDER], *[deltas[n] for n in WEIGHT_ORDER],
            *[new_m[n] for n in WEIGHT_ORDER], *[new_v[n] for n in WEIGHT_ORDER])
```

```python
import functools

import jax
import jax.numpy as jnp
from jax import lax
from jax.experimental import pallas as pl
from jax.experimental.pallas import tpu as pltpu

F32, BF16 = jnp.float32, jnp.bfloat16
HIGHEST = lax.Precision.HIGHEST
MESH = pl.DeviceIdType.MESH

N_META = 16
BLOCK = 128
LANES = 128
PAD = BLOCK - N_META
EPS = 1e-6
NEG = -1e30
HALO = 16
VMEM_LIMIT = 56 * 1024 * 1024
ADAM_BLOCK_BYTES = 1 << 20
EW_BLOCK_BYTES = 3 << 19

D_MODEL = 2048
FOX_HEADS, FOX_HD = 8, 128
FOX_WIDTH = FOX_HEADS * FOX_HD
CONV_CH = 1024
GLA_HEADS, GLA_DK, GLA_DV, GLA_RANK, GLA_TAU = 4, 128, 256, 16, 16.0
GLA_SUB = 32
D_FF = 5632
N_IN = 15384
DEPTH = 2

_R = dict(qa=0, ka=1024, va=2048, fa=3072, scb=3080, scc=4104, sch=5128, qc=6152, kc=6664,
          vc=7176, rc=8200, glr=9224, ga=9240, gb=11288, gc=13336)
QA, KA, VA, SCB, SCC, SCH, QC, KC, VC, RC, GA, GB, GC = (
    0, 1024, 2048, 3072, 4096, 5120, 6144, 6656, 7168, 8192, 9216, 11264, 13312)
N_MAIN = 15360
N_SIDE = 256

ADAM_LR, ADAM_B1, ADAM_B2, ADAM_EPS, ADAM_WD, ADAM_STEP = 0.001, 0.9, 0.999, 1e-08, 0.01, 10

BIG = (("w_in", "stack"), ("w_a_o", "cols"), ("w_b_o", "cols"), ("w_c_o", "cols"), ("w_o", "rows"), ("w_up", "cols"),
       ("w_down", "rows"))
SMALL = (("conv_w", (3, CONV_CH)), ("mlp_conv_w", (3, 2 * D_FF)), ("gla_w_g2", (GLA_RANK, 512)),
         ("meta_tokens", (N_META // DEPTH, D_MODEL)))
REPLICATED = (("norm1_g", (2, D_MODEL)), ("fox_b_f", (2, 8)), ("gate_b", (2, 3 * D_MODEL)), ("gla_b_g", (2, 512)),
              ("gla_norm_g", (2, 1024)), ("norm2_g", (2, D_MODEL)), ("final_norm_g", (D_MODEL,)))
WEIGHT_ORDER = ("meta_tokens", "norm1_g", "w_in", "fox_b_f", "gate_b", "conv_w", "gla_w_g2", "gla_b_g",
                "gla_norm_g", "w_a_o", "w_b_o", "w_c_o", "w_o", "norm2_g", "w_up", "mlp_conv_w", "w_down",
                "final_norm_g")
PACK_COLS = 1024
GATHER_F32 = ("conv_w", "mlp_conv_w", "meta_tokens")


def _pick(n, cands):
    for c in cands:
        if n % c == 0:
            return c
    return n


def _params(sem):
    return pltpu.CompilerParams(dimension_semantics=sem, vmem_limit_bytes=VMEM_LIMIT)


def _sigmoid(x):
    return jax.nn.sigmoid(x)


def _log_sigmoid(x):
    return jnp.minimum(x, 0.0) - jnp.log(1.0 + jnp.exp(-jnp.abs(x)))


def _mm(a, b, mode, out_dtype, name, add=None, b_lead=None, slot=None):
    bshape = b.shape if b_lead is None else b.shape[1:]
    if mode == "nn":
        (M, K), (K2, N) = a.shape, bshape
    elif mode == "nt":
        (M, K), (N, K2) = a.shape, bshape
    else:
        (K, M), (K2, N) = a.shape, bshape
    assert K == K2, (name, a.shape, b.shape)
    if mode == "tn":
        tm = _pick(M, (2048, 1408, 1024, 512, 256, 128))
        tn = _pick(N, (512, 256, 128))
        tk = _pick(K, (640, 512, 384, 256, 128))
    else:
        tm = _pick(M, (1664, 640, 384, 128))
        tn = _pick(N, (512, 256, 128))
        tk = K if K <= 2048 else _pick(K, (1408, 1024, 512, 256, 128))
    nk = K // tk
    dims = {"nn": (((1,), (0,)), ((), ())), "nt": (((1,), (1,)), ((), ())), "tn": (((0,), (0,)), ((), ()))}[mode]
    n_in = 2 + (add is not None) + (slot is not None and slot[0] is not None)

    def body(*refs):
        a_ref, b_ref = refs[:2]
        add_ref = refs[2] if add is not None else None
        o_ref, acc = refs[n_in:]
        k = pl.program_id(2)

        @pl.when(k == 0)
        def _():
            acc[...] = jnp.zeros_like(acc)

        acc[...] += lax.dot_general(a_ref[...].astype(BF16), b_ref[...].astype(BF16), dims,
                                    preferred_element_type=F32)

        @pl.when(k == nk - 1)
        def _():
            r = acc[...]
            if add is not None:
                r = r + add_ref[...].astype(F32)
            o_ref[...] = r.astype(o_ref.dtype)

    a_spec = {"nn": pl.BlockSpec((tm, tk), lambda i, j, k: (i, k)),
              "nt": pl.BlockSpec((tm, tk), lambda i, j, k: (i, k)),
              "tn": pl.BlockSpec((tk, tm), lambda i, j, k: (k, i))}[mode]
    b_blk, b_idx = {"nn": ((tk, tn), lambda i, j, k: (k, j)),
                    "nt": ((tn, tk), lambda i, j, k: (j, k)),
                    "tn": ((tk, tn), lambda i, j, k: (k, j))}[mode]
    if b_lead is None:
        b_spec = pl.BlockSpec(b_blk, b_idx)
    else:
        b_spec = pl.BlockSpec((None,) + b_blk, lambda i, j, k: (b_lead,) + b_idx(i, j, k))
    o_spec = pl.BlockSpec((tm, tn), lambda i, j, k: (i, j))
    ins, specs = [a, b], [a_spec, b_spec]
    if add is not None:
        ins.append(add)
        specs.append(o_spec)
    aliases = {}
    out_shape = jax.ShapeDtypeStruct((M, N), out_dtype)
    if slot is not None:
        buf, l = slot
        o_spec = pl.BlockSpec((None, tm, tn), lambda i, j, k: (l, i, j))
        out_shape = jax.ShapeDtypeStruct((DEPTH, M, N), out_dtype)
        if buf is not None:
            aliases = {len(ins): 0}
            ins.append(buf)
            specs.append(pl.BlockSpec(memory_space=pl.ANY))
    return pl.pallas_call(
        body, name=name, grid=(M // tm, N // tn, nk), in_specs=specs, out_specs=o_spec, out_shape=out_shape,
        scratch_shapes=[pltpu.VMEM((tm, tn), F32)], input_output_aliases=aliases,
        compiler_params=_params(("parallel", "parallel", "arbitrary")),
    )(*ins)


class Row:
    def __init__(self, arr, w, cb=None):
        self.arr, self.w, self.cb = arr, w, (cb if cb is not None else (lambda g: g))


class Const:
    def __init__(self, arr, shape=None, idx=None):
        self.arr = arr
        self.shape = shape if shape is not None else arr.shape
        self.idx = idx if idx is not None else (lambda g: (0,) * arr.ndim)


def _row_spec(r, tm):
    return pl.BlockSpec((tm, r.w), lambda g, i, r=r: (i, r.cb(g)))


def _const_spec(c):
    return pl.BlockSpec(c.shape, lambda g, i, c=c: c.idx(g))


def _valid_rows(i, tm):
    return (i * tm + lax.broadcasted_iota(jnp.int32, (tm, 1), 0)) >= PAD


def _rw_fwd(name, f, rows, consts, outs, Lp, tm, G=1):
    nr, nc = len(rows), len(consts)

    def body(*refs):
        i = pl.program_id(1)
        rv = [r[...].astype(F32) for r in refs[:nr]]
        cv = [r[...].astype(F32) for r in refs[nr:nr + nc]]
        res = f(_valid_rows(i, tm), *rv, *cv)
        for o_ref, v in zip(refs[nr + nc:], res):
            o_ref[...] = v.astype(o_ref.dtype)

    return pl.pallas_call(
        body, name=name, grid=(G, Lp // tm),
        in_specs=[_row_spec(r, tm) for r in rows] + [_const_spec(c) for c in consts],
        out_specs=[pl.BlockSpec((tm, w), lambda g, i: (i, g)) for w, _ in outs],
        out_shape=[jax.ShapeDtypeStruct((Lp, w * G), dt) for w, dt in outs],
        compiler_params=_params(("parallel", "arbitrary")),
    )(*[r.arr for r in rows], *[c.arr for c in consts])


def _rw_bwd(name, f, rows, consts, cts, drow_dtypes, adds, Lp, tm, G=1):
    nr, nc, nt = len(rows), len(consts), len(cts)
    want = [k for k, dt in enumerate(drow_dtypes) if dt is not None]
    add_k = [k for k in want if adds[k] is not None]

    def body(*refs):
        i = pl.program_id(1)
        pos = 0
        rv = [r[...].astype(F32) for r in refs[pos:pos + nr]]
        pos += nr
        cv = [r[...].astype(F32) for r in refs[pos:pos + nc]]
        pos += nc
        tv = [r[...].astype(F32) for r in refs[pos:pos + nt]]
        pos += nt
        av = {k: refs[pos + n][...].astype(F32) for n, k in enumerate(add_k)}
        pos += len(add_k)
        drow_refs = refs[pos:pos + len(want)]
        pos += len(want)
        dconst_refs = refs[pos:pos + nc]
        valid = _valid_rows(i, tm)
        _, vjp = jax.vjp(lambda *a: tuple(f(valid, *a)), *rv, *cv)
        grads = vjp(tuple(tv))
        for o_ref, k in zip(drow_refs, want):
            gk = grads[k]
            if k in av:
                gk = gk + av[k]
            o_ref[...] = gk.astype(o_ref.dtype)
        for n, o_ref in enumerate(dconst_refs):
            gc = grads[nr + n]

            @pl.when(i == 0)
            def _(o_ref=o_ref, gc=gc):
                o_ref[...] = gc

            @pl.when(i > 0)
            def _(o_ref=o_ref, gc=gc):
                o_ref[...] += gc

    out_row = lambda w: pl.BlockSpec((tm, w), lambda g, i: (i, g))
    res = pl.pallas_call(
        body, name=name, grid=(G, Lp // tm),
        in_specs=([_row_spec(r, tm) for r in rows] + [_const_spec(c) for c in consts]
                  + [_row_spec(r, tm) for r in cts] + [out_row(rows[k].w) for k in add_k]),
        out_specs=[out_row(rows[k].w) for k in want] + [_const_spec(c) for c in consts],
        out_shape=([jax.ShapeDtypeStruct((Lp, rows[k].w * G), drow_dtypes[k]) for k in want]
                   + [jax.ShapeDtypeStruct(c.arr.shape, F32) for c in consts]),
        compiler_params=_params(("parallel", "arbitrary")),
    )(*[r.arr for r in rows], *[c.arr for c in consts], *[r.arr for r in cts], *[adds[k] for k in add_k])
    drows = [None] * nr
    for n, k in enumerate(want):
        drows[k] = res[n]
    return drows, list(res[len(want):])


def _f_rms(valid, h, g):
    r = lax.rsqrt(jnp.mean(h * h, axis=-1, keepdims=True) + EPS)
    return (jnp.where(valid, h * r * g, 0.0),)


def _f_logg(valid, glr, w, b):
    pre = jnp.dot(glr.astype(BF16), w.astype(BF16), preferred_element_type=F32) + b
    return (jnp.where(valid, _log_sigmoid(pre) / GLA_TAU, 0.0),)


def _f_gla_post(valid, oc, rc, g):
    y = oc * lax.rsqrt(jnp.mean(oc * oc, axis=-1, keepdims=True) + EPS) * g
    return (jnp.where(valid, rc * _sigmoid(rc) * y, 0.0),)


def _f_merge(valid, ya, yb, yc, ga, gb, gc, ba, bb, bc):
    mix = _sigmoid(ga + ba) * ya + _sigmoid(gb + bb) * yb + _sigmoid(gc + bc) * yc
    return (jnp.where(valid, mix, 0.0),)


def _fox_gate_fwd(side, bf, Lp):
    t = BLOCK
    n = Lp // t

    def body(s_ref, b_ref, c_ref, carry):
        i = pl.program_id(0)

        @pl.when(i == 0)
        def _():
            carry[...] = jnp.zeros_like(carry)

        lane = lax.broadcasted_iota(jnp.int32, (t, LANES), 1)
        ok = _valid_rows(i, t) & (lane < FOX_HEADS)
        logf = jnp.where(ok, _log_sigmoid(s_ref[...] + b_ref[...]), 0.0)
        tril = (lax.broadcasted_iota(jnp.int32, (t, t), 1) <= lax.broadcasted_iota(jnp.int32, (t, t), 0)).astype(F32)
        c = jnp.dot(tril, logf, precision=HIGHEST, preferred_element_type=F32) + carry[...]
        c_ref[...] = c
        carry[...] = c[t - 1:t, :]

    return pl.pallas_call(
        body, name="fox_gate_fwd", grid=(n,),
        in_specs=[pl.BlockSpec((t, LANES), lambda i: (i, 0)), pl.BlockSpec((1, LANES), lambda i: (0, 0))],
        out_specs=pl.BlockSpec((t, LANES), lambda i: (i, 0)),
        out_shape=jax.ShapeDtypeStruct((Lp, LANES), F32),
        scratch_shapes=[pltpu.VMEM((1, LANES), F32)],
        compiler_params=_params(("arbitrary",)),
    )(side, bf)


def _fox_gate_bwd(side, bf, dc, Lp):
    t = BLOCK
    n = Lp // t

    def body(s_ref, b_ref, dc_ref, dfa_ref, db_ref, carry):
        i = pl.program_id(0)

        @pl.when(i == 0)
        def _():
            carry[...] = jnp.zeros_like(carry)

        lane = lax.broadcasted_iota(jnp.int32, (t, LANES), 1)
        ok = _valid_rows(n - 1 - i, t) & (lane < FOX_HEADS)
        triu = (lax.broadcasted_iota(jnp.int32, (t, t), 1) >= lax.broadcasted_iota(jnp.int32, (t, t), 0)).astype(F32)
        dlogf = jnp.dot(triu, dc_ref[...], precision=HIGHEST, preferred_element_type=F32) + carry[...]
        carry[...] = dlogf[0:1, :]
        dpre = jnp.where(ok, dlogf * _sigmoid(-(s_ref[...] + b_ref[...])), 0.0)
        dfa_ref[...] = dpre
        part = jnp.sum(dpre, axis=0, keepdims=True)

        @pl.when(i == 0)
        def _():
            db_ref[...] = part

        @pl.when(i > 0)
        def _():
            db_ref[...] += part

    rev = lambda i: (n - 1 - i, 0)
    return pl.pallas_call(
        body, name="fox_gate_bwd", grid=(n,),
        in_specs=[pl.BlockSpec((t, LANES), rev), pl.BlockSpec((1, LANES), lambda i: (0, 0)),
                  pl.BlockSpec((t, LANES), rev)],
        out_specs=[pl.BlockSpec((t, LANES), rev), pl.BlockSpec((1, LANES), lambda i: (0, 0))],
        out_shape=[jax.ShapeDtypeStruct((Lp, LANES), F32), jax.ShapeDtypeStruct((1, LANES), F32)],
        scratch_shapes=[pltpu.VMEM((1, LANES), F32)],
        compiler_params=_params(("arbitrary",)),
    )(side, bf, dc)


def _fox_key_bias(cq_ref, ck_ref, j, t):
    col = j * t + lax.broadcasted_iota(jnp.int32, (1, t), 1)
    return jnp.where(col >= PAD, ck_ref[...] - cq_ref[0:1, :], -NEG)


def _fox_s(q, k, bias, diagonal, t):
    s = lax.dot_general(q, k, (((1,), (1,)), ((), ())), preferred_element_type=F32) * (FOX_HD ** -0.5) - bias
    if diagonal:
        causal = lax.broadcasted_iota(jnp.int32, (t, t), 1) <= lax.broadcasted_iota(jnp.int32, (t, t), 0)
        s = jnp.where(causal, s, NEG)
    return s


def _fox_fwd(main, c_col, c_row, Lp, t):
    n = Lp // t
    qb, kb, vb = QA // FOX_HD, KA // FOX_HD, VA // FOX_HD

    def body(q_ref, k_ref, v_ref, cq_ref, ck_ref, o_ref, lse_ref, m_s, l_s, acc):
        i, j = pl.program_id(1), pl.program_id(2)

        @pl.when(j == 0)
        def _():
            m_s[...] = jnp.full_like(m_s, NEG)
            l_s[...] = jnp.zeros_like(l_s)
            acc[...] = jnp.zeros_like(acc)

        def update(diagonal):
            s = _fox_s(q_ref[...], k_ref[...], _fox_key_bias(cq_ref, ck_ref, j, t), diagonal, t)
            m_new = jnp.maximum(m_s[...], jnp.max(s, axis=1, keepdims=True))
            alpha = jnp.exp(m_s[...] - m_new)
            p = jnp.exp(s - m_new)
            l_s[...] = alpha * l_s[...] + jnp.sum(p, axis=1, keepdims=True)
            acc[...] = alpha * acc[...] + jnp.dot(p.astype(BF16), v_ref[...], preferred_element_type=F32)
            m_s[...] = m_new

        @pl.when(j < i)
        def _():
            update(False)

        @pl.when(j == i)
        def _():
            update(True)
            o_ref[...] = jnp.where(_valid_rows(i, t), acc[...] / l_s[...], 0.0).astype(o_ref.dtype)
            lse_ref[...] = m_s[...] + jnp.log(l_s[...])

    kv = lambda base: pl.BlockSpec((t, FOX_HD), lambda h, i, j: (jnp.minimum(j, i), base + h))
    return pl.pallas_call(
        body, name="fox_fwd", grid=(FOX_HEADS, n, n),
        in_specs=[pl.BlockSpec((t, FOX_HD), lambda h, i, j: (i, qb + h)), kv(kb), kv(vb),
                  pl.BlockSpec((None, t, 1), lambda h, i, j: (h, i, 0)),
                  pl.BlockSpec((None, 1, t), lambda h, i, j: (h, 0, jnp.minimum(j, i)))],
        out_specs=[pl.BlockSpec((t, FOX_HD), lambda h, i, j: (i, h)),
                   pl.BlockSpec((None, t, 1), lambda h, i, j: (h, i, 0))],
        out_shape=[jax.ShapeDtypeStruct((Lp, FOX_WIDTH), BF16), jax.ShapeDtypeStruct((FOX_HEADS, Lp, 1), F32)],
        scratch_shapes=[pltpu.VMEM((t, 1), F32), pltpu.VMEM((t, 1), F32), pltpu.VMEM((t, FOX_HD), F32)],
        compiler_params=_params(("parallel", "parallel", "arbitrary")),
    )(main, main, main, c_col, c_row)


def _fox_p_dp(q_ref, k_ref, v_ref, cq_ref, ck_ref, lse_ref, do_ref, j, diagonal, t):
    s = _fox_s(q_ref[...], k_ref[...], _fox_key_bias(cq_ref, ck_ref, j, t), diagonal, t)
    p = jnp.exp(s - lse_ref[...])
    dp = lax.dot_general(do_ref[...], v_ref[...], (((1,), (1,)), ((), ())), preferred_element_type=F32)
    return p, dp


def _fox_delta(main, c_col, c_row, lse, doa, Lp, t):
    n = Lp // t
    qb, kb, vb = QA // FOX_HD, KA // FOX_HD, VA // FOX_HD

    def body(q_ref, k_ref, v_ref, cq_ref, ck_ref, lse_ref, do_ref, dl_ref, dl_s):
        i, j = pl.program_id(1), pl.program_id(2)

        @pl.when(j == 0)
        def _():
            dl_s[...] = jnp.zeros_like(dl_s)

        def sweep(diagonal):
            p, dp = _fox_p_dp(q_ref, k_ref, v_ref, cq_ref, ck_ref, lse_ref, do_ref, j, diagonal, t)
            dl_s[...] += jnp.sum(p * dp, axis=1, keepdims=True)

        @pl.when(j < i)
        def _():
            sweep(False)

        @pl.when(j == i)
        def _():
            sweep(True)
            dl_ref[...] = dl_s[...]

    kv = lambda base: pl.BlockSpec((t, FOX_HD), lambda h, i, j: (jnp.minimum(j, i), base + h))
    qrow = lambda base: pl.BlockSpec((t, FOX_HD), lambda h, i, j: (i, base + h))
    col = pl.BlockSpec((None, t, 1), lambda h, i, j: (h, i, 0))
    return pl.pallas_call(
        body, name="fox_delta", grid=(FOX_HEADS, n, n),
        in_specs=[qrow(qb), kv(kb), kv(vb), col,
                  pl.BlockSpec((None, 1, t), lambda h, i, j: (h, 0, jnp.minimum(j, i))), col, qrow(0)],
        out_specs=col,
        out_shape=jax.ShapeDtypeStruct((FOX_HEADS, Lp, 1), F32),
        scratch_shapes=[pltpu.VMEM((t, 1), F32)],
        compiler_params=_params(("parallel", "parallel", "arbitrary")),
    )(main, main, main, c_col, c_row, lse, doa)


def _fox_bwd(main, c_col, c_row, lse, delta, doa, Lp, t):
    n = Lp // t
    qb, kb, vb = QA // FOX_HD, KA // FOX_HD, VA // FOX_HD
    scale = FOX_HD ** -0.5

    def body(q_ref, k_ref, v_ref, cq_ref, ck_ref, lse_ref, dl_ref, do_ref, dq_ref, dk_ref, dv_ref, dck_ref,
             dq_s, dk_s, dv_s, dc_s):
        j, i = pl.program_id(1), pl.program_id(2)

        @pl.when(jnp.logical_and(j == 0, i == 0))
        def _():
            dq_s[...] = jnp.zeros_like(dq_s)

        @pl.when(i == 0)
        def _():
            dk_s[...] = jnp.zeros_like(dk_s)
            dv_s[...] = jnp.zeros_like(dv_s)
            dc_s[...] = jnp.zeros_like(dc_s)

        def sweep(diagonal):
            p, dp = _fox_p_dp(q_ref, k_ref, v_ref, cq_ref, ck_ref, lse_ref, do_ref, j, diagonal, t)
            ds = p * (dp - dl_ref[...])
            dsb = ds.astype(BF16)
            tn = (((0,), (0,)), ((), ()))
            dv_s[...] += lax.dot_general(p.astype(BF16), do_ref[...], tn, preferred_element_type=F32)
            dk_s[...] += lax.dot_general(dsb, q_ref[...], tn, preferred_element_type=F32)
            dc_s[...] -= jnp.sum(ds, axis=0, keepdims=True)
            rows = pl.ds(pl.multiple_of(i * t, t), t)
            dq_s[rows, :] += jnp.dot(dsb, k_ref[...], preferred_element_type=F32)

        @pl.when(i > j)
        def _():
            sweep(False)

        @pl.when(i == j)
        def _():
            sweep(True)

        @pl.when(i == n - 1)
        def _():
            dk_ref[...] = (dk_s[...] * scale).astype(dk_ref.dtype)
            dv_ref[...] = dv_s[...].astype(dv_ref.dtype)
            dck_ref[...] = dc_s[...]

        @pl.when(jnp.logical_and(j == n - 1, i == n - 1))
        def _():
            dq_ref[...] = (dq_s[...] * scale).astype(dq_ref.dtype)

    qrow = lambda base: pl.BlockSpec((t, FOX_HD), lambda h, j, i: (jnp.maximum(i, j), base + h))
    kv = lambda base: pl.BlockSpec((t, FOX_HD), lambda h, j, i: (j, base + h))
    col = pl.BlockSpec((None, t, 1), lambda h, j, i: (h, jnp.maximum(i, j), 0))
    row = pl.BlockSpec((None, 1, t), lambda h, j, i: (h, 0, j))
    wide = jax.ShapeDtypeStruct((Lp, FOX_WIDTH), BF16)
    return pl.pallas_call(
        body, name="fox_bwd", grid=(FOX_HEADS, n, n),
        in_specs=[qrow(qb), kv(kb), kv(vb), col, row, col, col, qrow(0)],
        out_specs=[pl.BlockSpec((Lp, FOX_HD), lambda h, j, i: (0, h)), kv(0), kv(0), row],
        out_shape=[wide, wide, wide, jax.ShapeDtypeStruct((FOX_HEADS, 1, Lp), F32)],
        scratch_shapes=[pltpu.VMEM((Lp, FOX_HD), F32), pltpu.VMEM((t, FOX_HD), F32), pltpu.VMEM((t, FOX_HD), F32),
                        pltpu.VMEM((1, t), F32)],
        compiler_params=_params(("parallel", "arbitrary", "arbitrary")),
    )(main, main, main, c_col, c_row, lse, delta, doa)


def _shift_down(x, n):
    return pltpu.roll(x, n, 0)


def _shift_up(x, n):
    return pltpu.roll(x, x.shape[0] - n, 0)


def _prev_spec(tm, ct, cb):
    return pl.BlockSpec((HALO, ct), lambda g, i: (jnp.maximum(i * (tm // HALO) - 1, 0), cb(g)))


def _next_spec(tm, ct, cb, nrows):
    last = nrows // HALO - 1
    return pl.BlockSpec((HALO, ct), lambda g, i: (jnp.minimum((i + 1) * (tm // HALO), last), cb(g)))


def _cur_spec(tm, ct, cb):
    return pl.BlockSpec((tm, ct), lambda g, i: (i, cb(g)))


def _wrow(w_ref, k):
    return w_ref[k:k + 1, :]


def _rows3(s0, s1, s2, ct):
    r = lax.broadcasted_iota(jnp.int32, (8, ct), 0)
    return jnp.where(r == 0, s0, jnp.where(r == 1, s1, jnp.where(r == 2, s2, 0.0)))


def _acc_out(ref, i, val):
    @pl.when(i == 0)
    def _():
        ref[...] = val

    @pl.when(i > 0)
    def _():
        ref[...] += val


def _sconv_fwd(main, w8, Lp, tm):
    ct = 256
    G = CONV_CH // ct
    bb, cb, hb = (lambda g: SCB // ct + g), (lambda g: SCC // ct + g), (lambda g: SCH // ct + g)

    def body(b_ref, c_ref, h_ref, cp_ref, hp_ref, w_ref, o_ref):
        i = pl.program_id(1)
        z = c_ref[...].astype(F32) * h_ref[...].astype(F32)
        zp = jnp.where(i > 0, cp_ref[...].astype(F32) * hp_ref[...].astype(F32), 0.0)
        zz = jnp.concatenate([zp, z], axis=0)
        cz = (_wrow(w_ref, 0) * _shift_down(zz, 2)[HALO:] + _wrow(w_ref, 1) * _shift_down(zz, 1)[HALO:]
              + _wrow(w_ref, 2) * z)
        o_ref[...] = (b_ref[...].astype(F32) * cz).astype(o_ref.dtype)

    return pl.pallas_call(
        body, name="sconv_fwd", grid=(G, Lp // tm),
        in_specs=[_cur_spec(tm, ct, bb), _cur_spec(tm, ct, cb), _cur_spec(tm, ct, hb),
                  _prev_spec(tm, ct, cb), _prev_spec(tm, ct, hb), pl.BlockSpec((8, ct), lambda g, i: (0, g))],
        out_specs=pl.BlockSpec((tm, ct), lambda g, i: (i, g)),
        out_shape=jax.ShapeDtypeStruct((Lp, CONV_CH), BF16),
        compiler_params=_params(("parallel", "arbitrary")),
    )(main, main, main, main, main, w8)


def _sconv_bwd(main, w8, dub, Lp, tm):
    ct = 256
    G = CONV_CH // ct
    n = Lp // tm
    bb, cb, hb, ob = (lambda g: SCB // ct + g), (lambda g: SCC // ct + g), (lambda g: SCH // ct + g), (lambda g: g)

    def body(b_ref, c_ref, h_ref, cp_ref, hp_ref, bn_ref, d_ref, dn_ref, w_ref, db_ref, dc_ref, dh_ref, dw_ref):
        i = pl.program_id(1)
        b, c, h = b_ref[...].astype(F32), c_ref[...].astype(F32), h_ref[...].astype(F32)
        z = c * h
        zp = jnp.where(i > 0, cp_ref[...].astype(F32) * hp_ref[...].astype(F32), 0.0)
        zz = jnp.concatenate([zp, z], axis=0)
        z1, z2 = _shift_down(zz, 1)[HALO:], _shift_down(zz, 2)[HALO:]
        w0, w1, w2 = _wrow(w_ref, 0), _wrow(w_ref, 1), _wrow(w_ref, 2)
        cz = w0 * z2 + w1 * z1 + w2 * z
        dub_c = d_ref[...].astype(F32)
        db_ref[...] = (dub_c * cz).astype(db_ref.dtype)
        dcz = dub_c * b
        dcz_n = jnp.where(i < n - 1, dn_ref[...].astype(F32) * bn_ref[...].astype(F32), 0.0)
        dd = jnp.concatenate([dcz, dcz_n], axis=0)
        dz = w2 * dcz + w1 * _shift_up(dd, 1)[:tm] + w0 * _shift_up(dd, 2)[:tm]
        dc_ref[...] = (dz * h).astype(dc_ref.dtype)
        dh_ref[...] = (dz * c).astype(dh_ref.dtype)
        s = lambda x: jnp.sum(dcz * x, axis=0, keepdims=True)
        _acc_out(dw_ref, i, _rows3(s(z2), s(z1), s(z), ct))

    out = pl.BlockSpec((tm, ct), lambda g, i: (i, g))
    return pl.pallas_call(
        body, name="sconv_bwd", grid=(G, n),
        in_specs=[_cur_spec(tm, ct, bb), _cur_spec(tm, ct, cb), _cur_spec(tm, ct, hb),
                  _prev_spec(tm, ct, cb), _prev_spec(tm, ct, hb), _next_spec(tm, ct, bb, Lp),
                  _cur_spec(tm, ct, ob), _next_spec(tm, ct, ob, Lp), pl.BlockSpec((8, ct), lambda g, i: (0, g))],
        out_specs=[out, out, out, pl.BlockSpec((8, ct), lambda g, i: (0, g))],
        out_shape=[jax.ShapeDtypeStruct((Lp, CONV_CH), BF16)] * 3 + [jax.ShapeDtypeStruct((8, CONV_CH), F32)],
        compiler_params=_params(("parallel", "arbitrary")),
    )(main, main, main, main, main, main, dub, dub, w8)


def _conv3(w_ref, ext):
    return _wrow(w_ref, 0) * _shift_down(ext, 2) + _wrow(w_ref, 1) * _shift_down(ext, 1) + _wrow(w_ref, 2) * ext


def _mlp_act_fwd(up, w8, Lp, tm):
    ct = 256
    G = D_FF // ct
    gb, ub = (lambda g: g), (lambda g: G + g)

    def body(g_ref, u_ref, gp_ref, up_ref, wg_ref, wu_ref, o_ref):
        i = pl.program_id(1)

        def conv(cur, prev, w_ref):
            ext = jnp.concatenate([jnp.where(i > 0, prev[...].astype(F32), 0.0), cur[...].astype(F32)], axis=0)
            return _conv3(w_ref, ext)[HALO:]

        ug, uu = conv(g_ref, gp_ref, wg_ref), conv(u_ref, up_ref, wu_ref)
        o_ref[...] = (ug * _sigmoid(ug) * uu).astype(o_ref.dtype)

    wspec = lambda cb: pl.BlockSpec((8, ct), lambda g, i: (0, cb(g)))
    return pl.pallas_call(
        body, name="mlp_act_fwd", grid=(G, Lp // tm),
        in_specs=[_cur_spec(tm, ct, gb), _cur_spec(tm, ct, ub), _prev_spec(tm, ct, gb), _prev_spec(tm, ct, ub),
                  wspec(gb), wspec(ub)],
        out_specs=pl.BlockSpec((tm, ct), lambda g, i: (i, g)),
        out_shape=jax.ShapeDtypeStruct((Lp, D_FF), BF16),
        compiler_params=_params(("parallel", "arbitrary")),
    )(up, up, up, up, w8, w8)


def _mlp_act_bwd(up, w8, da, Lp, tm):
    ct = 256
    G = D_FF // ct
    n = Lp // tm
    gb, ub, ob = (lambda g: g), (lambda g: G + g), (lambda g: g)

    def body(g_ref, u_ref, gp_ref, up_ref, gn_ref, un_ref, d_ref, dn_ref, wg_ref, wu_ref,
             dg_ref, du_ref, dwg_ref, dwu_ref):
        i = pl.program_id(1)

        def ext_of(prev, cur, nxt):
            return jnp.concatenate([jnp.where(i > 0, prev[...].astype(F32), 0.0), cur[...].astype(F32),
                                    jnp.where(i < n - 1, nxt[...].astype(F32), 0.0)], axis=0)

        eg, eu = ext_of(gp_ref, g_ref, gn_ref), ext_of(up_ref, u_ref, un_ref)
        da_e = jnp.concatenate([jnp.zeros((HALO, ct), F32), d_ref[...].astype(F32),
                                jnp.where(i < n - 1, dn_ref[...].astype(F32), 0.0)], axis=0)
        ug, uu = _conv3(wg_ref, eg), _conv3(wu_ref, eu)
        sg = _sigmoid(ug)
        dug = da_e * uu * (sg * (1.0 + ug * (1.0 - sg)))
        duu = da_e * (ug * sg)
        cur = slice(HALO, HALO + tm)

        def back(w_ref, dx, e, dx_ref, dw_ref):
            d_in = _wrow(w_ref, 2) * dx + _wrow(w_ref, 1) * _shift_up(dx, 1) + _wrow(w_ref, 0) * _shift_up(dx, 2)
            dx_ref[...] = d_in[cur].astype(dx_ref.dtype)
            s = lambda x: jnp.sum(dx[cur] * x[cur], axis=0, keepdims=True)
            _acc_out(dw_ref, i, _rows3(s(_shift_down(e, 2)), s(_shift_down(e, 1)), s(e), ct))

        back(wg_ref, dug, eg, dg_ref, dwg_ref)
        back(wu_ref, duu, eu, du_ref, dwu_ref)

    wspec = lambda cb: pl.BlockSpec((8, ct), lambda g, i: (0, cb(g)))
    out = pl.BlockSpec((tm, ct), lambda g, i: (i, g))
    return pl.pallas_call(
        body, name="mlp_act_bwd", grid=(G, n),
        in_specs=[_cur_spec(tm, ct, gb), _cur_spec(tm, ct, ub), _prev_spec(tm, ct, gb), _prev_spec(tm, ct, ub),
                  _next_spec(tm, ct, gb, Lp), _next_spec(tm, ct, ub, Lp), _cur_spec(tm, ct, ob),
                  _next_spec(tm, ct, ob, Lp), wspec(gb), wspec(ub)],
        out_specs=[out, out, wspec(ob), wspec(ob)],
        out_shape=[jax.ShapeDtypeStruct((Lp, D_FF), BF16)] * 2 + [jax.ShapeDtypeStruct((8, D_FF), F32)] * 2,
        compiler_params=_params(("parallel", "arbitrary")),
    )(up, up, up, up, up, up, da, da, w8, w8)


def _gla_chunk(q, k, v, g, s0):
    C = BLOCK
    r_i = lax.broadcasted_iota(jnp.int32, (C, C), 0)
    c_i = lax.broadcasted_iota(jnp.int32, (C, C), 1)
    hdot = functools.partial(jnp.dot, precision=HIGHEST, preferred_element_type=F32)
    b = hdot((c_i <= r_i).astype(F32), g)
    sub_start = jnp.bitwise_and(r_i, -GLA_SUB)
    ref_all = hdot((c_i == sub_start).astype(F32), b)
    qs = q * (GLA_DK ** -0.5)
    qt = (qs * jnp.exp(b - ref_all)).astype(BF16)
    att = jnp.zeros((C, C), F32)
    for n in range(C // GLA_SUB):
        ref_n = hdot((c_i == n * GLA_SUB).astype(F32), b)
        kt = (k * jnp.exp(jnp.minimum(ref_n - b, 60.0))).astype(BF16)
        a_n = lax.dot_general(qt, kt, (((1,), (1,)), ((), ())), preferred_element_type=F32)
        att = att + jnp.where((sub_start == n * GLA_SUB) & (c_i <= r_i), a_n, 0.0)
    o = (jnp.dot(att.astype(BF16), v.astype(BF16), preferred_element_type=F32)
         + jnp.dot((qs * jnp.exp(b)).astype(BF16), s0.astype(BF16), preferred_element_type=F32))
    b_last = hdot((c_i == C - 1).astype(F32), b)
    kd = (k * jnp.exp(b_last - b)).astype(BF16)
    last_rows = (lax.broadcasted_iota(jnp.int32, (C, GLA_DV), 0) == C - 1).astype(F32)
    decay = lax.dot_general(b, last_rows, (((0,), (0,)), ((), ())), precision=HIGHEST,
                            preferred_element_type=F32)
    s1 = jnp.exp(decay) * s0 + lax.dot_general(kd, v.astype(BF16), (((0,), (0,)), ((), ())),
                                               preferred_element_type=F32)
    return o, s1


def _gla_fwd(main, logg, Lp):
    n = Lp // BLOCK
    qb, kb, vb = QC // GLA_DK, KC // GLA_DK, VC // GLA_DV

    def body(q_ref, k_ref, v_ref, g_ref, o_ref, st_ref, s_s):
        c = pl.program_id(1)

        @pl.when(c == 0)
        def _():
            s_s[...] = jnp.zeros_like(s_s)

        s0 = s_s[...]
        st_ref[...] = s0
        o, s1 = _gla_chunk(q_ref[...].astype(F32), k_ref[...].astype(F32), v_ref[...].astype(F32), g_ref[...], s0)
        o_ref[...] = o
        s_s[...] = s1

    return pl.pallas_call(
        body, name="gla_fwd", grid=(GLA_HEADS, n),
        in_specs=[pl.BlockSpec((BLOCK, GLA_DK), lambda h, c: (c, qb + h)),
                  pl.BlockSpec((BLOCK, GLA_DK), lambda h, c: (c, kb + h)),
                  pl.BlockSpec((BLOCK, GLA_DV), lambda h, c: (c, vb + h)),
                  pl.BlockSpec((BLOCK, GLA_DK), lambda h, c: (c, h))],
        out_specs=[pl.BlockSpec((BLOCK, GLA_DV), lambda h, c: (c, h)),
                   pl.BlockSpec((None, None, GLA_DK, GLA_DV), lambda h, c: (h, c, 0, 0))],
        out_shape=[jax.ShapeDtypeStruct((Lp, GLA_HEADS * GLA_DV), F32),
                   jax.ShapeDtypeStruct((GLA_HEADS, n, GLA_DK, GLA_DV), F32)],
        scratch_shapes=[pltpu.VMEM((GLA_DK, GLA_DV), F32)],
        compiler_params=_params(("parallel", "arbitrary")),
    )(main, main, main, logg)


def _gla_bwd(main, logg, states, do, Lp):
    n = Lp // BLOCK
    qb, kb, vb = QC // GLA_DK, KC // GLA_DK, VC // GLA_DV

    def body(q_ref, k_ref, v_ref, g_ref, st_ref, do_ref, dq_ref, dk_ref, dv_ref, dg_ref, ds_s):
        c = pl.program_id(1)

        @pl.when(c == 0)
        def _():
            ds_s[...] = jnp.zeros_like(ds_s)

        _, vjp = jax.vjp(_gla_chunk, q_ref[...].astype(F32), k_ref[...].astype(F32), v_ref[...].astype(F32),
                         g_ref[...], st_ref[...])
        dq, dk, dv, dg, ds0 = vjp((do_ref[...], ds_s[...]))
        dq_ref[...] = dq.astype(dq_ref.dtype)
        dk_ref[...] = dk.astype(dk_ref.dtype)
        dv_ref[...] = dv.astype(dv_ref.dtype)
        dg_ref[...] = dg
        ds_s[...] = ds0

    rk = lambda base: pl.BlockSpec((BLOCK, GLA_DK), lambda h, c: (n - 1 - c, base + h))
    rv = lambda base: pl.BlockSpec((BLOCK, GLA_DV), lambda h, c: (n - 1 - c, base + h))
    return pl.pallas_call(
        body, name="gla_bwd", grid=(GLA_HEADS, n),
        in_specs=[rk(qb), rk(kb), rv(vb), rk(0),
                  pl.BlockSpec((None, None, GLA_DK, GLA_DV), lambda h, c: (h, n - 1 - c, 0, 0)), rv(0)],
        out_specs=[rk(0), rk(0), rv(0), rk(0)],
        out_shape=[jax.ShapeDtypeStruct((Lp, GLA_HEADS * GLA_DK), BF16), jax.ShapeDtypeStruct((Lp, GLA_HEADS * GLA_DK), BF16),
                   jax.ShapeDtypeStruct((Lp, GLA_HEADS * GLA_DV), BF16), jax.ShapeDtypeStruct((Lp, GLA_HEADS * GLA_DK), F32)],
        scratch_shapes=[pltpu.VMEM((GLA_DK, GLA_DV), F32)],
        compiler_params=_params(("parallel", "arbitrary")),
    )(main, main, main, logg, states, do)


def _loss_head(h, g, target, Lp):
    t = BLOCK
    D = D_MODEL

    def body(h_ref, g_ref, t_ref, loss_ref, dh_ref, dg_ref):
        i = pl.program_id(0)
        x = h_ref[...]
        tok = (i * t + lax.broadcasted_iota(jnp.int32, (t, 1), 0)) >= BLOCK
        r = lax.rsqrt(jnp.mean(x * x, axis=-1, keepdims=True) + EPS)
        nrm = x * r
        e = jnp.where(tok, nrm * g_ref[...] - t_ref[...], 0.0)
        part = 0.5 * jnp.sum(jnp.sum(e * e, axis=1, keepdims=True), axis=0, keepdims=True) / D
        dy = e / D
        dn = dy * g_ref[...]
        dh_ref[...] = r * (dn - nrm * jnp.mean(dn * nrm, axis=-1, keepdims=True))
        _acc_out(dg_ref, i, jnp.sum(dy * nrm, axis=0, keepdims=True))
        _acc_out(loss_ref, i, jnp.broadcast_to(part, (1, LANES)))

    return pl.pallas_call(
        body, name="loss_head", grid=(Lp // t,),
        in_specs=[pl.BlockSpec((t, D), lambda i: (i, 0)), pl.BlockSpec((1, D), lambda i: (0, 0)),
                  pl.BlockSpec((t, D), lambda i: (jnp.maximum(i - 1, 0), 0))],
        out_specs=[pl.BlockSpec((1, LANES), lambda i: (0, 0)), pl.BlockSpec((t, D), lambda i: (i, 0)),
                   pl.BlockSpec((1, D), lambda i: (0, 0))],
        out_shape=[jax.ShapeDtypeStruct((1, LANES), F32), jax.ShapeDtypeStruct((Lp, D), F32),
                   jax.ShapeDtypeStruct((1, D), F32)],
        compiler_params=_params(("arbitrary",)),
    )(h, g, target)


def _adamw(w, g, m, v, name):
    shape = w.shape
    cols = shape[-1]
    rows = w.size // cols
    w2, g2, m2, v2 = (a.reshape(rows, cols) for a in (w, g, m, v))
    budget_rows = max(8, ADAM_BLOCK_BYTES // (4 * cols))
    tr = rows if rows <= budget_rows else _pick(rows, tuple(t for t in (512, 256, 128, 64, 32, 16, 8) if t <= budget_rows))

    def body(w_ref, g_ref, m_ref, v_ref, d_ref, nm_ref, nv_ref):
        gg = g_ref[...]
        mm = ADAM_B1 * m_ref[...] + (1.0 - ADAM_B1) * gg
        vv = ADAM_B2 * v_ref[...] + (1.0 - ADAM_B2) * jnp.square(gg)
        m_hat = mm / (1.0 - ADAM_B1 ** ADAM_STEP)
        v_hat = vv / (1.0 - ADAM_B2 ** ADAM_STEP)
        d_ref[...] = -ADAM_LR * (m_hat / (jnp.sqrt(v_hat) + ADAM_EPS) + ADAM_WD * w_ref[...])
        nm_ref[...] = mm
        nv_ref[...] = vv

    spec = pl.BlockSpec((tr, cols), lambda i: (i, 0))
    d, nm, nv = pl.pallas_call(
        body, name=name, grid=(rows // tr,), in_specs=[spec] * 4, out_specs=[spec] * 3,
        out_shape=[jax.ShapeDtypeStruct((rows, cols), F32)] * 3,
        compiler_params=_params(("parallel",)),
    )(w2, g2, m2, v2)
    return d.reshape(shape), nm.reshape(shape), nv.reshape(shape)


def _place():
    x, y, c = lax.axis_index("x"), lax.axis_index("y"), lax.axis_index("c")
    chips = [(1 - x, y), (x, 1 - y), (1 - x, 1 - y)]
    return x, y, c, chips


def _rcopy(src, dst, send_sems, recv_sems, k, to):
    return pltpu.make_async_remote_copy(src_ref=src, dst_ref=dst, send_sem=send_sems.at[k], recv_sem=recv_sems.at[k],
                                        device_id=to, device_id_type=MESH)


def _any_spec():
    return pl.BlockSpec(memory_space=pl.ANY)


def _shard_ref(ref, mode, t, r, c):
    if mode == "rows":
        return ref.at[pl.ds(pl.multiple_of(t * r, 16), r), :]
    if mode == "cols":
        return ref.at[:, pl.ds(pl.multiple_of(t * c, LANES), c)]
    return ref.at[t]


def _gathered_shape(mode, r, c):
    return {"rows": (4 * r, c), "cols": (r, 4 * c), "stack": (4, r, c)}[mode]


def _gather_weights(shards, modes):
    n = len(shards)
    dims = [s.shape[1:] for s in shards]

    def body(*refs):
        ins, outs = refs[:n], refs[n:2 * n]
        send_sems, recv_sems, local_sems = refs[2 * n:]
        x, y, c, chips = _place()
        me = 2 * x + y
        place = lambda k, l, t: _shard_ref(outs[k].at[l], modes[k], t, *dims[k])
        own = [pltpu.make_async_copy(ins[k].at[l], place(k, l, me), local_sems.at[DEPTH * k + l])
               for k in range(n) for l in range(DEPTH)]
        for cp in own:
            cp.start()
        first = [_rcopy(ins[k].at[c], place(k, c, me), send_sems, recv_sems, 6 * k + j, (*chip, c))
                 for j, chip in enumerate(chips) for k in range(n)]
        for cp in first:
            cp.start()
        passed = []
        for j, (px, py) in enumerate(chips):
            for k in range(n):
                blk = place(k, c, 2 * px + py)
                _rcopy(blk, blk, send_sems, recv_sems, 6 * k + j, (x, y, c)).wait_recv()
                fwd = _rcopy(blk, blk, send_sems, recv_sems, 6 * k + 3 + j, (x, y, 1 - c))
                fwd.start()
                passed.append(fwd)
        for j, (px, py) in enumerate(chips):
            for k in range(n):
                blk = place(k, 1 - c, 2 * px + py)
                _rcopy(blk, blk, send_sems, recv_sems, 6 * k + 3 + j, (x, y, c)).wait_recv()
        for cp in first + passed:
            cp.wait_send()
        for cp in own:
            cp.wait()

    return pl.pallas_call(
        body, name="gather_weights", in_specs=[_any_spec()] * n, out_specs=[_any_spec()] * n,
        out_shape=[jax.ShapeDtypeStruct((DEPTH,) + _gathered_shape(modes[k], *dims[k]), shards[k].dtype)
                   for k in range(n)],
        scratch_shapes=[pltpu.SemaphoreType.DMA((6 * n,)), pltpu.SemaphoreType.DMA((6 * n,)),
                        pltpu.SemaphoreType.DMA((DEPTH * n,))],
    )(*shards)


def _swap_layers(gs):
    n = len(gs)

    def body(*refs):
        ins, outs = refs[:n], refs[n:2 * n]
        send_sems, recv_sems = refs[2 * n:]
        x, y, c, _ = _place()
        cps = [_rcopy(ins[k].at[1 - c], outs[k], send_sems, recv_sems, k, (x, y, 1 - c)) for k in range(n)]
        for cp in cps:
            cp.start()
        for cp in cps:
            cp.wait()

    return pl.pallas_call(
        body, name="rs_swap_layers", in_specs=[_any_spec()] * n, out_specs=[_any_spec()] * n,
        out_shape=[jax.ShapeDtypeStruct(g.shape[1:], g.dtype) for g in gs],
        scratch_shapes=[pltpu.SemaphoreType.DMA((n,)), pltpu.SemaphoreType.DMA((n,))],
    )(*gs)


def _scatter_chips(hs, modes, dims):
    n = len(hs)

    def body(*refs):
        ins, outs = refs[:n], refs[n:2 * n]
        send_sems, recv_sems, local_sems = refs[2 * n:]
        x, y, c, chips = _place()
        me = 2 * x + y
        part = lambda k, t: _shard_ref(ins[k], modes[k], t, *dims[k])
        own = [pltpu.make_async_copy(part(k, me), outs[k].at[me], local_sems.at[k]) for k in range(n)]
        for cp in own:
            cp.start()
        cps = [_rcopy(part(k, 2 * px + py), outs[k].at[me], send_sems, recv_sems, 3 * k + j, (px, py, c))
               for j, (px, py) in enumerate(chips) for k in range(n)]
        for cp in cps:
            cp.start()
        for j, (px, py) in enumerate(chips):
            for k in range(n):
                blk = outs[k].at[2 * px + py]
                _rcopy(blk, blk, send_sems, recv_sems, 3 * k + j, (x, y, c)).wait_recv()
        for cp in cps:
            cp.wait_send()
        for cp in own:
            cp.wait()

    return pl.pallas_call(
        body, name="rs_scatter_chips", in_specs=[_any_spec()] * n, out_specs=[_any_spec()] * n,
        out_shape=[jax.ShapeDtypeStruct((4,) + tuple(dims[k]), hs[k].dtype) for k in range(n)],
        scratch_shapes=[pltpu.SemaphoreType.DMA((3 * n,)), pltpu.SemaphoreType.DMA((3 * n,)),
                        pltpu.SemaphoreType.DMA((n,))],
    )(*hs)


def _join_layers(fs):
    n = len(fs)

    def body(*refs):
        ins, outs = refs[:n], refs[n:2 * n]
        send_sems, recv_sems, local_sems = refs[2 * n:]
        x, y, c, _ = _place()
        own = [pltpu.make_async_copy(ins[k], outs[k].at[c], local_sems.at[k]) for k in range(n)]
        for cp in own:
            cp.start()
        cps = [_rcopy(ins[k], outs[k].at[c], send_sems, recv_sems, k, (x, y, 1 - c)) for k in range(n)]
        for cp in cps:
            cp.start()
        for k in range(n):
            blk = outs[k].at[1 - c]
            _rcopy(blk, blk, send_sems, recv_sems, k, (x, y, c)).wait_recv()
        for cp in cps:
            cp.wait_send()
        for cp in own:
            cp.wait()

    return pl.pallas_call(
        body, name="rs_join_layers", in_specs=[_any_spec()] * n, out_specs=[_any_spec()] * n,
        out_shape=[jax.ShapeDtypeStruct((DEPTH,) + f.shape, f.dtype) for f in fs],
        scratch_shapes=[pltpu.SemaphoreType.DMA((n,)), pltpu.SemaphoreType.DMA((n,)), pltpu.SemaphoreType.DMA((n,))],
    )(*fs)


def _ew_rows(M, N):
    fit = [t for t in (512, 256, 128, 64, 32, 16) if M % t == 0 and t * N * 4 <= EW_BLOCK_BYTES]
    return fit[0] if fit else M


def _add_own(g, other, c1, out_dtype, name):
    _, M, N = g.shape
    tr = _ew_rows(M, N)

    def body(c_ref, g_ref, o_ref, out_ref):
        out_ref[...] = (g_ref[...] + o_ref[...]).astype(out_ref.dtype)

    return pl.pallas_call(
        body, name=name,
        grid_spec=pltpu.PrefetchScalarGridSpec(
            num_scalar_prefetch=1, grid=(M // tr,),
            in_specs=[pl.BlockSpec((None, tr, N), lambda i, cr: (cr[0], i, 0)),
                      pl.BlockSpec((tr, N), lambda i, cr: (i, 0))],
            out_specs=pl.BlockSpec((tr, N), lambda i, cr: (i, 0))),
        out_shape=jax.ShapeDtypeStruct((M, N), out_dtype),
        compiler_params=_params(("parallel",)),
    )(c1, g, other)


def _sum_chips(q, name):
    _, M, N = q.shape
    tr = _ew_rows(M, N)

    def body(q_ref, out_ref):
        out_ref[...] = ((q_ref[0].astype(F32) + q_ref[1].astype(F32)) + q_ref[2].astype(F32)) + q_ref[3].astype(F32)

    return pl.pallas_call(
        body, name=name, grid=(M // tr,),
        in_specs=[pl.BlockSpec((4, tr, N), lambda i: (0, i, 0))],
        out_specs=pl.BlockSpec((tr, N), lambda i: (i, 0)),
        out_shape=jax.ShapeDtypeStruct((M, N), F32),
        compiler_params=_params(("parallel",)),
    )(q)


def _reduce_scatter(gs, modes, dims, wire):
    c1 = jnp.reshape(lax.axis_index("c"), (1,)).astype(jnp.int32)
    flat = lambda a, lead: a.reshape(a.shape[:lead] + (-1, a.shape[-1]))
    others = _swap_layers(gs)
    hs = [_add_own(flat(g, 1), flat(o, 0), c1, wire[k], f"rs_add_own_{k}").reshape(o.shape)
          for k, (g, o) in enumerate(zip(gs, others))]
    qs = _scatter_chips(hs, modes, dims)
    fs = [_sum_chips(q, f"rs_sum_chips_{k}") for k, q in enumerate(qs)]
    return _join_layers(fs)


def _allreduce_small(v):
    R, C = v.shape

    def body(v_ref, sum_ref, all_ref, send_sems, recv_sems):
        x, y, c, _ = _place()
        me = 4 * x + 2 * y + c
        rows = lambda d: all_ref.at[pl.ds(pl.multiple_of(d * R, 8), R), :]

        def peer(k):
            flip = lambda bit, v: (1 - v) if ((k + 1) >> bit) & 1 else v
            return flip(2, x), flip(1, y), flip(0, c)

        outs = [_rcopy(v_ref, rows(me), send_sems, recv_sems, k, peer(k)) for k in range(7)]
        for cp in outs:
            cp.start()
        all_ref[pl.ds(pl.multiple_of(me * R, 8), R), :] = v_ref[...]
        for k in range(7):
            px, py, pc = peer(k)
            blk = rows(4 * px + 2 * py + pc)
            _rcopy(blk, blk, send_sems, recv_sems, k, (x, y, c)).wait_recv()
        for cp in outs:
            cp.wait_send()
        tot = all_ref[0:R, :]
        for d in range(1, 8):
            tot = tot + all_ref[d * R:(d + 1) * R, :]
        sum_ref[...] = tot

    vm = pl.BlockSpec(memory_space=pltpu.VMEM)
    return pl.pallas_call(
        body, name="allreduce_small", in_specs=[vm], out_specs=[vm, vm],
        out_shape=[jax.ShapeDtypeStruct((R, C), F32), jax.ShapeDtypeStruct((8 * R, C), F32)],
        scratch_shapes=[pltpu.SemaphoreType.DMA((7,)), pltpu.SemaphoreType.DMA((7,))],
    )(v)[0]


def _size(shape):
    n = 1
    for d in shape:
        n *= d
    return n


def _pack(pieces, dtype):
    flat = jnp.concatenate([p.astype(dtype).reshape(-1) for p in pieces])
    rows = -(-flat.shape[0] // (PACK_COLS * 16)) * 16
    return jnp.pad(flat, (0, rows * PACK_COLS - flat.shape[0])).reshape(rows, PACK_COLS)


def _unpack(buf, shapes):
    flat = buf.reshape(-1)
    out, pos = [], 0
    for s in shapes:
        n = _size(s)
        out.append(flat[pos:pos + n].reshape(s))
        pos += n
    return out


def _small_piece(name, arr, l):
    if name == "meta_tokens":
        return arr[l * (N_META // DEPTH):(l + 1) * (N_META // DEPTH)]
    return arr[l]


def _prep_w_in(w_in4):
    w_in = jnp.concatenate([w_in4[t] for t in range(4)], axis=1)
    col = lambda a, n: w_in[:, _R[a]:_R[a] + n]
    main = jnp.concatenate([col("qa", 3072), col("scb", 3072), col("qc", 3072), col("ga", 6144)], axis=1)
    zpad = lambda n: jnp.zeros((D_MODEL, n), w_in.dtype)
    side = jnp.concatenate([col("fa", 8), zpad(LANES - 8), col("glr", GLA_RANK), zpad(LANES - GLA_RANK)], axis=1)
    return main.astype(BF16), side.astype(BF16)


def _pad_rows(a, rows):
    return jnp.pad(a.astype(F32), ((0, rows - a.shape[0]), (0, 0)))


def _row2(v):
    return v.reshape(1, -1).astype(F32)


def _layer_fwd(h, p, rep, l, Lp, tm, ta):
    tag = lambda s: f"{s}_l{l}"
    g1, g2 = _row2(rep["norm1_g"][l]), _row2(rep["norm2_g"][l])
    bf = jnp.pad(_row2(rep["fox_b_f"][l]), ((0, 0), (0, LANES - FOX_HEADS)))
    gate_b, b_g, gnorm = _row2(rep["gate_b"][l]), _row2(rep["gla_b_g"][l]), _row2(rep["gla_norm_g"][l])
    (xn,) = _rw_fwd(tag("rms1_fwd"), _f_rms, [Row(h, D_MODEL)], [Const(g1)], [(D_MODEL, BF16)], Lp, BLOCK)
    main = _mm(xn, p["main"][l], "nn", BF16, tag("proj_main"))
    side = _mm(xn, p["side"][l], "nn", F32, tag("proj_side"))
    c = _fox_gate_fwd(side, bf, Lp)
    c_t = c[:, :FOX_HEADS].T
    c_col, c_row = c_t[:, :, None], c_t[:, None, :]
    oa, lse = _fox_fwd(main, c_col, c_row, Lp, ta)
    ya = _mm(oa, p["w_a_o"], "nn", BF16, tag("ya"), b_lead=l)
    ub = _sconv_fwd(main, p["conv_w"][l], Lp, tm)
    yb = _mm(ub, p["w_b_o"], "nn", BF16, tag("yb"), b_lead=l)
    glr = Row(side, LANES, lambda g: 1)
    (logg,) = _rw_fwd(tag("logg_fwd"), _f_logg, [glr], [Const(p["w_g2"][l]), Const(b_g)], [(512, F32)], Lp, tm)
    oc, states = _gla_fwd(main, logg, Lp)
    rc = Row(main, GLA_DV, lambda g: RC // GLA_DV + g)
    gn = Const(gnorm, (1, GLA_DV), lambda g: (0, g))
    (uc,) = _rw_fwd(tag("gla_post_fwd"), _f_gla_post, [Row(oc, GLA_DV), rc], [gn], [(GLA_DV, BF16)], Lp, tm,
                    G=GLA_HEADS)
    yc = _mm(uc, p["w_c_o"], "nn", BF16, tag("yc"), b_lead=l)
    cw = 512
    G = D_MODEL // cw
    mrows = [Row(ya, cw), Row(yb, cw), Row(yc, cw), Row(main, cw, lambda g: GA // cw + g),
             Row(main, cw, lambda g: GB // cw + g), Row(main, cw, lambda g: GC // cw + g)]
    mconsts = [Const(gate_b, (1, cw), lambda g, k=k: (0, k * G + g)) for k in range(3)]
    (mix,) = _rw_fwd(tag("merge_fwd"), _f_merge, mrows, mconsts, [(cw, BF16)], Lp, tm, G=G)
    h1 = _mm(mix, p["w_o"], "nn", F32, tag("h1"), add=h, b_lead=l)
    (xn2,) = _rw_fwd(tag("rms2_fwd"), _f_rms, [Row(h1, D_MODEL)], [Const(g2)], [(D_MODEL, BF16)], Lp, BLOCK)
    up = _mm(xn2, p["w_up"], "nn", BF16, tag("up"), b_lead=l)
    act = _mlp_act_fwd(up, p["mlp_conv_w"][l], Lp, tm)
    h2 = _mm(act, p["w_down"], "nn", F32, tag("h2"), add=h1, b_lead=l)
    res = dict(h=h, xn=xn, main=main, side=side, c_col=c_col, c_row=c_row, oa=oa, lse=lse, ya=ya, ub=ub, yb=yb,
               logg=logg, oc=oc, states=states, uc=uc, yc=yc, mix=mix, h1=h1, xn2=xn2, up=up, act=act,
               g1=g1, g2=g2, bf=bf, gate_b=gate_b, b_g=b_g, gnorm=gnorm)
    return h2, res


def _layer_bwd(dh2, p, r, l, Lp, tm, ta, big):
    tag = lambda s: f"{s}_l{l}"
    g = {}

    def wgrad(name, a, b):
        big[name] = _mm(a, b, "tn", F32, tag("d_" + name), slot=(big.get(name), l))

    wgrad("w_down", r["act"], dh2)
    dact = _mm(dh2, p["w_down"], "nt", BF16, tag("d_act"), b_lead=l)
    dgate, dval, dwg, dwu = _mlp_act_bwd(r["up"], p["mlp_conv_w"][l], dact, Lp, tm)
    g["mlp_conv_w"] = jnp.concatenate([dwg[:3], dwu[:3]], axis=1)
    dup = jnp.concatenate([dgate, dval], axis=1)
    wgrad("w_up", r["xn2"], dup)
    dxn2 = _mm(dup, p["w_up"], "nt", F32, tag("d_xn2"), b_lead=l)
    (dh1,), (dg2,) = _rw_bwd(tag("rms2_bwd"), _f_rms, [Row(r["h1"], D_MODEL)], [Const(r["g2"])],
                             [Row(dxn2, D_MODEL)], [F32], [dh2], Lp, BLOCK)
    g["norm2_g"] = dg2[0]
    wgrad("w_o", r["mix"], dh1)
    dmix = _mm(dh1, p["w_o"], "nt", BF16, tag("d_mix"), b_lead=l)
    cw = 512
    G = D_MODEL // cw
    main = r["main"]
    mrows = [Row(r["ya"], cw), Row(r["yb"], cw), Row(r["yc"], cw), Row(main, cw, lambda g_: GA // cw + g_),
             Row(main, cw, lambda g_: GB // cw + g_), Row(main, cw, lambda g_: GC // cw + g_)]
    mconsts = [Const(r["gate_b"], (1, cw), lambda g_, k=k: (0, k * G + g_)) for k in range(3)]
    (dya, dyb, dyc, dga, dgb, dgc), dbs = _rw_bwd(tag("merge_bwd"), _f_merge, mrows, mconsts, [Row(dmix, cw)],
                                                  [BF16] * 6, [None] * 6, Lp, tm, G=G)
    g["gate_b"] = jnp.concatenate([dbs[k][0, k * D_MODEL:(k + 1) * D_MODEL] for k in range(3)])
    wgrad("w_a_o", r["oa"], dya)
    doa = _mm(dya, p["w_a_o"], "nt", BF16, tag("d_oa"), b_lead=l)
    wgrad("w_b_o", r["ub"], dyb)
    dub = _mm(dyb, p["w_b_o"], "nt", BF16, tag("d_ub"), b_lead=l)
    wgrad("w_c_o", r["uc"], dyc)
    duc = _mm(dyc, p["w_c_o"], "nt", BF16, tag("d_uc"), b_lead=l)
    delta = _fox_delta(main, r["c_col"], r["c_row"], r["lse"], doa, Lp, ta)
    dq, dk, dv, dck = _fox_bwd(main, r["c_col"], r["c_row"], r["lse"], delta, doa, Lp, ta)
    dc = jnp.pad(dck[:, 0, :].T, ((0, 0), (0, LANES - FOX_HEADS)))
    dfa, dbf = _fox_gate_bwd(r["side"], r["bf"], dc, Lp)
    g["fox_b_f"] = dbf[0, :FOX_HEADS]
    dscb, dscc, dsch, dcw = _sconv_bwd(main, p["conv_w"][l], dub, Lp, tm)
    g["conv_w"] = dcw[:3]
    rc = Row(main, GLA_DV, lambda g_: RC // GLA_DV + g_)
    gn = Const(r["gnorm"], (1, GLA_DV), lambda g_: (0, g_))
    (doc, drc), (dgn,) = _rw_bwd(tag("gla_post_bwd"), _f_gla_post, [Row(r["oc"], GLA_DV), rc], [gn],
                                 [Row(duc, GLA_DV)], [F32, BF16], [None, None], Lp, tm, G=GLA_HEADS)
    g["gla_norm_g"] = dgn[0]
    dqc, dkc, dvc, dlogg = _gla_bwd(main, r["logg"], r["states"], doc, Lp)
    glr = Row(r["side"], LANES, lambda g_: 1)
    (dglr,), (dwg2, dbg) = _rw_bwd(tag("logg_bwd"), _f_logg, [glr], [Const(p["w_g2"][l]), Const(r["b_g"])],
                                   [Row(dlogg, 512)], [F32], [None], Lp, tm)
    g["gla_w_g2"] = dwg2[:GLA_RANK]
    g["gla_b_g"] = dbg[0]
    dmain = jnp.concatenate([dq, dk, dv, dscb, dscc, dsch, dqc, dkc, dvc, drc, dga, dgb, dgc], axis=1)
    dside = jnp.concatenate([dfa, dglr], axis=1)
    wgrad("main", r["xn"], dmain)
    wgrad("side", r["xn"], dside)
    dxn = _mm(dmain, p["main"][l], "nt", F32, tag("d_xn_main"))
    dxn = _mm(dside, p["side"][l], "nt", F32, tag("d_xn_side"), add=dxn)
    (dh,), (dg1,) = _rw_bwd(tag("rms1_bwd"), _f_rms, [Row(r["h"], D_MODEL)], [Const(r["g1"])], [Row(dxn, D_MODEL)],
                            [F32], [dh1], Lp, BLOCK)
    g["norm1_g"] = dg1[0]
    return dh, g


def _local_step(x, target, meta, p, rep):
    seq = x.shape[0]
    Lp = PAD + N_META + seq
    tm = _pick(Lp, (640, 384, 128))
    ta = tm
    h = jnp.concatenate([jnp.zeros((PAD, D_MODEL), F32), meta.astype(F32), x], axis=0)
    saved = []
    for l in range(DEPTH):
        h, res = _layer_fwd(h, p, rep, l, Lp, tm, ta)
        saved.append(res)
    loss, dh, dgf = _loss_head(h, _row2(rep["final_norm_g"]), target, Lp)
    big, small = {}, [None] * DEPTH
    for l in reversed(range(DEPTH)):
        dh, small[l] = _layer_bwd(dh, p, saved[l], l, Lp, tm, ta, big)
    return loss[0, 0], dh[BLOCK:], dh[PAD:BLOCK], big, small, dgf[0]


def kernel(x, meta_tokens, norm1_g, w_in, fox_b_f, gate_b, conv_w, gla_w_g2, gla_b_g, gla_norm_g, w_a_o, w_b_o, w_c_o, w_o, norm2_g, w_up, mlp_conv_w, w_down, final_norm_g, loss_target, m_meta_tokens, m_norm1_g, m_w_in, m_fox_b_f, m_gate_b, m_conv_w, m_gla_w_g2, m_gla_b_g, m_gla_norm_g, m_w_a_o, m_w_b_o, m_w_c_o, m_w_o, m_norm2_g, m_w_up, m_mlp_conv_w, m_w_down, m_final_norm_g, v_meta_tokens, v_norm1_g, v_w_in, v_fox_b_f, v_gate_b, v_conv_w, v_gla_w_g2, v_gla_b_g, v_gla_norm_g, v_w_a_o, v_w_b_o, v_w_c_o, v_w_o, v_norm2_g, v_w_up, v_mlp_conv_w, v_w_down, v_final_norm_g):
    given = dict(locals())
    weights = {n: given[n] for n in WEIGHT_ORDER}
    rep = {n: weights[n] for n, _ in REPLICATED}
    big_names = [n for n, _ in BIG]
    big_modes = [m for _, m in BIG] + ["stack"]
    small_shapes = [(s[0], s[1] // 4) for _, s in SMALL]
    exact = [k for k, (n, _) in enumerate(SMALL) if n in GATHER_F32]

    def small_wire(l):
        ws = [_small_piece(n, weights[n], l) for n, _ in SMALL]
        his = [w.astype(BF16) for w in ws]
        return his + [(ws[k] - his[k].astype(F32)).astype(BF16) for k in exact]

    shards = [weights[n].astype(BF16) for n in big_names] + [jnp.stack([_pack(small_wire(l), BF16) for l in range(DEPTH)])]
    gathered = _gather_weights(shards, big_modes)
    gw = dict(zip(big_names, gathered[:-1]))
    p = {n: gw[n] for n in big_names if n != "w_in"}
    p["main"], p["side"] = zip(*[_prep_w_in(gw["w_in"][l]) for l in range(DEPTH)])
    small_full = []
    for l in range(DEPTH):
        per_chip = [_unpack(gathered[-1][l, t], small_shapes + [small_shapes[k] for k in exact]) for t in range(4)]
        full = [jnp.concatenate([per_chip[t][k] for t in range(4)], axis=1).astype(F32) for k in range(len(per_chip[0]))]
        for e, k in enumerate(exact):
            full[k] = full[k] + full[len(SMALL) + e]
        small_full.append(dict(zip([n for n, _ in SMALL], full[:len(SMALL)])))
    p["conv_w"] = [_pad_rows(s["conv_w"], 8) for s in small_full]
    p["mlp_conv_w"] = [_pad_rows(s["mlp_conv_w"], 8) for s in small_full]
    p["w_g2"] = [_pad_rows(s["gla_w_g2"], LANES) for s in small_full]
    meta_full = jnp.concatenate([s["meta_tokens"] for s in small_full], axis=0)

    loss, grad_x, grad_meta, big, small, d_final = _local_step(x[0], loss_target[0], meta_full, p, rep)
    loss = lax.psum(loss, ("x", "y", "c"))

    dwm, dws = big.pop("main"), big.pop("side")
    d_w_in = jnp.concatenate([dwm[:, :, :3072], dws[:, :, :8], dwm[:, :, 3072:9216], dws[:, :, LANES:LANES + GLA_RANK],
                              dwm[:, :, 9216:]], axis=2)
    big["w_in"] = d_w_in.reshape(DEPTH, D_MODEL, 4, N_IN // 4).transpose(0, 2, 1, 3)
    for l in range(DEPTH):
        small[l]["meta_tokens"] = grad_meta[l * (N_META // DEPTH):(l + 1) * (N_META // DEPTH)]
    shard_of = lambda a, t: lax.slice_in_dim(a, t * (a.shape[1] // 4), (t + 1) * (a.shape[1] // 4), axis=1)
    small_g = jnp.stack([jnp.stack([_pack([shard_of(small[l][n], t) for n, _ in SMALL], F32) for t in range(4)])
                         for l in range(DEPTH)])
    dims = [shards[k].shape[1:] for k in range(len(BIG))] + [small_g.shape[2:]]
    summed = _reduce_scatter([big[n] for n in big_names] + [small_g], big_modes, dims,
                             [BF16] * len(BIG) + [F32])
    gout = dict(zip(big_names, summed[:-1]))
    pieces = [_unpack(summed[-1][l], small_shapes) for l in range(DEPTH)]
    for k, (n, _) in enumerate(SMALL):
        per_layer = [pieces[l][k] for l in range(DEPTH)]
        gout[n] = jnp.concatenate(per_layer, axis=0) if n == "meta_tokens" else jnp.stack(per_layer)

    rep_g = {n: (d_final if n == "final_norm_g" else jnp.stack([small[l][n] for l in range(DEPTH)])) for n, _ in REPLICATED}
    flat = jnp.concatenate([rep_g[n].astype(F32).reshape(-1) for n, _ in REPLICATED])
    rrows = -(-flat.shape[0] // (PACK_COLS * 8)) * 8
    summed_small = _allreduce_small(jnp.pad(flat, (0, rrows * PACK_COLS - flat.shape[0])).reshape(rrows, PACK_COLS))
    pos = 0
    for n, shape in REPLICATED:
        gout[n] = summed_small.reshape(-1)[pos:pos + _size(shape)].reshape(shape)
        pos += _size(shape)

    deltas, new_m, new_v = {}, {}, {}
    for n in WEIGHT_ORDER:
        deltas[n], new_m[n], new_v[n] = _adamw(weights[n], gout[n], given["m_" + n], given["v_" + n], "adamw_" + n)
    return (loss, grad_x[None], *[gout[n] for n in WEIGHT_ORDER], *[deltas[n] for n in WEIGHT_ORDER],
            *[new_m[n] for n in WEIGHT_ORDER], *[new_v[n] for n in WEIGHT_ORDER])
```

```python
import functools

import jax
import jax.numpy as jnp
from jax import lax
from jax.experimental import pallas as pl
from jax.experimental.pallas import tpu as pltpu

F32, BF16 = jnp.float32, jnp.bfloat16
HIGHEST = lax.Precision.HIGHEST
MESH = pl.DeviceIdType.MESH

N_META = 16
BLOCK = 128
LANES = 128
PAD = BLOCK - N_META
EPS = 1e-6
NEG = -1e30
HALO = 16
VMEM_LIMIT = 56 * 1024 * 1024
ADAM_BLOCK_BYTES = 1 << 20
EW_BLOCK_BYTES = 3 << 19

D_MODEL = 2048
FOX_HEADS, FOX_HD = 8, 128
FOX_WIDTH = FOX_HEADS * FOX_HD
CONV_CH = 1024
GLA_HEADS, GLA_DK, GLA_DV, GLA_RANK, GLA_TAU = 4, 128, 256, 16, 16.0
GLA_SUB = 32
D_FF = 5632
N_IN = 15384
DEPTH = 2

_R = dict(qa=0, ka=1024, va=2048, fa=3072, scb=3080, scc=4104, sch=5128, qc=6152, kc=6664,
          vc=7176, rc=8200, glr=9224, ga=9240, gb=11288, gc=13336)
QA, KA, VA, SCB, SCC, SCH, QC, KC, VC, RC, GA, GB, GC = (
    0, 1024, 2048, 3072, 4096, 5120, 6144, 6656, 7168, 8192, 9216, 11264, 13312)
N_MAIN = 15360
N_SIDE = 256

ADAM_LR, ADAM_B1, ADAM_B2, ADAM_EPS, ADAM_WD, ADAM_STEP = 0.001, 0.9, 0.999, 1e-08, 0.01, 10

BIG = (("w_in", "stack"), ("w_a_o", "cols"), ("w_b_o", "cols"), ("w_c_o", "cols"), ("w_o", "rows"), ("w_up", "cols"),
       ("w_down", "rows"))
SMALL = (("conv_w", (3, CONV_CH)), ("mlp_conv_w", (3, 2 * D_FF)), ("gla_w_g2", (GLA_RANK, 512)),
         ("meta_tokens", (N_META // DEPTH, D_MODEL)))
REPLICATED = (("norm1_g", (2, D_MODEL)), ("fox_b_f", (2, 8)), ("gate_b", (2, 3 * D_MODEL)), ("gla_b_g", (2, 512)),
              ("gla_norm_g", (2, 1024)), ("norm2_g", (2, D_MODEL)), ("final_norm_g", (D_MODEL,)))
WEIGHT_ORDER = ("meta_tokens", "norm1_g", "w_in", "fox_b_f", "gate_b", "conv_w", "gla_w_g2", "gla_b_g",
                "gla_norm_g", "w_a_o", "w_b_o", "w_c_o", "w_o", "norm2_g", "w_up", "mlp_conv_w", "w_down",
                "final_norm_g")
PACK_COLS = 1024
GATHER_F32 = ("conv_w", "mlp_conv_w", "meta_tokens")


def _pick(n, cands):
    for c in cands:
        if n % c == 0:
            return c
    return n


def _params(sem):
    return pltpu.CompilerParams(dimension_semantics=sem, vmem_limit_bytes=VMEM_LIMIT)


def _sigmoid(x):
    return jax.nn.sigmoid(x)


def _log_sigmoid(x):
    return jnp.minimum(x, 0.0) - jnp.log(1.0 + jnp.exp(-jnp.abs(x)))


def _mm(a, b, mode, out_dtype, name, add=None, b_lead=None, slot=None):
    bshape = b.shape if b_lead is None else b.shape[1:]
    if mode == "nn":
        (M, K), (K2, N) = a.shape, bshape
    elif mode == "nt":
        (M, K), (N, K2) = a.shape, bshape
    else:
        (K, M), (K2, N) = a.shape, bshape
    assert K == K2, (name, a.shape, b.shape)
    if mode == "tn":
        tm = _pick(M, (2048, 1408, 1024, 512, 256, 128))
        tn = _pick(N, (1024, 512, 256, 128))
        tk = _pick(K, (640, 512, 384, 256, 128))
    else:
        tm = _pick(M, (1664, 640, 384, 128))
        tn = _pick(N, (512, 256, 128))
        tk = K if K <= 2048 else _pick(K, (1408, 1024, 512, 256, 128))
    nk = K // tk
    dims = {"nn": (((1,), (0,)), ((), ())), "nt": (((1,), (1,)), ((), ())), "tn": (((0,), (0,)), ((), ()))}[mode]
    n_in = 2 + (add is not None) + (slot is not None and slot[0] is not None)

    def body(*refs):
        a_ref, b_ref = refs[:2]
        add_ref = refs[2] if add is not None else None
        o_ref, acc = refs[n_in:]
        k = pl.program_id(2)

        @pl.when(k == 0)
        def _():
            acc[...] = jnp.zeros_like(acc)

        acc[...] += lax.dot_general(a_ref[...].astype(BF16), b_ref[...].astype(BF16), dims,
                                    preferred_element_type=F32)

        @pl.when(k == nk - 1)
        def _():
            r = acc[...]
            if add is not None:
                r = r + add_ref[...].astype(F32)
            o_ref[...] = r.astype(o_ref.dtype)

    a_spec = {"nn": pl.BlockSpec((tm, tk), lambda i, j, k: (i, k)),
              "nt": pl.BlockSpec((tm, tk), lambda i, j, k: (i, k)),
              "tn": pl.BlockSpec((tk, tm), lambda i, j, k: (k, i))}[mode]
    b_blk, b_idx = {"nn": ((tk, tn), lambda i, j, k: (k, j)),
                    "nt": ((tn, tk), lambda i, j, k: (j, k)),
                    "tn": ((tk, tn), lambda i, j, k: (k, j))}[mode]
    if b_lead is None:
        b_spec = pl.BlockSpec(b_blk, b_idx)
    else:
        b_spec = pl.BlockSpec((None,) + b_blk, lambda i, j, k: (b_lead,) + b_idx(i, j, k))
    o_spec = pl.BlockSpec((tm, tn), lambda i, j, k: (i, j))
    ins, specs = [a, b], [a_spec, b_spec]
    if add is not None:
        ins.append(add)
        specs.append(o_spec)
    aliases = {}
    out_shape = jax.ShapeDtypeStruct((M, N), out_dtype)
    if slot is not None:
        buf, l = slot
        o_spec = pl.BlockSpec((None, tm, tn), lambda i, j, k: (l, i, j))
        out_shape = jax.ShapeDtypeStruct((DEPTH, M, N), out_dtype)
        if buf is not None:
            aliases = {len(ins): 0}
            ins.append(buf)
            specs.append(pl.BlockSpec(memory_space=pl.ANY))
    return pl.pallas_call(
        body, name=name, grid=(M // tm, N // tn, nk), in_specs=specs, out_specs=o_spec, out_shape=out_shape,
        scratch_shapes=[pltpu.VMEM((tm, tn), F32)], input_output_aliases=aliases,
        compiler_params=_params(("parallel", "parallel", "arbitrary")),
    )(*ins)


class Row:
    def __init__(self, arr, w, cb=None):
        self.arr, self.w, self.cb = arr, w, (cb if cb is not None else (lambda g: g))


class Const:
    def __init__(self, arr, shape=None, idx=None):
        self.arr = arr
        self.shape = shape if shape is not None else arr.shape
        self.idx = idx if idx is not None else (lambda g: (0,) * arr.ndim)


def _row_spec(r, tm):
    return pl.BlockSpec((tm, r.w), lambda g, i, r=r: (i, r.cb(g)))


def _const_spec(c):
    return pl.BlockSpec(c.shape, lambda g, i, c=c: c.idx(g))


def _valid_rows(i, tm):
    return (i * tm + lax.broadcasted_iota(jnp.int32, (tm, 1), 0)) >= PAD


def _rw_fwd(name, f, rows, consts, outs, Lp, tm, G=1):
    nr, nc = len(rows), len(consts)

    def body(*refs):
        i = pl.program_id(1)
        rv = [r[...].astype(F32) for r in refs[:nr]]
        cv = [r[...].astype(F32) for r in refs[nr:nr + nc]]
        res = f(_valid_rows(i, tm), *rv, *cv)
        for o_ref, v in zip(refs[nr + nc:], res):
            o_ref[...] = v.astype(o_ref.dtype)

    return pl.pallas_call(
        body, name=name, grid=(G, Lp // tm),
        in_specs=[_row_spec(r, tm) for r in rows] + [_const_spec(c) for c in consts],
        out_specs=[pl.BlockSpec((tm, w), lambda g, i: (i, g)) for w, _ in outs],
        out_shape=[jax.ShapeDtypeStruct((Lp, w * G), dt) for w, dt in outs],
        compiler_params=_params(("parallel", "arbitrary")),
    )(*[r.arr for r in rows], *[c.arr for c in consts])


def _rw_bwd(name, f, rows, consts, cts, drow_dtypes, adds, Lp, tm, G=1):
    nr, nc, nt = len(rows), len(consts), len(cts)
    want = [k for k, dt in enumerate(drow_dtypes) if dt is not None]
    add_k = [k for k in want if adds[k] is not None]

    def body(*refs):
        i = pl.program_id(1)
        pos = 0
        rv = [r[...].astype(F32) for r in refs[pos:pos + nr]]
        pos += nr
        cv = [r[...].astype(F32) for r in refs[pos:pos + nc]]
        pos += nc
        tv = [r[...].astype(F32) for r in refs[pos:pos + nt]]
        pos += nt
        av = {k: refs[pos + n][...].astype(F32) for n, k in enumerate(add_k)}
        pos += len(add_k)
        drow_refs = refs[pos:pos + len(want)]
        pos += len(want)
        dconst_refs = refs[pos:pos + nc]
        valid = _valid_rows(i, tm)
        _, vjp = jax.vjp(lambda *a: tuple(f(valid, *a)), *rv, *cv)
        grads = vjp(tuple(tv))
        for o_ref, k in zip(drow_refs, want):
            gk = grads[k]
            if k in av:
                gk = gk + av[k]
            o_ref[...] = gk.astype(o_ref.dtype)
        for n, o_ref in enumerate(dconst_refs):
            gc = grads[nr + n]

            @pl.when(i == 0)
            def _(o_ref=o_ref, gc=gc):
                o_ref[...] = gc

            @pl.when(i > 0)
            def _(o_ref=o_ref, gc=gc):
                o_ref[...] += gc

    out_row = lambda w: pl.BlockSpec((tm, w), lambda g, i: (i, g))
    res = pl.pallas_call(
        body, name=name, grid=(G, Lp // tm),
        in_specs=([_row_spec(r, tm) for r in rows] + [_const_spec(c) for c in consts]
                  + [_row_spec(r, tm) for r in cts] + [out_row(rows[k].w) for k in add_k]),
        out_specs=[out_row(rows[k].w) for k in want] + [_const_spec(c) for c in consts],
        out_shape=([jax.ShapeDtypeStruct((Lp, rows[k].w * G), drow_dtypes[k]) for k in want]
                   + [jax.ShapeDtypeStruct(c.arr.shape, F32) for c in consts]),
        compiler_params=_params(("parallel", "arbitrary")),
    )(*[r.arr for r in rows], *[c.arr for c in consts], *[r.arr for r in cts], *[adds[k] for k in add_k])
    drows = [None] * nr
    for n, k in enumerate(want):
        drows[k] = res[n]
    return drows, list(res[len(want):])


def _f_rms(valid, h, g):
    r = lax.rsqrt(jnp.mean(h * h, axis=-1, keepdims=True) + EPS)
    return (jnp.where(valid, h * r * g, 0.0),)


def _f_logg(valid, glr, w, b):
    pre = jnp.dot(glr.astype(BF16), w.astype(BF16), preferred_element_type=F32) + b
    return (jnp.where(valid, _log_sigmoid(pre) / GLA_TAU, 0.0),)


def _f_gla_post(valid, oc, rc, g):
    y = oc * lax.rsqrt(jnp.mean(oc * oc, axis=-1, keepdims=True) + EPS) * g
    return (jnp.where(valid, rc * _sigmoid(rc) * y, 0.0),)


def _f_merge(valid, ya, yb, yc, ga, gb, gc, ba, bb, bc):
    mix = _sigmoid(ga + ba) * ya + _sigmoid(gb + bb) * yb + _sigmoid(gc + bc) * yc
    return (jnp.where(valid, mix, 0.0),)


def _fox_gate_fwd(side, bf, Lp):
    t = BLOCK
    n = Lp // t

    def body(s_ref, b_ref, c_ref, carry):
        i = pl.program_id(0)

        @pl.when(i == 0)
        def _():
            carry[...] = jnp.zeros_like(carry)

        lane = lax.broadcasted_iota(jnp.int32, (t, LANES), 1)
        ok = _valid_rows(i, t) & (lane < FOX_HEADS)
        logf = jnp.where(ok, _log_sigmoid(s_ref[...] + b_ref[...]), 0.0)
        tril = (lax.broadcasted_iota(jnp.int32, (t, t), 1) <= lax.broadcasted_iota(jnp.int32, (t, t), 0)).astype(F32)
        c = jnp.dot(tril, logf, precision=HIGHEST, preferred_element_type=F32) + carry[...]
        c_ref[...] = c
        carry[...] = c[t - 1:t, :]

    return pl.pallas_call(
        body, name="fox_gate_fwd", grid=(n,),
        in_specs=[pl.BlockSpec((t, LANES), lambda i: (i, 0)), pl.BlockSpec((1, LANES), lambda i: (0, 0))],
        out_specs=pl.BlockSpec((t, LANES), lambda i: (i, 0)),
        out_shape=jax.ShapeDtypeStruct((Lp, LANES), F32),
        scratch_shapes=[pltpu.VMEM((1, LANES), F32)],
        compiler_params=_params(("arbitrary",)),
    )(side, bf)


def _fox_gate_bwd(side, bf, dc, Lp):
    t = BLOCK
    n = Lp // t

    def body(s_ref, b_ref, dc_ref, dfa_ref, db_ref, carry):
        i = pl.program_id(0)

        @pl.when(i == 0)
        def _():
            carry[...] = jnp.zeros_like(carry)

        lane = lax.broadcasted_iota(jnp.int32, (t, LANES), 1)
        ok = _valid_rows(n - 1 - i, t) & (lane < FOX_HEADS)
        triu = (lax.broadcasted_iota(jnp.int32, (t, t), 1) >= lax.broadcasted_iota(jnp.int32, (t, t), 0)).astype(F32)
        dlogf = jnp.dot(triu, dc_ref[...], precision=HIGHEST, preferred_element_type=F32) + carry[...]
        carry[...] = dlogf[0:1, :]
        dpre = jnp.where(ok, dlogf * _sigmoid(-(s_ref[...] + b_ref[...])), 0.0)
        dfa_ref[...] = dpre
        part = jnp.sum(dpre, axis=0, keepdims=True)

        @pl.when(i == 0)
        def _():
            db_ref[...] = part

        @pl.when(i > 0)
        def _():
            db_ref[...] += part

    rev = lambda i: (n - 1 - i, 0)
    return pl.pallas_call(
        body, name="fox_gate_bwd", grid=(n,),
        in_specs=[pl.BlockSpec((t, LANES), rev), pl.BlockSpec((1, LANES), lambda i: (0, 0)),
                  pl.BlockSpec((t, LANES), rev)],
        out_specs=[pl.BlockSpec((t, LANES), rev), pl.BlockSpec((1, LANES), lambda i: (0, 0))],
        out_shape=[jax.ShapeDtypeStruct((Lp, LANES), F32), jax.ShapeDtypeStruct((1, LANES), F32)],
        scratch_shapes=[pltpu.VMEM((1, LANES), F32)],
        compiler_params=_params(("arbitrary",)),
    )(side, bf, dc)


def _fox_key_bias(cq_ref, ck_ref, j, t):
    col = j * t + lax.broadcasted_iota(jnp.int32, (1, t), 1)
    return jnp.where(col >= PAD, ck_ref[...] - cq_ref[0:1, :], -NEG)


def _fox_s(q, k, bias, diagonal, t):
    s = lax.dot_general(q, k, (((1,), (1,)), ((), ())), preferred_element_type=F32) * (FOX_HD ** -0.5) - bias
    if diagonal:
        causal = lax.broadcasted_iota(jnp.int32, (t, t), 1) <= lax.broadcasted_iota(jnp.int32, (t, t), 0)
        s = jnp.where(causal, s, NEG)
    return s


def _fox_fwd(main, c_col, c_row, Lp, t):
    n = Lp // t
    qb, kb, vb = QA // FOX_HD, KA // FOX_HD, VA // FOX_HD

    def body(q_ref, k_ref, v_ref, cq_ref, ck_ref, o_ref, lse_ref, m_s, l_s, acc):
        i, j = pl.program_id(1), pl.program_id(2)

        @pl.when(j == 0)
        def _():
            m_s[...] = jnp.full_like(m_s, NEG)
            l_s[...] = jnp.zeros_like(l_s)
            acc[...] = jnp.zeros_like(acc)

        def update(diagonal):
            s = _fox_s(q_ref[...], k_ref[...], _fox_key_bias(cq_ref, ck_ref, j, t), diagonal, t)
            m_new = jnp.maximum(m_s[...], jnp.max(s, axis=1, keepdims=True))
            alpha = jnp.exp(m_s[...] - m_new)
            p = jnp.exp(s - m_new)
            l_s[...] = alpha * l_s[...] + jnp.sum(p, axis=1, keepdims=True)
            acc[...] = alpha * acc[...] + jnp.dot(p.astype(BF16), v_ref[...], preferred_element_type=F32)
            m_s[...] = m_new

        @pl.when(j < i)
        def _():
            update(False)

        @pl.when(j == i)
        def _():
            update(True)
            o_ref[...] = jnp.where(_valid_rows(i, t), acc[...] / l_s[...], 0.0).astype(o_ref.dtype)
            lse_ref[...] = m_s[...] + jnp.log(l_s[...])

    kv = lambda base: pl.BlockSpec((t, FOX_HD), lambda h, i, j: (jnp.minimum(j, i), base + h))
    return pl.pallas_call(
        body, name="fox_fwd", grid=(FOX_HEADS, n, n),
        in_specs=[pl.BlockSpec((t, FOX_HD), lambda h, i, j: (i, qb + h)), kv(kb), kv(vb),
                  pl.BlockSpec((None, t, 1), lambda h, i, j: (h, i, 0)),
                  pl.BlockSpec((None, 1, t), lambda h, i, j: (h, 0, jnp.minimum(j, i)))],
        out_specs=[pl.BlockSpec((t, FOX_HD), lambda h, i, j: (i, h)),
                   pl.BlockSpec((None, t, 1), lambda h, i, j: (h, i, 0))],
        out_shape=[jax.ShapeDtypeStruct((Lp, FOX_WIDTH), BF16), jax.ShapeDtypeStruct((FOX_HEADS, Lp, 1), F32)],
        scratch_shapes=[pltpu.VMEM((t, 1), F32), pltpu.VMEM((t, 1), F32), pltpu.VMEM((t, FOX_HD), F32)],
        compiler_params=_params(("parallel", "parallel", "arbitrary")),
    )(main, main, main, c_col, c_row)


def _fox_p_dp(q_ref, k_ref, v_ref, cq_ref, ck_ref, lse_ref, do_ref, j, diagonal, t):
    s = _fox_s(q_ref[...], k_ref[...], _fox_key_bias(cq_ref, ck_ref, j, t), diagonal, t)
    p = jnp.exp(s - lse_ref[...])
    dp = lax.dot_general(do_ref[...], v_ref[...], (((1,), (1,)), ((), ())), preferred_element_type=F32)
    return p, dp


def _fox_delta(main, c_col, c_row, lse, doa, Lp, t):
    n = Lp // t
    qb, kb, vb = QA // FOX_HD, KA // FOX_HD, VA // FOX_HD

    def body(q_ref, k_ref, v_ref, cq_ref, ck_ref, lse_ref, do_ref, dl_ref, dl_s):
        i, j = pl.program_id(1), pl.program_id(2)

        @pl.when(j == 0)
        def _():
            dl_s[...] = jnp.zeros_like(dl_s)

        def sweep(diagonal):
            p, dp = _fox_p_dp(q_ref, k_ref, v_ref, cq_ref, ck_ref, lse_ref, do_ref, j, diagonal, t)
            dl_s[...] += jnp.sum(p * dp, axis=1, keepdims=True)

        @pl.when(j < i)
        def _():
            sweep(False)

        @pl.when(j == i)
        def _():
            sweep(True)
            dl_ref[...] = dl_s[...]

    kv = lambda base: pl.BlockSpec((t, FOX_HD), lambda h, i, j: (jnp.minimum(j, i), base + h))
    qrow = lambda base: pl.BlockSpec((t, FOX_HD), lambda h, i, j: (i, base + h))
    col = pl.BlockSpec((None, t, 1), lambda h, i, j: (h, i, 0))
    return pl.pallas_call(
        body, name="fox_delta", grid=(FOX_HEADS, n, n),
        in_specs=[qrow(qb), kv(kb), kv(vb), col,
                  pl.BlockSpec((None, 1, t), lambda h, i, j: (h, 0, jnp.minimum(j, i))), col, qrow(0)],
        out_specs=col,
        out_shape=jax.ShapeDtypeStruct((FOX_HEADS, Lp, 1), F32),
        scratch_shapes=[pltpu.VMEM((t, 1), F32)],
        compiler_params=_params(("parallel", "parallel", "arbitrary")),
    )(main, main, main, c_col, c_row, lse, doa)


def _fox_bwd(main, c_col, c_row, lse, delta, doa, Lp, t):
    n = Lp // t
    qb, kb, vb = QA // FOX_HD, KA // FOX_HD, VA // FOX_HD
    scale = FOX_HD ** -0.5

    def body(q_ref, k_ref, v_ref, cq_ref, ck_ref, lse_ref, dl_ref, do_ref, dq_ref, dk_ref, dv_ref, dck_ref,
             dq_s, dk_s, dv_s, dc_s):
        j, i = pl.program_id(1), pl.program_id(2)

        @pl.when(jnp.logical_and(j == 0, i == 0))
        def _():
            dq_s[...] = jnp.zeros_like(dq_s)

        @pl.when(i == 0)
        def _():
            dk_s[...] = jnp.zeros_like(dk_s)
            dv_s[...] = jnp.zeros_like(dv_s)
            dc_s[...] = jnp.zeros_like(dc_s)

        def sweep(diagonal):
            p, dp = _fox_p_dp(q_ref, k_ref, v_ref, cq_ref, ck_ref, lse_ref, do_ref, j, diagonal, t)
            ds = p * (dp - dl_ref[...])
            dsb = ds.astype(BF16)
            tn = (((0,), (0,)), ((), ()))
            dv_s[...] += lax.dot_general(p.astype(BF16), do_ref[...], tn, preferred_element_type=F32)
            dk_s[...] += lax.dot_general(dsb, q_ref[...], tn, preferred_element_type=F32)
            dc_s[...] -= jnp.sum(ds, axis=0, keepdims=True)
            rows = pl.ds(pl.multiple_of(i * t, t), t)
            dq_s[rows, :] += jnp.dot(dsb, k_ref[...], preferred_element_type=F32)

        @pl.when(i > j)
        def _():
            sweep(False)

        @pl.when(i == j)
        def _():
            sweep(True)

        @pl.when(i == n - 1)
        def _():
            dk_ref[...] = (dk_s[...] * scale).astype(dk_ref.dtype)
            dv_ref[...] = dv_s[...].astype(dv_ref.dtype)
            dck_ref[...] = dc_s[...]

        @pl.when(jnp.logical_and(j == n - 1, i == n - 1))
        def _():
            dq_ref[...] = (dq_s[...] * scale).astype(dq_ref.dtype)

    qrow = lambda base: pl.BlockSpec((t, FOX_HD), lambda h, j, i: (jnp.maximum(i, j), base + h))
    kv = lambda base: pl.BlockSpec((t, FOX_HD), lambda h, j, i: (j, base + h))
    col = pl.BlockSpec((None, t, 1), lambda h, j, i: (h, jnp.maximum(i, j), 0))
    row = pl.BlockSpec((None, 1, t), lambda h, j, i: (h, 0, j))
    wide = jax.ShapeDtypeStruct((Lp, FOX_WIDTH), BF16)
    return pl.pallas_call(
        body, name="fox_bwd", grid=(FOX_HEADS, n, n),
        in_specs=[qrow(qb), kv(kb), kv(vb), col, row, col, col, qrow(0)],
        out_specs=[pl.BlockSpec((Lp, FOX_HD), lambda h, j, i: (0, h)), kv(0), kv(0), row],
        out_shape=[wide, wide, wide, jax.ShapeDtypeStruct((FOX_HEADS, 1, Lp), F32)],
        scratch_shapes=[pltpu.VMEM((Lp, FOX_HD), F32), pltpu.VMEM((t, FOX_HD), F32), pltpu.VMEM((t, FOX_HD), F32),
                        pltpu.VMEM((1, t), F32)],
        compiler_params=_params(("parallel", "arbitrary", "arbitrary")),
    )(main, main, main, c_col, c_row, lse, delta, doa)


def _shift_down(x, n):
    return pltpu.roll(x, n, 0)


def _shift_up(x, n):
    return pltpu.roll(x, x.shape[0] - n, 0)


def _prev_spec(tm, ct, cb):
    return pl.BlockSpec((HALO, ct), lambda g, i: (jnp.maximum(i * (tm // HALO) - 1, 0), cb(g)))


def _next_spec(tm, ct, cb, nrows):
    last = nrows // HALO - 1
    return pl.BlockSpec((HALO, ct), lambda g, i: (jnp.minimum((i + 1) * (tm // HALO), last), cb(g)))


def _cur_spec(tm, ct, cb):
    return pl.BlockSpec((tm, ct), lambda g, i: (i, cb(g)))


def _wrow(w_ref, k):
    return w_ref[k:k + 1, :]


def _rows3(s0, s1, s2, ct):
    r = lax.broadcasted_iota(jnp.int32, (8, ct), 0)
    return jnp.where(r == 0, s0, jnp.where(r == 1, s1, jnp.where(r == 2, s2, 0.0)))


def _acc_out(ref, i, val):
    @pl.when(i == 0)
    def _():
        ref[...] = val

    @pl.when(i > 0)
    def _():
        ref[...] += val


def _sconv_fwd(main, w8, Lp, tm):
    ct = 256
    G = CONV_CH // ct
    bb, cb, hb = (lambda g: SCB // ct + g), (lambda g: SCC // ct + g), (lambda g: SCH // ct + g)

    def body(b_ref, c_ref, h_ref, cp_ref, hp_ref, w_ref, o_ref):
        i = pl.program_id(1)
        z = c_ref[...].astype(F32) * h_ref[...].astype(F32)
        zp = jnp.where(i > 0, cp_ref[...].astype(F32) * hp_ref[...].astype(F32), 0.0)
        zz = jnp.concatenate([zp, z], axis=0)
        cz = (_wrow(w_ref, 0) * _shift_down(zz, 2)[HALO:] + _wrow(w_ref, 1) * _shift_down(zz, 1)[HALO:]
              + _wrow(w_ref, 2) * z)
        o_ref[...] = (b_ref[...].astype(F32) * cz).astype(o_ref.dtype)

    return pl.pallas_call(
        body, name="sconv_fwd", grid=(G, Lp // tm),
        in_specs=[_cur_spec(tm, ct, bb), _cur_spec(tm, ct, cb), _cur_spec(tm, ct, hb),
                  _prev_spec(tm, ct, cb), _prev_spec(tm, ct, hb), pl.BlockSpec((8, ct), lambda g, i: (0, g))],
        out_specs=pl.BlockSpec((tm, ct), lambda g, i: (i, g)),
        out_shape=jax.ShapeDtypeStruct((Lp, CONV_CH), BF16),
        compiler_params=_params(("parallel", "arbitrary")),
    )(main, main, main, main, main, w8)


def _sconv_bwd(main, w8, dub, Lp, tm):
    ct = 256
    G = CONV_CH // ct
    n = Lp // tm
    bb, cb, hb, ob = (lambda g: SCB // ct + g), (lambda g: SCC // ct + g), (lambda g: SCH // ct + g), (lambda g: g)

    def body(b_ref, c_ref, h_ref, cp_ref, hp_ref, bn_ref, d_ref, dn_ref, w_ref, db_ref, dc_ref, dh_ref, dw_ref):
        i = pl.program_id(1)
        b, c, h = b_ref[...].astype(F32), c_ref[...].astype(F32), h_ref[...].astype(F32)
        z = c * h
        zp = jnp.where(i > 0, cp_ref[...].astype(F32) * hp_ref[...].astype(F32), 0.0)
        zz = jnp.concatenate([zp, z], axis=0)
        z1, z2 = _shift_down(zz, 1)[HALO:], _shift_down(zz, 2)[HALO:]
        w0, w1, w2 = _wrow(w_ref, 0), _wrow(w_ref, 1), _wrow(w_ref, 2)
        cz = w0 * z2 + w1 * z1 + w2 * z
        dub_c = d_ref[...].astype(F32)
        db_ref[...] = (dub_c * cz).astype(db_ref.dtype)
        dcz = dub_c * b
        dcz_n = jnp.where(i < n - 1, dn_ref[...].astype(F32) * bn_ref[...].astype(F32), 0.0)
        dd = jnp.concatenate([dcz, dcz_n], axis=0)
        dz = w2 * dcz + w1 * _shift_up(dd, 1)[:tm] + w0 * _shift_up(dd, 2)[:tm]
        dc_ref[...] = (dz * h).astype(dc_ref.dtype)
        dh_ref[...] = (dz * c).astype(dh_ref.dtype)
        s = lambda x: jnp.sum(dcz * x, axis=0, keepdims=True)
        _acc_out(dw_ref, i, _rows3(s(z2), s(z1), s(z), ct))

    out = pl.BlockSpec((tm, ct), lambda g, i: (i, g))
    return pl.pallas_call(
        body, name="sconv_bwd", grid=(G, n),
        in_specs=[_cur_spec(tm, ct, bb), _cur_spec(tm, ct, cb), _cur_spec(tm, ct, hb),
                  _prev_spec(tm, ct, cb), _prev_spec(tm, ct, hb), _next_spec(tm, ct, bb, Lp),
                  _cur_spec(tm, ct, ob), _next_spec(tm, ct, ob, Lp), pl.BlockSpec((8, ct), lambda g, i: (0, g))],
        out_specs=[out, out, out, pl.BlockSpec((8, ct), lambda g, i: (0, g))],
        out_shape=[jax.ShapeDtypeStruct((Lp, CONV_CH), BF16)] * 3 + [jax.ShapeDtypeStruct((8, CONV_CH), F32)],
        compiler_params=_params(("parallel", "arbitrary")),
    )(main, main, main, main, main, main, dub, dub, w8)


def _conv3(w_ref, ext):
    return _wrow(w_ref, 0) * _shift_down(ext, 2) + _wrow(w_ref, 1) * _shift_down(ext, 1) + _wrow(w_ref, 2) * ext


def _mlp_act_fwd(up, w8, Lp, tm):
    ct = 256
    G = D_FF // ct
    gb, ub = (lambda g: g), (lambda g: G + g)

    def body(g_ref, u_ref, gp_ref, up_ref, wg_ref, wu_ref, o_ref):
        i = pl.program_id(1)

        def conv(cur, prev, w_ref):
            ext = jnp.concatenate([jnp.where(i > 0, prev[...].astype(F32), 0.0), cur[...].astype(F32)], axis=0)
            return _conv3(w_ref, ext)[HALO:]

        ug, uu = conv(g_ref, gp_ref, wg_ref), conv(u_ref, up_ref, wu_ref)
        o_ref[...] = (ug * _sigmoid(ug) * uu).astype(o_ref.dtype)

    wspec = lambda cb: pl.BlockSpec((8, ct), lambda g, i: (0, cb(g)))
    return pl.pallas_call(
        body, name="mlp_act_fwd", grid=(G, Lp // tm),
        in_specs=[_cur_spec(tm, ct, gb), _cur_spec(tm, ct, ub), _prev_spec(tm, ct, gb), _prev_spec(tm, ct, ub),
                  wspec(gb), wspec(ub)],
        out_specs=pl.BlockSpec((tm, ct), lambda g, i: (i, g)),
        out_shape=jax.ShapeDtypeStruct((Lp, D_FF), BF16),
        compiler_params=_params(("parallel", "arbitrary")),
    )(up, up, up, up, w8, w8)


def _mlp_act_bwd(up, w8, da, Lp, tm):
    ct = 256
    G = D_FF // ct
    n = Lp // tm
    gb, ub, ob = (lambda g: g), (lambda g: G + g), (lambda g: g)

    def body(g_ref, u_ref, gp_ref, up_ref, gn_ref, un_ref, d_ref, dn_ref, wg_ref, wu_ref,
             dg_ref, du_ref, dwg_ref, dwu_ref):
        i = pl.program_id(1)

        def ext_of(prev, cur, nxt):
            return jnp.concatenate([jnp.where(i > 0, prev[...].astype(F32), 0.0), cur[...].astype(F32),
                                    jnp.where(i < n - 1, nxt[...].astype(F32), 0.0)], axis=0)

        eg, eu = ext_of(gp_ref, g_ref, gn_ref), ext_of(up_ref, u_ref, un_ref)
        da_e = jnp.concatenate([jnp.zeros((HALO, ct), F32), d_ref[...].astype(F32),
                                jnp.where(i < n - 1, dn_ref[...].astype(F32), 0.0)], axis=0)
        ug, uu = _conv3(wg_ref, eg), _conv3(wu_ref, eu)
        sg = _sigmoid(ug)
        dug = da_e * uu * (sg * (1.0 + ug * (1.0 - sg)))
        duu = da_e * (ug * sg)
        cur = slice(HALO, HALO + tm)

        def back(w_ref, dx, e, dx_ref, dw_ref):
            d_in = _wrow(w_ref, 2) * dx + _wrow(w_ref, 1) * _shift_up(dx, 1) + _wrow(w_ref, 0) * _shift_up(dx, 2)
            dx_ref[...] = d_in[cur].astype(dx_ref.dtype)
            s = lambda x: jnp.sum(dx[cur] * x[cur], axis=0, keepdims=True)
            _acc_out(dw_ref, i, _rows3(s(_shift_down(e, 2)), s(_shift_down(e, 1)), s(e), ct))

        back(wg_ref, dug, eg, dg_ref, dwg_ref)
        back(wu_ref, duu, eu, du_ref, dwu_ref)

    wspec = lambda cb: pl.BlockSpec((8, ct), lambda g, i: (0, cb(g)))
    out = pl.BlockSpec((tm, ct), lambda g, i: (i, g))
    return pl.pallas_call(
        body, name="mlp_act_bwd", grid=(G, n),
        in_specs=[_cur_spec(tm, ct, gb), _cur_spec(tm, ct, ub), _prev_spec(tm, ct, gb), _prev_spec(tm, ct, ub),
                  _next_spec(tm, ct, gb, Lp), _next_spec(tm, ct, ub, Lp), _cur_spec(tm, ct, ob),
                  _next_spec(tm, ct, ob, Lp), wspec(gb), wspec(ub)],
        out_specs=[out, out, wspec(ob), wspec(ob)],
        out_shape=[jax.ShapeDtypeStruct((Lp, D_FF), BF16)] * 2 + [jax.ShapeDtypeStruct((8, D_FF), F32)] * 2,
        compiler_params=_params(("parallel", "arbitrary")),
    )(up, up, up, up, up, up, da, da, w8, w8)


def _gla_chunk(q, k, v, g, s0):
    C = BLOCK
    r_i = lax.broadcasted_iota(jnp.int32, (C, C), 0)
    c_i = lax.broadcasted_iota(jnp.int32, (C, C), 1)
    hdot = functools.partial(jnp.dot, precision=HIGHEST, preferred_element_type=F32)
    b = hdot((c_i <= r_i).astype(F32), g)
    sub_start = jnp.bitwise_and(r_i, -GLA_SUB)
    ref_all = hdot((c_i == sub_start).astype(F32), b)
    qs = q * (GLA_DK ** -0.5)
    qt = (qs * jnp.exp(b - ref_all)).astype(BF16)
    att = jnp.zeros((C, C), F32)
    for n in range(C // GLA_SUB):
        ref_n = hdot((c_i == n * GLA_SUB).astype(F32), b)
        kt = (k * jnp.exp(jnp.minimum(ref_n - b, 60.0))).astype(BF16)
        a_n = lax.dot_general(qt, kt, (((1,), (1,)), ((), ())), preferred_element_type=F32)
        att = att + jnp.where((sub_start == n * GLA_SUB) & (c_i <= r_i), a_n, 0.0)
    o = (jnp.dot(att.astype(BF16), v.astype(BF16), preferred_element_type=F32)
         + jnp.dot((qs * jnp.exp(b)).astype(BF16), s0.astype(BF16), preferred_element_type=F32))
    b_last = hdot((c_i == C - 1).astype(F32), b)
    kd = (k * jnp.exp(b_last - b)).astype(BF16)
    last_rows = (lax.broadcasted_iota(jnp.int32, (C, GLA_DV), 0) == C - 1).astype(F32)
    decay = lax.dot_general(b, last_rows, (((0,), (0,)), ((), ())), precision=HIGHEST,
                            preferred_element_type=F32)
    s1 = jnp.exp(decay) * s0 + lax.dot_general(kd, v.astype(BF16), (((0,), (0,)), ((), ())),
                                               preferred_element_type=F32)
    return o, s1


def _gla_fwd(main, logg, Lp):
    n = Lp // BLOCK
    qb, kb, vb = QC // GLA_DK, KC // GLA_DK, VC // GLA_DV

    def body(q_ref, k_ref, v_ref, g_ref, o_ref, st_ref, s_s):
        c = pl.program_id(1)

        @pl.when(c == 0)
        def _():
            s_s[...] = jnp.zeros_like(s_s)

        s0 = s_s[...]
        st_ref[...] = s0
        o, s1 = _gla_chunk(q_ref[...].astype(F32), k_ref[...].astype(F32), v_ref[...].astype(F32), g_ref[...], s0)
        o_ref[...] = o
        s_s[...] = s1

    return pl.pallas_call(
        body, name="gla_fwd", grid=(GLA_HEADS, n),
        in_specs=[pl.BlockSpec((BLOCK, GLA_DK), lambda h, c: (c, qb + h)),
                  pl.BlockSpec((BLOCK, GLA_DK), lambda h, c: (c, kb + h)),
                  pl.BlockSpec((BLOCK, GLA_DV), lambda h, c: (c, vb + h)),
                  pl.BlockSpec((BLOCK, GLA_DK), lambda h, c: (c, h))],
        out_specs=[pl.BlockSpec((BLOCK, GLA_DV), lambda h, c: (c, h)),
                   pl.BlockSpec((None, None, GLA_DK, GLA_DV), lambda h, c: (h, c, 0, 0))],
        out_shape=[jax.ShapeDtypeStruct((Lp, GLA_HEADS * GLA_DV), F32),
                   jax.ShapeDtypeStruct((GLA_HEADS, n, GLA_DK, GLA_DV), F32)],
        scratch_shapes=[pltpu.VMEM((GLA_DK, GLA_DV), F32)],
        compiler_params=_params(("parallel", "arbitrary")),
    )(main, main, main, logg)


def _gla_bwd(main, logg, states, do, Lp):
    n = Lp // BLOCK
    qb, kb, vb = QC // GLA_DK, KC // GLA_DK, VC // GLA_DV

    def body(q_ref, k_ref, v_ref, g_ref, st_ref, do_ref, dq_ref, dk_ref, dv_ref, dg_ref, ds_s):
        c = pl.program_id(1)

        @pl.when(c == 0)
        def _():
            ds_s[...] = jnp.zeros_like(ds_s)

        _, vjp = jax.vjp(_gla_chunk, q_ref[...].astype(F32), k_ref[...].astype(F32), v_ref[...].astype(F32),
                         g_ref[...], st_ref[...])
        dq, dk, dv, dg, ds0 = vjp((do_ref[...], ds_s[...]))
        dq_ref[...] = dq.astype(dq_ref.dtype)
        dk_ref[...] = dk.astype(dk_ref.dtype)
        dv_ref[...] = dv.astype(dv_ref.dtype)
        dg_ref[...] = dg
        ds_s[...] = ds0

    rk = lambda base: pl.BlockSpec((BLOCK, GLA_DK), lambda h, c: (n - 1 - c, base + h))
    rv = lambda base: pl.BlockSpec((BLOCK, GLA_DV), lambda h, c: (n - 1 - c, base + h))
    return pl.pallas_call(
        body, name="gla_bwd", grid=(GLA_HEADS, n),
        in_specs=[rk(qb), rk(kb), rv(vb), rk(0),
                  pl.BlockSpec((None, None, GLA_DK, GLA_DV), lambda h, c: (h, n - 1 - c, 0, 0)), rv(0)],
        out_specs=[rk(0), rk(0), rv(0), rk(0)],
        out_shape=[jax.ShapeDtypeStruct((Lp, GLA_HEADS * GLA_DK), BF16), jax.ShapeDtypeStruct((Lp, GLA_HEADS * GLA_DK), BF16),
                   jax.ShapeDtypeStruct((Lp, GLA_HEADS * GLA_DV), BF16), jax.ShapeDtypeStruct((Lp, GLA_HEADS * GLA_DK), F32)],
        scratch_shapes=[pltpu.VMEM((GLA_DK, GLA_DV), F32)],
        compiler_params=_params(("parallel", "arbitrary")),
    )(main, main, main, logg, states, do)


def _loss_head(h, g, target, Lp):
    t = BLOCK
    D = D_MODEL

    def body(h_ref, g_ref, t_ref, loss_ref, dh_ref, dg_ref):
        i = pl.program_id(0)
        x = h_ref[...]
        tok = (i * t + lax.broadcasted_iota(jnp.int32, (t, 1), 0)) >= BLOCK
        r = lax.rsqrt(jnp.mean(x * x, axis=-1, keepdims=True) + EPS)
        nrm = x * r
        e = jnp.where(tok, nrm * g_ref[...] - t_ref[...], 0.0)
        part = 0.5 * jnp.sum(jnp.sum(e * e, axis=1, keepdims=True), axis=0, keepdims=True) / D
        dy = e / D
        dn = dy * g_ref[...]
        dh_ref[...] = r * (dn - nrm * jnp.mean(dn * nrm, axis=-1, keepdims=True))
        _acc_out(dg_ref, i, jnp.sum(dy * nrm, axis=0, keepdims=True))
        _acc_out(loss_ref, i, jnp.broadcast_to(part, (1, LANES)))

    return pl.pallas_call(
        body, name="loss_head", grid=(Lp // t,),
        in_specs=[pl.BlockSpec((t, D), lambda i: (i, 0)), pl.BlockSpec((1, D), lambda i: (0, 0)),
                  pl.BlockSpec((t, D), lambda i: (jnp.maximum(i - 1, 0), 0))],
        out_specs=[pl.BlockSpec((1, LANES), lambda i: (0, 0)), pl.BlockSpec((t, D), lambda i: (i, 0)),
                   pl.BlockSpec((1, D), lambda i: (0, 0))],
        out_shape=[jax.ShapeDtypeStruct((1, LANES), F32), jax.ShapeDtypeStruct((Lp, D), F32),
                   jax.ShapeDtypeStruct((1, D), F32)],
        compiler_params=_params(("arbitrary",)),
    )(h, g, target)


def _adamw(w, g, m, v, name):
    if w.ndim == 1:
        outs = _adamw(*(a.reshape(1, -1) for a in (w, g, m, v)), name)
        return tuple(o.reshape(w.shape) for o in outs)
    rows, cols = w.shape[-2:]
    budget_rows = max(8, ADAM_BLOCK_BYTES // (4 * cols))
    tr = rows if rows <= budget_rows else _pick(rows, tuple(t for t in (512, 256, 128, 64, 32, 16, 8) if t <= budget_rows))

    def body(w_ref, g_ref, m_ref, v_ref, go_ref, d_ref, nm_ref, nv_ref):
        gg = g_ref[...]
        mm = ADAM_B1 * m_ref[...] + (1.0 - ADAM_B1) * gg
        vv = ADAM_B2 * v_ref[...] + (1.0 - ADAM_B2) * jnp.square(gg)
        m_hat = mm / (1.0 - ADAM_B1 ** ADAM_STEP)
        v_hat = vv / (1.0 - ADAM_B2 ** ADAM_STEP)
        d_ref[...] = -ADAM_LR * (m_hat / (jnp.sqrt(v_hat) + ADAM_EPS) + ADAM_WD * w_ref[...])
        go_ref[...] = gg
        nm_ref[...] = mm
        nv_ref[...] = vv

    if w.ndim == 3:
        spec, grid = pl.BlockSpec((None, tr, cols), lambda l, i: (l, i, 0)), (w.shape[0], rows // tr)
    else:
        spec, grid = pl.BlockSpec((tr, cols), lambda i: (i, 0)), (rows // tr,)
    return pl.pallas_call(
        body, name=name, grid=grid, in_specs=[spec] * 4, out_specs=[spec] * 4,
        out_shape=[jax.ShapeDtypeStruct(w.shape, F32)] * 4,
        compiler_params=_params(("parallel",) * len(grid)),
    )(w, g, m, v)


def _place():
    x, y, c = lax.axis_index("x"), lax.axis_index("y"), lax.axis_index("c")
    chips = [(1 - x, y), (x, 1 - y), (1 - x, 1 - y)]
    return x, y, c, chips


def _rcopy(src, dst, send_sems, recv_sems, k, to):
    return pltpu.make_async_remote_copy(src_ref=src, dst_ref=dst, send_sem=send_sems.at[k], recv_sem=recv_sems.at[k],
                                        device_id=to, device_id_type=MESH)


def _any_spec():
    return pl.BlockSpec(memory_space=pl.ANY)


def _shard_ref(ref, mode, t, r, c):
    if mode == "rows":
        return ref.at[pl.ds(pl.multiple_of(t * r, 16), r), :]
    if mode == "cols":
        return ref.at[:, pl.ds(pl.multiple_of(t * c, LANES), c)]
    return ref.at[t]


def _gathered_shape(mode, r, c):
    return {"rows": (4 * r, c), "cols": (r, 4 * c), "stack": (4, r, c)}[mode]


def _place_own(shard, mode, me1, name):
    _, r, c = shard.shape
    tr = _ew_rows(r, c)
    blk = {"rows": (None, tr, c), "cols": (None, tr, c), "stack": (None, None, tr, c)}[mode]
    idx = {"rows": lambda l, i, me: (l, me[0] * (r // tr) + i, 0),
           "cols": lambda l, i, me: (l, i, me[0]),
           "stack": lambda l, i, me: (l, me[0], i, 0)}[mode]

    def body(me_ref, in_ref, out_ref):
        out_ref[...] = in_ref[...]

    return pl.pallas_call(
        body, name=name,
        grid_spec=pltpu.PrefetchScalarGridSpec(
            num_scalar_prefetch=1, grid=(DEPTH, r // tr),
            in_specs=[pl.BlockSpec((None, tr, c), lambda l, i, me: (l, i, 0))],
            out_specs=pl.BlockSpec(blk, idx)),
        out_shape=jax.ShapeDtypeStruct((DEPTH,) + _gathered_shape(mode, r, c), shard.dtype),
        compiler_params=_params(("parallel", "parallel")),
    )(me1, shard)


def _gather_weights(shards, modes):
    n = len(shards)
    dims = [s.shape[1:] for s in shards]
    me1 = jnp.reshape(2 * lax.axis_index("x") + lax.axis_index("y"), (1,)).astype(jnp.int32)
    placed = [_place_own(shards[k], modes[k], me1, f"gather_place_{k}") for k in range(n)]

    def body(*refs):
        ins, outs = refs[:n], refs[2 * n:3 * n]
        send_sems, recv_sems = refs[3 * n:]
        x, y, c, chips = _place()
        me = 2 * x + y
        place = lambda k, l, t: _shard_ref(outs[k].at[l], modes[k], t, *dims[k])
        first = [_rcopy(ins[k].at[c], place(k, c, me), send_sems, recv_sems, 6 * k + j, (*chip, c))
                 for j, chip in enumerate(chips) for k in range(n)]
        for cp in first:
            cp.start()
        passed = []
        for j, (px, py) in enumerate(chips):
            for k in range(n):
                blk = place(k, c, 2 * px + py)
                _rcopy(blk, blk, send_sems, recv_sems, 6 * k + j, (x, y, c)).wait_recv()
                fwd = _rcopy(blk, blk, send_sems, recv_sems, 6 * k + 3 + j, (x, y, 1 - c))
                fwd.start()
                passed.append(fwd)
        for j, (px, py) in enumerate(chips):
            for k in range(n):
                blk = place(k, 1 - c, 2 * px + py)
                _rcopy(blk, blk, send_sems, recv_sems, 6 * k + 3 + j, (x, y, c)).wait_recv()
        for cp in first + passed:
            cp.wait_send()

    return pl.pallas_call(
        body, name="gather_weights", in_specs=[_any_spec()] * (2 * n), out_specs=[_any_spec()] * n,
        out_shape=[jax.ShapeDtypeStruct(a.shape, a.dtype) for a in placed],
        input_output_aliases={n + k: k for k in range(n)},
        scratch_shapes=[pltpu.SemaphoreType.DMA((6 * n,)), pltpu.SemaphoreType.DMA((6 * n,))],
    )(*shards, *placed)


def _swap_layers(gs):
    n = len(gs)

    def body(*refs):
        ins, outs = refs[:n], refs[n:2 * n]
        send_sems, recv_sems = refs[2 * n:]
        x, y, c, _ = _place()
        cps = [_rcopy(ins[k].at[1 - c], outs[k], send_sems, recv_sems, k, (x, y, 1 - c)) for k in range(n)]
        for cp in cps:
            cp.start()
        for cp in cps:
            cp.wait()

    return pl.pallas_call(
        body, name="rs_swap_layers", in_specs=[_any_spec()] * n, out_specs=[_any_spec()] * n,
        out_shape=[jax.ShapeDtypeStruct(g.shape[1:], g.dtype) for g in gs],
        scratch_shapes=[pltpu.SemaphoreType.DMA((n,)), pltpu.SemaphoreType.DMA((n,))],
    )(*gs)


def _scatter_chips(hs, modes, dims):
    n = len(hs)

    def body(*refs):
        ins, outs = refs[:n], refs[n:2 * n]
        send_sems, recv_sems, local_sems = refs[2 * n:]
        x, y, c, chips = _place()
        me = 2 * x + y
        part = lambda k, t: _shard_ref(ins[k], modes[k], t, *dims[k])
        own = [pltpu.make_async_copy(part(k, me), outs[k].at[me], local_sems.at[k]) for k in range(n)]
        for cp in own:
            cp.start()
        cps = [_rcopy(part(k, 2 * px + py), outs[k].at[me], send_sems, recv_sems, 3 * k + j, (px, py, c))
               for j, (px, py) in enumerate(chips) for k in range(n)]
        for cp in cps:
            cp.start()
        for j, (px, py) in enumerate(chips):
            for k in range(n):
                blk = outs[k].at[2 * px + py]
                _rcopy(blk, blk, send_sems, recv_sems, 3 * k + j, (x, y, c)).wait_recv()
        for cp in cps:
            cp.wait_send()
        for cp in own:
            cp.wait()

    return pl.pallas_call(
        body, name="rs_scatter_chips", in_specs=[_any_spec()] * n, out_specs=[_any_spec()] * n,
        out_shape=[jax.ShapeDtypeStruct((4,) + tuple(dims[k]), hs[k].dtype) for k in range(n)],
        scratch_shapes=[pltpu.SemaphoreType.DMA((3 * n,)), pltpu.SemaphoreType.DMA((3 * n,)),
                        pltpu.SemaphoreType.DMA((n,))],
    )(*hs)


def _join_layers(fs):
    n = len(fs)

    def body(*refs):
        outs = refs[n:2 * n]
        send_sems, recv_sems = refs[2 * n:]
        x, y, c, _ = _place()
        cps = [_rcopy(outs[k].at[c], outs[k].at[c], send_sems, recv_sems, k, (x, y, 1 - c)) for k in range(n)]
        for cp in cps:
            cp.start()
        for k in range(n):
            blk = outs[k].at[1 - c]
            _rcopy(blk, blk, send_sems, recv_sems, k, (x, y, c)).wait_recv()
        for cp in cps:
            cp.wait_send()

    return pl.pallas_call(
        body, name="rs_join_layers", in_specs=[_any_spec()] * n, out_specs=[_any_spec()] * n,
        out_shape=[jax.ShapeDtypeStruct(f.shape, f.dtype) for f in fs],
        input_output_aliases={k: k for k in range(n)},
        scratch_shapes=[pltpu.SemaphoreType.DMA((n,)), pltpu.SemaphoreType.DMA((n,))],
    )(*fs)


def _ew_rows(M, N):
    fit = [t for t in (512, 256, 128, 64, 32, 16) if M % t == 0 and t * N * 4 <= EW_BLOCK_BYTES]
    return fit[0] if fit else M


def _add_own(g, other, c1, out_dtype, name):
    _, M, N = g.shape
    tr = _ew_rows(M, N)

    def body(c_ref, g_ref, o_ref, out_ref):
        out_ref[...] = (g_ref[...] + o_ref[...]).astype(out_ref.dtype)

    return pl.pallas_call(
        body, name=name,
        grid_spec=pltpu.PrefetchScalarGridSpec(
            num_scalar_prefetch=1, grid=(M // tr,),
            in_specs=[pl.BlockSpec((None, tr, N), lambda i, cr: (cr[0], i, 0)),
                      pl.BlockSpec((tr, N), lambda i, cr: (i, 0))],
            out_specs=pl.BlockSpec((tr, N), lambda i, cr: (i, 0))),
        out_shape=jax.ShapeDtypeStruct((M, N), out_dtype),
        compiler_params=_params(("parallel",)),
    )(c1, g, other)


def _sum_chips(q, c1, name):
    _, M, N = q.shape
    tr = _ew_rows(M, N)

    def body(c_ref, q_ref, out_ref):
        out_ref[...] = ((q_ref[0].astype(F32) + q_ref[1].astype(F32)) + q_ref[2].astype(F32)) + q_ref[3].astype(F32)

    return pl.pallas_call(
        body, name=name,
        grid_spec=pltpu.PrefetchScalarGridSpec(
            num_scalar_prefetch=1, grid=(M // tr,),
            in_specs=[pl.BlockSpec((4, tr, N), lambda i, cr: (0, i, 0))],
            out_specs=pl.BlockSpec((None, tr, N), lambda i, cr: (cr[0], i, 0))),
        out_shape=jax.ShapeDtypeStruct((DEPTH, M, N), F32),
        compiler_params=_params(("parallel",)),
    )(c1, q)


def _reduce_scatter(gs, modes, dims, wire):
    c1 = jnp.reshape(lax.axis_index("c"), (1,)).astype(jnp.int32)
    flat = lambda a, lead: a.reshape(a.shape[:lead] + (-1, a.shape[-1]))
    others = _swap_layers(gs)
    hs = [_add_own(flat(g, 1), flat(o, 0), c1, wire[k], f"rs_add_own_{k}").reshape(o.shape)
          for k, (g, o) in enumerate(zip(gs, others))]
    qs = _scatter_chips(hs, modes, dims)
    fs = [_sum_chips(q, c1, f"rs_sum_chips_{k}") for k, q in enumerate(qs)]
    return _join_layers(fs)


def _allreduce_small(v):
    R, C = v.shape

    def body(v_ref, sum_ref, all_ref, send_sems, recv_sems):
        x, y, c, _ = _place()
        me = 4 * x + 2 * y + c
        rows = lambda d: all_ref.at[pl.ds(pl.multiple_of(d * R, 8), R), :]

        def peer(k):
            flip = lambda bit, v: (1 - v) if ((k + 1) >> bit) & 1 else v
            return flip(2, x), flip(1, y), flip(0, c)

        outs = [_rcopy(v_ref, rows(me), send_sems, recv_sems, k, peer(k)) for k in range(7)]
        for cp in outs:
            cp.start()
        all_ref[pl.ds(pl.multiple_of(me * R, 8), R), :] = v_ref[...]
        for k in range(7):
            px, py, pc = peer(k)
            blk = rows(4 * px + 2 * py + pc)
            _rcopy(blk, blk, send_sems, recv_sems, k, (x, y, c)).wait_recv()
        for cp in outs:
            cp.wait_send()
        tot = all_ref[0:R, :]
        for d in range(1, 8):
            tot = tot + all_ref[d * R:(d + 1) * R, :]
        sum_ref[...] = tot

    vm = pl.BlockSpec(memory_space=pltpu.VMEM)
    return pl.pallas_call(
        body, name="allreduce_small", in_specs=[vm], out_specs=[vm, vm],
        out_shape=[jax.ShapeDtypeStruct((R, C), F32), jax.ShapeDtypeStruct((8 * R, C), F32)],
        scratch_shapes=[pltpu.SemaphoreType.DMA((7,)), pltpu.SemaphoreType.DMA((7,))],
    )(v)[0]


def _size(shape):
    n = 1
    for d in shape:
        n *= d
    return n


def _pack(pieces, dtype):
    flat = jnp.concatenate([p.astype(dtype).reshape(-1) for p in pieces])
    rows = -(-flat.shape[0] // (PACK_COLS * 16)) * 16
    return jnp.pad(flat, (0, rows * PACK_COLS - flat.shape[0])).reshape(rows, PACK_COLS)


def _unpack(buf, shapes):
    flat = buf.reshape(-1)
    out, pos = [], 0
    for s in shapes:
        n = _size(s)
        out.append(flat[pos:pos + n].reshape(s))
        pos += n
    return out


def _small_piece(name, arr, l):
    if name == "meta_tokens":
        return arr[l * (N_META // DEPTH):(l + 1) * (N_META // DEPTH)]
    return arr[l]


def _prep_w_in(w_in4):
    w_in = jnp.concatenate([w_in4[t] for t in range(4)], axis=1)
    col = lambda a, n: w_in[:, _R[a]:_R[a] + n]
    main = jnp.concatenate([col("qa", 3072), col("scb", 3072), col("qc", 3072), col("ga", 6144)], axis=1)
    zpad = lambda n: jnp.zeros((D_MODEL, n), w_in.dtype)
    side = jnp.concatenate([col("fa", 8), zpad(LANES - 8), col("glr", GLA_RANK), zpad(LANES - GLA_RANK)], axis=1)
    return main.astype(BF16), side.astype(BF16)


def _w_in_cols(dmain, dside, lo, hi):
    segs = ((0, _R["fa"], dmain, 0), (_R["fa"], _R["scb"], dside, 0), (_R["scb"], _R["glr"], dmain, SCB),
            (_R["glr"], _R["ga"], dside, LANES), (_R["ga"], N_IN, dmain, GA))
    parts = [src[..., off + max(a, lo) - a:off + min(b, hi) - a] for a, b, src, off in segs if max(a, lo) < min(b, hi)]
    return jnp.concatenate(parts, axis=-1)


def _pad_rows(a, rows):
    return jnp.pad(a.astype(F32), ((0, rows - a.shape[0]), (0, 0)))


def _row2(v):
    return v.reshape(1, -1).astype(F32)


def _layer_fwd(h, p, rep, l, Lp, tm, ta):
    tag = lambda s: f"{s}_l{l}"
    g1, g2 = _row2(rep["norm1_g"][l]), _row2(rep["norm2_g"][l])
    bf = jnp.pad(_row2(rep["fox_b_f"][l]), ((0, 0), (0, LANES - FOX_HEADS)))
    gate_b, b_g, gnorm = _row2(rep["gate_b"][l]), _row2(rep["gla_b_g"][l]), _row2(rep["gla_norm_g"][l])
    (xn,) = _rw_fwd(tag("rms1_fwd"), _f_rms, [Row(h, D_MODEL)], [Const(g1)], [(D_MODEL, BF16)], Lp, BLOCK)
    main = _mm(xn, p["main"][l], "nn", BF16, tag("proj_main"))
    side = _mm(xn, p["side"][l], "nn", F32, tag("proj_side"))
    c = _fox_gate_fwd(side, bf, Lp)
    c_t = c[:, :FOX_HEADS].T
    c_col, c_row = c_t[:, :, None], c_t[:, None, :]
    oa, lse = _fox_fwd(main, c_col, c_row, Lp, ta)
    ya = _mm(oa, p["w_a_o"], "nn", BF16, tag("ya"), b_lead=l)
    ub = _sconv_fwd(main, p["conv_w"][l], Lp, tm)
    yb = _mm(ub, p["w_b_o"], "nn", BF16, tag("yb"), b_lead=l)
    glr = Row(side, LANES, lambda g: 1)
    (logg,) = _rw_fwd(tag("logg_fwd"), _f_logg, [glr], [Const(p["w_g2"][l]), Const(b_g)], [(512, F32)], Lp, tm)
    oc, states = _gla_fwd(main, logg, Lp)
    rc = Row(main, GLA_DV, lambda g: RC // GLA_DV + g)
    gn = Const(gnorm, (1, GLA_DV), lambda g: (0, g))
    (uc,) = _rw_fwd(tag("gla_post_fwd"), _f_gla_post, [Row(oc, GLA_DV), rc], [gn], [(GLA_DV, BF16)], Lp, tm,
                    G=GLA_HEADS)
    yc = _mm(uc, p["w_c_o"], "nn", BF16, tag("yc"), b_lead=l)
    cw = 512
    G = D_MODEL // cw
    mrows = [Row(ya, cw), Row(yb, cw), Row(yc, cw), Row(main, cw, lambda g: GA // cw + g),
             Row(main, cw, lambda g: GB // cw + g), Row(main, cw, lambda g: GC // cw + g)]
    mconsts = [Const(gate_b, (1, cw), lambda g, k=k: (0, k * G + g)) for k in range(3)]
    (mix,) = _rw_fwd(tag("merge_fwd"), _f_merge, mrows, mconsts, [(cw, BF16)], Lp, tm, G=G)
    h1 = _mm(mix, p["w_o"], "nn", F32, tag("h1"), add=h, b_lead=l)
    (xn2,) = _rw_fwd(tag("rms2_fwd"), _f_rms, [Row(h1, D_MODEL)], [Const(g2)], [(D_MODEL, BF16)], Lp, BLOCK)
    up = _mm(xn2, p["w_up"], "nn", BF16, tag("up"), b_lead=l)
    act = _mlp_act_fwd(up, p["mlp_conv_w"][l], Lp, tm)
    h2 = _mm(act, p["w_down"], "nn", F32, tag("h2"), add=h1, b_lead=l)
    res = dict(h=h, xn=xn, main=main, side=side, c_col=c_col, c_row=c_row, oa=oa, lse=lse, ya=ya, ub=ub, yb=yb,
               logg=logg, oc=oc, states=states, uc=uc, yc=yc, mix=mix, h1=h1, xn2=xn2, up=up, act=act,
               g1=g1, g2=g2, bf=bf, gate_b=gate_b, b_g=b_g, gnorm=gnorm)
    return h2, res


def _layer_bwd(dh2, p, r, l, Lp, tm, ta, big):
    tag = lambda s: f"{s}_l{l}"
    g = {}

    def wgrad(name, a, b):
        big[name] = _mm(a, b, "tn", F32, tag("d_" + name), slot=(big.get(name), l))

    wgrad("w_down", r["act"], dh2)
    dact = _mm(dh2, p["w_down"], "nt", BF16, tag("d_act"), b_lead=l)
    dgate, dval, dwg, dwu = _mlp_act_bwd(r["up"], p["mlp_conv_w"][l], dact, Lp, tm)
    g["mlp_conv_w"] = jnp.concatenate([dwg[:3], dwu[:3]], axis=1)
    dup = jnp.concatenate([dgate, dval], axis=1)
    wgrad("w_up", r["xn2"], dup)
    dxn2 = _mm(dup, p["w_up"], "nt", F32, tag("d_xn2"), b_lead=l)
    (dh1,), (dg2,) = _rw_bwd(tag("rms2_bwd"), _f_rms, [Row(r["h1"], D_MODEL)], [Const(r["g2"])],
                             [Row(dxn2, D_MODEL)], [F32], [dh2], Lp, BLOCK)
    g["norm2_g"] = dg2[0]
    wgrad("w_o", r["mix"], dh1)
    dmix = _mm(dh1, p["w_o"], "nt", BF16, tag("d_mix"), b_lead=l)
    cw = 512
    G = D_MODEL // cw
    main = r["main"]
    mrows = [Row(r["ya"], cw), Row(r["yb"], cw), Row(r["yc"], cw), Row(main, cw, lambda g_: GA // cw + g_),
             Row(main, cw, lambda g_: GB // cw + g_), Row(main, cw, lambda g_: GC // cw + g_)]
    mconsts = [Const(r["gate_b"], (1, cw), lambda g_, k=k: (0, k * G + g_)) for k in range(3)]
    (dya, dyb, dyc, dga, dgb, dgc), dbs = _rw_bwd(tag("merge_bwd"), _f_merge, mrows, mconsts, [Row(dmix, cw)],
                                                  [BF16] * 6, [None] * 6, Lp, tm, G=G)
    g["gate_b"] = jnp.concatenate([dbs[k][0, k * D_MODEL:(k + 1) * D_MODEL] for k in range(3)])
    wgrad("w_a_o", r["oa"], dya)
    doa = _mm(dya, p["w_a_o"], "nt", BF16, tag("d_oa"), b_lead=l)
    wgrad("w_b_o", r["ub"], dyb)
    dub = _mm(dyb, p["w_b_o"], "nt", BF16, tag("d_ub"), b_lead=l)
    wgrad("w_c_o", r["uc"], dyc)
    duc = _mm(dyc, p["w_c_o"], "nt", BF16, tag("d_uc"), b_lead=l)
    delta = _fox_delta(main, r["c_col"], r["c_row"], r["lse"], doa, Lp, ta)
    dq, dk, dv, dck = _fox_bwd(main, r["c_col"], r["c_row"], r["lse"], delta, doa, Lp, ta)
    dc = jnp.pad(dck[:, 0, :].T, ((0, 0), (0, LANES - FOX_HEADS)))
    dfa, dbf = _fox_gate_bwd(r["side"], r["bf"], dc, Lp)
    g["fox_b_f"] = dbf[0, :FOX_HEADS]
    dscb, dscc, dsch, dcw = _sconv_bwd(main, p["conv_w"][l], dub, Lp, tm)
    g["conv_w"] = dcw[:3]
    rc = Row(main, GLA_DV, lambda g_: RC // GLA_DV + g_)
    gn = Const(r["gnorm"], (1, GLA_DV), lambda g_: (0, g_))
    (doc, drc), (dgn,) = _rw_bwd(tag("gla_post_bwd"), _f_gla_post, [Row(r["oc"], GLA_DV), rc], [gn],
                                 [Row(duc, GLA_DV)], [F32, BF16], [None, None], Lp, tm, G=GLA_HEADS)
    g["gla_norm_g"] = dgn[0]
    dqc, dkc, dvc, dlogg = _gla_bwd(main, r["logg"], r["states"], doc, Lp)
    glr = Row(r["side"], LANES, lambda g_: 1)
    (dglr,), (dwg2, dbg) = _rw_bwd(tag("logg_bwd"), _f_logg, [glr], [Const(p["w_g2"][l]), Const(r["b_g"])],
                                   [Row(dlogg, 512)], [F32], [None], Lp, tm)
    g["gla_w_g2"] = dwg2[:GLA_RANK]
    g["gla_b_g"] = dbg[0]
    dmain = jnp.concatenate([dq, dk, dv, dscb, dscc, dsch, dqc, dkc, dvc, drc, dga, dgb, dgc], axis=1)
    dside = jnp.concatenate([dfa, dglr], axis=1)
    wgrad("main", r["xn"], dmain)
    wgrad("side", r["xn"], dside)
    dxn = _mm(dmain, p["main"][l], "nt", F32, tag("d_xn_main"))
    dxn = _mm(dside, p["side"][l], "nt", F32, tag("d_xn_side"), add=dxn)
    (dh,), (dg1,) = _rw_bwd(tag("rms1_bwd"), _f_rms, [Row(r["h"], D_MODEL)], [Const(r["g1"])], [Row(dxn, D_MODEL)],
                            [F32], [dh1], Lp, BLOCK)
    g["norm1_g"] = dg1[0]
    return dh, g


def _local_step(x, target, meta, p, rep):
    seq = x.shape[0]
    Lp = PAD + N_META + seq
    tm = _pick(Lp, (640, 384, 128))
    ta = tm
    h = jnp.concatenate([jnp.zeros((PAD, D_MODEL), F32), meta.astype(F32), x], axis=0)
    saved = []
    for l in range(DEPTH):
        h, res = _layer_fwd(h, p, rep, l, Lp, tm, ta)
        saved.append(res)
    loss, dh, dgf = _loss_head(h, _row2(rep["final_norm_g"]), target, Lp)
    big, small = {}, [None] * DEPTH
    for l in reversed(range(DEPTH)):
        dh, small[l] = _layer_bwd(dh, p, saved[l], l, Lp, tm, ta, big)
    return loss[0, 0], dh[BLOCK:], dh[PAD:BLOCK], big, small, dgf[0]


def kernel(x, meta_tokens, norm1_g, w_in, fox_b_f, gate_b, conv_w, gla_w_g2, gla_b_g, gla_norm_g, w_a_o, w_b_o, w_c_o, w_o, norm2_g, w_up, mlp_conv_w, w_down, final_norm_g, loss_target, m_meta_tokens, m_norm1_g, m_w_in, m_fox_b_f, m_gate_b, m_conv_w, m_gla_w_g2, m_gla_b_g, m_gla_norm_g, m_w_a_o, m_w_b_o, m_w_c_o, m_w_o, m_norm2_g, m_w_up, m_mlp_conv_w, m_w_down, m_final_norm_g, v_meta_tokens, v_norm1_g, v_w_in, v_fox_b_f, v_gate_b, v_conv_w, v_gla_w_g2, v_gla_b_g, v_gla_norm_g, v_w_a_o, v_w_b_o, v_w_c_o, v_w_o, v_norm2_g, v_w_up, v_mlp_conv_w, v_w_down, v_final_norm_g):
    given = dict(locals())
    weights = {n: given[n] for n in WEIGHT_ORDER}
    rep = {n: weights[n] for n, _ in REPLICATED}
    big_names = [n for n, _ in BIG]
    big_modes = [m for _, m in BIG] + ["stack"]
    small_shapes = [(s[0], s[1] // 4) for _, s in SMALL]
    exact = [k for k, (n, _) in enumerate(SMALL) if n in GATHER_F32]

    def small_wire(l):
        ws = [_small_piece(n, weights[n], l) for n, _ in SMALL]
        his = [w.astype(BF16) for w in ws]
        return his + [(ws[k] - his[k].astype(F32)).astype(BF16) for k in exact]

    shards = [weights[n].astype(BF16) for n in big_names] + [jnp.stack([_pack(small_wire(l), BF16) for l in range(DEPTH)])]
    gathered = _gather_weights(shards, big_modes)
    gw = dict(zip(big_names, gathered[:-1]))
    p = {n: gw[n] for n in big_names if n != "w_in"}
    p["main"], p["side"] = zip(*[_prep_w_in(gw["w_in"][l]) for l in range(DEPTH)])
    small_full = []
    for l in range(DEPTH):
        per_chip = [_unpack(gathered[-1][l, t], small_shapes + [small_shapes[k] for k in exact]) for t in range(4)]
        full = [jnp.concatenate([per_chip[t][k] for t in range(4)], axis=1).astype(F32) for k in range(len(per_chip[0]))]
        for e, k in enumerate(exact):
            full[k] = full[k] + full[len(SMALL) + e]
        small_full.append(dict(zip([n for n, _ in SMALL], full[:len(SMALL)])))
    p["conv_w"] = [_pad_rows(s["conv_w"], 8) for s in small_full]
    p["mlp_conv_w"] = [_pad_rows(s["mlp_conv_w"], 8) for s in small_full]
    p["w_g2"] = [_pad_rows(s["gla_w_g2"], LANES) for s in small_full]
    meta_full = jnp.concatenate([s["meta_tokens"] for s in small_full], axis=0)

    loss, grad_x, grad_meta, big, small, d_final = _local_step(x[0], loss_target[0], meta_full, p, rep)
    loss = lax.psum(loss, ("x", "y", "c"))

    big["w_in"] = jnp.stack([_w_in_cols(big["main"], big["side"], t * (N_IN // 4), (t + 1) * (N_IN // 4))
                             for t in range(4)], axis=1)
    for l in range(DEPTH):
        small[l]["meta_tokens"] = grad_meta[l * (N_META // DEPTH):(l + 1) * (N_META // DEPTH)]
    shard_of = lambda a, t: lax.slice_in_dim(a, t * (a.shape[1] // 4), (t + 1) * (a.shape[1] // 4), axis=1)
    small_g = jnp.stack([jnp.stack([_pack([shard_of(small[l][n], t) for n, _ in SMALL], F32) for t in range(4)])
                         for l in range(DEPTH)])
    dims = [shards[k].shape[1:] for k in range(len(BIG))] + [small_g.shape[2:]]
    summed = _reduce_scatter([big[n] for n in big_names] + [small_g], big_modes, dims,
                             [BF16] * len(BIG) + [F32])
    gout = dict(zip(big_names, summed[:-1]))
    pieces = [_unpack(summed[-1][l], small_shapes) for l in range(DEPTH)]
    for k, (n, _) in enumerate(SMALL):
        per_layer = [pieces[l][k] for l in range(DEPTH)]
        gout[n] = jnp.concatenate(per_layer, axis=0) if n == "meta_tokens" else jnp.stack(per_layer)

    rep_g = {n: (d_final if n == "final_norm_g" else jnp.stack([small[l][n] for l in range(DEPTH)])) for n, _ in REPLICATED}
    flat = jnp.concatenate([rep_g[n].astype(F32).reshape(-1) for n, _ in REPLICATED])
    rrows = -(-flat.shape[0] // (PACK_COLS * 8)) * 8
    summed_small = _allreduce_small(jnp.pad(flat, (0, rrows * PACK_COLS - flat.shape[0])).reshape(rrows, PACK_COLS))
    pos = 0
    for n, shape in REPLICATED:
        gout[n] = summed_small.reshape(-1)[pos:pos + _size(shape)].reshape(shape)
        pos += _size(shape)

    deltas, new_m, new_v = {}, {}, {}
    for n in WEIGHT_ORDER:
        gout[n], deltas[n], new_m[n], new_v[n] = _adamw(weights[n], gout[n], given["m_" + n], given["v_" + n],
                                                        "adamw_" + n)
    return (loss, grad_x[None], *[gout[n] for n in WEIGHT_ORDER], *[deltas[n] for n in WEIGHT_ORDER],
            *[new_m[n] for n in WEIGHT_ORDER], *[new_v[n] for n in WEIGHT_ORDER])
```

```python
import functools

import jax
import jax.numpy as jnp
from jax import lax
from jax.experimental import pallas as pl
from jax.experimental.pallas import tpu as pltpu

F32, BF16 = jnp.float32, jnp.bfloat16
HIGHEST = lax.Precision.HIGHEST
MESH = pl.DeviceIdType.MESH

N_META = 16
BLOCK = 128
LANES = 128
PAD = BLOCK - N_META
EPS = 1e-6
NEG = -1e30
HALO = 16
VMEM_LIMIT = 56 * 1024 * 1024
ADAM_BLOCK_BYTES = 1 << 20
EW_BLOCK_BYTES = 3 << 19

D_MODEL = 2048
FOX_HEADS, FOX_HD = 8, 128
FOX_WIDTH = FOX_HEADS * FOX_HD
CONV_CH = 1024
GLA_HEADS, GLA_DK, GLA_DV, GLA_RANK, GLA_TAU = 4, 128, 256, 16, 16.0
GLA_SUB = 32
D_FF = 5632
N_IN = 15384
DEPTH = 2

_R = dict(qa=0, ka=1024, va=2048, fa=3072, scb=3080, scc=4104, sch=5128, qc=6152, kc=6664,
          vc=7176, rc=8200, glr=9224, ga=9240, gb=11288, gc=13336)
QA, KA, VA, SCB, SCC, SCH, QC, KC, VC, RC, GA, GB, GC = (
    0, 1024, 2048, 3072, 4096, 5120, 6144, 6656, 7168, 8192, 9216, 11264, 13312)
N_MAIN = 15360
N_SIDE = 256

ADAM_LR, ADAM_B1, ADAM_B2, ADAM_EPS, ADAM_WD, ADAM_STEP = 0.001, 0.9, 0.999, 1e-08, 0.01, 10

BIG = (("w_in", "stack"), ("w_a_o", "cols"), ("w_b_o", "cols"), ("w_c_o", "cols"), ("w_o", "rows"), ("w_up", "cols"),
       ("w_down", "rows"))
SMALL = (("conv_w", (3, CONV_CH)), ("mlp_conv_w", (3, 2 * D_FF)), ("gla_w_g2", (GLA_RANK, 512)),
         ("meta_tokens", (N_META // DEPTH, D_MODEL)))
REPLICATED = (("norm1_g", (2, D_MODEL)), ("fox_b_f", (2, 8)), ("gate_b", (2, 3 * D_MODEL)), ("gla_b_g", (2, 512)),
              ("gla_norm_g", (2, 1024)), ("norm2_g", (2, D_MODEL)), ("final_norm_g", (D_MODEL,)))
WEIGHT_ORDER = ("meta_tokens", "norm1_g", "w_in", "fox_b_f", "gate_b", "conv_w", "gla_w_g2", "gla_b_g",
                "gla_norm_g", "w_a_o", "w_b_o", "w_c_o", "w_o", "norm2_g", "w_up", "mlp_conv_w", "w_down",
                "final_norm_g")
PACK_COLS = 1024
GATHER_F32 = ("conv_w", "mlp_conv_w", "meta_tokens")


def _pick(n, cands):
    for c in cands:
        if n % c == 0:
            return c
    return n


def _params(sem):
    return pltpu.CompilerParams(dimension_semantics=sem, vmem_limit_bytes=VMEM_LIMIT)


def _sigmoid(x):
    return jax.nn.sigmoid(x)


def _log_sigmoid(x):
    return jnp.minimum(x, 0.0) - jnp.log(1.0 + jnp.exp(-jnp.abs(x)))


def _mm(a, b, mode, out_dtype, name, add=None, b_lead=None, slot=None):
    bshape = b.shape if b_lead is None else b.shape[1:]
    if mode == "nn":
        (M, K), (K2, N) = a.shape, bshape
    elif mode == "nt":
        (M, K), (N, K2) = a.shape, bshape
    else:
        (K, M), (K2, N) = a.shape, bshape
    assert K == K2, (name, a.shape, b.shape)
    if mode == "tn":
        tm = _pick(M, (2048, 1408, 1024, 512, 256, 128))
        tn = _pick(N, (1024, 512, 256, 128))
        tk = _pick(K, (640, 512, 384, 256, 128))
    else:
        tm = _pick(M, (1664, 640, 384, 128))
        tn = _pick(N, (512, 256, 128))
        tk = K if K <= 2048 else _pick(K, (1408, 1024, 512, 256, 128))
    nk = K // tk
    dims = {"nn": (((1,), (0,)), ((), ())), "nt": (((1,), (1,)), ((), ())), "tn": (((0,), (0,)), ((), ()))}[mode]
    n_in = 2 + (add is not None) + (slot is not None and slot[0] is not None)

    def body(*refs):
        a_ref, b_ref = refs[:2]
        add_ref = refs[2] if add is not None else None
        o_ref, acc = refs[n_in:]
        k = pl.program_id(2)

        @pl.when(k == 0)
        def _():
            acc[...] = jnp.zeros_like(acc)

        acc[...] += lax.dot_general(a_ref[...].astype(BF16), b_ref[...].astype(BF16), dims,
                                    preferred_element_type=F32)

        @pl.when(k == nk - 1)
        def _():
            r = acc[...]
            if add is not None:
                r = r + add_ref[...].astype(F32)
            o_ref[...] = r.astype(o_ref.dtype)

    a_spec = {"nn": pl.BlockSpec((tm, tk), lambda i, j, k: (i, k)),
              "nt": pl.BlockSpec((tm, tk), lambda i, j, k: (i, k)),
              "tn": pl.BlockSpec((tk, tm), lambda i, j, k: (k, i))}[mode]
    b_blk, b_idx = {"nn": ((tk, tn), lambda i, j, k: (k, j)),
                    "nt": ((tn, tk), lambda i, j, k: (j, k)),
                    "tn": ((tk, tn), lambda i, j, k: (k, j))}[mode]
    if b_lead is None:
        b_spec = pl.BlockSpec(b_blk, b_idx)
    else:
        b_spec = pl.BlockSpec((None,) + b_blk, lambda i, j, k: (b_lead,) + b_idx(i, j, k))
    o_spec = pl.BlockSpec((tm, tn), lambda i, j, k: (i, j))
    ins, specs = [a, b], [a_spec, b_spec]
    if add is not None:
        ins.append(add)
        specs.append(o_spec)
    aliases = {}
    out_shape = jax.ShapeDtypeStruct((M, N), out_dtype)
    if slot is not None:
        buf, l = slot
        o_spec = pl.BlockSpec((None, tm, tn), lambda i, j, k: (l, i, j))
        out_shape = jax.ShapeDtypeStruct((DEPTH, M, N), out_dtype)
        if buf is not None:
            aliases = {len(ins): 0}
            ins.append(buf)
            specs.append(pl.BlockSpec(memory_space=pl.ANY))
    return pl.pallas_call(
        body, name=name, grid=(M // tm, N // tn, nk), in_specs=specs, out_specs=o_spec, out_shape=out_shape,
        scratch_shapes=[pltpu.VMEM((tm, tn), F32)], input_output_aliases=aliases,
        compiler_params=_params(("parallel", "parallel", "arbitrary")),
    )(*ins)


class Row:
    def __init__(self, arr, w, cb=None):
        self.arr, self.w, self.cb = arr, w, (cb if cb is not None else (lambda g: g))


class Const:
    def __init__(self, arr, shape=None, idx=None):
        self.arr = arr
        self.shape = shape if shape is not None else arr.shape
        self.idx = idx if idx is not None else (lambda g: (0,) * arr.ndim)


def _row_spec(r, tm):
    return pl.BlockSpec((tm, r.w), lambda g, i, r=r: (i, r.cb(g)))


def _const_spec(c):
    return pl.BlockSpec(c.shape, lambda g, i, c=c: c.idx(g))


def _valid_rows(i, tm):
    return (i * tm + lax.broadcasted_iota(jnp.int32, (tm, 1), 0)) >= PAD


def _rw_fwd(name, f, rows, consts, outs, Lp, tm, G=1):
    nr, nc = len(rows), len(consts)

    def body(*refs):
        i = pl.program_id(1)
        rv = [r[...].astype(F32) for r in refs[:nr]]
        cv = [r[...].astype(F32) for r in refs[nr:nr + nc]]
        res = f(_valid_rows(i, tm), *rv, *cv)
        for o_ref, v in zip(refs[nr + nc:], res):
            o_ref[...] = v.astype(o_ref.dtype)

    return pl.pallas_call(
        body, name=name, grid=(G, Lp // tm),
        in_specs=[_row_spec(r, tm) for r in rows] + [_const_spec(c) for c in consts],
        out_specs=[pl.BlockSpec((tm, w), lambda g, i: (i, g)) for w, _ in outs],
        out_shape=[jax.ShapeDtypeStruct((Lp, w * G), dt) for w, dt in outs],
        compiler_params=_params(("parallel", "arbitrary")),
    )(*[r.arr for r in rows], *[c.arr for c in consts])


def _rw_bwd(name, f, rows, consts, cts, drow_dtypes, adds, Lp, tm, G=1):
    nr, nc, nt = len(rows), len(consts), len(cts)
    want = [k for k, dt in enumerate(drow_dtypes) if dt is not None]
    add_k = [k for k in want if adds[k] is not None]

    def body(*refs):
        i = pl.program_id(1)
        pos = 0
        rv = [r[...].astype(F32) for r in refs[pos:pos + nr]]
        pos += nr
        cv = [r[...].astype(F32) for r in refs[pos:pos + nc]]
        pos += nc
        tv = [r[...].astype(F32) for r in refs[pos:pos + nt]]
        pos += nt
        av = {k: refs[pos + n][...].astype(F32) for n, k in enumerate(add_k)}
        pos += len(add_k)
        drow_refs = refs[pos:pos + len(want)]
        pos += len(want)
        dconst_refs = refs[pos:pos + nc]
        valid = _valid_rows(i, tm)
        _, vjp = jax.vjp(lambda *a: tuple(f(valid, *a)), *rv, *cv)
        grads = vjp(tuple(tv))
        for o_ref, k in zip(drow_refs, want):
            gk = grads[k]
            if k in av:
                gk = gk + av[k]
            o_ref[...] = gk.astype(o_ref.dtype)
        for n, o_ref in enumerate(dconst_refs):
            gc = grads[nr + n]

            @pl.when(i == 0)
            def _(o_ref=o_ref, gc=gc):
                o_ref[...] = gc

            @pl.when(i > 0)
            def _(o_ref=o_ref, gc=gc):
                o_ref[...] += gc

    out_row = lambda w: pl.BlockSpec((tm, w), lambda g, i: (i, g))
    res = pl.pallas_call(
        body, name=name, grid=(G, Lp // tm),
        in_specs=([_row_spec(r, tm) for r in rows] + [_const_spec(c) for c in consts]
                  + [_row_spec(r, tm) for r in cts] + [out_row(rows[k].w) for k in add_k]),
        out_specs=[out_row(rows[k].w) for k in want] + [_const_spec(c) for c in consts],
        out_shape=([jax.ShapeDtypeStruct((Lp, rows[k].w * G), drow_dtypes[k]) for k in want]
                   + [jax.ShapeDtypeStruct(c.arr.shape, F32) for c in consts]),
        compiler_params=_params(("parallel", "arbitrary")),
    )(*[r.arr for r in rows], *[c.arr for c in consts], *[r.arr for r in cts], *[adds[k] for k in add_k])
    drows = [None] * nr
    for n, k in enumerate(want):
        drows[k] = res[n]
    return drows, list(res[len(want):])


def _f_rms(valid, h, g):
    r = lax.rsqrt(jnp.mean(h * h, axis=-1, keepdims=True) + EPS)
    return (jnp.where(valid, h * r * g, 0.0),)


def _f_logg(valid, glr, w, b):
    pre = jnp.dot(glr.astype(BF16), w.astype(BF16), preferred_element_type=F32) + b
    return (jnp.where(valid, _log_sigmoid(pre) / GLA_TAU, 0.0),)


def _f_gla_post(valid, oc, rc, g):
    y = oc * lax.rsqrt(jnp.mean(oc * oc, axis=-1, keepdims=True) + EPS) * g
    return (jnp.where(valid, rc * _sigmoid(rc) * y, 0.0),)


def _f_merge(valid, ya, yb, yc, ga, gb, gc, ba, bb, bc):
    mix = _sigmoid(ga + ba) * ya + _sigmoid(gb + bb) * yb + _sigmoid(gc + bc) * yc
    return (jnp.where(valid, mix, 0.0),)


def _fox_gate_fwd(side, bf, Lp):
    t = BLOCK
    n = Lp // t

    def body(s_ref, b_ref, c_ref, carry):
        i = pl.program_id(0)

        @pl.when(i == 0)
        def _():
            carry[...] = jnp.zeros_like(carry)

        lane = lax.broadcasted_iota(jnp.int32, (t, LANES), 1)
        ok = _valid_rows(i, t) & (lane < FOX_HEADS)
        logf = jnp.where(ok, _log_sigmoid(s_ref[...] + b_ref[...]), 0.0)
        tril = (lax.broadcasted_iota(jnp.int32, (t, t), 1) <= lax.broadcasted_iota(jnp.int32, (t, t), 0)).astype(F32)
        c = jnp.dot(tril, logf, precision=HIGHEST, preferred_element_type=F32) + carry[...]
        c_ref[...] = c
        carry[...] = c[t - 1:t, :]

    return pl.pallas_call(
        body, name="fox_gate_fwd", grid=(n,),
        in_specs=[pl.BlockSpec((t, LANES), lambda i: (i, 0)), pl.BlockSpec((1, LANES), lambda i: (0, 0))],
        out_specs=pl.BlockSpec((t, LANES), lambda i: (i, 0)),
        out_shape=jax.ShapeDtypeStruct((Lp, LANES), F32),
        scratch_shapes=[pltpu.VMEM((1, LANES), F32)],
        compiler_params=_params(("arbitrary",)),
    )(side, bf)


def _fox_gate_bwd(side, bf, dc, Lp):
    t = BLOCK
    n = Lp // t

    def body(s_ref, b_ref, dc_ref, dfa_ref, db_ref, carry):
        i = pl.program_id(0)

        @pl.when(i == 0)
        def _():
            carry[...] = jnp.zeros_like(carry)

        lane = lax.broadcasted_iota(jnp.int32, (t, LANES), 1)
        ok = _valid_rows(n - 1 - i, t) & (lane < FOX_HEADS)
        triu = (lax.broadcasted_iota(jnp.int32, (t, t), 1) >= lax.broadcasted_iota(jnp.int32, (t, t), 0)).astype(F32)
        dlogf = jnp.dot(triu, dc_ref[...], precision=HIGHEST, preferred_element_type=F32) + carry[...]
        carry[...] = dlogf[0:1, :]
        dpre = jnp.where(ok, dlogf * _sigmoid(-(s_ref[...] + b_ref[...])), 0.0)
        dfa_ref[...] = dpre
        part = jnp.sum(dpre, axis=0, keepdims=True)

        @pl.when(i == 0)
        def _():
            db_ref[...] = part

        @pl.when(i > 0)
        def _():
            db_ref[...] += part

    rev = lambda i: (n - 1 - i, 0)
    return pl.pallas_call(
        body, name="fox_gate_bwd", grid=(n,),
        in_specs=[pl.BlockSpec((t, LANES), rev), pl.BlockSpec((1, LANES), lambda i: (0, 0)),
                  pl.BlockSpec((t, LANES), rev)],
        out_specs=[pl.BlockSpec((t, LANES), rev), pl.BlockSpec((1, LANES), lambda i: (0, 0))],
        out_shape=[jax.ShapeDtypeStruct((Lp, LANES), F32), jax.ShapeDtypeStruct((1, LANES), F32)],
        scratch_shapes=[pltpu.VMEM((1, LANES), F32)],
        compiler_params=_params(("arbitrary",)),
    )(side, bf, dc)


def _fox_key_bias(cq_ref, ck_ref, j, t):
    col = j * t + lax.broadcasted_iota(jnp.int32, (1, t), 1)
    return jnp.where(col >= PAD, ck_ref[...] - cq_ref[0:1, :], -NEG)


def _fox_s(q, k, bias, diagonal, t):
    s = lax.dot_general(q, k, (((1,), (1,)), ((), ())), preferred_element_type=F32) * (FOX_HD ** -0.5) - bias
    if diagonal:
        causal = lax.broadcasted_iota(jnp.int32, (t, t), 1) <= lax.broadcasted_iota(jnp.int32, (t, t), 0)
        s = jnp.where(causal, s, NEG)
    return s


def _fox_fwd(main, c_col, c_row, Lp, t):
    n = Lp // t
    qb, kb, vb = QA // FOX_HD, KA // FOX_HD, VA // FOX_HD

    def body(q_ref, k_ref, v_ref, cq_ref, ck_ref, o_ref, lse_ref, m_s, l_s, acc):
        i, j = pl.program_id(1), pl.program_id(2)

        @pl.when(j == 0)
        def _():
            m_s[...] = jnp.full_like(m_s, NEG)
            l_s[...] = jnp.zeros_like(l_s)
            acc[...] = jnp.zeros_like(acc)

        def update(diagonal):
            s = _fox_s(q_ref[...], k_ref[...], _fox_key_bias(cq_ref, ck_ref, j, t), diagonal, t)
            m_new = jnp.maximum(m_s[...], jnp.max(s, axis=1, keepdims=True))
            alpha = jnp.exp(m_s[...] - m_new)
            p = jnp.exp(s - m_new)
            l_s[...] = alpha * l_s[...] + jnp.sum(p, axis=1, keepdims=True)
            acc[...] = alpha * acc[...] + jnp.dot(p.astype(BF16), v_ref[...], preferred_element_type=F32)
            m_s[...] = m_new

        @pl.when(j < i)
        def _():
            update(False)

        @pl.when(j == i)
        def _():
            update(True)
            o_ref[...] = jnp.where(_valid_rows(i, t), acc[...] / l_s[...], 0.0).astype(o_ref.dtype)
            lse_ref[...] = m_s[...] + jnp.log(l_s[...])

    kv = lambda base: pl.BlockSpec((t, FOX_HD), lambda h, i, j: (jnp.minimum(j, i), base + h))
    return pl.pallas_call(
        body, name="fox_fwd", grid=(FOX_HEADS, n, n),
        in_specs=[pl.BlockSpec((t, FOX_HD), lambda h, i, j: (i, qb + h)), kv(kb), kv(vb),
                  pl.BlockSpec((None, t, 1), lambda h, i, j: (h, i, 0)),
                  pl.BlockSpec((None, 1, t), lambda h, i, j: (h, 0, jnp.minimum(j, i)))],
        out_specs=[pl.BlockSpec((t, FOX_HD), lambda h, i, j: (i, h)),
                   pl.BlockSpec((None, t, 1), lambda h, i, j: (h, i, 0))],
        out_shape=[jax.ShapeDtypeStruct((Lp, FOX_WIDTH), BF16), jax.ShapeDtypeStruct((FOX_HEADS, Lp, 1), F32)],
        scratch_shapes=[pltpu.VMEM((t, 1), F32), pltpu.VMEM((t, 1), F32), pltpu.VMEM((t, FOX_HD), F32)],
        compiler_params=_params(("parallel", "parallel", "arbitrary")),
    )(main, main, main, c_col, c_row)


def _fox_p_dp(q_ref, k_ref, v_ref, cq_ref, ck_ref, lse_ref, do_ref, j, diagonal, t):
    s = _fox_s(q_ref[...], k_ref[...], _fox_key_bias(cq_ref, ck_ref, j, t), diagonal, t)
    p = jnp.exp(s - lse_ref[...])
    dp = lax.dot_general(do_ref[...], v_ref[...], (((1,), (1,)), ((), ())), preferred_element_type=F32)
    return p, dp


def _fox_delta(main, c_col, c_row, lse, doa, Lp, t):
    n = Lp // t
    qb, kb, vb = QA // FOX_HD, KA // FOX_HD, VA // FOX_HD

    def body(q_ref, k_ref, v_ref, cq_ref, ck_ref, lse_ref, do_ref, dl_ref, dl_s):
        i, j = pl.program_id(1), pl.program_id(2)

        @pl.when(j == 0)
        def _():
            dl_s[...] = jnp.zeros_like(dl_s)

        def sweep(diagonal):
            p, dp = _fox_p_dp(q_ref, k_ref, v_ref, cq_ref, ck_ref, lse_ref, do_ref, j, diagonal, t)
            dl_s[...] += jnp.sum(p * dp, axis=1, keepdims=True)

        @pl.when(j < i)
        def _():
            sweep(False)

        @pl.when(j == i)
        def _():
            sweep(True)
            dl_ref[...] = dl_s[...]

    kv = lambda base: pl.BlockSpec((t, FOX_HD), lambda h, i, j: (jnp.minimum(j, i), base + h))
    qrow = lambda base: pl.BlockSpec((t, FOX_HD), lambda h, i, j: (i, base + h))
    col = pl.BlockSpec((None, t, 1), lambda h, i, j: (h, i, 0))
    return pl.pallas_call(
        body, name="fox_delta", grid=(FOX_HEADS, n, n),
        in_specs=[qrow(qb), kv(kb), kv(vb), col,
                  pl.BlockSpec((None, 1, t), lambda h, i, j: (h, 0, jnp.minimum(j, i))), col, qrow(0)],
        out_specs=col,
        out_shape=jax.ShapeDtypeStruct((FOX_HEADS, Lp, 1), F32),
        scratch_shapes=[pltpu.VMEM((t, 1), F32)],
        compiler_params=_params(("parallel", "parallel", "arbitrary")),
    )(main, main, main, c_col, c_row, lse, doa)


def _fox_bwd(main, c_col, c_row, lse, delta, doa, Lp, t):
    n = Lp // t
    qb, kb, vb = QA // FOX_HD, KA // FOX_HD, VA // FOX_HD
    scale = FOX_HD ** -0.5

    def body(q_ref, k_ref, v_ref, cq_ref, ck_ref, lse_ref, dl_ref, do_ref, dq_ref, dk_ref, dv_ref, dck_ref,
             dq_s, dk_s, dv_s, dc_s):
        j, i = pl.program_id(1), pl.program_id(2)

        @pl.when(jnp.logical_and(j == 0, i == 0))
        def _():
            dq_s[...] = jnp.zeros_like(dq_s)

        @pl.when(i == 0)
        def _():
            dk_s[...] = jnp.zeros_like(dk_s)
            dv_s[...] = jnp.zeros_like(dv_s)
            dc_s[...] = jnp.zeros_like(dc_s)

        def sweep(diagonal):
            p, dp = _fox_p_dp(q_ref, k_ref, v_ref, cq_ref, ck_ref, lse_ref, do_ref, j, diagonal, t)
            ds = p * (dp - dl_ref[...])
            dsb = ds.astype(BF16)
            tn = (((0,), (0,)), ((), ()))
            dv_s[...] += lax.dot_general(p.astype(BF16), do_ref[...], tn, preferred_element_type=F32)
            dk_s[...] += lax.dot_general(dsb, q_ref[...], tn, preferred_element_type=F32)
            dc_s[...] -= jnp.sum(ds, axis=0, keepdims=True)
            rows = pl.ds(pl.multiple_of(i * t, t), t)
            dq_s[rows, :] += jnp.dot(dsb, k_ref[...], preferred_element_type=F32)

        @pl.when(i > j)
        def _():
            sweep(False)

        @pl.when(i == j)
        def _():
            sweep(True)

        @pl.when(i == n - 1)
        def _():
            dk_ref[...] = (dk_s[...] * scale).astype(dk_ref.dtype)
            dv_ref[...] = dv_s[...].astype(dv_ref.dtype)
            dck_ref[...] = dc_s[...]

        @pl.when(jnp.logical_and(j == n - 1, i == n - 1))
        def _():
            dq_ref[...] = (dq_s[...] * scale).astype(dq_ref.dtype)

    qrow = lambda base: pl.BlockSpec((t, FOX_HD), lambda h, j, i: (jnp.maximum(i, j), base + h))
    kv = lambda base: pl.BlockSpec((t, FOX_HD), lambda h, j, i: (j, base + h))
    col = pl.BlockSpec((None, t, 1), lambda h, j, i: (h, jnp.maximum(i, j), 0))
    row = pl.BlockSpec((None, 1, t), lambda h, j, i: (h, 0, j))
    wide = jax.ShapeDtypeStruct((Lp, FOX_WIDTH), BF16)
    return pl.pallas_call(
        body, name="fox_bwd", grid=(FOX_HEADS, n, n),
        in_specs=[qrow(qb), kv(kb), kv(vb), col, row, col, col, qrow(0)],
        out_specs=[pl.BlockSpec((Lp, FOX_HD), lambda h, j, i: (0, h)), kv(0), kv(0), row],
        out_shape=[wide, wide, wide, jax.ShapeDtypeStruct((FOX_HEADS, 1, Lp), F32)],
        scratch_shapes=[pltpu.VMEM((Lp, FOX_HD), F32), pltpu.VMEM((t, FOX_HD), F32), pltpu.VMEM((t, FOX_HD), F32),
                        pltpu.VMEM((1, t), F32)],
        compiler_params=_params(("parallel", "arbitrary", "arbitrary")),
    )(main, main, main, c_col, c_row, lse, delta, doa)


def _shift_down(x, n):
    return pltpu.roll(x, n, 0)


def _shift_up(x, n):
    return pltpu.roll(x, x.shape[0] - n, 0)


def _prev_spec(tm, ct, cb):
    return pl.BlockSpec((HALO, ct), lambda g, i: (jnp.maximum(i * (tm // HALO) - 1, 0), cb(g)))


def _next_spec(tm, ct, cb, nrows):
    last = nrows // HALO - 1
    return pl.BlockSpec((HALO, ct), lambda g, i: (jnp.minimum((i + 1) * (tm // HALO), last), cb(g)))


def _cur_spec(tm, ct, cb):
    return pl.BlockSpec((tm, ct), lambda g, i: (i, cb(g)))


def _wrow(w_ref, k):
    return w_ref[k:k + 1, :]


def _rows3(s0, s1, s2, ct):
    r = lax.broadcasted_iota(jnp.int32, (8, ct), 0)
    return jnp.where(r == 0, s0, jnp.where(r == 1, s1, jnp.where(r == 2, s2, 0.0)))


def _acc_out(ref, i, val):
    @pl.when(i == 0)
    def _():
        ref[...] = val

    @pl.when(i > 0)
    def _():
        ref[...] += val


def _sconv_fwd(main, w8, Lp, tm):
    ct = 256
    G = CONV_CH // ct
    bb, cb, hb = (lambda g: SCB // ct + g), (lambda g: SCC // ct + g), (lambda g: SCH // ct + g)

    def body(b_ref, c_ref, h_ref, cp_ref, hp_ref, w_ref, o_ref):
        i = pl.program_id(1)
        z = c_ref[...].astype(F32) * h_ref[...].astype(F32)
        zp = jnp.where(i > 0, cp_ref[...].astype(F32) * hp_ref[...].astype(F32), 0.0)
        zz = jnp.concatenate([zp, z], axis=0)
        cz = (_wrow(w_ref, 0) * _shift_down(zz, 2)[HALO:] + _wrow(w_ref, 1) * _shift_down(zz, 1)[HALO:]
              + _wrow(w_ref, 2) * z)
        o_ref[...] = (b_ref[...].astype(F32) * cz).astype(o_ref.dtype)

    return pl.pallas_call(
        body, name="sconv_fwd", grid=(G, Lp // tm),
        in_specs=[_cur_spec(tm, ct, bb), _cur_spec(tm, ct, cb), _cur_spec(tm, ct, hb),
                  _prev_spec(tm, ct, cb), _prev_spec(tm, ct, hb), pl.BlockSpec((8, ct), lambda g, i: (0, g))],
        out_specs=pl.BlockSpec((tm, ct), lambda g, i: (i, g)),
        out_shape=jax.ShapeDtypeStruct((Lp, CONV_CH), BF16),
        compiler_params=_params(("parallel", "arbitrary")),
    )(main, main, main, main, main, w8)


def _sconv_bwd(main, w8, dub, Lp, tm):
    ct = 256
    G = CONV_CH // ct
    n = Lp // tm
    bb, cb, hb, ob = (lambda g: SCB // ct + g), (lambda g: SCC // ct + g), (lambda g: SCH // ct + g), (lambda g: g)

    def body(b_ref, c_ref, h_ref, cp_ref, hp_ref, bn_ref, d_ref, dn_ref, w_ref, db_ref, dc_ref, dh_ref, dw_ref):
        i = pl.program_id(1)
        b, c, h = b_ref[...].astype(F32), c_ref[...].astype(F32), h_ref[...].astype(F32)
        z = c * h
        zp = jnp.where(i > 0, cp_ref[...].astype(F32) * hp_ref[...].astype(F32), 0.0)
        zz = jnp.concatenate([zp, z], axis=0)
        z1, z2 = _shift_down(zz, 1)[HALO:], _shift_down(zz, 2)[HALO:]
        w0, w1, w2 = _wrow(w_ref, 0), _wrow(w_ref, 1), _wrow(w_ref, 2)
        cz = w0 * z2 + w1 * z1 + w2 * z
        dub_c = d_ref[...].astype(F32)
        db_ref[...] = (dub_c * cz).astype(db_ref.dtype)
        dcz = dub_c * b
        dcz_n = jnp.where(i < n - 1, dn_ref[...].astype(F32) * bn_ref[...].astype(F32), 0.0)
        dd = jnp.concatenate([dcz, dcz_n], axis=0)
        dz = w2 * dcz + w1 * _shift_up(dd, 1)[:tm] + w0 * _shift_up(dd, 2)[:tm]
        dc_ref[...] = (dz * h).astype(dc_ref.dtype)
        dh_ref[...] = (dz * c).astype(dh_ref.dtype)
        s = lambda x: jnp.sum(dcz * x, axis=0, keepdims=True)
        _acc_out(dw_ref, i, _rows3(s(z2), s(z1), s(z), ct))

    out = pl.BlockSpec((tm, ct), lambda g, i: (i, g))
    return pl.pallas_call(
        body, name="sconv_bwd", grid=(G, n),
        in_specs=[_cur_spec(tm, ct, bb), _cur_spec(tm, ct, cb), _cur_spec(tm, ct, hb),
                  _prev_spec(tm, ct, cb), _prev_spec(tm, ct, hb), _next_spec(tm, ct, bb, Lp),
                  _cur_spec(tm, ct, ob), _next_spec(tm, ct, ob, Lp), pl.BlockSpec((8, ct), lambda g, i: (0, g))],
        out_specs=[out, out, out, pl.BlockSpec((8, ct), lambda g, i: (0, g))],
        out_shape=[jax.ShapeDtypeStruct((Lp, CONV_CH), BF16)] * 3 + [jax.ShapeDtypeStruct((8, CONV_CH), F32)],
        compiler_params=_params(("parallel", "arbitrary")),
    )(main, main, main, main, main, main, dub, dub, w8)


def _conv3(w_ref, ext):
    return _wrow(w_ref, 0) * _shift_down(ext, 2) + _wrow(w_ref, 1) * _shift_down(ext, 1) + _wrow(w_ref, 2) * ext


def _mlp_act_fwd(up, w8, Lp, tm):
    ct = 256
    G = D_FF // ct
    gb, ub = (lambda g: g), (lambda g: G + g)

    def body(g_ref, u_ref, gp_ref, up_ref, wg_ref, wu_ref, o_ref):
        i = pl.program_id(1)

        def conv(cur, prev, w_ref):
            ext = jnp.concatenate([jnp.where(i > 0, prev[...].astype(F32), 0.0), cur[...].astype(F32)], axis=0)
            return _conv3(w_ref, ext)[HALO:]

        ug, uu = conv(g_ref, gp_ref, wg_ref), conv(u_ref, up_ref, wu_ref)
        o_ref[...] = (ug * _sigmoid(ug) * uu).astype(o_ref.dtype)

    wspec = lambda cb: pl.BlockSpec((8, ct), lambda g, i: (0, cb(g)))
    return pl.pallas_call(
        body, name="mlp_act_fwd", grid=(G, Lp // tm),
        in_specs=[_cur_spec(tm, ct, gb), _cur_spec(tm, ct, ub), _prev_spec(tm, ct, gb), _prev_spec(tm, ct, ub),
                  wspec(gb), wspec(ub)],
        out_specs=pl.BlockSpec((tm, ct), lambda g, i: (i, g)),
        out_shape=jax.ShapeDtypeStruct((Lp, D_FF), BF16),
        compiler_params=_params(("parallel", "arbitrary")),
    )(up, up, up, up, w8, w8)


def _mlp_act_bwd(up, w8, da, Lp, tm):
    ct = 256
    G = D_FF // ct
    n = Lp // tm
    gb, ub, ob = (lambda g: g), (lambda g: G + g), (lambda g: g)

    def body(g_ref, u_ref, gp_ref, up_ref, gn_ref, un_ref, d_ref, dn_ref, wg_ref, wu_ref,
             dg_ref, du_ref, dwg_ref, dwu_ref):
        i = pl.program_id(1)

        def ext_of(prev, cur, nxt):
            return jnp.concatenate([jnp.where(i > 0, prev[...].astype(F32), 0.0), cur[...].astype(F32),
                                    jnp.where(i < n - 1, nxt[...].astype(F32), 0.0)], axis=0)

        eg, eu = ext_of(gp_ref, g_ref, gn_ref), ext_of(up_ref, u_ref, un_ref)
        da_e = jnp.concatenate([jnp.zeros((HALO, ct), F32), d_ref[...].astype(F32),
                                jnp.where(i < n - 1, dn_ref[...].astype(F32), 0.0)], axis=0)
        ug, uu = _conv3(wg_ref, eg), _conv3(wu_ref, eu)
        sg = _sigmoid(ug)
        dug = da_e * uu * (sg * (1.0 + ug * (1.0 - sg)))
        duu = da_e * (ug * sg)
        cur = slice(HALO, HALO + tm)

        def back(w_ref, dx, e, dx_ref, dw_ref):
            d_in = _wrow(w_ref, 2) * dx + _wrow(w_ref, 1) * _shift_up(dx, 1) + _wrow(w_ref, 0) * _shift_up(dx, 2)
            dx_ref[...] = d_in[cur].astype(dx_ref.dtype)
            s = lambda x: jnp.sum(dx[cur] * x[cur], axis=0, keepdims=True)
            _acc_out(dw_ref, i, _rows3(s(_shift_down(e, 2)), s(_shift_down(e, 1)), s(e), ct))

        back(wg_ref, dug, eg, dg_ref, dwg_ref)
        back(wu_ref, duu, eu, du_ref, dwu_ref)

    wspec = lambda cb: pl.BlockSpec((8, ct), lambda g, i: (0, cb(g)))
    out = pl.BlockSpec((tm, ct), lambda g, i: (i, g))
    return pl.pallas_call(
        body, name="mlp_act_bwd", grid=(G, n),
        in_specs=[_cur_spec(tm, ct, gb), _cur_spec(tm, ct, ub), _prev_spec(tm, ct, gb), _prev_spec(tm, ct, ub),
                  _next_spec(tm, ct, gb, Lp), _next_spec(tm, ct, ub, Lp), _cur_spec(tm, ct, ob),
                  _next_spec(tm, ct, ob, Lp), wspec(gb), wspec(ub)],
        out_specs=[out, out, wspec(ob), wspec(ob)],
        out_shape=[jax.ShapeDtypeStruct((Lp, D_FF), BF16)] * 2 + [jax.ShapeDtypeStruct((8, D_FF), F32)] * 2,
        compiler_params=_params(("parallel", "arbitrary")),
    )(up, up, up, up, up, up, da, da, w8, w8)


def _gla_chunk(q, k, v, g, s0):
    C = BLOCK
    r_i = lax.broadcasted_iota(jnp.int32, (C, C), 0)
    c_i = lax.broadcasted_iota(jnp.int32, (C, C), 1)
    row = lax.broadcasted_iota(jnp.int32, (C, GLA_DK), 0)
    b = jnp.dot((c_i <= r_i).astype(F32), g, precision=HIGHEST, preferred_element_type=F32)
    row_of = lambda n: jnp.sum(jnp.where(row == n, b, 0.0), axis=0, keepdims=True)
    refs = [row_of(n * GLA_SUB) for n in range(C // GLA_SUB)]
    sub = jnp.bitwise_and(row, -GLA_SUB)
    ref_all = sum(jnp.where(sub == n * GLA_SUB, refs[n], 0.0) for n in range(C // GLA_SUB))
    qs = q * (GLA_DK ** -0.5)
    qt = (qs * jnp.exp(b - ref_all)).astype(BF16)
    sub_start = jnp.bitwise_and(r_i, -GLA_SUB)
    att = jnp.zeros((C, C), F32)
    for n in range(C // GLA_SUB):
        kt = (k * jnp.exp(jnp.minimum(refs[n] - b, 60.0))).astype(BF16)
        a_n = lax.dot_general(qt, kt, (((1,), (1,)), ((), ())), preferred_element_type=F32)
        att = att + jnp.where((sub_start == n * GLA_SUB) & (c_i <= r_i), a_n, 0.0)
    o = (jnp.dot(att.astype(BF16), v.astype(BF16), preferred_element_type=F32)
         + jnp.dot((qs * jnp.exp(b)).astype(BF16), s0.astype(BF16), preferred_element_type=F32))
    kd = (k * jnp.exp(row_of(C - 1) - b)).astype(BF16)
    last_rows = (lax.broadcasted_iota(jnp.int32, (C, GLA_DV), 0) == C - 1).astype(F32)
    decay = lax.dot_general(b, last_rows, (((0,), (0,)), ((), ())), precision=HIGHEST,
                            preferred_element_type=F32)
    s1 = jnp.exp(decay) * s0 + lax.dot_general(kd, v.astype(BF16), (((0,), (0,)), ((), ())),
                                               preferred_element_type=F32)
    return o, s1


def _gla_fwd(main, logg, Lp):
    n = Lp // BLOCK
    qb, kb, vb = QC // GLA_DK, KC // GLA_DK, VC // GLA_DV

    def body(q_ref, k_ref, v_ref, g_ref, o_ref, st_ref, s_s):
        c = pl.program_id(1)

        @pl.when(c == 0)
        def _():
            s_s[...] = jnp.zeros_like(s_s)

        s0 = s_s[...]
        st_ref[...] = s0
        o, s1 = _gla_chunk(q_ref[...].astype(F32), k_ref[...].astype(F32), v_ref[...].astype(F32), g_ref[...], s0)
        o_ref[...] = o
        s_s[...] = s1

    return pl.pallas_call(
        body, name="gla_fwd", grid=(GLA_HEADS, n),
        in_specs=[pl.BlockSpec((BLOCK, GLA_DK), lambda h, c: (c, qb + h)),
                  pl.BlockSpec((BLOCK, GLA_DK), lambda h, c: (c, kb + h)),
                  pl.BlockSpec((BLOCK, GLA_DV), lambda h, c: (c, vb + h)),
                  pl.BlockSpec((BLOCK, GLA_DK), lambda h, c: (c, h))],
        out_specs=[pl.BlockSpec((BLOCK, GLA_DV), lambda h, c: (c, h)),
                   pl.BlockSpec((None, None, GLA_DK, GLA_DV), lambda h, c: (h, c, 0, 0))],
        out_shape=[jax.ShapeDtypeStruct((Lp, GLA_HEADS * GLA_DV), F32),
                   jax.ShapeDtypeStruct((GLA_HEADS, n, GLA_DK, GLA_DV), F32)],
        scratch_shapes=[pltpu.VMEM((GLA_DK, GLA_DV), F32)],
        compiler_params=_params(("parallel", "arbitrary")),
    )(main, main, main, logg)


def _gla_bwd(main, logg, states, do, Lp):
    n = Lp // BLOCK
    qb, kb, vb = QC // GLA_DK, KC // GLA_DK, VC // GLA_DV

    def body(q_ref, k_ref, v_ref, g_ref, st_ref, do_ref, dq_ref, dk_ref, dv_ref, dg_ref, ds_s):
        c = pl.program_id(1)

        @pl.when(c == 0)
        def _():
            ds_s[...] = jnp.zeros_like(ds_s)

        _, vjp = jax.vjp(_gla_chunk, q_ref[...].astype(F32), k_ref[...].astype(F32), v_ref[...].astype(F32),
                         g_ref[...], st_ref[...])
        dq, dk, dv, dg, ds0 = vjp((do_ref[...], ds_s[...]))
        dq_ref[...] = dq.astype(dq_ref.dtype)
        dk_ref[...] = dk.astype(dk_ref.dtype)
        dv_ref[...] = dv.astype(dv_ref.dtype)
        dg_ref[...] = dg
        ds_s[...] = ds0

    rk = lambda base: pl.BlockSpec((BLOCK, GLA_DK), lambda h, c: (n - 1 - c, base + h))
    rv = lambda base: pl.BlockSpec((BLOCK, GLA_DV), lambda h, c: (n - 1 - c, base + h))
    return pl.pallas_call(
        body, name="gla_bwd", grid=(GLA_HEADS, n),
        in_specs=[rk(qb), rk(kb), rv(vb), rk(0),
                  pl.BlockSpec((None, None, GLA_DK, GLA_DV), lambda h, c: (h, n - 1 - c, 0, 0)), rv(0)],
        out_specs=[rk(0), rk(0), rv(0), rk(0)],
        out_shape=[jax.ShapeDtypeStruct((Lp, GLA_HEADS * GLA_DK), BF16), jax.ShapeDtypeStruct((Lp, GLA_HEADS * GLA_DK), BF16),
                   jax.ShapeDtypeStruct((Lp, GLA_HEADS * GLA_DV), BF16), jax.ShapeDtypeStruct((Lp, GLA_HEADS * GLA_DK), F32)],
        scratch_shapes=[pltpu.VMEM((GLA_DK, GLA_DV), F32)],
        compiler_params=_params(("parallel", "arbitrary")),
    )(main, main, main, logg, states, do)


def _loss_head(h, g, target, Lp):
    t = BLOCK
    D = D_MODEL

    def body(h_ref, g_ref, t_ref, loss_ref, dh_ref, dg_ref):
        i = pl.program_id(0)
        x = h_ref[...]
        tok = (i * t + lax.broadcasted_iota(jnp.int32, (t, 1), 0)) >= BLOCK
        r = lax.rsqrt(jnp.mean(x * x, axis=-1, keepdims=True) + EPS)
        nrm = x * r
        e = jnp.where(tok, nrm * g_ref[...] - t_ref[...], 0.0)
        part = 0.5 * jnp.sum(jnp.sum(e * e, axis=1, keepdims=True), axis=0, keepdims=True) / D
        dy = e / D
        dn = dy * g_ref[...]
        dh_ref[...] = r * (dn - nrm * jnp.mean(dn * nrm, axis=-1, keepdims=True))
        _acc_out(dg_ref, i, jnp.sum(dy * nrm, axis=0, keepdims=True))
        _acc_out(loss_ref, i, jnp.broadcast_to(part, (1, LANES)))

    return pl.pallas_call(
        body, name="loss_head", grid=(Lp // t,),
        in_specs=[pl.BlockSpec((t, D), lambda i: (i, 0)), pl.BlockSpec((1, D), lambda i: (0, 0)),
                  pl.BlockSpec((t, D), lambda i: (jnp.maximum(i - 1, 0), 0))],
        out_specs=[pl.BlockSpec((1, LANES), lambda i: (0, 0)), pl.BlockSpec((t, D), lambda i: (i, 0)),
                   pl.BlockSpec((1, D), lambda i: (0, 0))],
        out_shape=[jax.ShapeDtypeStruct((1, LANES), F32), jax.ShapeDtypeStruct((Lp, D), F32),
                   jax.ShapeDtypeStruct((1, D), F32)],
        compiler_params=_params(("arbitrary",)),
    )(h, g, target)


def _adamw(w, g, m, v, name):
    if w.ndim == 1:
        outs = _adamw(*(a.reshape(1, -1) for a in (w, g, m, v)), name)
        return tuple(o.reshape(w.shape) for o in outs)
    if w.ndim == 3 and w.shape[-1] % LANES and w.shape[-2] % LANES == 0:
        outs = _adamw(*(a.transpose(2, 0, 1) for a in (w, g, m, v)), name)
        return tuple(o.transpose(1, 2, 0) for o in outs)
    rows, cols = w.shape[-2:]
    budget_rows = max(8, ADAM_BLOCK_BYTES // (4 * cols))
    tr = rows if rows <= budget_rows else _pick(rows, tuple(t for t in (512, 256, 128, 64, 32, 16, 8) if t <= budget_rows))
    lead = 1 if w.ndim < 3 else _pick(w.shape[0], (6, 4, 3, 2, 1)) if tr == rows else 1

    def body(w_ref, g_ref, m_ref, v_ref, go_ref, d_ref, nm_ref, nv_ref):
        gg = g_ref[...]
        mm = ADAM_B1 * m_ref[...] + (1.0 - ADAM_B1) * gg
        vv = ADAM_B2 * v_ref[...] + (1.0 - ADAM_B2) * jnp.square(gg)
        m_hat = mm / (1.0 - ADAM_B1 ** ADAM_STEP)
        v_hat = vv / (1.0 - ADAM_B2 ** ADAM_STEP)
        d_ref[...] = -ADAM_LR * (m_hat / (jnp.sqrt(v_hat) + ADAM_EPS) + ADAM_WD * w_ref[...])
        go_ref[...] = gg
        nm_ref[...] = mm
        nv_ref[...] = vv

    if w.ndim == 3:
        spec, grid = pl.BlockSpec((lead, tr, cols), lambda l, i: (l, i, 0)), (w.shape[0] // lead, rows // tr)
    else:
        spec, grid = pl.BlockSpec((tr, cols), lambda i: (i, 0)), (rows // tr,)
    return pl.pallas_call(
        body, name=name, grid=grid, in_specs=[spec] * 4, out_specs=[spec] * 4,
        out_shape=[jax.ShapeDtypeStruct(w.shape, F32)] * 4,
        compiler_params=_params(("parallel",) * len(grid)),
    )(w, g, m, v)


def _place():
    x, y, c = lax.axis_index("x"), lax.axis_index("y"), lax.axis_index("c")
    chips = [(1 - x, y), (x, 1 - y), (1 - x, 1 - y)]
    return x, y, c, chips


def _rcopy(src, dst, send_sems, recv_sems, k, to):
    return pltpu.make_async_remote_copy(src_ref=src, dst_ref=dst, send_sem=send_sems.at[k], recv_sem=recv_sems.at[k],
                                        device_id=to, device_id_type=MESH)


def _any_spec():
    return pl.BlockSpec(memory_space=pl.ANY)


def _shard_ref(ref, mode, t, r, c):
    if mode == "rows":
        return ref.at[pl.ds(pl.multiple_of(t * r, 16), r), :]
    if mode == "cols":
        return ref.at[:, pl.ds(pl.multiple_of(t * c, LANES), c)]
    return ref.at[t]


def _gathered_shape(mode, r, c):
    return {"rows": (4 * r, c), "cols": (r, 4 * c), "stack": (4, r, c)}[mode]


def _place_own(shard, mode, me1, name):
    _, r, c = shard.shape
    tr = _ew_rows(r, c)
    blk = {"rows": (None, tr, c), "cols": (None, tr, c), "stack": (None, None, tr, c)}[mode]
    idx = {"rows": lambda l, i, me: (l, me[0] * (r // tr) + i, 0),
           "cols": lambda l, i, me: (l, i, me[0]),
           "stack": lambda l, i, me: (l, me[0], i, 0)}[mode]

    def body(me_ref, in_ref, out_ref):
        out_ref[...] = in_ref[...]

    return pl.pallas_call(
        body, name=name,
        grid_spec=pltpu.PrefetchScalarGridSpec(
            num_scalar_prefetch=1, grid=(DEPTH, r // tr),
            in_specs=[pl.BlockSpec((None, tr, c), lambda l, i, me: (l, i, 0))],
            out_specs=pl.BlockSpec(blk, idx)),
        out_shape=jax.ShapeDtypeStruct((DEPTH,) + _gathered_shape(mode, r, c), shard.dtype),
        compiler_params=_params(("parallel", "parallel")),
    )(me1, shard)


def _gather_weights(shards, modes):
    n = len(shards)
    dims = [s.shape[1:] for s in shards]
    me1 = jnp.reshape(2 * lax.axis_index("x") + lax.axis_index("y"), (1,)).astype(jnp.int32)
    placed = [_place_own(shards[k], modes[k], me1, f"gather_place_{k}") for k in range(n)]

    def body(*refs):
        ins, outs = refs[:n], refs[2 * n:3 * n]
        send_sems, recv_sems = refs[3 * n:]
        x, y, c, chips = _place()
        me = 2 * x + y
        place = lambda k, l, t: _shard_ref(outs[k].at[l], modes[k], t, *dims[k])
        first = [_rcopy(ins[k].at[c], place(k, c, me), send_sems, recv_sems, 6 * k + j, (*chip, c))
                 for j, chip in enumerate(chips) for k in range(n)]
        for cp in first:
            cp.start()
        passed = []
        for j, (px, py) in enumerate(chips):
            for k in range(n):
                blk = place(k, c, 2 * px + py)
                _rcopy(blk, blk, send_sems, recv_sems, 6 * k + j, (x, y, c)).wait_recv()
                fwd = _rcopy(blk, blk, send_sems, recv_sems, 6 * k + 3 + j, (x, y, 1 - c))
                fwd.start()
                passed.append(fwd)
        for j, (px, py) in enumerate(chips):
            for k in range(n):
                blk = place(k, 1 - c, 2 * px + py)
                _rcopy(blk, blk, send_sems, recv_sems, 6 * k + 3 + j, (x, y, c)).wait_recv()
        for cp in first + passed:
            cp.wait_send()

    return pl.pallas_call(
        body, name="gather_weights", in_specs=[_any_spec()] * (2 * n), out_specs=[_any_spec()] * n,
        out_shape=[jax.ShapeDtypeStruct(a.shape, a.dtype) for a in placed],
        input_output_aliases={n + k: k for k in range(n)},
        scratch_shapes=[pltpu.SemaphoreType.DMA((6 * n,)), pltpu.SemaphoreType.DMA((6 * n,))],
    )(*shards, *placed)


def _swap_layers(gs):
    n = len(gs)

    def body(*refs):
        ins, outs = refs[:n], refs[n:2 * n]
        send_sems, recv_sems = refs[2 * n:]
        x, y, c, _ = _place()
        cps = [_rcopy(ins[k].at[1 - c], outs[k], send_sems, recv_sems, k, (x, y, 1 - c)) for k in range(n)]
        for cp in cps:
            cp.start()
        for cp in cps:
            cp.wait()

    return pl.pallas_call(
        body, name="rs_swap_layers", in_specs=[_any_spec()] * n, out_specs=[_any_spec()] * n,
        out_shape=[jax.ShapeDtypeStruct(g.shape[1:], g.dtype) for g in gs],
        scratch_shapes=[pltpu.SemaphoreType.DMA((n,)), pltpu.SemaphoreType.DMA((n,))],
    )(*gs)


def _scatter_chips(hs, modes, dims):
    n = len(hs)

    def body(*refs):
        ins, outs = refs[:n], refs[n:2 * n]
        send_sems, recv_sems, local_sems = refs[2 * n:]
        x, y, c, chips = _place()
        me = 2 * x + y
        part = lambda k, t: _shard_ref(ins[k], modes[k], t, *dims[k])
        own = [pltpu.make_async_copy(part(k, me), outs[k].at[me], local_sems.at[k]) for k in range(n)]
        for cp in own:
            cp.start()
        cps = [_rcopy(part(k, 2 * px + py), outs[k].at[me], send_sems, recv_sems, 3 * k + j, (px, py, c))
               for j, (px, py) in enumerate(chips) for k in range(n)]
        for cp in cps:
            cp.start()
        for j, (px, py) in enumerate(chips):
            for k in range(n):
                blk = outs[k].at[2 * px + py]
                _rcopy(blk, blk, send_sems, recv_sems, 3 * k + j, (x, y, c)).wait_recv()
        for cp in cps:
            cp.wait_send()
        for cp in own:
            cp.wait()

    return pl.pallas_call(
        body, name="rs_scatter_chips", in_specs=[_any_spec()] * n, out_specs=[_any_spec()] * n,
        out_shape=[jax.ShapeDtypeStruct((4,) + tuple(dims[k]), hs[k].dtype) for k in range(n)],
        scratch_shapes=[pltpu.SemaphoreType.DMA((3 * n,)), pltpu.SemaphoreType.DMA((3 * n,)),
                        pltpu.SemaphoreType.DMA((n,))],
    )(*hs)


def _join_layers(fs):
    n = len(fs)

    def body(*refs):
        outs = refs[n:2 * n]
        send_sems, recv_sems = refs[2 * n:]
        x, y, c, _ = _place()
        cps = [_rcopy(outs[k].at[c], outs[k].at[c], send_sems, recv_sems, k, (x, y, 1 - c)) for k in range(n)]
        for cp in cps:
            cp.start()
        for k in range(n):
            blk = outs[k].at[1 - c]
            _rcopy(blk, blk, send_sems, recv_sems, k, (x, y, c)).wait_recv()
        for cp in cps:
            cp.wait_send()

    return pl.pallas_call(
        body, name="rs_join_layers", in_specs=[_any_spec()] * n, out_specs=[_any_spec()] * n,
        out_shape=[jax.ShapeDtypeStruct(f.shape, f.dtype) for f in fs],
        input_output_aliases={k: k for k in range(n)},
        scratch_shapes=[pltpu.SemaphoreType.DMA((n,)), pltpu.SemaphoreType.DMA((n,))],
    )(*fs)


def _ew_rows(M, N):
    fit = [t for t in (512, 256, 128, 64, 32, 16) if M % t == 0 and t * N * 4 <= EW_BLOCK_BYTES]
    return fit[0] if fit else M


def _add_own(g, other, c1, out_dtype, name):
    _, M, N = g.shape
    tr = _ew_rows(M, N)

    def body(c_ref, g_ref, o_ref, out_ref):
        out_ref[...] = (g_ref[...] + o_ref[...]).astype(out_ref.dtype)

    return pl.pallas_call(
        body, name=name,
        grid_spec=pltpu.PrefetchScalarGridSpec(
            num_scalar_prefetch=1, grid=(M // tr,),
            in_specs=[pl.BlockSpec((None, tr, N), lambda i, cr: (cr[0], i, 0)),
                      pl.BlockSpec((tr, N), lambda i, cr: (i, 0))],
            out_specs=pl.BlockSpec((tr, N), lambda i, cr: (i, 0))),
        out_shape=jax.ShapeDtypeStruct((M, N), out_dtype),
        compiler_params=_params(("parallel",)),
    )(c1, g, other)


def _sum_chips(q, c1, name):
    _, M, N = q.shape
    tr = _ew_rows(M, N)

    def body(c_ref, q_ref, out_ref):
        out_ref[...] = ((q_ref[0].astype(F32) + q_ref[1].astype(F32)) + q_ref[2].astype(F32)) + q_ref[3].astype(F32)

    return pl.pallas_call(
        body, name=name,
        grid_spec=pltpu.PrefetchScalarGridSpec(
            num_scalar_prefetch=1, grid=(M // tr,),
            in_specs=[pl.BlockSpec((4, tr, N), lambda i, cr: (0, i, 0))],
            out_specs=pl.BlockSpec((None, tr, N), lambda i, cr: (cr[0], i, 0))),
        out_shape=jax.ShapeDtypeStruct((DEPTH, M, N), F32),
        compiler_params=_params(("parallel",)),
    )(c1, q)


def _reduce_scatter(gs, modes, dims, wire):
    c1 = jnp.reshape(lax.axis_index("c"), (1,)).astype(jnp.int32)
    flat = lambda a, lead: a.reshape(a.shape[:lead] + (-1, a.shape[-1]))
    others = _swap_layers(gs)
    hs = [_add_own(flat(g, 1), flat(o, 0), c1, wire[k], f"rs_add_own_{k}").reshape(o.shape)
          for k, (g, o) in enumerate(zip(gs, others))]
    qs = _scatter_chips(hs, modes, dims)
    fs = [_sum_chips(q, c1, f"rs_sum_chips_{k}") for k, q in enumerate(qs)]
    return _join_layers(fs)


def _allreduce_small(v):
    R, C = v.shape

    def body(v_ref, sum_ref, all_ref, send_sems, recv_sems):
        x, y, c, _ = _place()
        me = 4 * x + 2 * y + c
        rows = lambda d: all_ref.at[pl.ds(pl.multiple_of(d * R, 8), R), :]

        def peer(k):
            flip = lambda bit, v: (1 - v) if ((k + 1) >> bit) & 1 else v
            return flip(2, x), flip(1, y), flip(0, c)

        outs = [_rcopy(v_ref, rows(me), send_sems, recv_sems, k, peer(k)) for k in range(7)]
        for cp in outs:
            cp.start()
        all_ref[pl.ds(pl.multiple_of(me * R, 8), R), :] = v_ref[...]
        for k in range(7):
            px, py, pc = peer(k)
            blk = rows(4 * px + 2 * py + pc)
            _rcopy(blk, blk, send_sems, recv_sems, k, (x, y, c)).wait_recv()
        for cp in outs:
            cp.wait_send()
        tot = all_ref[0:R, :]
        for d in range(1, 8):
            tot = tot + all_ref[d * R:(d + 1) * R, :]
        sum_ref[...] = tot

    vm = pl.BlockSpec(memory_space=pltpu.VMEM)
    return pl.pallas_call(
        body, name="allreduce_small", in_specs=[vm], out_specs=[vm, vm],
        out_shape=[jax.ShapeDtypeStruct((R, C), F32), jax.ShapeDtypeStruct((8 * R, C), F32)],
        scratch_shapes=[pltpu.SemaphoreType.DMA((7,)), pltpu.SemaphoreType.DMA((7,))],
    )(v)[0]


def _size(shape):
    n = 1
    for d in shape:
        n *= d
    return n


def _pack(pieces, dtype):
    flat = jnp.concatenate([p.astype(dtype).reshape(-1) for p in pieces])
    rows = -(-flat.shape[0] // (PACK_COLS * 16)) * 16
    return jnp.pad(flat, (0, rows * PACK_COLS - flat.shape[0])).reshape(rows, PACK_COLS)


def _unpack(buf, shapes):
    flat = buf.reshape(-1)
    out, pos = [], 0
    for s in shapes:
        n = _size(s)
        out.append(flat[pos:pos + n].reshape(s))
        pos += n
    return out


def _small_piece(name, arr, l):
    if name == "meta_tokens":
        return arr[l * (N_META // DEPTH):(l + 1) * (N_META // DEPTH)]
    return arr[l]


def _prep_w_in(w_in4):
    w_in = jnp.concatenate([w_in4[t] for t in range(4)], axis=1)
    col = lambda a, n: w_in[:, _R[a]:_R[a] + n]
    main = jnp.concatenate([col("qa", 3072), col("scb", 3072), col("qc", 3072), col("ga", 6144)], axis=1)
    zpad = lambda n: jnp.zeros((D_MODEL, n), w_in.dtype)
    side = jnp.concatenate([col("fa", 8), zpad(LANES - 8), col("glr", GLA_RANK), zpad(LANES - GLA_RANK)], axis=1)
    return main.astype(BF16), side.astype(BF16)


def _w_in_cols(dmain, dside, lo, hi):
    segs = ((0, _R["fa"], dmain, 0), (_R["fa"], _R["scb"], dside, 0), (_R["scb"], _R["glr"], dmain, SCB),
            (_R["glr"], _R["ga"], dside, LANES), (_R["ga"], N_IN, dmain, GA))
    parts = [src[..., off + max(a, lo) - a:off + min(b, hi) - a] for a, b, src, off in segs if max(a, lo) < min(b, hi)]
    return jnp.concatenate(parts, axis=-1)


def _pad_rows(a, rows):
    return jnp.pad(a.astype(F32), ((0, rows - a.shape[0]), (0, 0)))


def _row2(v):
    return v.reshape(1, -1).astype(F32)


def _layer_fwd(h, p, rep, l, Lp, tm, ta):
    tag = lambda s: f"{s}_l{l}"
    g1, g2 = _row2(rep["norm1_g"][l]), _row2(rep["norm2_g"][l])
    bf = jnp.pad(_row2(rep["fox_b_f"][l]), ((0, 0), (0, LANES - FOX_HEADS)))
    gate_b, b_g, gnorm = _row2(rep["gate_b"][l]), _row2(rep["gla_b_g"][l]), _row2(rep["gla_norm_g"][l])
    (xn,) = _rw_fwd(tag("rms1_fwd"), _f_rms, [Row(h, D_MODEL)], [Const(g1)], [(D_MODEL, BF16)], Lp, BLOCK)
    main = _mm(xn, p["main"][l], "nn", BF16, tag("proj_main"))
    side = _mm(xn, p["side"][l], "nn", F32, tag("proj_side"))
    c = _fox_gate_fwd(side, bf, Lp)
    c_t = c[:, :FOX_HEADS].T
    c_col, c_row = c_t[:, :, None], c_t[:, None, :]
    oa, lse = _fox_fwd(main, c_col, c_row, Lp, ta)
    ya = _mm(oa, p["w_a_o"], "nn", BF16, tag("ya"), b_lead=l)
    ub = _sconv_fwd(main, p["conv_w"][l], Lp, tm)
    yb = _mm(ub, p["w_b_o"], "nn", BF16, tag("yb"), b_lead=l)
    glr = Row(side, LANES, lambda g: 1)
    (logg,) = _rw_fwd(tag("logg_fwd"), _f_logg, [glr], [Const(p["w_g2"][l]), Const(b_g)], [(512, F32)], Lp, tm)
    oc, states = _gla_fwd(main, logg, Lp)
    rc = Row(main, GLA_DV, lambda g: RC // GLA_DV + g)
    gn = Const(gnorm, (1, GLA_DV), lambda g: (0, g))
    (uc,) = _rw_fwd(tag("gla_post_fwd"), _f_gla_post, [Row(oc, GLA_DV), rc], [gn], [(GLA_DV, BF16)], Lp, tm,
                    G=GLA_HEADS)
    yc = _mm(uc, p["w_c_o"], "nn", BF16, tag("yc"), b_lead=l)
    cw = 512
    G = D_MODEL // cw
    mrows = [Row(ya, cw), Row(yb, cw), Row(yc, cw), Row(main, cw, lambda g: GA // cw + g),
             Row(main, cw, lambda g: GB // cw + g), Row(main, cw, lambda g: GC // cw + g)]
    mconsts = [Const(gate_b, (1, cw), lambda g, k=k: (0, k * G + g)) for k in range(3)]
    (mix,) = _rw_fwd(tag("merge_fwd"), _f_merge, mrows, mconsts, [(cw, BF16)], Lp, tm, G=G)
    h1 = _mm(mix, p["w_o"], "nn", F32, tag("h1"), add=h, b_lead=l)
    (xn2,) = _rw_fwd(tag("rms2_fwd"), _f_rms, [Row(h1, D_MODEL)], [Const(g2)], [(D_MODEL, BF16)], Lp, BLOCK)
    up = _mm(xn2, p["w_up"], "nn", BF16, tag("up"), b_lead=l)
    act = _mlp_act_fwd(up, p["mlp_conv_w"][l], Lp, tm)
    h2 = _mm(act, p["w_down"], "nn", F32, tag("h2"), add=h1, b_lead=l)
    res = dict(h=h, xn=xn, main=main, side=side, c_col=c_col, c_row=c_row, oa=oa, lse=lse, ya=ya, ub=ub, yb=yb,
               logg=logg, oc=oc, states=states, uc=uc, yc=yc, mix=mix, h1=h1, xn2=xn2, up=up, act=act,
               g1=g1, g2=g2, bf=bf, gate_b=gate_b, b_g=b_g, gnorm=gnorm)
    return h2, res


def _layer_bwd(dh2, p, r, l, Lp, tm, ta, big):
    tag = lambda s: f"{s}_l{l}"
    g = {}

    def wgrad(name, a, b):
        big[name] = _mm(a, b, "tn", F32, tag("d_" + name), slot=(big.get(name), l))

    wgrad("w_down", r["act"], dh2)
    dact = _mm(dh2, p["w_down"], "nt", BF16, tag("d_act"), b_lead=l)
    dgate, dval, dwg, dwu = _mlp_act_bwd(r["up"], p["mlp_conv_w"][l], dact, Lp, tm)
    g["mlp_conv_w"] = jnp.concatenate([dwg[:3], dwu[:3]], axis=1)
    dup = jnp.concatenate([dgate, dval], axis=1)
    wgrad("w_up", r["xn2"], dup)
    dxn2 = _mm(dup, p["w_up"], "nt", F32, tag("d_xn2"), b_lead=l)
    (dh1,), (dg2,) = _rw_bwd(tag("rms2_bwd"), _f_rms, [Row(r["h1"], D_MODEL)], [Const(r["g2"])],
                             [Row(dxn2, D_MODEL)], [F32], [dh2], Lp, BLOCK)
    g["norm2_g"] = dg2[0]
    wgrad("w_o", r["mix"], dh1)
    dmix = _mm(dh1, p["w_o"], "nt", BF16, tag("d_mix"), b_lead=l)
    cw = 512
    G = D_MODEL // cw
    main = r["main"]
    mrows = [Row(r["ya"], cw), Row(r["yb"], cw), Row(r["yc"], cw), Row(main, cw, lambda g_: GA // cw + g_),
             Row(main, cw, lambda g_: GB // cw + g_), Row(main, cw, lambda g_: GC // cw + g_)]
    mconsts = [Const(r["gate_b"], (1, cw), lambda g_, k=k: (0, k * G + g_)) for k in range(3)]
    (dya, dyb, dyc, dga, dgb, dgc), dbs = _rw_bwd(tag("merge_bwd"), _f_merge, mrows, mconsts, [Row(dmix, cw)],
                                                  [BF16] * 6, [None] * 6, Lp, tm, G=G)
    g["gate_b"] = jnp.concatenate([dbs[k][0, k * D_MODEL:(k + 1) * D_MODEL] for k in range(3)])
    wgrad("w_a_o", r["oa"], dya)
    doa = _mm(dya, p["w_a_o"], "nt", BF16, tag("d_oa"), b_lead=l)
    wgrad("w_b_o", r["ub"], dyb)
    dub = _mm(dyb, p["w_b_o"], "nt", BF16, tag("d_ub"), b_lead=l)
    wgrad("w_c_o", r["uc"], dyc)
    duc = _mm(dyc, p["w_c_o"], "nt", BF16, tag("d_uc"), b_lead=l)
    delta = _fox_delta(main, r["c_col"], r["c_row"], r["lse"], doa, Lp, ta)
    dq, dk, dv, dck = _fox_bwd(main, r["c_col"], r["c_row"], r["lse"], delta, doa, Lp, ta)
    dc = jnp.pad(dck[:, 0, :].T, ((0, 0), (0, LANES - FOX_HEADS)))
    dfa, dbf = _fox_gate_bwd(r["side"], r["bf"], dc, Lp)
    g["fox_b_f"] = dbf[0, :FOX_HEADS]
    dscb, dscc, dsch, dcw = _sconv_bwd(main, p["conv_w"][l], dub, Lp, tm)
    g["conv_w"] = dcw[:3]
    rc = Row(main, GLA_DV, lambda g_: RC // GLA_DV + g_)
    gn = Const(r["gnorm"], (1, GLA_DV), lambda g_: (0, g_))
    (doc, drc), (dgn,) = _rw_bwd(tag("gla_post_bwd"), _f_gla_post, [Row(r["oc"], GLA_DV), rc], [gn],
                                 [Row(duc, GLA_DV)], [F32, BF16], [None, None], Lp, tm, G=GLA_HEADS)
    g["gla_norm_g"] = dgn[0]
    dqc, dkc, dvc, dlogg = _gla_bwd(main, r["logg"], r["states"], doc, Lp)
    glr = Row(r["side"], LANES, lambda g_: 1)
    (dglr,), (dwg2, dbg) = _rw_bwd(tag("logg_bwd"), _f_logg, [glr], [Const(p["w_g2"][l]), Const(r["b_g"])],
                                   [Row(dlogg, 512)], [F32], [None], Lp, tm)
    g["gla_w_g2"] = dwg2[:GLA_RANK]
    g["gla_b_g"] = dbg[0]
    dmain = jnp.concatenate([dq, dk, dv, dscb, dscc, dsch, dqc, dkc, dvc, drc, dga, dgb, dgc], axis=1)
    dside = jnp.concatenate([dfa, dglr], axis=1)
    wgrad("main", r["xn"], dmain)
    wgrad("side", r["xn"], dside)
    dxn = _mm(dmain, p["main"][l], "nt", F32, tag("d_xn_main"))
    dxn = _mm(dside, p["side"][l], "nt", F32, tag("d_xn_side"), add=dxn)
    (dh,), (dg1,) = _rw_bwd(tag("rms1_bwd"), _f_rms, [Row(r["h"], D_MODEL)], [Const(r["g1"])], [Row(dxn, D_MODEL)],
                            [F32], [dh1], Lp, BLOCK)
    g["norm1_g"] = dg1[0]
    return dh, g


def _local_step(x, target, meta, p, rep):
    seq = x.shape[0]
    Lp = PAD + N_META + seq
    tm = _pick(Lp, (640, 384, 128))
    ta = tm
    h = jnp.concatenate([jnp.zeros((PAD, D_MODEL), F32), meta.astype(F32), x], axis=0)
    saved = []
    for l in range(DEPTH):
        h, res = _layer_fwd(h, p, rep, l, Lp, tm, ta)
        saved.append(res)
    loss, dh, dgf = _loss_head(h, _row2(rep["final_norm_g"]), target, Lp)
    big, small = {}, [None] * DEPTH
    for l in reversed(range(DEPTH)):
        dh, small[l] = _layer_bwd(dh, p, saved[l], l, Lp, tm, ta, big)
    return loss[0, 0], dh[BLOCK:], dh[PAD:BLOCK], big, small, dgf[0]


def kernel(x, meta_tokens, norm1_g, w_in, fox_b_f, gate_b, conv_w, gla_w_g2, gla_b_g, gla_norm_g, w_a_o, w_b_o, w_c_o, w_o, norm2_g, w_up, mlp_conv_w, w_down, final_norm_g, loss_target, m_meta_tokens, m_norm1_g, m_w_in, m_fox_b_f, m_gate_b, m_conv_w, m_gla_w_g2, m_gla_b_g, m_gla_norm_g, m_w_a_o, m_w_b_o, m_w_c_o, m_w_o, m_norm2_g, m_w_up, m_mlp_conv_w, m_w_down, m_final_norm_g, v_meta_tokens, v_norm1_g, v_w_in, v_fox_b_f, v_gate_b, v_conv_w, v_gla_w_g2, v_gla_b_g, v_gla_norm_g, v_w_a_o, v_w_b_o, v_w_c_o, v_w_o, v_norm2_g, v_w_up, v_mlp_conv_w, v_w_down, v_final_norm_g):
    given = dict(locals())
    weights = {n: given[n] for n in WEIGHT_ORDER}
    rep = {n: weights[n] for n, _ in REPLICATED}
    big_names = [n for n, _ in BIG]
    big_modes = [m for _, m in BIG] + ["stack"]
    small_shapes = [(s[0], s[1] // 4) for _, s in SMALL]
    exact = [k for k, (n, _) in enumerate(SMALL) if n in GATHER_F32]

    def small_wire(l):
        ws = [_small_piece(n, weights[n], l) for n, _ in SMALL]
        his = [w.astype(BF16) for w in ws]
        return his + [(ws[k] - his[k].astype(F32)).astype(BF16) for k in exact]

    shards = [weights[n].astype(BF16) for n in big_names] + [jnp.stack([_pack(small_wire(l), BF16) for l in range(DEPTH)])]
    gathered = _gather_weights(shards, big_modes)
    gw = dict(zip(big_names, gathered[:-1]))
    p = {n: gw[n] for n in big_names if n != "w_in"}
    p["main"], p["side"] = zip(*[_prep_w_in(gw["w_in"][l]) for l in range(DEPTH)])
    small_full = []
    for l in range(DEPTH):
        per_chip = [_unpack(gathered[-1][l, t], small_shapes + [small_shapes[k] for k in exact]) for t in range(4)]
        full = [jnp.concatenate([per_chip[t][k] for t in range(4)], axis=1).astype(F32) for k in range(len(per_chip[0]))]
        for e, k in enumerate(exact):
            full[k] = full[k] + full[len(SMALL) + e]
        small_full.append(dict(zip([n for n, _ in SMALL], full[:len(SMALL)])))
    p["conv_w"] = [_pad_rows(s["conv_w"], 8) for s in small_full]
    p["mlp_conv_w"] = [_pad_rows(s["mlp_conv_w"], 8) for s in small_full]
    p["w_g2"] = [_pad_rows(s["gla_w_g2"], LANES) for s in small_full]
    meta_full = jnp.concatenate([s["meta_tokens"] for s in small_full], axis=0)

    loss, grad_x, grad_meta, big, small, d_final = _local_step(x[0], loss_target[0], meta_full, p, rep)
    loss = lax.psum(loss, ("x", "y", "c"))

    big["w_in"] = jnp.stack([_w_in_cols(big["main"], big["side"], t * (N_IN // 4), (t + 1) * (N_IN // 4))
                             for t in range(4)], axis=1)
    for l in range(DEPTH):
        small[l]["meta_tokens"] = grad_meta[l * (N_META // DEPTH):(l + 1) * (N_META // DEPTH)]
    shard_of = lambda a, t: lax.slice_in_dim(a, t * (a.shape[1] // 4), (t + 1) * (a.shape[1] // 4), axis=1)
    small_g = jnp.stack([jnp.stack([_pack([shard_of(small[l][n], t) for n, _ in SMALL], F32) for t in range(4)])
                         for l in range(DEPTH)])
    dims = [shards[k].shape[1:] for k in range(len(BIG))] + [small_g.shape[2:]]
    summed = _reduce_scatter([big[n] for n in big_names] + [small_g], big_modes, dims,
                             [BF16] * len(BIG) + [F32])
    gout = dict(zip(big_names, summed[:-1]))
    pieces = [_unpack(summed[-1][l], small_shapes) for l in range(DEPTH)]
    for k, (n, _) in enumerate(SMALL):
        per_layer = [pieces[l][k] for l in range(DEPTH)]
        gout[n] = jnp.concatenate(per_layer, axis=0) if n == "meta_tokens" else jnp.stack(per_layer)

    rep_g = {n: (d_final if n == "final_norm_g" else jnp.stack([small[l][n] for l in range(DEPTH)])) for n, _ in REPLICATED}
    flat = jnp.concatenate([rep_g[n].astype(F32).reshape(-1) for n, _ in REPLICATED])
    rrows = -(-flat.shape[0] // (PACK_COLS * 8)) * 8
    summed_small = _allreduce_small(jnp.pad(flat, (0, rrows * PACK_COLS - flat.shape[0])).reshape(rrows, PACK_COLS))
    pos = 0
    for n, shape in REPLICATED:
        gout[n] = summed_small.reshape(-1)[pos:pos + _size(shape)].reshape(shape)
        pos += _size(shape)

    deltas, new_m, new_v = {}, {}, {}
    for n in WEIGHT_ORDER:
        gout[n], deltas[n], new_m[n], new_v[n] = _adamw(weights[n], gout[n], given["m_" + n], given["v_" + n],
                                                        "adamw_" + n)
    return (loss, grad_x[None], *[gout[n] for n in WEIGHT_ORDER], *[deltas[n] for n in WEIGHT_ORDER],
            *[new_m[n] for n in WEIGHT_ORDER], *[new_v[n] for n in WEIGHT_ORDER])
```

```python
import functools

import jax
import jax.numpy as jnp
from jax import lax
from jax.experimental import pallas as pl
from jax.experimental.pallas import tpu as pltpu

F32, BF16 = jnp.float32, jnp.bfloat16
HIGHEST = lax.Precision.HIGHEST
MESH = pl.DeviceIdType.MESH

N_META = 16
BLOCK = 128
LANES = 128
PAD = BLOCK - N_META
EPS = 1e-6
NEG = -1e30
HALO = 16
VMEM_LIMIT = 56 * 1024 * 1024
ADAM_BLOCK_BYTES = 1 << 20
EW_BLOCK_BYTES = 3 << 19

D_MODEL = 2048
FOX_HEADS, FOX_HD = 8, 128
FOX_WIDTH = FOX_HEADS * FOX_HD
CONV_CH = 1024
GLA_HEADS, GLA_DK, GLA_DV, GLA_RANK, GLA_TAU = 4, 128, 256, 16, 16.0
GLA_SUB = 32
D_FF = 5632
N_IN = 15384
DEPTH = 2

_R = dict(qa=0, ka=1024, va=2048, fa=3072, scb=3080, scc=4104, sch=5128, qc=6152, kc=6664,
          vc=7176, rc=8200, glr=9224, ga=9240, gb=11288, gc=13336)
QA, KA, VA, SCB, SCC, SCH, QC, KC, VC, RC, GA, GB, GC = (
    0, 1024, 2048, 3072, 4096, 5120, 6144, 6656, 7168, 8192, 9216, 11264, 13312)
N_MAIN = 15360
N_SIDE = 256

ADAM_LR, ADAM_B1, ADAM_B2, ADAM_EPS, ADAM_WD, ADAM_STEP = 0.001, 0.9, 0.999, 1e-08, 0.01, 10

BIG = (("w_in", "stack"), ("w_a_o", "cols"), ("w_b_o", "cols"), ("w_c_o", "cols"), ("w_o", "rows"), ("w_up", "cols"),
       ("w_down", "rows"))
SMALL = (("conv_w", (3, CONV_CH)), ("mlp_conv_w", (3, 2 * D_FF)), ("gla_w_g2", (GLA_RANK, 512)),
         ("meta_tokens", (N_META // DEPTH, D_MODEL)))
REPLICATED = (("norm1_g", (2, D_MODEL)), ("fox_b_f", (2, 8)), ("gate_b", (2, 3 * D_MODEL)), ("gla_b_g", (2, 512)),
              ("gla_norm_g", (2, 1024)), ("norm2_g", (2, D_MODEL)), ("final_norm_g", (D_MODEL,)))
WEIGHT_ORDER = ("meta_tokens", "norm1_g", "w_in", "fox_b_f", "gate_b", "conv_w", "gla_w_g2", "gla_b_g",
                "gla_norm_g", "w_a_o", "w_b_o", "w_c_o", "w_o", "norm2_g", "w_up", "mlp_conv_w", "w_down",
                "final_norm_g")
PACK_COLS = 1024
GATHER_F32 = ("conv_w", "mlp_conv_w", "meta_tokens")


def _pick(n, cands):
    for c in cands:
        if n % c == 0:
            return c
    return n


def _params(sem):
    return pltpu.CompilerParams(dimension_semantics=sem, vmem_limit_bytes=VMEM_LIMIT)


def _sigmoid(x):
    return jax.nn.sigmoid(x)


def _log_sigmoid(x):
    return jnp.minimum(x, 0.0) - jnp.log(1.0 + jnp.exp(-jnp.abs(x)))


def _mm(a, b, mode, out_dtype, name, add=None, b_lead=None, slot=None):
    bshape = b.shape if b_lead is None else b.shape[1:]
    if mode == "nn":
        (M, K), (K2, N) = a.shape, bshape
    elif mode == "nt":
        (M, K), (N, K2) = a.shape, bshape
    else:
        (K, M), (K2, N) = a.shape, bshape
    assert K == K2, (name, a.shape, b.shape)
    if mode == "tn":
        tm = _pick(M, (2048, 1408, 1024, 512, 256, 128))
        tn = _pick(N, (1024, 512, 256, 128))
        tk = _pick(K, (640, 512, 384, 256, 128))
    else:
        tm = _pick(M, (1664, 640, 384, 128))
        tn = _pick(N, (512, 256, 128))
        tk = K if K <= 2048 else _pick(K, (1408, 1024, 512, 256, 128))
    nk = K // tk
    dims = {"nn": (((1,), (0,)), ((), ())), "nt": (((1,), (1,)), ((), ())), "tn": (((0,), (0,)), ((), ()))}[mode]
    n_in = 2 + (add is not None) + (slot is not None and slot[0] is not None)

    def body(*refs):
        a_ref, b_ref = refs[:2]
        add_ref = refs[2] if add is not None else None
        o_ref, acc = refs[n_in:]
        k = pl.program_id(2)

        @pl.when(k == 0)
        def _():
            acc[...] = jnp.zeros_like(acc)

        acc[...] += lax.dot_general(a_ref[...].astype(BF16), b_ref[...].astype(BF16), dims,
                                    preferred_element_type=F32)

        @pl.when(k == nk - 1)
        def _():
            r = acc[...]
            if add is not None:
                r = r + add_ref[...].astype(F32)
            o_ref[...] = r.astype(o_ref.dtype)

    a_spec = {"nn": pl.BlockSpec((tm, tk), lambda i, j, k: (i, k)),
              "nt": pl.BlockSpec((tm, tk), lambda i, j, k: (i, k)),
              "tn": pl.BlockSpec((tk, tm), lambda i, j, k: (k, i))}[mode]
    b_blk, b_idx = {"nn": ((tk, tn), lambda i, j, k: (k, j)),
                    "nt": ((tn, tk), lambda i, j, k: (j, k)),
                    "tn": ((tk, tn), lambda i, j, k: (k, j))}[mode]
    if b_lead is None:
        b_spec = pl.BlockSpec(b_blk, b_idx)
    else:
        b_spec = pl.BlockSpec((None,) + b_blk, lambda i, j, k: (b_lead,) + b_idx(i, j, k))
    o_spec = pl.BlockSpec((tm, tn), lambda i, j, k: (i, j))
    ins, specs = [a, b], [a_spec, b_spec]
    if add is not None:
        ins.append(add)
        specs.append(o_spec)
    aliases = {}
    out_shape = jax.ShapeDtypeStruct((M, N), out_dtype)
    if slot is not None:
        buf, l = slot
        o_spec = pl.BlockSpec((None, tm, tn), lambda i, j, k: (l, i, j))
        out_shape = jax.ShapeDtypeStruct((DEPTH, M, N), out_dtype)
        if buf is not None:
            aliases = {len(ins): 0}
            ins.append(buf)
            specs.append(pl.BlockSpec(memory_space=pl.ANY))
    return pl.pallas_call(
        body, name=name, grid=(M // tm, N // tn, nk), in_specs=specs, out_specs=o_spec, out_shape=out_shape,
        scratch_shapes=[pltpu.VMEM((tm, tn), F32)], input_output_aliases=aliases,
        compiler_params=_params(("parallel", "parallel", "arbitrary")),
    )(*ins)


class Row:
    def __init__(self, arr, w, cb=None):
        self.arr, self.w, self.cb = arr, w, (cb if cb is not None else (lambda g: g))


class Const:
    def __init__(self, arr, shape=None, idx=None):
        self.arr = arr
        self.shape = shape if shape is not None else arr.shape
        self.idx = idx if idx is not None else (lambda g: (0,) * arr.ndim)


def _row_spec(r, tm):
    return pl.BlockSpec((tm, r.w), lambda g, i, r=r: (i, r.cb(g)))


def _const_spec(c):
    return pl.BlockSpec(c.shape, lambda g, i, c=c: c.idx(g))


def _valid_rows(i, tm):
    return (i * tm + lax.broadcasted_iota(jnp.int32, (tm, 1), 0)) >= PAD


def _rw_fwd(name, f, rows, consts, outs, Lp, tm, G=1):
    nr, nc = len(rows), len(consts)

    def body(*refs):
        i = pl.program_id(1)
        rv = [r[...].astype(F32) for r in refs[:nr]]
        cv = [r[...].astype(F32) for r in refs[nr:nr + nc]]
        res = f(_valid_rows(i, tm), *rv, *cv)
        for o_ref, v in zip(refs[nr + nc:], res):
            o_ref[...] = v.astype(o_ref.dtype)

    return pl.pallas_call(
        body, name=name, grid=(G, Lp // tm),
        in_specs=[_row_spec(r, tm) for r in rows] + [_const_spec(c) for c in consts],
        out_specs=[pl.BlockSpec((tm, w), lambda g, i: (i, g)) for w, _ in outs],
        out_shape=[jax.ShapeDtypeStruct((Lp, w * G), dt) for w, dt in outs],
        compiler_params=_params(("parallel", "arbitrary")),
    )(*[r.arr for r in rows], *[c.arr for c in consts])


def _rw_bwd(name, f, rows, consts, cts, drow_dtypes, adds, Lp, tm, G=1):
    nr, nc, nt = len(rows), len(consts), len(cts)
    want = [k for k, dt in enumerate(drow_dtypes) if dt is not None]
    add_k = [k for k in want if adds[k] is not None]

    def body(*refs):
        i = pl.program_id(1)
        pos = 0
        rv = [r[...].astype(F32) for r in refs[pos:pos + nr]]
        pos += nr
        cv = [r[...].astype(F32) for r in refs[pos:pos + nc]]
        pos += nc
        tv = [r[...].astype(F32) for r in refs[pos:pos + nt]]
        pos += nt
        av = {k: refs[pos + n][...].astype(F32) for n, k in enumerate(add_k)}
        pos += len(add_k)
        drow_refs = refs[pos:pos + len(want)]
        pos += len(want)
        dconst_refs = refs[pos:pos + nc]
        valid = _valid_rows(i, tm)
        _, vjp = jax.vjp(lambda *a: tuple(f(valid, *a)), *rv, *cv)
        grads = vjp(tuple(tv))
        for o_ref, k in zip(drow_refs, want):
            gk = grads[k]
            if k in av:
                gk = gk + av[k]
            o_ref[...] = gk.astype(o_ref.dtype)
        for n, o_ref in enumerate(dconst_refs):
            gc = grads[nr + n]

            @pl.when(i == 0)
            def _(o_ref=o_ref, gc=gc):
                o_ref[...] = gc

            @pl.when(i > 0)
            def _(o_ref=o_ref, gc=gc):
                o_ref[...] += gc

    out_row = lambda w: pl.BlockSpec((tm, w), lambda g, i: (i, g))
    res = pl.pallas_call(
        body, name=name, grid=(G, Lp // tm),
        in_specs=([_row_spec(r, tm) for r in rows] + [_const_spec(c) for c in consts]
                  + [_row_spec(r, tm) for r in cts] + [out_row(rows[k].w) for k in add_k]),
        out_specs=[out_row(rows[k].w) for k in want] + [_const_spec(c) for c in consts],
        out_shape=([jax.ShapeDtypeStruct((Lp, rows[k].w * G), drow_dtypes[k]) for k in want]
                   + [jax.ShapeDtypeStruct(c.arr.shape, F32) for c in consts]),
        compiler_params=_params(("parallel", "arbitrary")),
    )(*[r.arr for r in rows], *[c.arr for c in consts], *[r.arr for r in cts], *[adds[k] for k in add_k])
    drows = [None] * nr
    for n, k in enumerate(want):
        drows[k] = res[n]
    return drows, list(res[len(want):])


def _f_rms(valid, h, g):
    r = lax.rsqrt(jnp.mean(h * h, axis=-1, keepdims=True) + EPS)
    return (jnp.where(valid, h * r * g, 0.0),)


def _f_logg(valid, glr, w, b):
    pre = jnp.dot(glr.astype(BF16), w.astype(BF16), preferred_element_type=F32) + b
    return (jnp.where(valid, _log_sigmoid(pre) / GLA_TAU, 0.0),)


def _f_gla_post(valid, oc, rc, g):
    y = oc * lax.rsqrt(jnp.mean(oc * oc, axis=-1, keepdims=True) + EPS) * g
    return (jnp.where(valid, rc * _sigmoid(rc) * y, 0.0),)


def _f_merge(valid, ya, yb, yc, ga, gb, gc, ba, bb, bc):
    mix = _sigmoid(ga + ba) * ya + _sigmoid(gb + bb) * yb + _sigmoid(gc + bc) * yc
    return (jnp.where(valid, mix, 0.0),)


def _fox_gate_fwd(side, bf, Lp):
    t = BLOCK
    n = Lp // t

    def body(s_ref, b_ref, c_ref, carry):
        i = pl.program_id(0)

        @pl.when(i == 0)
        def _():
            carry[...] = jnp.zeros_like(carry)

        lane = lax.broadcasted_iota(jnp.int32, (t, LANES), 1)
        ok = _valid_rows(i, t) & (lane < FOX_HEADS)
        logf = jnp.where(ok, _log_sigmoid(s_ref[...] + b_ref[...]), 0.0)
        tril = (lax.broadcasted_iota(jnp.int32, (t, t), 1) <= lax.broadcasted_iota(jnp.int32, (t, t), 0)).astype(F32)
        c = jnp.dot(tril, logf, precision=HIGHEST, preferred_element_type=F32) + carry[...]
        c_ref[...] = c
        carry[...] = c[t - 1:t, :]

    return pl.pallas_call(
        body, name="fox_gate_fwd", grid=(n,),
        in_specs=[pl.BlockSpec((t, LANES), lambda i: (i, 0)), pl.BlockSpec((1, LANES), lambda i: (0, 0))],
        out_specs=pl.BlockSpec((t, LANES), lambda i: (i, 0)),
        out_shape=jax.ShapeDtypeStruct((Lp, LANES), F32),
        scratch_shapes=[pltpu.VMEM((1, LANES), F32)],
        compiler_params=_params(("arbitrary",)),
    )(side, bf)


def _fox_gate_bwd(side, bf, dc, Lp):
    t = BLOCK
    n = Lp // t

    def body(s_ref, b_ref, dc_ref, dfa_ref, db_ref, carry):
        i = pl.program_id(0)

        @pl.when(i == 0)
        def _():
            carry[...] = jnp.zeros_like(carry)

        lane = lax.broadcasted_iota(jnp.int32, (t, LANES), 1)
        ok = _valid_rows(n - 1 - i, t) & (lane < FOX_HEADS)
        triu = (lax.broadcasted_iota(jnp.int32, (t, t), 1) >= lax.broadcasted_iota(jnp.int32, (t, t), 0)).astype(F32)
        dlogf = jnp.dot(triu, dc_ref[...], precision=HIGHEST, preferred_element_type=F32) + carry[...]
        carry[...] = dlogf[0:1, :]
        dpre = jnp.where(ok, dlogf * _sigmoid(-(s_ref[...] + b_ref[...])), 0.0)
        dfa_ref[...] = dpre
        part = jnp.sum(dpre, axis=0, keepdims=True)

        @pl.when(i == 0)
        def _():
            db_ref[...] = part

        @pl.when(i > 0)
        def _():
            db_ref[...] += part

    rev = lambda i: (n - 1 - i, 0)
    return pl.pallas_call(
        body, name="fox_gate_bwd", grid=(n,),
        in_specs=[pl.BlockSpec((t, LANES), rev), pl.BlockSpec((1, LANES), lambda i: (0, 0)),
                  pl.BlockSpec((t, LANES), rev)],
        out_specs=[pl.BlockSpec((t, LANES), rev), pl.BlockSpec((1, LANES), lambda i: (0, 0))],
        out_shape=[jax.ShapeDtypeStruct((Lp, LANES), F32), jax.ShapeDtypeStruct((1, LANES), F32)],
        scratch_shapes=[pltpu.VMEM((1, LANES), F32)],
        compiler_params=_params(("arbitrary",)),
    )(side, bf, dc)


def _fox_key_bias(cq_ref, ck_ref, j, t):
    col = j * t + lax.broadcasted_iota(jnp.int32, (1, t), 1)
    return jnp.where(col >= PAD, ck_ref[...] - cq_ref[0:1, :], -NEG)


def _fox_s(q, k, bias, diagonal, t):
    s = lax.dot_general(q, k, (((1,), (1,)), ((), ())), preferred_element_type=F32) * (FOX_HD ** -0.5) - bias
    if diagonal:
        causal = lax.broadcasted_iota(jnp.int32, (t, t), 1) <= lax.broadcasted_iota(jnp.int32, (t, t), 0)
        s = jnp.where(causal, s, NEG)
    return s


def _fox_fwd(main, c_col, c_row, Lp, t):
    n = Lp // t
    qb, kb, vb = QA // FOX_HD, KA // FOX_HD, VA // FOX_HD

    def body(q_ref, k_ref, v_ref, cq_ref, ck_ref, o_ref, ox_ref, lse_ref, m_s, l_s, acc):
        i, j = pl.program_id(1), pl.program_id(2)

        @pl.when(j == 0)
        def _():
            m_s[...] = jnp.full_like(m_s, NEG)
            l_s[...] = jnp.zeros_like(l_s)
            acc[...] = jnp.zeros_like(acc)

        def update(diagonal):
            s = _fox_s(q_ref[...], k_ref[...], _fox_key_bias(cq_ref, ck_ref, j, t), diagonal, t)
            m_new = jnp.maximum(m_s[...], jnp.max(s, axis=1, keepdims=True))
            alpha = jnp.exp(m_s[...] - m_new)
            p = jnp.exp(s - m_new)
            l_s[...] = alpha * l_s[...] + jnp.sum(p, axis=1, keepdims=True)
            p_hi = p.astype(BF16)
            p_lo = (p - p_hi.astype(F32)).astype(BF16)
            pv = (jnp.dot(p_hi, v_ref[...], preferred_element_type=F32)
                  + jnp.dot(p_lo, v_ref[...], preferred_element_type=F32))
            acc[...] = alpha * acc[...] + pv
            m_s[...] = m_new

        @pl.when(j < i)
        def _():
            update(False)

        @pl.when(j == i)
        def _():
            update(True)
            o = jnp.where(_valid_rows(i, t), acc[...] / l_s[...], 0.0)
            o_ref[...] = o.astype(o_ref.dtype)
            ox_ref[...] = o
            lse_ref[...] = m_s[...] + jnp.log(l_s[...])

    kv = lambda base: pl.BlockSpec((t, FOX_HD), lambda h, i, j: (jnp.minimum(j, i), base + h))
    return pl.pallas_call(
        body, name="fox_fwd", grid=(FOX_HEADS, n, n),
        in_specs=[pl.BlockSpec((t, FOX_HD), lambda h, i, j: (i, qb + h)), kv(kb), kv(vb),
                  pl.BlockSpec((None, t, 1), lambda h, i, j: (h, i, 0)),
                  pl.BlockSpec((None, 1, t), lambda h, i, j: (h, 0, jnp.minimum(j, i)))],
        out_specs=[pl.BlockSpec((t, FOX_HD), lambda h, i, j: (i, h)), pl.BlockSpec((t, FOX_HD), lambda h, i, j: (i, h)),
                   pl.BlockSpec((None, t, 1), lambda h, i, j: (h, i, 0))],
        out_shape=[jax.ShapeDtypeStruct((Lp, FOX_WIDTH), BF16), jax.ShapeDtypeStruct((Lp, FOX_WIDTH), F32),
                   jax.ShapeDtypeStruct((FOX_HEADS, Lp, 1), F32)],
        scratch_shapes=[pltpu.VMEM((t, 1), F32), pltpu.VMEM((t, 1), F32), pltpu.VMEM((t, FOX_HD), F32)],
        compiler_params=_params(("parallel", "parallel", "arbitrary")),
    )(main, main, main, c_col, c_row)


def _fox_p_dp(q_ref, k_ref, v_ref, cq_ref, ck_ref, lse_ref, do_ref, j, diagonal, t):
    s = _fox_s(q_ref[...], k_ref[...], _fox_key_bias(cq_ref, ck_ref, j, t), diagonal, t)
    p = jnp.exp(s - lse_ref[...])
    dp = lax.dot_general(do_ref[...], v_ref[...], (((1,), (1,)), ((), ())), preferred_element_type=F32)
    return p, dp


def _fox_delta(ox, doa, Lp, t):
    n = Lp // t

    def body(o_ref, do_ref, dl_ref):
        dl_ref[...] = jnp.sum(o_ref[...] * do_ref[...].astype(F32), axis=1, keepdims=True)

    blk = pl.BlockSpec((t, FOX_HD), lambda h, i: (i, h))
    return pl.pallas_call(
        body, name="fox_delta", grid=(FOX_HEADS, n), in_specs=[blk, blk],
        out_specs=pl.BlockSpec((None, t, 1), lambda h, i: (h, i, 0)),
        out_shape=jax.ShapeDtypeStruct((FOX_HEADS, Lp, 1), F32),
        compiler_params=_params(("parallel", "parallel")),
    )(ox, doa)


def _fox_bwd(main, c_col, c_row, lse, delta, doa, Lp, t):
    n = Lp // t
    qb, kb, vb = QA // FOX_HD, KA // FOX_HD, VA // FOX_HD
    scale = FOX_HD ** -0.5

    def body(q_ref, k_ref, v_ref, cq_ref, ck_ref, lse_ref, dl_ref, do_ref, dq_ref, dk_ref, dv_ref, dck_ref,
             dq_s, dk_s, dv_s, dc_s):
        j, i = pl.program_id(1), pl.program_id(2)

        @pl.when(jnp.logical_and(j == 0, i == 0))
        def _():
            dq_s[...] = jnp.zeros_like(dq_s)

        @pl.when(i == 0)
        def _():
            dk_s[...] = jnp.zeros_like(dk_s)
            dv_s[...] = jnp.zeros_like(dv_s)
            dc_s[...] = jnp.zeros_like(dc_s)

        def sweep(diagonal):
            p, dp = _fox_p_dp(q_ref, k_ref, v_ref, cq_ref, ck_ref, lse_ref, do_ref, j, diagonal, t)
            ds = p * (dp - dl_ref[...])
            dsb = ds.astype(BF16)
            tn = (((0,), (0,)), ((), ()))
            dv_s[...] += lax.dot_general(p.astype(BF16), do_ref[...], tn, preferred_element_type=F32)
            dk_s[...] += lax.dot_general(dsb, q_ref[...], tn, preferred_element_type=F32)
            dc_s[...] -= jnp.sum(ds, axis=0, keepdims=True)
            rows = pl.ds(pl.multiple_of(i * t, t), t)
            dq_s[rows, :] += jnp.dot(dsb, k_ref[...], preferred_element_type=F32)

        @pl.when(i > j)
        def _():
            sweep(False)

        @pl.when(i == j)
        def _():
            sweep(True)

        @pl.when(i == n - 1)
        def _():
            dk_ref[...] = (dk_s[...] * scale).astype(dk_ref.dtype)
            dv_ref[...] = dv_s[...].astype(dv_ref.dtype)
            dck_ref[...] = dc_s[...]

        @pl.when(jnp.logical_and(j == n - 1, i == n - 1))
        def _():
            dq_ref[...] = (dq_s[...] * scale).astype(dq_ref.dtype)

    qrow = lambda base: pl.BlockSpec((t, FOX_HD), lambda h, j, i: (jnp.maximum(i, j), base + h))
    kv = lambda base: pl.BlockSpec((t, FOX_HD), lambda h, j, i: (j, base + h))
    col = pl.BlockSpec((None, t, 1), lambda h, j, i: (h, jnp.maximum(i, j), 0))
    row = pl.BlockSpec((None, 1, t), lambda h, j, i: (h, 0, j))
    wide = jax.ShapeDtypeStruct((Lp, FOX_WIDTH), BF16)
    return pl.pallas_call(
        body, name="fox_bwd", grid=(FOX_HEADS, n, n),
        in_specs=[qrow(qb), kv(kb), kv(vb), col, row, col, col, qrow(0)],
        out_specs=[pl.BlockSpec((Lp, FOX_HD), lambda h, j, i: (0, h)), kv(0), kv(0), row],
        out_shape=[wide, wide, wide, jax.ShapeDtypeStruct((FOX_HEADS, 1, Lp), F32)],
        scratch_shapes=[pltpu.VMEM((Lp, FOX_HD), F32), pltpu.VMEM((t, FOX_HD), F32), pltpu.VMEM((t, FOX_HD), F32),
                        pltpu.VMEM((1, t), F32)],
        compiler_params=_params(("parallel", "arbitrary", "arbitrary")),
    )(main, main, main, c_col, c_row, lse, delta, doa)


def _shift_down(x, n):
    return pltpu.roll(x, n, 0)


def _shift_up(x, n):
    return pltpu.roll(x, x.shape[0] - n, 0)


def _prev_spec(tm, ct, cb):
    return pl.BlockSpec((HALO, ct), lambda g, i: (jnp.maximum(i * (tm // HALO) - 1, 0), cb(g)))


def _next_spec(tm, ct, cb, nrows):
    last = nrows // HALO - 1
    return pl.BlockSpec((HALO, ct), lambda g, i: (jnp.minimum((i + 1) * (tm // HALO), last), cb(g)))


def _cur_spec(tm, ct, cb):
    return pl.BlockSpec((tm, ct), lambda g, i: (i, cb(g)))


def _wrow(w_ref, k):
    return w_ref[k:k + 1, :]


def _rows3(s0, s1, s2, ct):
    r = lax.broadcasted_iota(jnp.int32, (8, ct), 0)
    return jnp.where(r == 0, s0, jnp.where(r == 1, s1, jnp.where(r == 2, s2, 0.0)))


def _acc_out(ref, i, val):
    @pl.when(i == 0)
    def _():
        ref[...] = val

    @pl.when(i > 0)
    def _():
        ref[...] += val


def _sconv_fwd(main, w8, Lp, tm):
    ct = 256
    G = CONV_CH // ct
    bb, cb, hb = (lambda g: SCB // ct + g), (lambda g: SCC // ct + g), (lambda g: SCH // ct + g)

    def body(b_ref, c_ref, h_ref, cp_ref, hp_ref, w_ref, o_ref):
        i = pl.program_id(1)
        z = c_ref[...].astype(F32) * h_ref[...].astype(F32)
        zp = jnp.where(i > 0, cp_ref[...].astype(F32) * hp_ref[...].astype(F32), 0.0)
        zz = jnp.concatenate([zp, z], axis=0)
        cz = (_wrow(w_ref, 0) * _shift_down(zz, 2)[HALO:] + _wrow(w_ref, 1) * _shift_down(zz, 1)[HALO:]
              + _wrow(w_ref, 2) * z)
        o_ref[...] = (b_ref[...].astype(F32) * cz).astype(o_ref.dtype)

    return pl.pallas_call(
        body, name="sconv_fwd", grid=(G, Lp // tm),
        in_specs=[_cur_spec(tm, ct, bb), _cur_spec(tm, ct, cb), _cur_spec(tm, ct, hb),
                  _prev_spec(tm, ct, cb), _prev_spec(tm, ct, hb), pl.BlockSpec((8, ct), lambda g, i: (0, g))],
        out_specs=pl.BlockSpec((tm, ct), lambda g, i: (i, g)),
        out_shape=jax.ShapeDtypeStruct((Lp, CONV_CH), BF16),
        compiler_params=_params(("parallel", "arbitrary")),
    )(main, main, main, main, main, w8)


def _sconv_bwd(main, w8, dub, Lp, tm):
    ct = 256
    G = CONV_CH // ct
    n = Lp // tm
    bb, cb, hb, ob = (lambda g: SCB // ct + g), (lambda g: SCC // ct + g), (lambda g: SCH // ct + g), (lambda g: g)

    def body(b_ref, c_ref, h_ref, cp_ref, hp_ref, bn_ref, d_ref, dn_ref, w_ref, db_ref, dc_ref, dh_ref, dw_ref):
        i = pl.program_id(1)
        b, c, h = b_ref[...].astype(F32), c_ref[...].astype(F32), h_ref[...].astype(F32)
        z = c * h
        zp = jnp.where(i > 0, cp_ref[...].astype(F32) * hp_ref[...].astype(F32), 0.0)
        zz = jnp.concatenate([zp, z], axis=0)
        z1, z2 = _shift_down(zz, 1)[HALO:], _shift_down(zz, 2)[HALO:]
        w0, w1, w2 = _wrow(w_ref, 0), _wrow(w_ref, 1), _wrow(w_ref, 2)
        cz = w0 * z2 + w1 * z1 + w2 * z
        dub_c = d_ref[...].astype(F32)
        db_ref[...] = (dub_c * cz).astype(db_ref.dtype)
        dcz = dub_c * b
        dcz_n = jnp.where(i < n - 1, dn_ref[...].astype(F32) * bn_ref[...].astype(F32), 0.0)
        dd = jnp.concatenate([dcz, dcz_n], axis=0)
        dz = w2 * dcz + w1 * _shift_up(dd, 1)[:tm] + w0 * _shift_up(dd, 2)[:tm]
        dc_ref[...] = (dz * h).astype(dc_ref.dtype)
        dh_ref[...] = (dz * c).astype(dh_ref.dtype)
        s = lambda x: jnp.sum(dcz * x, axis=0, keepdims=True)
        _acc_out(dw_ref, i, _rows3(s(z2), s(z1), s(z), ct))

    out = pl.BlockSpec((tm, ct), lambda g, i: (i, g))
    return pl.pallas_call(
        body, name="sconv_bwd", grid=(G, n),
        in_specs=[_cur_spec(tm, ct, bb), _cur_spec(tm, ct, cb), _cur_spec(tm, ct, hb),
                  _prev_spec(tm, ct, cb), _prev_spec(tm, ct, hb), _next_spec(tm, ct, bb, Lp),
                  _cur_spec(tm, ct, ob), _next_spec(tm, ct, ob, Lp), pl.BlockSpec((8, ct), lambda g, i: (0, g))],
        out_specs=[out, out, out, pl.BlockSpec((8, ct), lambda g, i: (0, g))],
        out_shape=[jax.ShapeDtypeStruct((Lp, CONV_CH), BF16)] * 3 + [jax.ShapeDtypeStruct((8, CONV_CH), F32)],
        compiler_params=_params(("parallel", "arbitrary")),
    )(main, main, main, main, main, main, dub, dub, w8)


def _conv3(w_ref, ext):
    return _wrow(w_ref, 0) * _shift_down(ext, 2) + _wrow(w_ref, 1) * _shift_down(ext, 1) + _wrow(w_ref, 2) * ext


def _mlp_act_fwd(up, w8, Lp, tm):
    ct = 256
    G = D_FF // ct
    gb, ub = (lambda g: g), (lambda g: G + g)

    def body(g_ref, u_ref, gp_ref, up_ref, wg_ref, wu_ref, o_ref):
        i = pl.program_id(1)

        def conv(cur, prev, w_ref):
            ext = jnp.concatenate([jnp.where(i > 0, prev[...].astype(F32), 0.0), cur[...].astype(F32)], axis=0)
            return _conv3(w_ref, ext)[HALO:]

        ug, uu = conv(g_ref, gp_ref, wg_ref), conv(u_ref, up_ref, wu_ref)
        o_ref[...] = (ug * _sigmoid(ug) * uu).astype(o_ref.dtype)

    wspec = lambda cb: pl.BlockSpec((8, ct), lambda g, i: (0, cb(g)))
    return pl.pallas_call(
        body, name="mlp_act_fwd", grid=(G, Lp // tm),
        in_specs=[_cur_spec(tm, ct, gb), _cur_spec(tm, ct, ub), _prev_spec(tm, ct, gb), _prev_spec(tm, ct, ub),
                  wspec(gb), wspec(ub)],
        out_specs=pl.BlockSpec((tm, ct), lambda g, i: (i, g)),
        out_shape=jax.ShapeDtypeStruct((Lp, D_FF), BF16),
        compiler_params=_params(("parallel", "arbitrary")),
    )(up, up, up, up, w8, w8)


def _mlp_act_bwd(up, w8, da, Lp, tm):
    ct = 256
    G = D_FF // ct
    n = Lp // tm
    gb, ub, ob = (lambda g: g), (lambda g: G + g), (lambda g: g)

    def body(g_ref, u_ref, gp_ref, up_ref, gn_ref, un_ref, d_ref, dn_ref, wg_ref, wu_ref,
             dg_ref, du_ref, dwg_ref, dwu_ref):
        i = pl.program_id(1)

        def ext_of(prev, cur, nxt):
            return jnp.concatenate([jnp.where(i > 0, prev[...].astype(F32), 0.0), cur[...].astype(F32),
                                    jnp.where(i < n - 1, nxt[...].astype(F32), 0.0)], axis=0)

        eg, eu = ext_of(gp_ref, g_ref, gn_ref), ext_of(up_ref, u_ref, un_ref)
        da_e = jnp.concatenate([jnp.zeros((HALO, ct), F32), d_ref[...].astype(F32),
                                jnp.where(i < n - 1, dn_ref[...].astype(F32), 0.0)], axis=0)
        ug, uu = _conv3(wg_ref, eg), _conv3(wu_ref, eu)
        sg = _sigmoid(ug)
        dug = da_e * uu * (sg * (1.0 + ug * (1.0 - sg)))
        duu = da_e * (ug * sg)
        cur = slice(HALO, HALO + tm)

        def back(w_ref, dx, e, dx_ref, dw_ref):
            d_in = _wrow(w_ref, 2) * dx + _wrow(w_ref, 1) * _shift_up(dx, 1) + _wrow(w_ref, 0) * _shift_up(dx, 2)
            dx_ref[...] = d_in[cur].astype(dx_ref.dtype)
            s = lambda x: jnp.sum(dx[cur] * x[cur], axis=0, keepdims=True)
            _acc_out(dw_ref, i, _rows3(s(_shift_down(e, 2)), s(_shift_down(e, 1)), s(e), ct))

        back(wg_ref, dug, eg, dg_ref, dwg_ref)
        back(wu_ref, duu, eu, du_ref, dwu_ref)

    wspec = lambda cb: pl.BlockSpec((8, ct), lambda g, i: (0, cb(g)))
    out = pl.BlockSpec((tm, ct), lambda g, i: (i, g))
    return pl.pallas_call(
        body, name="mlp_act_bwd", grid=(G, n),
        in_specs=[_cur_spec(tm, ct, gb), _cur_spec(tm, ct, ub), _prev_spec(tm, ct, gb), _prev_spec(tm, ct, ub),
                  _next_spec(tm, ct, gb, Lp), _next_spec(tm, ct, ub, Lp), _cur_spec(tm, ct, ob),
                  _next_spec(tm, ct, ob, Lp), wspec(gb), wspec(ub)],
        out_specs=[out, out, wspec(ob), wspec(ob)],
        out_shape=[jax.ShapeDtypeStruct((Lp, D_FF), BF16)] * 2 + [jax.ShapeDtypeStruct((8, D_FF), F32)] * 2,
        compiler_params=_params(("parallel", "arbitrary")),
    )(up, up, up, up, up, up, da, da, w8, w8)


def _gla_chunk(q, k, v, g, s0):
    C = BLOCK
    r_i = lax.broadcasted_iota(jnp.int32, (C, C), 0)
    c_i = lax.broadcasted_iota(jnp.int32, (C, C), 1)
    row = lax.broadcasted_iota(jnp.int32, (C, GLA_DK), 0)
    b = jnp.dot((c_i <= r_i).astype(F32), g, precision=HIGHEST, preferred_element_type=F32)
    row_of = lambda n: jnp.sum(jnp.where(row == n, b, 0.0), axis=0, keepdims=True)
    refs = [row_of(n * GLA_SUB) for n in range(C // GLA_SUB)]
    sub = jnp.bitwise_and(row, -GLA_SUB)
    ref_all = sum(jnp.where(sub == n * GLA_SUB, refs[n], 0.0) for n in range(C // GLA_SUB))
    qs = q * (GLA_DK ** -0.5)
    qt = (qs * jnp.exp(b - ref_all)).astype(BF16)
    sub_start = jnp.bitwise_and(r_i, -GLA_SUB)
    att = jnp.zeros((C, C), F32)
    for n in range(C // GLA_SUB):
        kt = (k * jnp.exp(jnp.minimum(refs[n] - b, 60.0))).astype(BF16)
        a_n = lax.dot_general(qt, kt, (((1,), (1,)), ((), ())), preferred_element_type=F32)
        att = att + jnp.where((sub_start == n * GLA_SUB) & (c_i <= r_i), a_n, 0.0)
    o = (jnp.dot(att.astype(BF16), v.astype(BF16), preferred_element_type=F32)
         + jnp.dot((qs * jnp.exp(b)).astype(BF16), s0.astype(BF16), preferred_element_type=F32))
    kd = (k * jnp.exp(row_of(C - 1) - b)).astype(BF16)
    last_rows = (lax.broadcasted_iota(jnp.int32, (C, GLA_DV), 0) == C - 1).astype(F32)
    decay = lax.dot_general(b, last_rows, (((0,), (0,)), ((), ())), precision=HIGHEST,
                            preferred_element_type=F32)
    s1 = jnp.exp(decay) * s0 + lax.dot_general(kd, v.astype(BF16), (((0,), (0,)), ((), ())),
                                               preferred_element_type=F32)
    return o, s1


def _gla_fwd(main, logg, Lp):
    n = Lp // BLOCK
    qb, kb, vb = QC // GLA_DK, KC // GLA_DK, VC // GLA_DV

    def body(q_ref, k_ref, v_ref, g_ref, o_ref, st_ref, s_s):
        c = pl.program_id(1)

        @pl.when(c == 0)
        def _():
            s_s[...] = jnp.zeros_like(s_s)

        s0 = s_s[...]
        st_ref[...] = s0
        o, s1 = _gla_chunk(q_ref[...].astype(F32), k_ref[...].astype(F32), v_ref[...].astype(F32), g_ref[...], s0)
        o_ref[...] = o
        s_s[...] = s1

    return pl.pallas_call(
        body, name="gla_fwd", grid=(GLA_HEADS, n),
        in_specs=[pl.BlockSpec((BLOCK, GLA_DK), lambda h, c: (c, qb + h)),
                  pl.BlockSpec((BLOCK, GLA_DK), lambda h, c: (c, kb + h)),
                  pl.BlockSpec((BLOCK, GLA_DV), lambda h, c: (c, vb + h)),
                  pl.BlockSpec((BLOCK, GLA_DK), lambda h, c: (c, h))],
        out_specs=[pl.BlockSpec((BLOCK, GLA_DV), lambda h, c: (c, h)),
                   pl.BlockSpec((None, None, GLA_DK, GLA_DV), lambda h, c: (h, c, 0, 0))],
        out_shape=[jax.ShapeDtypeStruct((Lp, GLA_HEADS * GLA_DV), F32),
                   jax.ShapeDtypeStruct((GLA_HEADS, n, GLA_DK, GLA_DV), F32)],
        scratch_shapes=[pltpu.VMEM((GLA_DK, GLA_DV), F32)],
        compiler_params=_params(("parallel", "arbitrary")),
    )(main, main, main, logg)


def _gla_bwd(main, logg, states, do, Lp):
    n = Lp // BLOCK
    qb, kb, vb = QC // GLA_DK, KC // GLA_DK, VC // GLA_DV

    def body(q_ref, k_ref, v_ref, g_ref, st_ref, do_ref, dq_ref, dk_ref, dv_ref, dg_ref, ds_s):
        c = pl.program_id(1)

        @pl.when(c == 0)
        def _():
            ds_s[...] = jnp.zeros_like(ds_s)

        _, vjp = jax.vjp(_gla_chunk, q_ref[...].astype(F32), k_ref[...].astype(F32), v_ref[...].astype(F32),
                         g_ref[...], st_ref[...])
        dq, dk, dv, dg, ds0 = vjp((do_ref[...], ds_s[...]))
        dq_ref[...] = dq.astype(dq_ref.dtype)
        dk_ref[...] = dk.astype(dk_ref.dtype)
        dv_ref[...] = dv.astype(dv_ref.dtype)
        dg_ref[...] = dg
        ds_s[...] = ds0

    rk = lambda base: pl.BlockSpec((BLOCK, GLA_DK), lambda h, c: (n - 1 - c, base + h))
    rv = lambda base: pl.BlockSpec((BLOCK, GLA_DV), lambda h, c: (n - 1 - c, base + h))
    return pl.pallas_call(
        body, name="gla_bwd", grid=(GLA_HEADS, n),
        in_specs=[rk(qb), rk(kb), rv(vb), rk(0),
                  pl.BlockSpec((None, None, GLA_DK, GLA_DV), lambda h, c: (h, n - 1 - c, 0, 0)), rv(0)],
        out_specs=[rk(0), rk(0), rv(0), rk(0)],
        out_shape=[jax.ShapeDtypeStruct((Lp, GLA_HEADS * GLA_DK), BF16), jax.ShapeDtypeStruct((Lp, GLA_HEADS * GLA_DK), BF16),
                   jax.ShapeDtypeStruct((Lp, GLA_HEADS * GLA_DV), BF16), jax.ShapeDtypeStruct((Lp, GLA_HEADS * GLA_DK), F32)],
        scratch_shapes=[pltpu.VMEM((GLA_DK, GLA_DV), F32)],
        compiler_params=_params(("parallel", "arbitrary")),
    )(main, main, main, logg, states, do)


def _loss_head(h, g, target, Lp):
    t = BLOCK
    D = D_MODEL

    def body(h_ref, g_ref, t_ref, loss_ref, dh_ref, dg_ref):
        i = pl.program_id(0)
        x = h_ref[...]
        tok = (i * t + lax.broadcasted_iota(jnp.int32, (t, 1), 0)) >= BLOCK
        r = lax.rsqrt(jnp.mean(x * x, axis=-1, keepdims=True) + EPS)
        nrm = x * r
        e = jnp.where(tok, nrm * g_ref[...] - t_ref[...], 0.0)
        part = 0.5 * jnp.sum(jnp.sum(e * e, axis=1, keepdims=True), axis=0, keepdims=True) / D
        dy = e / D
        dn = dy * g_ref[...]
        dh_ref[...] = r * (dn - nrm * jnp.mean(dn * nrm, axis=-1, keepdims=True))
        _acc_out(dg_ref, i, jnp.sum(dy * nrm, axis=0, keepdims=True))
        _acc_out(loss_ref, i, jnp.broadcast_to(part, (1, LANES)))

    return pl.pallas_call(
        body, name="loss_head", grid=(Lp // t,),
        in_specs=[pl.BlockSpec((t, D), lambda i: (i, 0)), pl.BlockSpec((1, D), lambda i: (0, 0)),
                  pl.BlockSpec((t, D), lambda i: (jnp.maximum(i - 1, 0), 0))],
        out_specs=[pl.BlockSpec((1, LANES), lambda i: (0, 0)), pl.BlockSpec((t, D), lambda i: (i, 0)),
                   pl.BlockSpec((1, D), lambda i: (0, 0))],
        out_shape=[jax.ShapeDtypeStruct((1, LANES), F32), jax.ShapeDtypeStruct((Lp, D), F32),
                   jax.ShapeDtypeStruct((1, D), F32)],
        compiler_params=_params(("arbitrary",)),
    )(h, g, target)


def _adamw(w, g, m, v, name):
    if w.ndim == 1:
        outs = _adamw(*(a.reshape(1, -1) for a in (w, g, m, v)), name)
        return tuple(o.reshape(w.shape) for o in outs)
    if w.ndim == 3 and w.shape[-1] % LANES and w.shape[-2] % LANES == 0:
        outs = _adamw(*(a.transpose(2, 0, 1) for a in (w, g, m, v)), name)
        return tuple(o.transpose(1, 2, 0) for o in outs)
    rows, cols = w.shape[-2:]
    budget_rows = max(8, ADAM_BLOCK_BYTES // (4 * cols))
    tr = rows if rows <= budget_rows else _pick(rows, tuple(t for t in (512, 256, 128, 64, 32, 16, 8) if t <= budget_rows))
    lead = 1 if w.ndim < 3 else _pick(w.shape[0], (6, 4, 3, 2, 1)) if tr == rows else 1

    def body(w_ref, g_ref, m_ref, v_ref, go_ref, d_ref, nm_ref, nv_ref):
        gg = g_ref[...]
        mm = ADAM_B1 * m_ref[...] + (1.0 - ADAM_B1) * gg
        vv = ADAM_B2 * v_ref[...] + (1.0 - ADAM_B2) * jnp.square(gg)
        m_hat = mm / (1.0 - ADAM_B1 ** ADAM_STEP)
        v_hat = vv / (1.0 - ADAM_B2 ** ADAM_STEP)
        d_ref[...] = -ADAM_LR * (m_hat / (jnp.sqrt(v_hat) + ADAM_EPS) + ADAM_WD * w_ref[...])
        go_ref[...] = gg
        nm_ref[...] = mm
        nv_ref[...] = vv

    if w.ndim == 3:
        spec, grid = pl.BlockSpec((lead, tr, cols), lambda l, i: (l, i, 0)), (w.shape[0] // lead, rows // tr)
    else:
        spec, grid = pl.BlockSpec((tr, cols), lambda i: (i, 0)), (rows // tr,)
    return pl.pallas_call(
        body, name=name, grid=grid, in_specs=[spec] * 4, out_specs=[spec] * 4,
        out_shape=[jax.ShapeDtypeStruct(w.shape, F32)] * 4,
        compiler_params=_params(("parallel",) * len(grid)),
    )(w, g, m, v)


def _place():
    x, y, c = lax.axis_index("x"), lax.axis_index("y"), lax.axis_index("c")
    chips = [(1 - x, y), (x, 1 - y), (1 - x, 1 - y)]
    return x, y, c, chips


def _rcopy(src, dst, send_sems, recv_sems, k, to):
    return pltpu.make_async_remote_copy(src_ref=src, dst_ref=dst, send_sem=send_sems.at[k], recv_sem=recv_sems.at[k],
                                        device_id=to, device_id_type=MESH)


def _any_spec():
    return pl.BlockSpec(memory_space=pl.ANY)


def _shard_ref(ref, mode, t, r, c):
    if mode == "rows":
        return ref.at[pl.ds(pl.multiple_of(t * r, 16), r), :]
    if mode == "cols":
        return ref.at[:, pl.ds(pl.multiple_of(t * c, LANES), c)]
    return ref.at[t]


def _gathered_shape(mode, r, c):
    return {"rows": (4 * r, c), "cols": (r, 4 * c), "stack": (4, r, c)}[mode]


def _place_own(shard, mode, me1, name):
    _, r, c = shard.shape
    tr = _ew_rows(r, c)
    blk = {"rows": (None, tr, c), "cols": (None, tr, c), "stack": (None, None, tr, c)}[mode]
    idx = {"rows": lambda l, i, me: (l, me[0] * (r // tr) + i, 0),
           "cols": lambda l, i, me: (l, i, me[0]),
           "stack": lambda l, i, me: (l, me[0], i, 0)}[mode]

    def body(me_ref, in_ref, out_ref):
        out_ref[...] = in_ref[...]

    return pl.pallas_call(
        body, name=name,
        grid_spec=pltpu.PrefetchScalarGridSpec(
            num_scalar_prefetch=1, grid=(DEPTH, r // tr),
            in_specs=[pl.BlockSpec((None, tr, c), lambda l, i, me: (l, i, 0))],
            out_specs=pl.BlockSpec(blk, idx)),
        out_shape=jax.ShapeDtypeStruct((DEPTH,) + _gathered_shape(mode, r, c), shard.dtype),
        compiler_params=_params(("parallel", "parallel")),
    )(me1, shard)


def _gather_weights(shards, modes):
    n = len(shards)
    dims = [s.shape[1:] for s in shards]
    me1 = jnp.reshape(2 * lax.axis_index("x") + lax.axis_index("y"), (1,)).astype(jnp.int32)
    placed = [_place_own(shards[k], modes[k], me1, f"gather_place_{k}") for k in range(n)]

    def body(*refs):
        ins, outs = refs[:n], refs[2 * n:3 * n]
        send_sems, recv_sems = refs[3 * n:]
        x, y, c, chips = _place()
        me = 2 * x + y
        place = lambda k, l, t: _shard_ref(outs[k].at[l], modes[k], t, *dims[k])
        first = [_rcopy(ins[k].at[c], place(k, c, me), send_sems, recv_sems, 6 * k + j, (*chip, c))
                 for j, chip in enumerate(chips) for k in range(n)]
        for cp in first:
            cp.start()
        passed = []
        for j, (px, py) in enumerate(chips):
            for k in range(n):
                blk = place(k, c, 2 * px + py)
                _rcopy(blk, blk, send_sems, recv_sems, 6 * k + j, (x, y, c)).wait_recv()
                fwd = _rcopy(blk, blk, send_sems, recv_sems, 6 * k + 3 + j, (x, y, 1 - c))
                fwd.start()
                passed.append(fwd)
        for j, (px, py) in enumerate(chips):
            for k in range(n):
                blk = place(k, 1 - c, 2 * px + py)
                _rcopy(blk, blk, send_sems, recv_sems, 6 * k + 3 + j, (x, y, c)).wait_recv()
        for cp in first + passed:
            cp.wait_send()

    return pl.pallas_call(
        body, name="gather_weights", in_specs=[_any_spec()] * (2 * n), out_specs=[_any_spec()] * n,
        out_shape=[jax.ShapeDtypeStruct(a.shape, a.dtype) for a in placed],
        input_output_aliases={n + k: k for k in range(n)},
        scratch_shapes=[pltpu.SemaphoreType.DMA((6 * n,)), pltpu.SemaphoreType.DMA((6 * n,))],
    )(*shards, *placed)


def _swap_layers(gs):
    n = len(gs)

    def body(*refs):
        ins, outs = refs[:n], refs[n:2 * n]
        send_sems, recv_sems = refs[2 * n:]
        x, y, c, _ = _place()
        cps = [_rcopy(ins[k].at[1 - c], outs[k], send_sems, recv_sems, k, (x, y, 1 - c)) for k in range(n)]
        for cp in cps:
            cp.start()
        for cp in cps:
            cp.wait()

    return pl.pallas_call(
        body, name="rs_swap_layers", in_specs=[_any_spec()] * n, out_specs=[_any_spec()] * n,
        out_shape=[jax.ShapeDtypeStruct(g.shape[1:], g.dtype) for g in gs],
        scratch_shapes=[pltpu.SemaphoreType.DMA((n,)), pltpu.SemaphoreType.DMA((n,))],
    )(*gs)


def _scatter_chips(hs, modes, dims):
    n = len(hs)

    def body(*refs):
        ins, outs = refs[:n], refs[n:2 * n]
        send_sems, recv_sems, local_sems = refs[2 * n:]
        x, y, c, chips = _place()
        me = 2 * x + y
        part = lambda k, t: _shard_ref(ins[k], modes[k], t, *dims[k])
        own = [pltpu.make_async_copy(part(k, me), outs[k].at[me], local_sems.at[k]) for k in range(n)]
        for cp in own:
            cp.start()
        cps = [_rcopy(part(k, 2 * px + py), outs[k].at[me], send_sems, recv_sems, 3 * k + j, (px, py, c))
               for j, (px, py) in enumerate(chips) for k in range(n)]
        for cp in cps:
            cp.start()
        for j, (px, py) in enumerate(chips):
            for k in range(n):
                blk = outs[k].at[2 * px + py]
                _rcopy(blk, blk, send_sems, recv_sems, 3 * k + j, (x, y, c)).wait_recv()
        for cp in cps:
            cp.wait_send()
        for cp in own:
            cp.wait()

    return pl.pallas_call(
        body, name="rs_scatter_chips", in_specs=[_any_spec()] * n, out_specs=[_any_spec()] * n,
        out_shape=[jax.ShapeDtypeStruct((4,) + tuple(dims[k]), hs[k].dtype) for k in range(n)],
        scratch_shapes=[pltpu.SemaphoreType.DMA((3 * n,)), pltpu.SemaphoreType.DMA((3 * n,)),
                        pltpu.SemaphoreType.DMA((n,))],
    )(*hs)


def _join_layers(fs):
    n = len(fs)

    def body(*refs):
        outs = refs[n:2 * n]
        send_sems, recv_sems = refs[2 * n:]
        x, y, c, _ = _place()
        cps = [_rcopy(outs[k].at[c], outs[k].at[c], send_sems, recv_sems, k, (x, y, 1 - c)) for k in range(n)]
        for cp in cps:
            cp.start()
        for k in range(n):
            blk = outs[k].at[1 - c]
            _rcopy(blk, blk, send_sems, recv_sems, k, (x, y, c)).wait_recv()
        for cp in cps:
            cp.wait_send()

    return pl.pallas_call(
        body, name="rs_join_layers", in_specs=[_any_spec()] * n, out_specs=[_any_spec()] * n,
        out_shape=[jax.ShapeDtypeStruct(f.shape, f.dtype) for f in fs],
        input_output_aliases={k: k for k in range(n)},
        scratch_shapes=[pltpu.SemaphoreType.DMA((n,)), pltpu.SemaphoreType.DMA((n,))],
    )(*fs)


def _ew_rows(M, N):
    fit = [t for t in (512, 256, 128, 64, 32, 16) if M % t == 0 and t * N * 4 <= EW_BLOCK_BYTES]
    return fit[0] if fit else M


def _add_own(g, other, c1, out_dtype, name):
    _, M, N = g.shape
    tr = _ew_rows(M, N)

    def body(c_ref, g_ref, o_ref, out_ref):
        out_ref[...] = (g_ref[...] + o_ref[...]).astype(out_ref.dtype)

    return pl.pallas_call(
        body, name=name,
        grid_spec=pltpu.PrefetchScalarGridSpec(
            num_scalar_prefetch=1, grid=(M // tr,),
            in_specs=[pl.BlockSpec((None, tr, N), lambda i, cr: (cr[0], i, 0)),
                      pl.BlockSpec((tr, N), lambda i, cr: (i, 0))],
            out_specs=pl.BlockSpec((tr, N), lambda i, cr: (i, 0))),
        out_shape=jax.ShapeDtypeStruct((M, N), out_dtype),
        compiler_params=_params(("parallel",)),
    )(c1, g, other)


def _sum_chips(q, c1, name):
    _, M, N = q.shape
    tr = _ew_rows(M, N)

    def body(c_ref, q_ref, out_ref):
        out_ref[...] = ((q_ref[0].astype(F32) + q_ref[1].astype(F32)) + q_ref[2].astype(F32)) + q_ref[3].astype(F32)

    return pl.pallas_call(
        body, name=name,
        grid_spec=pltpu.PrefetchScalarGridSpec(
            num_scalar_prefetch=1, grid=(M // tr,),
            in_specs=[pl.BlockSpec((4, tr, N), lambda i, cr: (0, i, 0))],
            out_specs=pl.BlockSpec((None, tr, N), lambda i, cr: (cr[0], i, 0))),
        out_shape=jax.ShapeDtypeStruct((DEPTH, M, N), F32),
        compiler_params=_params(("parallel",)),
    )(c1, q)


def _reduce_scatter(gs, modes, dims, wire):
    c1 = jnp.reshape(lax.axis_index("c"), (1,)).astype(jnp.int32)
    flat = lambda a, lead: a.reshape(a.shape[:lead] + (-1, a.shape[-1]))
    others = _swap_layers(gs)
    hs = [_add_own(flat(g, 1), flat(o, 0), c1, wire[k], f"rs_add_own_{k}").reshape(o.shape)
          for k, (g, o) in enumerate(zip(gs, others))]
    qs = _scatter_chips(hs, modes, dims)
    fs = [_sum_chips(q, c1, f"rs_sum_chips_{k}") for k, q in enumerate(qs)]
    return _join_layers(fs)


def _allreduce_small(v):
    R, C = v.shape

    def body(v_ref, sum_ref, all_ref, send_sems, recv_sems):
        x, y, c, _ = _place()
        me = 4 * x + 2 * y + c
        rows = lambda d: all_ref.at[pl.ds(pl.multiple_of(d * R, 8), R), :]

        def peer(k):
            flip = lambda bit, v: (1 - v) if ((k + 1) >> bit) & 1 else v
            return flip(2, x), flip(1, y), flip(0, c)

        outs = [_rcopy(v_ref, rows(me), send_sems, recv_sems, k, peer(k)) for k in range(7)]
        for cp in outs:
            cp.start()
        all_ref[pl.ds(pl.multiple_of(me * R, 8), R), :] = v_ref[...]
        for k in range(7):
            px, py, pc = peer(k)
            blk = rows(4 * px + 2 * py + pc)
            _rcopy(blk, blk, send_sems, recv_sems, k, (x, y, c)).wait_recv()
        for cp in outs:
            cp.wait_send()
        tot = all_ref[0:R, :]
        for d in range(1, 8):
            tot = tot + all_ref[d * R:(d + 1) * R, :]
        sum_ref[...] = tot

    vm = pl.BlockSpec(memory_space=pltpu.VMEM)
    return pl.pallas_call(
        body, name="allreduce_small", in_specs=[vm], out_specs=[vm, vm],
        out_shape=[jax.ShapeDtypeStruct((R, C), F32), jax.ShapeDtypeStruct((8 * R, C), F32)],
        scratch_shapes=[pltpu.SemaphoreType.DMA((7,)), pltpu.SemaphoreType.DMA((7,))],
    )(v)[0]


def _size(shape):
    n = 1
    for d in shape:
        n *= d
    return n


def _pack(pieces, dtype):
    flat = jnp.concatenate([p.astype(dtype).reshape(-1) for p in pieces])
    rows = -(-flat.shape[0] // (PACK_COLS * 16)) * 16
    return jnp.pad(flat, (0, rows * PACK_COLS - flat.shape[0])).reshape(rows, PACK_COLS)


def _unpack(buf, shapes):
    flat = buf.reshape(-1)
    out, pos = [], 0
    for s in shapes:
        n = _size(s)
        out.append(flat[pos:pos + n].reshape(s))
        pos += n
    return out


def _small_piece(name, arr, l):
    if name == "meta_tokens":
        return arr[l * (N_META // DEPTH):(l + 1) * (N_META // DEPTH)]
    return arr[l]


def _prep_w_in(w_in4):
    w_in = jnp.concatenate([w_in4[t] for t in range(4)], axis=1)
    col = lambda a, n: w_in[:, _R[a]:_R[a] + n]
    main = jnp.concatenate([col("qa", 3072), col("scb", 3072), col("qc", 3072), col("ga", 6144)], axis=1)
    zpad = lambda n: jnp.zeros((D_MODEL, n), w_in.dtype)
    side = jnp.concatenate([col("fa", 8), zpad(LANES - 8), col("glr", GLA_RANK), zpad(LANES - GLA_RANK)], axis=1)
    return main.astype(BF16), side.astype(BF16)


def _w_in_cols(dmain, dside, lo, hi):
    segs = ((0, _R["fa"], dmain, 0), (_R["fa"], _R["scb"], dside, 0), (_R["scb"], _R["glr"], dmain, SCB),
            (_R["glr"], _R["ga"], dside, LANES), (_R["ga"], N_IN, dmain, GA))
    parts = [src[..., off + max(a, lo) - a:off + min(b, hi) - a] for a, b, src, off in segs if max(a, lo) < min(b, hi)]
    return jnp.concatenate(parts, axis=-1)


def _pad_rows(a, rows):
    return jnp.pad(a.astype(F32), ((0, rows - a.shape[0]), (0, 0)))


def _row2(v):
    return v.reshape(1, -1).astype(F32)


def _layer_fwd(h, p, rep, l, Lp, tm, ta):
    tag = lambda s: f"{s}_l{l}"
    g1, g2 = _row2(rep["norm1_g"][l]), _row2(rep["norm2_g"][l])
    bf = jnp.pad(_row2(rep["fox_b_f"][l]), ((0, 0), (0, LANES - FOX_HEADS)))
    gate_b, b_g, gnorm = _row2(rep["gate_b"][l]), _row2(rep["gla_b_g"][l]), _row2(rep["gla_norm_g"][l])
    (xn,) = _rw_fwd(tag("rms1_fwd"), _f_rms, [Row(h, D_MODEL)], [Const(g1)], [(D_MODEL, BF16)], Lp, BLOCK)
    main = _mm(xn, p["main"][l], "nn", BF16, tag("proj_main"))
    side = _mm(xn, p["side"][l], "nn", F32, tag("proj_side"))
    c = _fox_gate_fwd(side, bf, Lp)
    c_t = c[:, :FOX_HEADS].T
    c_col, c_row = c_t[:, :, None], c_t[:, None, :]
    oa, ox, lse = _fox_fwd(main, c_col, c_row, Lp, ta)
    ya = _mm(oa, p["w_a_o"], "nn", BF16, tag("ya"), b_lead=l)
    ub = _sconv_fwd(main, p["conv_w"][l], Lp, tm)
    yb = _mm(ub, p["w_b_o"], "nn", BF16, tag("yb"), b_lead=l)
    glr = Row(side, LANES, lambda g: 1)
    (logg,) = _rw_fwd(tag("logg_fwd"), _f_logg, [glr], [Const(p["w_g2"][l]), Const(b_g)], [(512, F32)], Lp, tm)
    oc, states = _gla_fwd(main, logg, Lp)
    rc = Row(main, GLA_DV, lambda g: RC // GLA_DV + g)
    gn = Const(gnorm, (1, GLA_DV), lambda g: (0, g))
    (uc,) = _rw_fwd(tag("gla_post_fwd"), _f_gla_post, [Row(oc, GLA_DV), rc], [gn], [(GLA_DV, BF16)], Lp, tm,
                    G=GLA_HEADS)
    yc = _mm(uc, p["w_c_o"], "nn", BF16, tag("yc"), b_lead=l)
    cw = 512
    G = D_MODEL // cw
    mrows = [Row(ya, cw), Row(yb, cw), Row(yc, cw), Row(main, cw, lambda g: GA // cw + g),
             Row(main, cw, lambda g: GB // cw + g), Row(main, cw, lambda g: GC // cw + g)]
    mconsts = [Const(gate_b, (1, cw), lambda g, k=k: (0, k * G + g)) for k in range(3)]
    (mix,) = _rw_fwd(tag("merge_fwd"), _f_merge, mrows, mconsts, [(cw, BF16)], Lp, tm, G=G)
    h1 = _mm(mix, p["w_o"], "nn", F32, tag("h1"), add=h, b_lead=l)
    (xn2,) = _rw_fwd(tag("rms2_fwd"), _f_rms, [Row(h1, D_MODEL)], [Const(g2)], [(D_MODEL, BF16)], Lp, BLOCK)
    up = _mm(xn2, p["w_up"], "nn", BF16, tag("up"), b_lead=l)
    act = _mlp_act_fwd(up, p["mlp_conv_w"][l], Lp, tm)
    h2 = _mm(act, p["w_down"], "nn", F32, tag("h2"), add=h1, b_lead=l)
    res = dict(h=h, xn=xn, main=main, side=side, c_col=c_col, c_row=c_row, oa=oa, ox=ox, lse=lse, ya=ya, ub=ub, yb=yb,
               logg=logg, oc=oc, states=states, uc=uc, yc=yc, mix=mix, h1=h1, xn2=xn2, up=up, act=act,
               g1=g1, g2=g2, bf=bf, gate_b=gate_b, b_g=b_g, gnorm=gnorm)
    return h2, res


def _layer_bwd(dh2, p, r, l, Lp, tm, ta, big):
    tag = lambda s: f"{s}_l{l}"
    g = {}

    def wgrad(name, a, b):
        big[name] = _mm(a, b, "tn", F32, tag("d_" + name), slot=(big.get(name), l))

    wgrad("w_down", r["act"], dh2)
    dact = _mm(dh2, p["w_down"], "nt", BF16, tag("d_act"), b_lead=l)
    dgate, dval, dwg, dwu = _mlp_act_bwd(r["up"], p["mlp_conv_w"][l], dact, Lp, tm)
    g["mlp_conv_w"] = jnp.concatenate([dwg[:3], dwu[:3]], axis=1)
    dup = jnp.concatenate([dgate, dval], axis=1)
    wgrad("w_up", r["xn2"], dup)
    dxn2 = _mm(dup, p["w_up"], "nt", F32, tag("d_xn2"), b_lead=l)
    (dh1,), (dg2,) = _rw_bwd(tag("rms2_bwd"), _f_rms, [Row(r["h1"], D_MODEL)], [Const(r["g2"])],
                             [Row(dxn2, D_MODEL)], [F32], [dh2], Lp, BLOCK)
    g["norm2_g"] = dg2[0]
    wgrad("w_o", r["mix"], dh1)
    dmix = _mm(dh1, p["w_o"], "nt", BF16, tag("d_mix"), b_lead=l)
    cw = 512
    G = D_MODEL // cw
    main = r["main"]
    mrows = [Row(r["ya"], cw), Row(r["yb"], cw), Row(r["yc"], cw), Row(main, cw, lambda g_: GA // cw + g_),
             Row(main, cw, lambda g_: GB // cw + g_), Row(main, cw, lambda g_: GC // cw + g_)]
    mconsts = [Const(r["gate_b"], (1, cw), lambda g_, k=k: (0, k * G + g_)) for k in range(3)]
    (dya, dyb, dyc, dga, dgb, dgc), dbs = _rw_bwd(tag("merge_bwd"), _f_merge, mrows, mconsts, [Row(dmix, cw)],
                                                  [BF16] * 6, [None] * 6, Lp, tm, G=G)
    g["gate_b"] = jnp.concatenate([dbs[k][0, k * D_MODEL:(k + 1) * D_MODEL] for k in range(3)])
    wgrad("w_a_o", r["oa"], dya)
    doa = _mm(dya, p["w_a_o"], "nt", BF16, tag("d_oa"), b_lead=l)
    wgrad("w_b_o", r["ub"], dyb)
    dub = _mm(dyb, p["w_b_o"], "nt", BF16, tag("d_ub"), b_lead=l)
    wgrad("w_c_o", r["uc"], dyc)
    duc = _mm(dyc, p["w_c_o"], "nt", BF16, tag("d_uc"), b_lead=l)
    delta = _fox_delta(r["ox"], doa, Lp, ta)
    dq, dk, dv, dck = _fox_bwd(main, r["c_col"], r["c_row"], r["lse"], delta, doa, Lp, ta)
    dc = jnp.pad(dck[:, 0, :].T, ((0, 0), (0, LANES - FOX_HEADS)))
    dfa, dbf = _fox_gate_bwd(r["side"], r["bf"], dc, Lp)
    g["fox_b_f"] = dbf[0, :FOX_HEADS]
    dscb, dscc, dsch, dcw = _sconv_bwd(main, p["conv_w"][l], dub, Lp, tm)
    g["conv_w"] = dcw[:3]
    rc = Row(main, GLA_DV, lambda g_: RC // GLA_DV + g_)
    gn = Const(r["gnorm"], (1, GLA_DV), lambda g_: (0, g_))
    (doc, drc), (dgn,) = _rw_bwd(tag("gla_post_bwd"), _f_gla_post, [Row(r["oc"], GLA_DV), rc], [gn],
                                 [Row(duc, GLA_DV)], [F32, BF16], [None, None], Lp, tm, G=GLA_HEADS)
    g["gla_norm_g"] = dgn[0]
    dqc, dkc, dvc, dlogg = _gla_bwd(main, r["logg"], r["states"], doc, Lp)
    glr = Row(r["side"], LANES, lambda g_: 1)
    (dglr,), (dwg2, dbg) = _rw_bwd(tag("logg_bwd"), _f_logg, [glr], [Const(p["w_g2"][l]), Const(r["b_g"])],
                                   [Row(dlogg, 512)], [F32], [None], Lp, tm)
    g["gla_w_g2"] = dwg2[:GLA_RANK]
    g["gla_b_g"] = dbg[0]
    dmain = jnp.concatenate([dq, dk, dv, dscb, dscc, dsch, dqc, dkc, dvc, drc, dga, dgb, dgc], axis=1)
    dside = jnp.concatenate([dfa, dglr], axis=1)
    wgrad("main", r["xn"], dmain)
    wgrad("side", r["xn"], dside)
    dxn = _mm(dmain, p["main"][l], "nt", F32, tag("d_xn_main"))
    dxn = _mm(dside, p["side"][l], "nt", F32, tag("d_xn_side"), add=dxn)
    (dh,), (dg1,) = _rw_bwd(tag("rms1_bwd"), _f_rms, [Row(r["h"], D_MODEL)], [Const(r["g1"])], [Row(dxn, D_MODEL)],
                            [F32], [dh1], Lp, BLOCK)
    g["norm1_g"] = dg1[0]
    return dh, g


def _local_step(x, target, meta, p, rep):
    seq = x.shape[0]
    Lp = PAD + N_META + seq
    tm = _pick(Lp, (640, 384, 128))
    ta = tm
    h = jnp.concatenate([jnp.zeros((PAD, D_MODEL), F32), meta.astype(F32), x], axis=0)
    saved = []
    for l in range(DEPTH):
        h, res = _layer_fwd(h, p, rep, l, Lp, tm, ta)
        saved.append(res)
    loss, dh, dgf = _loss_head(h, _row2(rep["final_norm_g"]), target, Lp)
    big, small = {}, [None] * DEPTH
    for l in reversed(range(DEPTH)):
        dh, small[l] = _layer_bwd(dh, p, saved[l], l, Lp, tm, ta, big)
    return loss[0, 0], dh[BLOCK:], dh[PAD:BLOCK], big, small, dgf[0]


def kernel(x, meta_tokens, norm1_g, w_in, fox_b_f, gate_b, conv_w, gla_w_g2, gla_b_g, gla_norm_g, w_a_o, w_b_o, w_c_o, w_o, norm2_g, w_up, mlp_conv_w, w_down, final_norm_g, loss_target, m_meta_tokens, m_norm1_g, m_w_in, m_fox_b_f, m_gate_b, m_conv_w, m_gla_w_g2, m_gla_b_g, m_gla_norm_g, m_w_a_o, m_w_b_o, m_w_c_o, m_w_o, m_norm2_g, m_w_up, m_mlp_conv_w, m_w_down, m_final_norm_g, v_meta_tokens, v_norm1_g, v_w_in, v_fox_b_f, v_gate_b, v_conv_w, v_gla_w_g2, v_gla_b_g, v_gla_norm_g, v_w_a_o, v_w_b_o, v_w_c_o, v_w_o, v_norm2_g, v_w_up, v_mlp_conv_w, v_w_down, v_final_norm_g):
    given = dict(locals())
    weights = {n: given[n] for n in WEIGHT_ORDER}
    rep = {n: weights[n] for n, _ in REPLICATED}
    big_names = [n for n, _ in BIG]
    big_modes = [m for _, m in BIG] + ["stack"]
    small_shapes = [(s[0], s[1] // 4) for _, s in SMALL]
    exact = [k for k, (n, _) in enumerate(SMALL) if n in GATHER_F32]

    def small_wire(l):
        ws = [_small_piece(n, weights[n], l) for n, _ in SMALL]
        his = [w.astype(BF16) for w in ws]
        return his + [(ws[k] - his[k].astype(F32)).astype(BF16) for k in exact]

    shards = [weights[n].astype(BF16) for n in big_names] + [jnp.stack([_pack(small_wire(l), BF16) for l in range(DEPTH)])]
    gathered = _gather_weights(shards, big_modes)
    gw = dict(zip(big_names, gathered[:-1]))
    p = {n: gw[n] for n in big_names if n != "w_in"}
    p["main"], p["side"] = zip(*[_prep_w_in(gw["w_in"][l]) for l in range(DEPTH)])
    small_full = []
    for l in range(DEPTH):
        per_chip = [_unpack(gathered[-1][l, t], small_shapes + [small_shapes[k] for k in exact]) for t in range(4)]
        full = [jnp.concatenate([per_chip[t][k] for t in range(4)], axis=1).astype(F32) for k in range(len(per_chip[0]))]
        for e, k in enumerate(exact):
            full[k] = full[k] + full[len(SMALL) + e]
        small_full.append(dict(zip([n for n, _ in SMALL], full[:len(SMALL)])))
    p["conv_w"] = [_pad_rows(s["conv_w"], 8) for s in small_full]
    p["mlp_conv_w"] = [_pad_rows(s["mlp_conv_w"], 8) for s in small_full]
    p["w_g2"] = [_pad_rows(s["gla_w_g2"], LANES) for s in small_full]
    meta_full = jnp.concatenate([s["meta_tokens"] for s in small_full], axis=0)

    loss, grad_x, grad_meta, big, small, d_final = _local_step(x[0], loss_target[0], meta_full, p, rep)
    loss = lax.psum(loss, ("x", "y", "c"))

    big["w_in"] = jnp.stack([_w_in_cols(big["main"], big["side"], t * (N_IN // 4), (t + 1) * (N_IN // 4))
                             for t in range(4)], axis=1)
    for l in range(DEPTH):
        small[l]["meta_tokens"] = grad_meta[l * (N_META // DEPTH):(l + 1) * (N_META // DEPTH)]
    shard_of = lambda a, t: lax.slice_in_dim(a, t * (a.shape[1] // 4), (t + 1) * (a.shape[1] // 4), axis=1)
    small_g = jnp.stack([jnp.stack([_pack([shard_of(small[l][n], t) for n, _ in SMALL], F32) for t in range(4)])
                         for l in range(DEPTH)])
    dims = [shards[k].shape[1:] for k in range(len(BIG))] + [small_g.shape[2:]]
    summed = _reduce_scatter([big[n] for n in big_names] + [small_g], big_modes, dims,
                             [BF16] * len(BIG) + [F32])
    gout = dict(zip(big_names, summed[:-1]))
    pieces = [_unpack(summed[-1][l], small_shapes) for l in range(DEPTH)]
    for k, (n, _) in enumerate(SMALL):
        per_layer = [pieces[l][k] for l in range(DEPTH)]
        gout[n] = jnp.concatenate(per_layer, axis=0) if n == "meta_tokens" else jnp.stack(per_layer)

    rep_g = {n: (d_final if n == "final_norm_g" else jnp.stack([small[l][n] for l in range(DEPTH)])) for n, _ in REPLICATED}
    flat = jnp.concatenate([rep_g[n].astype(F32).reshape(-1) for n, _ in REPLICATED])
    rrows = -(-flat.shape[0] // (PACK_COLS * 8)) * 8
    summed_small = _allreduce_small(jnp.pad(flat, (0, rrows * PACK_COLS - flat.shape[0])).reshape(rrows, PACK_COLS))
    pos = 0
    for n, shape in REPLICATED:
        gout[n] = summed_small.reshape(-1)[pos:pos + _size(shape)].reshape(shape)
        pos += _size(shape)

    deltas, new_m, new_v = {}, {}, {}
    for n in WEIGHT_ORDER:
        gout[n], deltas[n], new_m[n], new_v[n] = _adamw(weights[n], gout[n], given["m_" + n], given["v_" + n],
                                                        "adamw_" + n)
    return (loss, grad_x[None], *[gout[n] for n in WEIGHT_ORDER], *[deltas[n] for n in WEIGHT_ORDER],
            *[new_m[n] for n in WEIGHT_ORDER], *[new_v[n] for n in WEIGHT_ORDER])
```

```python
import functools

import jax
import jax.numpy as jnp
from jax import lax
from jax.experimental import pallas as pl
from jax.experimental.pallas import tpu as pltpu

F32, BF16 = jnp.float32, jnp.bfloat16
HIGHEST = lax.Precision.HIGHEST
MESH = pl.DeviceIdType.MESH

N_META = 16
BLOCK = 128
LANES = 128
PAD = BLOCK - N_META
EPS = 1e-6
NEG = -1e30
HALO = 16
VMEM_LIMIT = 56 * 1024 * 1024
ADAM_BLOCK_BYTES = 1 << 20
EW_BLOCK_BYTES = 3 << 19

D_MODEL = 2048
FOX_HEADS, FOX_HD = 8, 128
FOX_WIDTH = FOX_HEADS * FOX_HD
CONV_CH = 1024
GLA_HEADS, GLA_DK, GLA_DV, GLA_RANK, GLA_TAU = 4, 128, 256, 16, 16.0
GLA_SUB = 32
D_FF = 5632
N_IN = 15384
DEPTH = 2

_R = dict(qa=0, ka=1024, va=2048, fa=3072, scb=3080, scc=4104, sch=5128, qc=6152, kc=6664,
          vc=7176, rc=8200, glr=9224, ga=9240, gb=11288, gc=13336)
QA, KA, VA, SCB, SCC, SCH, QC, KC, VC, RC, GA, GB, GC = (
    0, 1024, 2048, 3072, 4096, 5120, 6144, 6656, 7168, 8192, 9216, 11264, 13312)
N_MAIN = 15360
N_SIDE = 256

ADAM_LR, ADAM_B1, ADAM_B2, ADAM_EPS, ADAM_WD, ADAM_STEP = 0.001, 0.9, 0.999, 1e-08, 0.01, 10

BIG = (("w_in", "stack"), ("w_a_o", "cols"), ("w_b_o", "cols"), ("w_c_o", "cols"), ("w_o", "rows"), ("w_up", "cols"),
       ("w_down", "rows"))
SMALL = (("conv_w", (3, CONV_CH)), ("mlp_conv_w", (3, 2 * D_FF)), ("gla_w_g2", (GLA_RANK, 512)),
         ("meta_tokens", (N_META // DEPTH, D_MODEL)))
REPLICATED = (("norm1_g", (2, D_MODEL)), ("fox_b_f", (2, 8)), ("gate_b", (2, 3 * D_MODEL)), ("gla_b_g", (2, 512)),
              ("gla_norm_g", (2, 1024)), ("norm2_g", (2, D_MODEL)), ("final_norm_g", (D_MODEL,)))
WEIGHT_ORDER = ("meta_tokens", "norm1_g", "w_in", "fox_b_f", "gate_b", "conv_w", "gla_w_g2", "gla_b_g",
                "gla_norm_g", "w_a_o", "w_b_o", "w_c_o", "w_o", "norm2_g", "w_up", "mlp_conv_w", "w_down",
                "final_norm_g")
PACK_COLS = 1024
GATHER_F32 = ("conv_w", "mlp_conv_w", "meta_tokens")


def _pick(n, cands):
    for c in cands:
        if n % c == 0:
            return c
    return n


def _params(sem):
    return pltpu.CompilerParams(dimension_semantics=sem, vmem_limit_bytes=VMEM_LIMIT)


def _sigmoid(x):
    return jax.nn.sigmoid(x)


def _log_sigmoid(x):
    return jnp.minimum(x, 0.0) - jnp.log(1.0 + jnp.exp(-jnp.abs(x)))


def _mm(a, b, mode, out_dtype, name, add=None, b_lead=None, slot=None):
    bshape = b.shape if b_lead is None else b.shape[1:]
    if mode == "nn":
        (M, K), (K2, N) = a.shape, bshape
    elif mode == "nt":
        (M, K), (N, K2) = a.shape, bshape
    else:
        (K, M), (K2, N) = a.shape, bshape
    assert K == K2, (name, a.shape, b.shape)
    if mode == "tn":
        tm = _pick(M, (2048, 1408, 1024, 512, 256, 128))
        tn = _pick(N, (1024, 512, 256, 128))
        tk = _pick(K, (640, 512, 384, 256, 128))
    else:
        tm = _pick(M, (1664, 2048, 1408, 1024, 640, 384, 128))
        tn = _pick(N, (512, 256, 128))
        tk = K if K <= 2048 else _pick(K, (1664, 1408, 1024, 640, 512, 384, 256, 128))
    nk = K // tk
    dims = {"nn": (((1,), (0,)), ((), ())), "nt": (((1,), (1,)), ((), ())), "tn": (((0,), (0,)), ((), ()))}[mode]
    n_in = 2 + (add is not None) + (slot is not None and slot[0] is not None)

    def body(*refs):
        a_ref, b_ref = refs[:2]
        add_ref = refs[2] if add is not None else None
        o_ref, acc = refs[n_in:]
        k = pl.program_id(2)

        @pl.when(k == 0)
        def _():
            acc[...] = jnp.zeros_like(acc)

        acc[...] += lax.dot_general(a_ref[...].astype(BF16), b_ref[...].astype(BF16), dims,
                                    preferred_element_type=F32)

        @pl.when(k == nk - 1)
        def _():
            r = acc[...]
            if add is not None:
                r = r + add_ref[...].astype(F32)
            o_ref[...] = r.astype(o_ref.dtype)

    a_spec = {"nn": pl.BlockSpec((tm, tk), lambda i, j, k: (i, k)),
              "nt": pl.BlockSpec((tm, tk), lambda i, j, k: (i, k)),
              "tn": pl.BlockSpec((tk, tm), lambda i, j, k: (k, i))}[mode]
    b_blk, b_idx = {"nn": ((tk, tn), lambda i, j, k: (k, j)),
                    "nt": ((tn, tk), lambda i, j, k: (j, k)),
                    "tn": ((tk, tn), lambda i, j, k: (k, j))}[mode]
    if b_lead is None:
        b_spec = pl.BlockSpec(b_blk, b_idx)
    else:
        b_spec = pl.BlockSpec((None,) + b_blk, lambda i, j, k: (b_lead,) + b_idx(i, j, k))
    o_spec = pl.BlockSpec((tm, tn), lambda i, j, k: (i, j))
    ins, specs = [a, b], [a_spec, b_spec]
    if add is not None:
        ins.append(add)
        specs.append(o_spec)
    aliases = {}
    out_shape = jax.ShapeDtypeStruct((M, N), out_dtype)
    if slot is not None:
        buf, l = slot
        o_spec = pl.BlockSpec((None, tm, tn), lambda i, j, k: (l, i, j))
        out_shape = jax.ShapeDtypeStruct((DEPTH, M, N), out_dtype)
        if buf is not None:
            aliases = {len(ins): 0}
            ins.append(buf)
            specs.append(pl.BlockSpec(memory_space=pl.ANY))
    return pl.pallas_call(
        body, name=name, grid=(M // tm, N // tn, nk), in_specs=specs, out_specs=o_spec, out_shape=out_shape,
        scratch_shapes=[pltpu.VMEM((tm, tn), F32)], input_output_aliases=aliases,
        compiler_params=_params(("parallel", "parallel", "arbitrary")),
    )(*ins)


class Row:
    def __init__(self, arr, w, cb=None):
        self.arr, self.w, self.cb = arr, w, (cb if cb is not None else (lambda g: g))


class Const:
    def __init__(self, arr, shape=None, idx=None):
        self.arr = arr
        self.shape = shape if shape is not None else arr.shape
        self.idx = idx if idx is not None else (lambda g: (0,) * arr.ndim)


def _row_spec(r, tm):
    return pl.BlockSpec((tm, r.w), lambda g, i, r=r: (i, r.cb(g)))


def _const_spec(c):
    return pl.BlockSpec(c.shape, lambda g, i, c=c: c.idx(g))


def _valid_rows(i, tm):
    return (i * tm + lax.broadcasted_iota(jnp.int32, (tm, 1), 0)) >= PAD


def _rw_fwd(name, f, rows, consts, outs, Lp, tm, G=1):
    nr, nc = len(rows), len(consts)

    def body(*refs):
        i = pl.program_id(1)
        rv = [r[...].astype(F32) for r in refs[:nr]]
        cv = [r[...].astype(F32) for r in refs[nr:nr + nc]]
        res = f(_valid_rows(i, tm), *rv, *cv)
        for o_ref, v in zip(refs[nr + nc:], res):
            o_ref[...] = v.astype(o_ref.dtype)

    return pl.pallas_call(
        body, name=name, grid=(G, Lp // tm),
        in_specs=[_row_spec(r, tm) for r in rows] + [_const_spec(c) for c in consts],
        out_specs=[pl.BlockSpec((tm, w), lambda g, i: (i, g)) for w, _ in outs],
        out_shape=[jax.ShapeDtypeStruct((Lp, w * G), dt) for w, dt in outs],
        compiler_params=_params(("parallel", "arbitrary")),
    )(*[r.arr for r in rows], *[c.arr for c in consts])


def _rw_bwd(name, f, rows, consts, cts, drow_dtypes, adds, Lp, tm, G=1):
    nr, nc, nt = len(rows), len(consts), len(cts)
    want = [k for k, dt in enumerate(drow_dtypes) if dt is not None]
    add_k = [k for k in want if adds[k] is not None]

    def body(*refs):
        i = pl.program_id(1)
        pos = 0
        rv = [r[...].astype(F32) for r in refs[pos:pos + nr]]
        pos += nr
        cv = [r[...].astype(F32) for r in refs[pos:pos + nc]]
        pos += nc
        tv = [r[...].astype(F32) for r in refs[pos:pos + nt]]
        pos += nt
        av = {k: refs[pos + n][...].astype(F32) for n, k in enumerate(add_k)}
        pos += len(add_k)
        drow_refs = refs[pos:pos + len(want)]
        pos += len(want)
        dconst_refs = refs[pos:pos + nc]
        valid = _valid_rows(i, tm)
        _, vjp = jax.vjp(lambda *a: tuple(f(valid, *a)), *rv, *cv)
        grads = vjp(tuple(tv))
        for o_ref, k in zip(drow_refs, want):
            gk = grads[k]
            if k in av:
                gk = gk + av[k]
            o_ref[...] = gk.astype(o_ref.dtype)
        for n, o_ref in enumerate(dconst_refs):
            gc = grads[nr + n]

            @pl.when(i == 0)
            def _(o_ref=o_ref, gc=gc):
                o_ref[...] = gc

            @pl.when(i > 0)
            def _(o_ref=o_ref, gc=gc):
                o_ref[...] += gc

    out_row = lambda w: pl.BlockSpec((tm, w), lambda g, i: (i, g))
    res = pl.pallas_call(
        body, name=name, grid=(G, Lp // tm),
        in_specs=([_row_spec(r, tm) for r in rows] + [_const_spec(c) for c in consts]
                  + [_row_spec(r, tm) for r in cts] + [out_row(rows[k].w) for k in add_k]),
        out_specs=[out_row(rows[k].w) for k in want] + [_const_spec(c) for c in consts],
        out_shape=([jax.ShapeDtypeStruct((Lp, rows[k].w * G), drow_dtypes[k]) for k in want]
                   + [jax.ShapeDtypeStruct(c.arr.shape, F32) for c in consts]),
        compiler_params=_params(("parallel", "arbitrary")),
    )(*[r.arr for r in rows], *[c.arr for c in consts], *[r.arr for r in cts], *[adds[k] for k in add_k])
    drows = [None] * nr
    for n, k in enumerate(want):
        drows[k] = res[n]
    return drows, list(res[len(want):])


def _f_rms(valid, h, g):
    r = lax.rsqrt(jnp.mean(h * h, axis=-1, keepdims=True) + EPS)
    return (jnp.where(valid, h * r * g, 0.0),)


def _rms_fwd(h, g, name, Lp):
    t = BLOCK

    def body(h_ref, g_ref, o_ref, ot_ref):
        (y,) = _f_rms(_valid_rows(pl.program_id(0), t), h_ref[...], g_ref[...])
        o_ref[...] = y.astype(o_ref.dtype)
        ot_ref[...] = y.T.astype(ot_ref.dtype)

    return pl.pallas_call(
        body, name=name, grid=(Lp // t,),
        in_specs=[pl.BlockSpec((t, D_MODEL), lambda i: (i, 0)), pl.BlockSpec((1, D_MODEL), lambda i: (0, 0))],
        out_specs=[pl.BlockSpec((t, D_MODEL), lambda i: (i, 0)), pl.BlockSpec((D_MODEL, t), lambda i: (0, i))],
        out_shape=[jax.ShapeDtypeStruct((Lp, D_MODEL), BF16), jax.ShapeDtypeStruct((D_MODEL, Lp), BF16)],
        compiler_params=_params(("parallel",)),
    )(h, g)


def _f_logg(valid, glr, w, b):
    pre = jnp.dot(glr.astype(BF16), w.astype(BF16), preferred_element_type=F32) + b
    return (jnp.where(valid, _log_sigmoid(pre) / GLA_TAU, 0.0),)


def _f_gla_post(valid, oc, rc, g):
    y = oc * lax.rsqrt(jnp.mean(oc * oc, axis=-1, keepdims=True) + EPS) * g
    return (jnp.where(valid, rc * _sigmoid(rc) * y, 0.0),)


def _f_merge(valid, ya, yb, yc, ga, gb, gc, ba, bb, bc):
    mix = _sigmoid(ga + ba) * ya + _sigmoid(gb + bb) * yb + _sigmoid(gc + bc) * yc
    return (jnp.where(valid, mix, 0.0),)


def _fox_gate_fwd(side, bf, Lp):
    t = BLOCK
    n = Lp // t

    def body(s_ref, b_ref, c_ref, carry):
        i = pl.program_id(0)

        @pl.when(i == 0)
        def _():
            carry[...] = jnp.zeros_like(carry)

        lane = lax.broadcasted_iota(jnp.int32, (t, LANES), 1)
        ok = _valid_rows(i, t) & (lane < FOX_HEADS)
        logf = jnp.where(ok, _log_sigmoid(s_ref[...] + b_ref[...]), 0.0)
        tril = (lax.broadcasted_iota(jnp.int32, (t, t), 1) <= lax.broadcasted_iota(jnp.int32, (t, t), 0)).astype(F32)
        c = jnp.dot(tril, logf, precision=HIGHEST, preferred_element_type=F32) + carry[...]
        c_ref[...] = c
        carry[...] = c[t - 1:t, :]

    return pl.pallas_call(
        body, name="fox_gate_fwd", grid=(n,),
        in_specs=[pl.BlockSpec((t, LANES), lambda i: (i, 0)), pl.BlockSpec((1, LANES), lambda i: (0, 0))],
        out_specs=pl.BlockSpec((t, LANES), lambda i: (i, 0)),
        out_shape=jax.ShapeDtypeStruct((Lp, LANES), F32),
        scratch_shapes=[pltpu.VMEM((1, LANES), F32)],
        compiler_params=_params(("arbitrary",)),
    )(side, bf)


def _fox_gate_bwd(side, bf, dc, Lp):
    t = BLOCK
    n = Lp // t

    def body(s_ref, b_ref, dc_ref, dfa_ref, db_ref, carry):
        i = pl.program_id(0)

        @pl.when(i == 0)
        def _():
            carry[...] = jnp.zeros_like(carry)

        lane = lax.broadcasted_iota(jnp.int32, (t, LANES), 1)
        ok = _valid_rows(n - 1 - i, t) & (lane < FOX_HEADS)
        triu = (lax.broadcasted_iota(jnp.int32, (t, t), 1) >= lax.broadcasted_iota(jnp.int32, (t, t), 0)).astype(F32)
        dlogf = jnp.dot(triu, dc_ref[...], precision=HIGHEST, preferred_element_type=F32) + carry[...]
        carry[...] = dlogf[0:1, :]
        dpre = jnp.where(ok, dlogf * _sigmoid(-(s_ref[...] + b_ref[...])), 0.0)
        dfa_ref[...] = dpre
        part = jnp.sum(dpre, axis=0, keepdims=True)

        @pl.when(i == 0)
        def _():
            db_ref[...] = part

        @pl.when(i > 0)
        def _():
            db_ref[...] += part

    rev = lambda i: (n - 1 - i, 0)
    return pl.pallas_call(
        body, name="fox_gate_bwd", grid=(n,),
        in_specs=[pl.BlockSpec((t, LANES), rev), pl.BlockSpec((1, LANES), lambda i: (0, 0)),
                  pl.BlockSpec((t, LANES), rev)],
        out_specs=[pl.BlockSpec((t, LANES), rev), pl.BlockSpec((1, LANES), lambda i: (0, 0))],
        out_shape=[jax.ShapeDtypeStruct((Lp, LANES), F32), jax.ShapeDtypeStruct((1, LANES), F32)],
        scratch_shapes=[pltpu.VMEM((1, LANES), F32)],
        compiler_params=_params(("arbitrary",)),
    )(side, bf, dc)


def _fox_key_bias(cq_ref, ck_ref, j, t):
    col = j * t + lax.broadcasted_iota(jnp.int32, (1, t), 1)
    return jnp.where(col >= PAD, ck_ref[...] - cq_ref[0:1, :], -NEG)


def _fox_s(q, k, bias, diagonal, t):
    s = lax.dot_general(q, k, (((1,), (1,)), ((), ())), preferred_element_type=F32) * (FOX_HD ** -0.5) - bias
    if diagonal:
        causal = lax.broadcasted_iota(jnp.int32, (t, t), 1) <= lax.broadcasted_iota(jnp.int32, (t, t), 0)
        s = jnp.where(causal, s, NEG)
    return s


def _fox_fwd(main, c_col, c_row, Lp, t):
    n = Lp // t
    qb, kb, vb = QA // FOX_HD, KA // FOX_HD, VA // FOX_HD

    def body(q_ref, k_ref, v_ref, cq_ref, ck_ref, o_ref, ox_ref, lse_ref, m_s, l_s, acc):
        i, j = pl.program_id(1), pl.program_id(2)

        @pl.when(j == 0)
        def _():
            m_s[...] = jnp.full_like(m_s, NEG)
            l_s[...] = jnp.zeros_like(l_s)
            acc[...] = jnp.zeros_like(acc)

        def update(diagonal):
            s = _fox_s(q_ref[...], k_ref[...], _fox_key_bias(cq_ref, ck_ref, j, t), diagonal, t)
            m_new = jnp.maximum(m_s[...], jnp.max(s, axis=1, keepdims=True))
            alpha = jnp.exp(m_s[...] - m_new)
            p = jnp.exp(s - m_new)
            l_s[...] = alpha * l_s[...] + jnp.sum(p, axis=1, keepdims=True)
            p_hi = p.astype(BF16)
            p_lo = (p - p_hi.astype(F32)).astype(BF16)
            pv = (jnp.dot(p_hi, v_ref[...], preferred_element_type=F32)
                  + jnp.dot(p_lo, v_ref[...], preferred_element_type=F32))
            acc[...] = alpha * acc[...] + pv
            m_s[...] = m_new

        @pl.when(j < i)
        def _():
            update(False)

        @pl.when(j == i)
        def _():
            update(True)
            o = jnp.where(_valid_rows(i, t), acc[...] / l_s[...], 0.0)
            o_ref[...] = o.astype(o_ref.dtype)
            ox_ref[...] = o
            lse_ref[...] = m_s[...] + jnp.log(l_s[...])

    kv = lambda base: pl.BlockSpec((t, FOX_HD), lambda h, i, j: (jnp.minimum(j, i), base + h))
    return pl.pallas_call(
        body, name="fox_fwd", grid=(FOX_HEADS, n, n),
        in_specs=[pl.BlockSpec((t, FOX_HD), lambda h, i, j: (i, qb + h)), kv(kb), kv(vb),
                  pl.BlockSpec((None, t, 1), lambda h, i, j: (h, i, 0)),
                  pl.BlockSpec((None, 1, t), lambda h, i, j: (h, 0, jnp.minimum(j, i)))],
        out_specs=[pl.BlockSpec((t, FOX_HD), lambda h, i, j: (i, h)), pl.BlockSpec((t, FOX_HD), lambda h, i, j: (i, h)),
                   pl.BlockSpec((None, t, 1), lambda h, i, j: (h, i, 0))],
        out_shape=[jax.ShapeDtypeStruct((Lp, FOX_WIDTH), BF16), jax.ShapeDtypeStruct((Lp, FOX_WIDTH), F32),
                   jax.ShapeDtypeStruct((FOX_HEADS, Lp, 1), F32)],
        scratch_shapes=[pltpu.VMEM((t, 1), F32), pltpu.VMEM((t, 1), F32), pltpu.VMEM((t, FOX_HD), F32)],
        compiler_params=_params(("parallel", "parallel", "arbitrary")),
    )(main, main, main, c_col, c_row)


def _fox_p_dp(q_ref, k_ref, v_ref, cq_ref, ck_ref, lse_ref, do_ref, j, diagonal, t):
    s = _fox_s(q_ref[...], k_ref[...], _fox_key_bias(cq_ref, ck_ref, j, t), diagonal, t)
    p = jnp.exp(s - lse_ref[...])
    dp = lax.dot_general(do_ref[...], v_ref[...], (((1,), (1,)), ((), ())), preferred_element_type=F32)
    return p, dp


def _fox_delta(ox, doa, Lp, t):
    n = Lp // t

    def body(o_ref, do_ref, dl_ref):
        dl_ref[...] = jnp.sum(o_ref[...] * do_ref[...].astype(F32), axis=1, keepdims=True)

    blk = pl.BlockSpec((t, FOX_HD), lambda h, i: (i, h))
    return pl.pallas_call(
        body, name="fox_delta", grid=(FOX_HEADS, n), in_specs=[blk, blk],
        out_specs=pl.BlockSpec((None, t, 1), lambda h, i: (h, i, 0)),
        out_shape=jax.ShapeDtypeStruct((FOX_HEADS, Lp, 1), F32),
        compiler_params=_params(("parallel", "parallel")),
    )(ox, doa)


def _fox_bwd(main, c_col, c_row, lse, delta, doa, Lp, t):
    n = Lp // t
    qb, kb, vb = QA // FOX_HD, KA // FOX_HD, VA // FOX_HD
    scale = FOX_HD ** -0.5

    def body(q_ref, k_ref, v_ref, cq_ref, ck_ref, lse_ref, dl_ref, do_ref, dq_ref, dk_ref, dv_ref, dck_ref,
             dq_s, dk_s, dv_s, dc_s):
        j, i = pl.program_id(1), pl.program_id(2)

        @pl.when(jnp.logical_and(j == 0, i == 0))
        def _():
            dq_s[...] = jnp.zeros_like(dq_s)

        @pl.when(i == 0)
        def _():
            dk_s[...] = jnp.zeros_like(dk_s)
            dv_s[...] = jnp.zeros_like(dv_s)
            dc_s[...] = jnp.zeros_like(dc_s)

        def sweep(diagonal):
            p, dp = _fox_p_dp(q_ref, k_ref, v_ref, cq_ref, ck_ref, lse_ref, do_ref, j, diagonal, t)
            ds = p * (dp - dl_ref[...])
            dsb = ds.astype(BF16)
            tn = (((0,), (0,)), ((), ()))
            dv_s[...] += lax.dot_general(p.astype(BF16), do_ref[...], tn, preferred_element_type=F32)
            dk_s[...] += lax.dot_general(dsb, q_ref[...], tn, preferred_element_type=F32)
            dc_s[...] -= jnp.sum(ds, axis=0, keepdims=True)
            rows = pl.ds(pl.multiple_of(i * t, t), t)
            dq_s[rows, :] += jnp.dot(dsb, k_ref[...], preferred_element_type=F32)

        @pl.when(i > j)
        def _():
            sweep(False)

        @pl.when(i == j)
        def _():
            sweep(True)

        @pl.when(i == n - 1)
        def _():
            dk_ref[...] = (dk_s[...] * scale).astype(dk_ref.dtype)
            dv_ref[...] = dv_s[...].astype(dv_ref.dtype)
            dck_ref[...] = dc_s[...]

        @pl.when(jnp.logical_and(j == n - 1, i == n - 1))
        def _():
            dq_ref[...] = (dq_s[...] * scale).astype(dq_ref.dtype)

    qrow = lambda base: pl.BlockSpec((t, FOX_HD), lambda h, j, i: (jnp.maximum(i, j), base + h))
    kv = lambda base: pl.BlockSpec((t, FOX_HD), lambda h, j, i: (j, base + h))
    col = pl.BlockSpec((None, t, 1), lambda h, j, i: (h, jnp.maximum(i, j), 0))
    row = pl.BlockSpec((None, 1, t), lambda h, j, i: (h, 0, j))
    wide = jax.ShapeDtypeStruct((Lp, FOX_WIDTH), BF16)
    return pl.pallas_call(
        body, name="fox_bwd", grid=(FOX_HEADS, n, n),
        in_specs=[qrow(qb), kv(kb), kv(vb), col, row, col, col, qrow(0)],
        out_specs=[pl.BlockSpec((Lp, FOX_HD), lambda h, j, i: (0, h)), kv(0), kv(0), row],
        out_shape=[wide, wide, wide, jax.ShapeDtypeStruct((FOX_HEADS, 1, Lp), F32)],
        scratch_shapes=[pltpu.VMEM((Lp, FOX_HD), F32), pltpu.VMEM((t, FOX_HD), F32), pltpu.VMEM((t, FOX_HD), F32),
                        pltpu.VMEM((1, t), F32)],
        compiler_params=_params(("parallel", "arbitrary", "arbitrary")),
    )(main, main, main, c_col, c_row, lse, delta, doa)


def _shift_down(x, n):
    return pltpu.roll(x, n, 0)


def _shift_up(x, n):
    return pltpu.roll(x, x.shape[0] - n, 0)


def _prev_spec(tm, ct, cb):
    return pl.BlockSpec((HALO, ct), lambda g, i: (jnp.maximum(i * (tm // HALO) - 1, 0), cb(g)))


def _next_spec(tm, ct, cb, nrows):
    last = nrows // HALO - 1
    return pl.BlockSpec((HALO, ct), lambda g, i: (jnp.minimum((i + 1) * (tm // HALO), last), cb(g)))


def _cur_spec(tm, ct, cb):
    return pl.BlockSpec((tm, ct), lambda g, i: (i, cb(g)))


def _wrow(w_ref, k):
    return w_ref[k:k + 1, :]


def _rows3(s0, s1, s2, ct):
    r = lax.broadcasted_iota(jnp.int32, (8, ct), 0)
    return jnp.where(r == 0, s0, jnp.where(r == 1, s1, jnp.where(r == 2, s2, 0.0)))


def _acc_out(ref, i, val):
    @pl.when(i == 0)
    def _():
        ref[...] = val

    @pl.when(i > 0)
    def _():
        ref[...] += val


def _sconv_fwd(main, w8, Lp, tm):
    ct = 256
    G = CONV_CH // ct
    bb, cb, hb = (lambda g: SCB // ct + g), (lambda g: SCC // ct + g), (lambda g: SCH // ct + g)

    def body(b_ref, c_ref, h_ref, cp_ref, hp_ref, w_ref, o_ref):
        i = pl.program_id(1)
        z = c_ref[...].astype(F32) * h_ref[...].astype(F32)
        zp = jnp.where(i > 0, cp_ref[...].astype(F32) * hp_ref[...].astype(F32), 0.0)
        zz = jnp.concatenate([zp, z], axis=0)
        cz = (_wrow(w_ref, 0) * _shift_down(zz, 2)[HALO:] + _wrow(w_ref, 1) * _shift_down(zz, 1)[HALO:]
              + _wrow(w_ref, 2) * z)
        o_ref[...] = (b_ref[...].astype(F32) * cz).astype(o_ref.dtype)

    return pl.pallas_call(
        body, name="sconv_fwd", grid=(G, Lp // tm),
        in_specs=[_cur_spec(tm, ct, bb), _cur_spec(tm, ct, cb), _cur_spec(tm, ct, hb),
                  _prev_spec(tm, ct, cb), _prev_spec(tm, ct, hb), pl.BlockSpec((8, ct), lambda g, i: (0, g))],
        out_specs=pl.BlockSpec((tm, ct), lambda g, i: (i, g)),
        out_shape=jax.ShapeDtypeStruct((Lp, CONV_CH), BF16),
        compiler_params=_params(("parallel", "arbitrary")),
    )(main, main, main, main, main, w8)


def _sconv_bwd(main, w8, dub, Lp, tm):
    ct = 256
    G = CONV_CH // ct
    n = Lp // tm
    bb, cb, hb, ob = (lambda g: SCB // ct + g), (lambda g: SCC // ct + g), (lambda g: SCH // ct + g), (lambda g: g)

    def body(b_ref, c_ref, h_ref, cp_ref, hp_ref, bn_ref, d_ref, dn_ref, w_ref, db_ref, dc_ref, dh_ref, dw_ref):
        i = pl.program_id(1)
        b, c, h = b_ref[...].astype(F32), c_ref[...].astype(F32), h_ref[...].astype(F32)
        z = c * h
        zp = jnp.where(i > 0, cp_ref[...].astype(F32) * hp_ref[...].astype(F32), 0.0)
        zz = jnp.concatenate([zp, z], axis=0)
        z1, z2 = _shift_down(zz, 1)[HALO:], _shift_down(zz, 2)[HALO:]
        w0, w1, w2 = _wrow(w_ref, 0), _wrow(w_ref, 1), _wrow(w_ref, 2)
        cz = w0 * z2 + w1 * z1 + w2 * z
        dub_c = d_ref[...].astype(F32)
        db_ref[...] = (dub_c * cz).astype(db_ref.dtype)
        dcz = dub_c * b
        dcz_n = jnp.where(i < n - 1, dn_ref[...].astype(F32) * bn_ref[...].astype(F32), 0.0)
        dd = jnp.concatenate([dcz, dcz_n], axis=0)
        dz = w2 * dcz + w1 * _shift_up(dd, 1)[:tm] + w0 * _shift_up(dd, 2)[:tm]
        dc_ref[...] = (dz * h).astype(dc_ref.dtype)
        dh_ref[...] = (dz * c).astype(dh_ref.dtype)
        s = lambda x: jnp.sum(dcz * x, axis=0, keepdims=True)
        _acc_out(dw_ref, i, _rows3(s(z2), s(z1), s(z), ct))

    out = pl.BlockSpec((tm, ct), lambda g, i: (i, g))
    return pl.pallas_call(
        body, name="sconv_bwd", grid=(G, n),
        in_specs=[_cur_spec(tm, ct, bb), _cur_spec(tm, ct, cb), _cur_spec(tm, ct, hb),
                  _prev_spec(tm, ct, cb), _prev_spec(tm, ct, hb), _next_spec(tm, ct, bb, Lp),
                  _cur_spec(tm, ct, ob), _next_spec(tm, ct, ob, Lp), pl.BlockSpec((8, ct), lambda g, i: (0, g))],
        out_specs=[out, out, out, pl.BlockSpec((8, ct), lambda g, i: (0, g))],
        out_shape=[jax.ShapeDtypeStruct((Lp, CONV_CH), BF16)] * 3 + [jax.ShapeDtypeStruct((8, CONV_CH), F32)],
        compiler_params=_params(("parallel", "arbitrary")),
    )(main, main, main, main, main, main, dub, dub, w8)


def _conv3(w_ref, ext):
    return _wrow(w_ref, 0) * _shift_down(ext, 2) + _wrow(w_ref, 1) * _shift_down(ext, 1) + _wrow(w_ref, 2) * ext


def _mlp_act_fwd(up, w8, Lp, tm):
    ct = 256
    G = D_FF // ct
    gb, ub = (lambda g: g), (lambda g: G + g)

    def body(g_ref, u_ref, gp_ref, up_ref, wg_ref, wu_ref, o_ref, ot_ref):
        i = pl.program_id(1)

        def conv(cur, prev, w_ref):
            ext = jnp.concatenate([jnp.where(i > 0, prev[...].astype(F32), 0.0), cur[...].astype(F32)], axis=0)
            return _conv3(w_ref, ext)[HALO:]

        ug, uu = conv(g_ref, gp_ref, wg_ref), conv(u_ref, up_ref, wu_ref)
        a = ug * _sigmoid(ug) * uu
        o_ref[...] = a.astype(o_ref.dtype)
        ot_ref[...] = a.T.astype(ot_ref.dtype)

    wspec = lambda cb: pl.BlockSpec((8, ct), lambda g, i: (0, cb(g)))
    return pl.pallas_call(
        body, name="mlp_act_fwd", grid=(G, Lp // tm),
        in_specs=[_cur_spec(tm, ct, gb), _cur_spec(tm, ct, ub), _prev_spec(tm, ct, gb), _prev_spec(tm, ct, ub),
                  wspec(gb), wspec(ub)],
        out_specs=[pl.BlockSpec((tm, ct), lambda g, i: (i, g)), pl.BlockSpec((ct, tm), lambda g, i: (g, i))],
        out_shape=[jax.ShapeDtypeStruct((Lp, D_FF), BF16), jax.ShapeDtypeStruct((D_FF, Lp), BF16)],
        compiler_params=_params(("parallel", "arbitrary")),
    )(up, up, up, up, w8, w8)


def _mlp_act_bwd(up, w8, da, Lp, tm):
    ct = 256
    G = D_FF // ct
    n = Lp // tm
    gb, ub, ob = (lambda g: g), (lambda g: G + g), (lambda g: g)

    def body(g_ref, u_ref, gp_ref, up_ref, gn_ref, un_ref, d_ref, dn_ref, wg_ref, wu_ref,
             dg_ref, du_ref, dwg_ref, dwu_ref):
        i = pl.program_id(1)

        def ext_of(prev, cur, nxt):
            return jnp.concatenate([jnp.where(i > 0, prev[...].astype(F32), 0.0), cur[...].astype(F32),
                                    jnp.where(i < n - 1, nxt[...].astype(F32), 0.0)], axis=0)

        eg, eu = ext_of(gp_ref, g_ref, gn_ref), ext_of(up_ref, u_ref, un_ref)
        da_e = jnp.concatenate([jnp.zeros((HALO, ct), F32), d_ref[...].astype(F32),
                                jnp.where(i < n - 1, dn_ref[...].astype(F32), 0.0)], axis=0)
        ug, uu = _conv3(wg_ref, eg), _conv3(wu_ref, eu)
        sg = _sigmoid(ug)
        dug = da_e * uu * (sg * (1.0 + ug * (1.0 - sg)))
        duu = da_e * (ug * sg)
        cur = slice(HALO, HALO + tm)

        def back(w_ref, dx, e, dx_ref, dw_ref):
            d_in = _wrow(w_ref, 2) * dx + _wrow(w_ref, 1) * _shift_up(dx, 1) + _wrow(w_ref, 0) * _shift_up(dx, 2)
            dx_ref[...] = d_in[cur].astype(dx_ref.dtype)
            s = lambda x: jnp.sum(dx[cur] * x[cur], axis=0, keepdims=True)
            _acc_out(dw_ref, i, _rows3(s(_shift_down(e, 2)), s(_shift_down(e, 1)), s(e), ct))

        back(wg_ref, dug, eg, dg_ref, dwg_ref)
        back(wu_ref, duu, eu, du_ref, dwu_ref)

    wspec = lambda cb: pl.BlockSpec((8, ct), lambda g, i: (0, cb(g)))
    out = pl.BlockSpec((tm, ct), lambda g, i: (i, g))
    return pl.pallas_call(
        body, name="mlp_act_bwd", grid=(G, n),
        in_specs=[_cur_spec(tm, ct, gb), _cur_spec(tm, ct, ub), _prev_spec(tm, ct, gb), _prev_spec(tm, ct, ub),
                  _next_spec(tm, ct, gb, Lp), _next_spec(tm, ct, ub, Lp), _cur_spec(tm, ct, ob),
                  _next_spec(tm, ct, ob, Lp), wspec(gb), wspec(ub)],
        out_specs=[out, out, wspec(ob), wspec(ob)],
        out_shape=[jax.ShapeDtypeStruct((Lp, D_FF), BF16)] * 2 + [jax.ShapeDtypeStruct((8, D_FF), F32)] * 2,
        compiler_params=_params(("parallel", "arbitrary")),
    )(up, up, up, up, up, up, da, da, w8, w8)


def _gla_chunk(q, k, v, g, s0):
    C = BLOCK
    r_i = lax.broadcasted_iota(jnp.int32, (C, C), 0)
    c_i = lax.broadcasted_iota(jnp.int32, (C, C), 1)
    row = lax.broadcasted_iota(jnp.int32, (C, GLA_DK), 0)
    b = jnp.dot((c_i <= r_i).astype(F32), g, precision=HIGHEST, preferred_element_type=F32)
    row_of = lambda n: jnp.sum(jnp.where(row == n, b, 0.0), axis=0, keepdims=True)
    refs = [row_of(n * GLA_SUB) for n in range(C // GLA_SUB)]
    sub = jnp.bitwise_and(row, -GLA_SUB)
    ref_all = sum(jnp.where(sub == n * GLA_SUB, refs[n], 0.0) for n in range(C // GLA_SUB))
    qs = q * (GLA_DK ** -0.5)
    qt = (qs * jnp.exp(b - ref_all)).astype(BF16)
    sub_start = jnp.bitwise_and(r_i, -GLA_SUB)
    att = jnp.zeros((C, C), F32)
    for n in range(C // GLA_SUB):
        kt = (k * jnp.exp(jnp.minimum(refs[n] - b, 60.0))).astype(BF16)
        a_n = lax.dot_general(qt, kt, (((1,), (1,)), ((), ())), preferred_element_type=F32)
        att = att + jnp.where((sub_start == n * GLA_SUB) & (c_i <= r_i), a_n, 0.0)
    o = (jnp.dot(att.astype(BF16), v.astype(BF16), preferred_element_type=F32)
         + jnp.dot((qs * jnp.exp(b)).astype(BF16), s0.astype(BF16), preferred_element_type=F32))
    kd = (k * jnp.exp(row_of(C - 1) - b)).astype(BF16)
    last_rows = (lax.broadcasted_iota(jnp.int32, (C, GLA_DV), 0) == C - 1).astype(F32)
    decay = lax.dot_general(b, last_rows, (((0,), (0,)), ((), ())), precision=HIGHEST,
                            preferred_element_type=F32)
    s1 = jnp.exp(decay) * s0 + lax.dot_general(kd, v.astype(BF16), (((0,), (0,)), ((), ())),
                                               preferred_element_type=F32)
    return o, s1


def _gla_fwd(main, logg, Lp):
    n = Lp // BLOCK
    qb, kb, vb = QC // GLA_DK, KC // GLA_DK, VC // GLA_DV

    def body(q_ref, k_ref, v_ref, g_ref, o_ref, st_ref, s_s):
        c = pl.program_id(1)

        @pl.when(c == 0)
        def _():
            s_s[...] = jnp.zeros_like(s_s)

        s0 = s_s[...]
        st_ref[...] = s0
        o, s1 = _gla_chunk(q_ref[...].astype(F32), k_ref[...].astype(F32), v_ref[...].astype(F32), g_ref[...], s0)
        o_ref[...] = o
        s_s[...] = s1

    return pl.pallas_call(
        body, name="gla_fwd", grid=(GLA_HEADS, n),
        in_specs=[pl.BlockSpec((BLOCK, GLA_DK), lambda h, c: (c, qb + h)),
                  pl.BlockSpec((BLOCK, GLA_DK), lambda h, c: (c, kb + h)),
                  pl.BlockSpec((BLOCK, GLA_DV), lambda h, c: (c, vb + h)),
                  pl.BlockSpec((BLOCK, GLA_DK), lambda h, c: (c, h))],
        out_specs=[pl.BlockSpec((BLOCK, GLA_DV), lambda h, c: (c, h)),
                   pl.BlockSpec((None, None, GLA_DK, GLA_DV), lambda h, c: (h, c, 0, 0))],
        out_shape=[jax.ShapeDtypeStruct((Lp, GLA_HEADS * GLA_DV), F32),
                   jax.ShapeDtypeStruct((GLA_HEADS, n, GLA_DK, GLA_DV), F32)],
        scratch_shapes=[pltpu.VMEM((GLA_DK, GLA_DV), F32)],
        compiler_params=_params(("parallel", "arbitrary")),
    )(main, main, main, logg)


def _gla_bwd(main, logg, states, do, Lp):
    n = Lp // BLOCK
    qb, kb, vb = QC // GLA_DK, KC // GLA_DK, VC // GLA_DV

    def body(q_ref, k_ref, v_ref, g_ref, st_ref, do_ref, dq_ref, dk_ref, dv_ref, dg_ref, ds_s):
        c = pl.program_id(1)

        @pl.when(c == 0)
        def _():
            ds_s[...] = jnp.zeros_like(ds_s)

        _, vjp = jax.vjp(_gla_chunk, q_ref[...].astype(F32), k_ref[...].astype(F32), v_ref[...].astype(F32),
                         g_ref[...], st_ref[...])
        dq, dk, dv, dg, ds0 = vjp((do_ref[...], ds_s[...]))
        dq_ref[...] = dq.astype(dq_ref.dtype)
        dk_ref[...] = dk.astype(dk_ref.dtype)
        dv_ref[...] = dv.astype(dv_ref.dtype)
        dg_ref[...] = dg
        ds_s[...] = ds0

    rk = lambda base: pl.BlockSpec((BLOCK, GLA_DK), lambda h, c: (n - 1 - c, base + h))
    rv = lambda base: pl.BlockSpec((BLOCK, GLA_DV), lambda h, c: (n - 1 - c, base + h))
    return pl.pallas_call(
        body, name="gla_bwd", grid=(GLA_HEADS, n),
        in_specs=[rk(qb), rk(kb), rv(vb), rk(0),
                  pl.BlockSpec((None, None, GLA_DK, GLA_DV), lambda h, c: (h, n - 1 - c, 0, 0)), rv(0)],
        out_specs=[rk(0), rk(0), rv(0), rk(0)],
        out_shape=[jax.ShapeDtypeStruct((Lp, GLA_HEADS * GLA_DK), BF16), jax.ShapeDtypeStruct((Lp, GLA_HEADS * GLA_DK), BF16),
                   jax.ShapeDtypeStruct((Lp, GLA_HEADS * GLA_DV), BF16), jax.ShapeDtypeStruct((Lp, GLA_HEADS * GLA_DK), F32)],
        scratch_shapes=[pltpu.VMEM((GLA_DK, GLA_DV), F32)],
        compiler_params=_params(("parallel", "arbitrary")),
    )(main, main, main, logg, states, do)


def _loss_head(h, g, target, Lp):
    t = BLOCK
    D = D_MODEL

    def body(h_ref, g_ref, t_ref, loss_ref, dh_ref, dg_ref):
        i = pl.program_id(0)
        x = h_ref[...]
        tok = (i * t + lax.broadcasted_iota(jnp.int32, (t, 1), 0)) >= BLOCK
        r = lax.rsqrt(jnp.mean(x * x, axis=-1, keepdims=True) + EPS)
        nrm = x * r
        e = jnp.where(tok, nrm * g_ref[...] - t_ref[...], 0.0)
        part = 0.5 * jnp.sum(jnp.sum(e * e, axis=1, keepdims=True), axis=0, keepdims=True) / D
        dy = e / D
        dn = dy * g_ref[...]
        dh_ref[...] = r * (dn - nrm * jnp.mean(dn * nrm, axis=-1, keepdims=True))
        _acc_out(dg_ref, i, jnp.sum(dy * nrm, axis=0, keepdims=True))
        _acc_out(loss_ref, i, jnp.broadcast_to(part, (1, LANES)))

    return pl.pallas_call(
        body, name="loss_head", grid=(Lp // t,),
        in_specs=[pl.BlockSpec((t, D), lambda i: (i, 0)), pl.BlockSpec((1, D), lambda i: (0, 0)),
                  pl.BlockSpec((t, D), lambda i: (jnp.maximum(i - 1, 0), 0))],
        out_specs=[pl.BlockSpec((1, LANES), lambda i: (0, 0)), pl.BlockSpec((t, D), lambda i: (i, 0)),
                   pl.BlockSpec((1, D), lambda i: (0, 0))],
        out_shape=[jax.ShapeDtypeStruct((1, LANES), F32), jax.ShapeDtypeStruct((Lp, D), F32),
                   jax.ShapeDtypeStruct((1, D), F32)],
        compiler_params=_params(("arbitrary",)),
    )(h, g, target)


def _adamw(w, g, m, v, name):
    if w.ndim == 1:
        outs = _adamw(*(a.reshape(1, -1) for a in (w, g, m, v)), name)
        return tuple(o.reshape(w.shape) for o in outs)
    if w.ndim == 3 and w.shape[-1] % LANES and w.shape[-2] % LANES == 0:
        outs = _adamw(*(a.transpose(2, 0, 1) for a in (w, g, m, v)), name)
        return tuple(o.transpose(1, 2, 0) for o in outs)
    rows, cols = w.shape[-2:]
    budget_rows = max(8, ADAM_BLOCK_BYTES // (4 * cols))
    tr = rows if rows <= budget_rows else _pick(rows, tuple(t for t in (512, 256, 128, 64, 32, 16, 8) if t <= budget_rows))
    lead, tc = 1, cols
    if w.ndim == 3 and tr == rows:
        lead = max(d for d in range(1, 1025) if w.shape[0] % d == 0)
        fits = [c for c in (cols, 2048, 1024, 512, 256, 128) if cols % c == 0 and 4 * lead * rows * c <= ADAM_BLOCK_BYTES]
        tc = fits[0] if fits else LANES

    def body(w_ref, g_ref, m_ref, v_ref, go_ref, d_ref, nm_ref, nv_ref):
        gg = g_ref[...]
        mm = ADAM_B1 * m_ref[...] + (1.0 - ADAM_B1) * gg
        vv = ADAM_B2 * v_ref[...] + (1.0 - ADAM_B2) * jnp.square(gg)
        m_hat = mm / (1.0 - ADAM_B1 ** ADAM_STEP)
        v_hat = vv / (1.0 - ADAM_B2 ** ADAM_STEP)
        d_ref[...] = -ADAM_LR * (m_hat / (jnp.sqrt(v_hat) + ADAM_EPS) + ADAM_WD * w_ref[...])
        go_ref[...] = gg
        nm_ref[...] = mm
        nv_ref[...] = vv

    if w.ndim == 3:
        spec = pl.BlockSpec((lead, tr, tc), lambda l, i, j: (l, i, j))
        grid = (w.shape[0] // lead, rows // tr, cols // tc)
    else:
        spec, grid = pl.BlockSpec((tr, cols), lambda i: (i, 0)), (rows // tr,)
    return pl.pallas_call(
        body, name=name, grid=grid, in_specs=[spec] * 4, out_specs=[spec] * 4,
        out_shape=[jax.ShapeDtypeStruct(w.shape, F32)] * 4,
        compiler_params=_params(("parallel",) * len(grid)),
    )(w, g, m, v)


def _place():
    x, y, c = lax.axis_index("x"), lax.axis_index("y"), lax.axis_index("c")
    chips = [(1 - x, y), (x, 1 - y), (1 - x, 1 - y)]
    return x, y, c, chips


def _rcopy(src, dst, send_sems, recv_sems, k, to):
    return pltpu.make_async_remote_copy(src_ref=src, dst_ref=dst, send_sem=send_sems.at[k], recv_sem=recv_sems.at[k],
                                        device_id=to, device_id_type=MESH)


def _any_spec():
    return pl.BlockSpec(memory_space=pl.ANY)


def _shard_ref(ref, mode, t, r, c):
    if mode == "rows":
        return ref.at[pl.ds(pl.multiple_of(t * r, 16), r), :]
    if mode == "cols":
        return ref.at[:, pl.ds(pl.multiple_of(t * c, LANES), c)]
    return ref.at[t]


def _gathered_shape(mode, r, c):
    return {"rows": (4 * r, c), "cols": (r, 4 * c), "stack": (4, r, c)}[mode]


def _place_own(shard, mode, me1, name):
    _, r, c = shard.shape
    tr = _ew_rows(r, c)
    blk = {"rows": (None, tr, c), "cols": (None, tr, c), "stack": (None, None, tr, c)}[mode]
    idx = {"rows": lambda l, i, me: (l, me[0] * (r // tr) + i, 0),
           "cols": lambda l, i, me: (l, i, me[0]),
           "stack": lambda l, i, me: (l, me[0], i, 0)}[mode]

    def body(me_ref, in_ref, out_ref):
        out_ref[...] = in_ref[...]

    return pl.pallas_call(
        body, name=name,
        grid_spec=pltpu.PrefetchScalarGridSpec(
            num_scalar_prefetch=1, grid=(DEPTH, r // tr),
            in_specs=[pl.BlockSpec((None, tr, c), lambda l, i, me: (l, i, 0))],
            out_specs=pl.BlockSpec(blk, idx)),
        out_shape=jax.ShapeDtypeStruct((DEPTH,) + _gathered_shape(mode, r, c), shard.dtype),
        compiler_params=_params(("parallel", "parallel")),
    )(me1, shard)


def _gather_weights(shards, modes):
    n = len(shards)
    dims = [s.shape[1:] for s in shards]
    me1 = jnp.reshape(2 * lax.axis_index("x") + lax.axis_index("y"), (1,)).astype(jnp.int32)
    placed = [_place_own(shards[k], modes[k], me1, f"gather_place_{k}") for k in range(n)]

    def body(*refs):
        ins, outs = refs[:n], refs[2 * n:3 * n]
        send_sems, recv_sems = refs[3 * n:]
        x, y, c, _ = _place()
        n1 = (x + (1 - c) * (1 - 2 * x), y + c * (1 - 2 * y))
        n2 = (x + c * (1 - 2 * x), y + (1 - c) * (1 - 2 * y))
        diag = (1 - x, 1 - y)
        chip = lambda ch: 2 * ch[0] + ch[1]
        me, here, sibling = (x, y), (x, y, c), (x, y, 1 - c)
        place = lambda k, l, t: _shard_ref(outs[k].at[l], modes[k], chip(t), *dims[k])

        def copy(k, m, l, t, to, src=None):
            blk = place(k, l, t)
            return _rcopy(blk if src is None else src, blk, send_sems, recv_sems, 6 * k + m, to)

        sent = [copy(k, 0, c, me, (*n1, c), ins[k].at[c]) for k in range(n)]
        sent += [copy(k, 1, c, me, (*n2, c), ins[k].at[c]) for k in range(n)]
        for cp in sent:
            cp.start()
        for k in range(n):
            copy(k, 0, c, n1, here).wait_recv()
            sent += [copy(k, 2, c, n1, (*n2, c)), copy(k, 3, c, n1, sibling)]
            sent[-2].start()
            sent[-1].start()
        for m, t in ((1, n2), (2, diag)):
            for k in range(n):
                copy(k, m, c, t, here).wait_recv()
                sent.append(copy(k, 3 + m, c, t, sibling))
                sent[-1].start()
        for m, t in ((3, n2), (4, n1), (5, diag)):
            for k in range(n):
                copy(k, m, 1 - c, t, here).wait_recv()
        for cp in sent:
            cp.wait_send()

    return pl.pallas_call(
        body, name="gather_weights", in_specs=[_any_spec()] * (2 * n), out_specs=[_any_spec()] * n,
        out_shape=[jax.ShapeDtypeStruct(a.shape, a.dtype) for a in placed],
        input_output_aliases={n + k: k for k in range(n)},
        scratch_shapes=[pltpu.SemaphoreType.DMA((6 * n,)), pltpu.SemaphoreType.DMA((6 * n,))],
    )(*shards, *placed)


def _swap_layers(gs):
    n = len(gs)

    def body(*refs):
        ins, outs = refs[:n], refs[n:2 * n]
        send_sems, recv_sems = refs[2 * n:]
        x, y, c, _ = _place()
        cps = [_rcopy(ins[k].at[1 - c], outs[k], send_sems, recv_sems, k, (x, y, 1 - c)) for k in range(n)]
        for cp in cps:
            cp.start()
        for cp in cps:
            cp.wait()

    return pl.pallas_call(
        body, name="rs_swap_layers", in_specs=[_any_spec()] * n, out_specs=[_any_spec()] * n,
        out_shape=[jax.ShapeDtypeStruct(g.shape[1:], g.dtype) for g in gs],
        scratch_shapes=[pltpu.SemaphoreType.DMA((n,)), pltpu.SemaphoreType.DMA((n,))],
    )(*gs)


def _scatter_chips(hs, modes, dims):
    n = len(hs)

    def body(*refs):
        ins, outs = refs[:n], refs[n:2 * n]
        send_sems, recv_sems, local_sems = refs[2 * n:]
        x, y, c, chips = _place()
        me = 2 * x + y
        part = lambda k, t: _shard_ref(ins[k], modes[k], t, *dims[k])
        own = [pltpu.make_async_copy(part(k, me), outs[k].at[me], local_sems.at[k]) for k in range(n)]
        for cp in own:
            cp.start()
        cps = [_rcopy(part(k, 2 * px + py), outs[k].at[me], send_sems, recv_sems, 3 * k + j, (px, py, c))
               for j, (px, py) in enumerate(chips) for k in range(n)]
        for cp in cps:
            cp.start()
        for j, (px, py) in enumerate(chips):
            for k in range(n):
                blk = outs[k].at[2 * px + py]
                _rcopy(blk, blk, send_sems, recv_sems, 3 * k + j, (x, y, c)).wait_recv()
        for cp in cps:
            cp.wait_send()
        for cp in own:
            cp.wait()

    return pl.pallas_call(
        body, name="rs_scatter_chips", in_specs=[_any_spec()] * n, out_specs=[_any_spec()] * n,
        out_shape=[jax.ShapeDtypeStruct((4,) + tuple(dims[k]), hs[k].dtype) for k in range(n)],
        scratch_shapes=[pltpu.SemaphoreType.DMA((3 * n,)), pltpu.SemaphoreType.DMA((3 * n,)),
                        pltpu.SemaphoreType.DMA((n,))],
    )(*hs)


def _join_layers(fs):
    n = len(fs)

    def body(*refs):
        outs = refs[n:2 * n]
        send_sems, recv_sems = refs[2 * n:]
        x, y, c, _ = _place()
        cps = [_rcopy(outs[k].at[c], outs[k].at[c], send_sems, recv_sems, k, (x, y, 1 - c)) for k in range(n)]
        for cp in cps:
            cp.start()
        for k in range(n):
            blk = outs[k].at[1 - c]
            _rcopy(blk, blk, send_sems, recv_sems, k, (x, y, c)).wait_recv()
        for cp in cps:
            cp.wait_send()

    return pl.pallas_call(
        body, name="rs_join_layers", in_specs=[_any_spec()] * n, out_specs=[_any_spec()] * n,
        out_shape=[jax.ShapeDtypeStruct(f.shape, f.dtype) for f in fs],
        input_output_aliases={k: k for k in range(n)},
        scratch_shapes=[pltpu.SemaphoreType.DMA((n,)), pltpu.SemaphoreType.DMA((n,))],
    )(*fs)


def _ew_rows(M, N):
    fit = [t for t in (512, 256, 128, 64, 32, 16) if M % t == 0 and t * N * 4 <= EW_BLOCK_BYTES]
    return fit[0] if fit else M


def _add_own(g, other, c1, out_dtype, name):
    _, M, N = g.shape
    tr = _ew_rows(M, N)

    def body(c_ref, g_ref, o_ref, out_ref):
        out_ref[...] = (g_ref[...] + o_ref[...]).astype(out_ref.dtype)

    return pl.pallas_call(
        body, name=name,
        grid_spec=pltpu.PrefetchScalarGridSpec(
            num_scalar_prefetch=1, grid=(M // tr,),
            in_specs=[pl.BlockSpec((None, tr, N), lambda i, cr: (cr[0], i, 0)),
                      pl.BlockSpec((tr, N), lambda i, cr: (i, 0))],
            out_specs=pl.BlockSpec((tr, N), lambda i, cr: (i, 0))),
        out_shape=jax.ShapeDtypeStruct((M, N), out_dtype),
        compiler_params=_params(("parallel",)),
    )(c1, g, other)


def _sum_chips(q, c1, name):
    _, M, N = q.shape
    tr = _ew_rows(M, N)

    def body(c_ref, q_ref, out_ref):
        out_ref[...] = ((q_ref[0].astype(F32) + q_ref[1].astype(F32)) + q_ref[2].astype(F32)) + q_ref[3].astype(F32)

    return pl.pallas_call(
        body, name=name,
        grid_spec=pltpu.PrefetchScalarGridSpec(
            num_scalar_prefetch=1, grid=(M // tr,),
            in_specs=[pl.BlockSpec((4, tr, N), lambda i, cr: (0, i, 0))],
            out_specs=pl.BlockSpec((None, tr, N), lambda i, cr: (cr[0], i, 0))),
        out_shape=jax.ShapeDtypeStruct((DEPTH, M, N), F32),
        compiler_params=_params(("parallel",)),
    )(c1, q)


def _reduce_scatter(gs, modes, dims, wire):
    c1 = jnp.reshape(lax.axis_index("c"), (1,)).astype(jnp.int32)
    flat = lambda a, lead: a.reshape(a.shape[:lead] + (-1, a.shape[-1]))
    others = _swap_layers(gs)
    hs = [_add_own(flat(g, 1), flat(o, 0), c1, wire[k], f"rs_add_own_{k}").reshape(o.shape)
          for k, (g, o) in enumerate(zip(gs, others))]
    qs = _scatter_chips(hs, modes, dims)
    fs = [_sum_chips(q, c1, f"rs_sum_chips_{k}") for k, q in enumerate(qs)]
    return _join_layers(fs)


def _allreduce_small(v):
    R, C = v.shape

    def body(v_ref, sum_ref, all_ref, send_sems, recv_sems):
        x, y, c, _ = _place()
        me = 4 * x + 2 * y + c
        rows = lambda d: all_ref.at[pl.ds(pl.multiple_of(d * R, 8), R), :]

        def peer(k):
            flip = lambda bit, v: (1 - v) if ((k + 1) >> bit) & 1 else v
            return flip(2, x), flip(1, y), flip(0, c)

        outs = [_rcopy(v_ref, rows(me), send_sems, recv_sems, k, peer(k)) for k in range(7)]
        for cp in outs:
            cp.start()
        all_ref[pl.ds(pl.multiple_of(me * R, 8), R), :] = v_ref[...]
        for k in range(7):
            px, py, pc = peer(k)
            blk = rows(4 * px + 2 * py + pc)
            _rcopy(blk, blk, send_sems, recv_sems, k, (x, y, c)).wait_recv()
        for cp in outs:
            cp.wait_send()
        tot = all_ref[0:R, :]
        for d in range(1, 8):
            tot = tot + all_ref[d * R:(d + 1) * R, :]
        sum_ref[...] = tot

    vm = pl.BlockSpec(memory_space=pltpu.VMEM)
    return pl.pallas_call(
        body, name="allreduce_small", in_specs=[vm], out_specs=[vm, vm],
        out_shape=[jax.ShapeDtypeStruct((R, C), F32), jax.ShapeDtypeStruct((8 * R, C), F32)],
        scratch_shapes=[pltpu.SemaphoreType.DMA((7,)), pltpu.SemaphoreType.DMA((7,))],
    )(v)[0]


def _size(shape):
    n = 1
    for d in shape:
        n *= d
    return n


def _pack(pieces, dtype):
    flat = jnp.concatenate([p.astype(dtype).reshape(-1) for p in pieces])
    rows = -(-flat.shape[0] // (PACK_COLS * 16)) * 16
    return jnp.pad(flat, (0, rows * PACK_COLS - flat.shape[0])).reshape(rows, PACK_COLS)


def _unpack(buf, shapes):
    flat = buf.reshape(-1)
    out, pos = [], 0
    for s in shapes:
        n = _size(s)
        out.append(flat[pos:pos + n].reshape(s))
        pos += n
    return out


def _small_piece(name, arr, l):
    if name == "meta_tokens":
        return arr[l * (N_META // DEPTH):(l + 1) * (N_META // DEPTH)]
    return arr[l]


def _prep_w_in(w_in4):
    w_in = jnp.concatenate([w_in4[t] for t in range(4)], axis=1)
    col = lambda a, n: w_in[:, _R[a]:_R[a] + n]
    main = jnp.concatenate([col("qa", 3072), col("scb", 3072), col("qc", 3072), col("ga", 6144)], axis=1)
    zpad = lambda n: jnp.zeros((D_MODEL, n), w_in.dtype)
    side = jnp.concatenate([col("fa", 8), zpad(LANES - 8), col("glr", GLA_RANK), zpad(LANES - GLA_RANK)], axis=1)
    return main.astype(BF16), side.astype(BF16)


def _w_in_cols(dmain, dside, lo, hi):
    segs = ((0, _R["fa"], dmain, 0), (_R["fa"], _R["scb"], dside, 0), (_R["scb"], _R["glr"], dmain, SCB),
            (_R["glr"], _R["ga"], dside, LANES), (_R["ga"], N_IN, dmain, GA))
    parts = [src[..., off + max(a, lo) - a:off + min(b, hi) - a] for a, b, src, off in segs if max(a, lo) < min(b, hi)]
    return jnp.concatenate(parts, axis=-1)


def _pad_rows(a, rows):
    return jnp.pad(a.astype(F32), ((0, rows - a.shape[0]), (0, 0)))


def _row2(v):
    return v.reshape(1, -1).astype(F32)


def _layer_fwd(h, p, rep, l, Lp, tm, ta):
    tag = lambda s: f"{s}_l{l}"
    g1, g2 = _row2(rep["norm1_g"][l]), _row2(rep["norm2_g"][l])
    bf = jnp.pad(_row2(rep["fox_b_f"][l]), ((0, 0), (0, LANES - FOX_HEADS)))
    gate_b, b_g, gnorm = _row2(rep["gate_b"][l]), _row2(rep["gla_b_g"][l]), _row2(rep["gla_norm_g"][l])
    xn, xn_t = _rms_fwd(h, g1, tag("rms1_fwd"), Lp)
    main = _mm(xn, p["main"][l], "nn", BF16, tag("proj_main"))
    side = _mm(xn, p["side"][l], "nn", F32, tag("proj_side"))
    c = _fox_gate_fwd(side, bf, Lp)
    c_t = c[:, :FOX_HEADS].T
    c_col, c_row = c_t[:, :, None], c_t[:, None, :]
    oa, ox, lse = _fox_fwd(main, c_col, c_row, Lp, ta)
    ya = _mm(oa, p["w_a_o"], "nn", BF16, tag("ya"), b_lead=l)
    ub = _sconv_fwd(main, p["conv_w"][l], Lp, tm)
    yb = _mm(ub, p["w_b_o"], "nn", BF16, tag("yb"), b_lead=l)
    glr = Row(side, LANES, lambda g: 1)
    (logg,) = _rw_fwd(tag("logg_fwd"), _f_logg, [glr], [Const(p["w_g2"][l]), Const(b_g)], [(512, F32)], Lp, tm)
    oc, states = _gla_fwd(main, logg, Lp)
    rc = Row(main, GLA_DV, lambda g: RC // GLA_DV + g)
    gn = Const(gnorm, (1, GLA_DV), lambda g: (0, g))
    (uc,) = _rw_fwd(tag("gla_post_fwd"), _f_gla_post, [Row(oc, GLA_DV), rc], [gn], [(GLA_DV, BF16)], Lp, tm,
                    G=GLA_HEADS)
    yc = _mm(uc, p["w_c_o"], "nn", BF16, tag("yc"), b_lead=l)
    cw = 512
    G = D_MODEL // cw
    mrows = [Row(ya, cw), Row(yb, cw), Row(yc, cw), Row(main, cw, lambda g: GA // cw + g),
             Row(main, cw, lambda g: GB // cw + g), Row(main, cw, lambda g: GC // cw + g)]
    mconsts = [Const(gate_b, (1, cw), lambda g, k=k: (0, k * G + g)) for k in range(3)]
    (mix,) = _rw_fwd(tag("merge_fwd"), _f_merge, mrows, mconsts, [(cw, BF16)], Lp, tm, G=G)
    h1 = _mm(mix, p["w_o"], "nn", F32, tag("h1"), add=h, b_lead=l)
    xn2, xn2_t = _rms_fwd(h1, g2, tag("rms2_fwd"), Lp)
    up = _mm(xn2, p["w_up"], "nn", BF16, tag("up"), b_lead=l)
    act, act_t = _mlp_act_fwd(up, p["mlp_conv_w"][l], Lp, tm)
    h2 = _mm(act, p["w_down"], "nn", F32, tag("h2"), add=h1, b_lead=l)
    res = dict(h=h, xn=xn, xn_t=xn_t, xn2_t=xn2_t, act_t=act_t, main=main, side=side, c_col=c_col, c_row=c_row, oa=oa, ox=ox, lse=lse, ya=ya, ub=ub, yb=yb,
               logg=logg, oc=oc, states=states, uc=uc, yc=yc, mix=mix, h1=h1, xn2=xn2, up=up, act=act,
               g1=g1, g2=g2, bf=bf, gate_b=gate_b, b_g=b_g, gnorm=gnorm)
    return h2, res


def _layer_bwd(dh2, p, r, l, Lp, tm, ta, big):
    tag = lambda s: f"{s}_l{l}"
    g = {}

    def wgrad(name, a, b, mode="tn"):
        big[name] = _mm(a, b, mode, F32, tag("d_" + name), slot=(big.get(name), l))

    wgrad("w_down", r["act_t"], dh2, "nn")
    dact = _mm(dh2, p["w_down"], "nt", BF16, tag("d_act"), b_lead=l)
    dgate, dval, dwg, dwu = _mlp_act_bwd(r["up"], p["mlp_conv_w"][l], dact, Lp, tm)
    g["mlp_conv_w"] = jnp.concatenate([dwg[:3], dwu[:3]], axis=1)
    dup = jnp.concatenate([dgate, dval], axis=1)
    wgrad("w_up", r["xn2_t"], dup, "nn")
    dxn2 = _mm(dup, p["w_up"], "nt", F32, tag("d_xn2"), b_lead=l)
    (dh1,), (dg2,) = _rw_bwd(tag("rms2_bwd"), _f_rms, [Row(r["h1"], D_MODEL)], [Const(r["g2"])],
                             [Row(dxn2, D_MODEL)], [F32], [dh2], Lp, BLOCK)
    g["norm2_g"] = dg2[0]
    wgrad("w_o", r["mix"], dh1)
    dmix = _mm(dh1, p["w_o"], "nt", BF16, tag("d_mix"), b_lead=l)
    cw = 512
    G = D_MODEL // cw
    main = r["main"]
    mrows = [Row(r["ya"], cw), Row(r["yb"], cw), Row(r["yc"], cw), Row(main, cw, lambda g_: GA // cw + g_),
             Row(main, cw, lambda g_: GB // cw + g_), Row(main, cw, lambda g_: GC // cw + g_)]
    mconsts = [Const(r["gate_b"], (1, cw), lambda g_, k=k: (0, k * G + g_)) for k in range(3)]
    (dya, dyb, dyc, dga, dgb, dgc), dbs = _rw_bwd(tag("merge_bwd"), _f_merge, mrows, mconsts, [Row(dmix, cw)],
                                                  [BF16] * 6, [None] * 6, Lp, tm, G=G)
    g["gate_b"] = jnp.concatenate([dbs[k][0, k * D_MODEL:(k + 1) * D_MODEL] for k in range(3)])
    wgrad("w_a_o", r["oa"], dya)
    doa = _mm(dya, p["w_a_o"], "nt", BF16, tag("d_oa"), b_lead=l)
    wgrad("w_b_o", r["ub"], dyb)
    dub = _mm(dyb, p["w_b_o"], "nt", BF16, tag("d_ub"), b_lead=l)
    wgrad("w_c_o", r["uc"], dyc)
    duc = _mm(dyc, p["w_c_o"], "nt", BF16, tag("d_uc"), b_lead=l)
    delta = _fox_delta(r["ox"], doa, Lp, ta)
    dq, dk, dv, dck = _fox_bwd(main, r["c_col"], r["c_row"], r["lse"], delta, doa, Lp, ta)
    dc = jnp.pad(dck[:, 0, :].T, ((0, 0), (0, LANES - FOX_HEADS)))
    dfa, dbf = _fox_gate_bwd(r["side"], r["bf"], dc, Lp)
    g["fox_b_f"] = dbf[0, :FOX_HEADS]
    dscb, dscc, dsch, dcw = _sconv_bwd(main, p["conv_w"][l], dub, Lp, tm)
    g["conv_w"] = dcw[:3]
    rc = Row(main, GLA_DV, lambda g_: RC // GLA_DV + g_)
    gn = Const(r["gnorm"], (1, GLA_DV), lambda g_: (0, g_))
    (doc, drc), (dgn,) = _rw_bwd(tag("gla_post_bwd"), _f_gla_post, [Row(r["oc"], GLA_DV), rc], [gn],
                                 [Row(duc, GLA_DV)], [F32, BF16], [None, None], Lp, tm, G=GLA_HEADS)
    g["gla_norm_g"] = dgn[0]
    dqc, dkc, dvc, dlogg = _gla_bwd(main, r["logg"], r["states"], doc, Lp)
    glr = Row(r["side"], LANES, lambda g_: 1)
    (dglr,), (dwg2, dbg) = _rw_bwd(tag("logg_bwd"), _f_logg, [glr], [Const(p["w_g2"][l]), Const(r["b_g"])],
                                   [Row(dlogg, 512)], [F32], [None], Lp, tm)
    g["gla_w_g2"] = dwg2[:GLA_RANK]
    g["gla_b_g"] = dbg[0]
    dmain = jnp.concatenate([dq, dk, dv, dscb, dscc, dsch, dqc, dkc, dvc, drc, dga, dgb, dgc], axis=1)
    dside = jnp.concatenate([dfa, dglr], axis=1)
    wgrad("main", r["xn_t"], dmain, "nn")
    wgrad("side", r["xn"], dside)
    dxn = _mm(dmain, p["main"][l], "nt", F32, tag("d_xn_main"))
    dxn = _mm(dside, p["side"][l], "nt", F32, tag("d_xn_side"), add=dxn)
    (dh,), (dg1,) = _rw_bwd(tag("rms1_bwd"), _f_rms, [Row(r["h"], D_MODEL)], [Const(r["g1"])], [Row(dxn, D_MODEL)],
                            [F32], [dh1], Lp, BLOCK)
    g["norm1_g"] = dg1[0]
    return dh, g


def _local_step(x, target, meta, p, rep):
    seq = x.shape[0]
    Lp = PAD + N_META + seq
    tm = _pick(Lp, (640, 384, 128))
    ta = tm
    h = jnp.concatenate([jnp.zeros((PAD, D_MODEL), F32), meta.astype(F32), x], axis=0)
    saved = []
    for l in range(DEPTH):
        h, res = _layer_fwd(h, p, rep, l, Lp, tm, ta)
        saved.append(res)
    loss, dh, dgf = _loss_head(h, _row2(rep["final_norm_g"]), target, Lp)
    big, small = {}, [None] * DEPTH
    for l in reversed(range(DEPTH)):
        dh, small[l] = _layer_bwd(dh, p, saved[l], l, Lp, tm, ta, big)
    return loss[0, 0], dh[BLOCK:], dh[PAD:BLOCK], big, small, dgf[0]


def kernel(x, meta_tokens, norm1_g, w_in, fox_b_f, gate_b, conv_w, gla_w_g2, gla_b_g, gla_norm_g, w_a_o, w_b_o, w_c_o, w_o, norm2_g, w_up, mlp_conv_w, w_down, final_norm_g, loss_target, m_meta_tokens, m_norm1_g, m_w_in, m_fox_b_f, m_gate_b, m_conv_w, m_gla_w_g2, m_gla_b_g, m_gla_norm_g, m_w_a_o, m_w_b_o, m_w_c_o, m_w_o, m_norm2_g, m_w_up, m_mlp_conv_w, m_w_down, m_final_norm_g, v_meta_tokens, v_norm1_g, v_w_in, v_fox_b_f, v_gate_b, v_conv_w, v_gla_w_g2, v_gla_b_g, v_gla_norm_g, v_w_a_o, v_w_b_o, v_w_c_o, v_w_o, v_norm2_g, v_w_up, v_mlp_conv_w, v_w_down, v_final_norm_g):
    given = dict(locals())
    weights = {n: given[n] for n in WEIGHT_ORDER}
    rep = {n: weights[n] for n, _ in REPLICATED}
    big_names = [n for n, _ in BIG]
    big_modes = [m for _, m in BIG] + ["stack"]
    small_shapes = [(s[0], s[1] // 4) for _, s in SMALL]
    exact = [k for k, (n, _) in enumerate(SMALL) if n in GATHER_F32]

    def small_wire(l):
        ws = [_small_piece(n, weights[n], l) for n, _ in SMALL]
        his = [w.astype(BF16) for w in ws]
        return his + [(ws[k] - his[k].astype(F32)).astype(BF16) for k in exact]

    shards = [weights[n].astype(BF16) for n in big_names] + [jnp.stack([_pack(small_wire(l), BF16) for l in range(DEPTH)])]
    gathered = _gather_weights(shards, big_modes)
    gw = dict(zip(big_names, gathered[:-1]))
    p = {n: gw[n] for n in big_names if n != "w_in"}
    p["main"], p["side"] = zip(*[_prep_w_in(gw["w_in"][l]) for l in range(DEPTH)])
    small_full = []
    for l in range(DEPTH):
        per_chip = [_unpack(gathered[-1][l, t], small_shapes + [small_shapes[k] for k in exact]) for t in range(4)]
        full = [jnp.concatenate([per_chip[t][k] for t in range(4)], axis=1).astype(F32) for k in range(len(per_chip[0]))]
        for e, k in enumerate(exact):
            full[k] = full[k] + full[len(SMALL) + e]
        small_full.append(dict(zip([n for n, _ in SMALL], full[:len(SMALL)])))
    p["conv_w"] = [_pad_rows(s["conv_w"], 8) for s in small_full]
    p["mlp_conv_w"] = [_pad_rows(s["mlp_conv_w"], 8) for s in small_full]
    p["w_g2"] = [_pad_rows(s["gla_w_g2"], LANES) for s in small_full]
    meta_full = jnp.concatenate([s["meta_tokens"] for s in small_full], axis=0)

    loss, grad_x, grad_meta, big, small, d_final = _local_step(x[0], loss_target[0], meta_full, p, rep)
    loss = lax.psum(loss, ("x", "y", "c"))

    big["w_in"] = jnp.stack([_w_in_cols(big["main"], big["side"], t * (N_IN // 4), (t + 1) * (N_IN // 4))
                             for t in range(4)], axis=1)
    for l in range(DEPTH):
        small[l]["meta_tokens"] = grad_meta[l * (N_META // DEPTH):(l + 1) * (N_META // DEPTH)]
    shard_of = lambda a, t: lax.slice_in_dim(a, t * (a.shape[1] // 4), (t + 1) * (a.shape[1] // 4), axis=1)
    small_g = jnp.stack([jnp.stack([_pack([shard_of(small[l][n], t) for n, _ in SMALL], F32) for t in range(4)])
                         for l in range(DEPTH)])
    dims = [shards[k].shape[1:] for k in range(len(BIG))] + [small_g.shape[2:]]
    summed = _reduce_scatter([big[n] for n in big_names] + [small_g], big_modes, dims,
                             [BF16] * len(BIG) + [F32])
    gout = dict(zip(big_names, summed[:-1]))
    pieces = [_unpack(summed[-1][l], small_shapes) for l in range(DEPTH)]
    for k, (n, _) in enumerate(SMALL):
        per_layer = [pieces[l][k] for l in range(DEPTH)]
        gout[n] = jnp.concatenate(per_layer, axis=0) if n == "meta_tokens" else jnp.stack(per_layer)

    rep_g = {n: (d_final if n == "final_norm_g" else jnp.stack([small[l][n] for l in range(DEPTH)])) for n, _ in REPLICATED}
    flat = jnp.concatenate([rep_g[n].astype(F32).reshape(-1) for n, _ in REPLICATED])
    rrows = -(-flat.shape[0] // (PACK_COLS * 8)) * 8
    summed_small = _allreduce_small(jnp.pad(flat, (0, rrows * PACK_COLS - flat.shape[0])).reshape(rrows, PACK_COLS))
    pos = 0
    for n, shape in REPLICATED:
        gout[n] = summed_small.reshape(-1)[pos:pos + _size(shape)].reshape(shape)
        pos += _size(shape)

    deltas, new_m, new_v = {}, {}, {}
    for n in WEIGHT_ORDER:
        gout[n], deltas[n], new_m[n], new_v[n] = _adamw(weights[n], gout[n], given["m_" + n], given["v_" + n],
                                                        "adamw_" + n)
    return (loss, grad_x[None], *[gout[n] for n in WEIGHT_ORDER], *[deltas[n] for n in WEIGHT_ORDER],
            *[new_m[n] for n in WEIGHT_ORDER], *[new_v[n] for n in WEIGHT_ORDER])
```

```python
import functools

import jax
import jax.numpy as jnp
from jax import lax
from jax.experimental import pallas as pl
from jax.experimental.pallas import tpu as pltpu

F32, BF16 = jnp.float32, jnp.bfloat16
HIGHEST = lax.Precision.HIGHEST
MESH = pl.DeviceIdType.MESH

N_META = 16
BLOCK = 128
LANES = 128
PAD = BLOCK - N_META
EPS = 1e-6
NEG = -1e30
HALO = 16
VMEM_LIMIT = 56 * 1024 * 1024
ADAM_BLOCK_BYTES = 1 << 20
EW_BLOCK_BYTES = 3 << 19

D_MODEL = 2048
FOX_HEADS, FOX_HD = 8, 128
FOX_WIDTH = FOX_HEADS * FOX_HD
CONV_CH = 1024
GLA_HEADS, GLA_DK, GLA_DV, GLA_RANK, GLA_TAU = 4, 128, 256, 16, 16.0
GLA_SUB = 32
D_FF = 5632
N_IN = 15384
DEPTH = 2

_R = dict(qa=0, ka=1024, va=2048, fa=3072, scb=3080, scc=4104, sch=5128, qc=6152, kc=6664,
          vc=7176, rc=8200, glr=9224, ga=9240, gb=11288, gc=13336)
QA, KA, VA, SCB, SCC, SCH, QC, KC, VC, RC, GA, GB, GC = (
    0, 1024, 2048, 3072, 4096, 5120, 6144, 6656, 7168, 8192, 9216, 11264, 13312)
N_MAIN = 15360
N_SIDE = 256

ADAM_LR, ADAM_B1, ADAM_B2, ADAM_EPS, ADAM_WD, ADAM_STEP = 0.001, 0.9, 0.999, 1e-08, 0.01, 10

BIG = (("w_in", "stack"), ("w_a_o", "cols"), ("w_b_o", "cols"), ("w_c_o", "cols"), ("w_o", "rows"), ("w_up", "cols"),
       ("w_down", "rows"))
SMALL = (("conv_w", (3, CONV_CH)), ("mlp_conv_w", (3, 2 * D_FF)), ("gla_w_g2", (GLA_RANK, 512)),
         ("meta_tokens", (N_META // DEPTH, D_MODEL)))
REPLICATED = (("norm1_g", (2, D_MODEL)), ("fox_b_f", (2, 8)), ("gate_b", (2, 3 * D_MODEL)), ("gla_b_g", (2, 512)),
              ("gla_norm_g", (2, 1024)), ("norm2_g", (2, D_MODEL)), ("final_norm_g", (D_MODEL,)))
WEIGHT_ORDER = ("meta_tokens", "norm1_g", "w_in", "fox_b_f", "gate_b", "conv_w", "gla_w_g2", "gla_b_g",
                "gla_norm_g", "w_a_o", "w_b_o", "w_c_o", "w_o", "norm2_g", "w_up", "mlp_conv_w", "w_down",
                "final_norm_g")
PACK_COLS = 1024
GATHER_F32 = ("conv_w", "mlp_conv_w", "meta_tokens")


def _pick(n, cands):
    for c in cands:
        if n % c == 0:
            return c
    return n


def _params(sem):
    return pltpu.CompilerParams(dimension_semantics=sem, vmem_limit_bytes=VMEM_LIMIT)


def _sigmoid(x):
    return jax.nn.sigmoid(x)


def _log_sigmoid(x):
    return jnp.minimum(x, 0.0) - jnp.log(1.0 + jnp.exp(-jnp.abs(x)))


def _mm(a, b, mode, out_dtype, name, add=None, b_lead=None, slot=None):
    bshape = b.shape if b_lead is None else b.shape[1:]
    if mode == "nn":
        (M, K), (K2, N) = a.shape, bshape
    elif mode == "nt":
        (M, K), (N, K2) = a.shape, bshape
    else:
        (K, M), (K2, N) = a.shape, bshape
    assert K == K2, (name, a.shape, b.shape)
    if mode == "tn":
        tm = _pick(M, (2048, 1408, 1024, 512, 256, 128))
        tn = _pick(N, (1024, 512, 256, 128))
        tk = _pick(K, (640, 512, 384, 256, 128))
    else:
        tm = _pick(M, (1664, 2048, 1408, 1024, 640, 384, 128))
        tn = _pick(N, (512, 256, 128))
        tk = K if K <= 2048 else _pick(K, (1664, 1408, 1024, 640, 512, 384, 256, 128))
    nk = K // tk
    dims = {"nn": (((1,), (0,)), ((), ())), "nt": (((1,), (1,)), ((), ())), "tn": (((0,), (0,)), ((), ()))}[mode]
    n_in = 2 + (add is not None) + (2 if slot is not None and slot[0] is not None else 0)

    def body(*refs):
        a_ref, b_ref = refs[:2]
        add_ref = refs[2] if add is not None else None
        o_ref, acc = refs[n_in], refs[-1]
        o16_ref = refs[n_in + 1] if slot is not None else None
        k = pl.program_id(2)

        @pl.when(k == 0)
        def _():
            acc[...] = jnp.zeros_like(acc)

        acc[...] += lax.dot_general(a_ref[...].astype(BF16), b_ref[...].astype(BF16), dims,
                                    preferred_element_type=F32)

        @pl.when(k == nk - 1)
        def _():
            r = acc[...]
            if add is not None:
                r = r + add_ref[...].astype(F32)
            o_ref[...] = r.astype(o_ref.dtype)
            if o16_ref is not None:
                o16_ref[...] = r.astype(BF16)

    a_spec = {"nn": pl.BlockSpec((tm, tk), lambda i, j, k: (i, k)),
              "nt": pl.BlockSpec((tm, tk), lambda i, j, k: (i, k)),
              "tn": pl.BlockSpec((tk, tm), lambda i, j, k: (k, i))}[mode]
    b_blk, b_idx = {"nn": ((tk, tn), lambda i, j, k: (k, j)),
                    "nt": ((tn, tk), lambda i, j, k: (j, k)),
                    "tn": ((tk, tn), lambda i, j, k: (k, j))}[mode]
    if b_lead is None:
        b_spec = pl.BlockSpec(b_blk, b_idx)
    else:
        b_spec = pl.BlockSpec((None,) + b_blk, lambda i, j, k: (b_lead,) + b_idx(i, j, k))
    o_spec = pl.BlockSpec((tm, tn), lambda i, j, k: (i, j))
    ins, specs = [a, b], [a_spec, b_spec]
    if add is not None:
        ins.append(add)
        specs.append(o_spec)
    aliases = {}
    out_shape = jax.ShapeDtypeStruct((M, N), out_dtype)
    if slot is not None:
        bufs, l = slot
        o_spec = [pl.BlockSpec((None, tm, tn), lambda i, j, k: (l, i, j))] * 2
        out_shape = [jax.ShapeDtypeStruct((DEPTH, M, N), out_dtype), jax.ShapeDtypeStruct((DEPTH, M, N), BF16)]
        if bufs is not None:
            aliases = {len(ins): 0, len(ins) + 1: 1}
            ins.extend(bufs)
            specs.extend([pl.BlockSpec(memory_space=pl.ANY)] * 2)
    return pl.pallas_call(
        body, name=name, grid=(M // tm, N // tn, nk), in_specs=specs, out_specs=o_spec, out_shape=out_shape,
        scratch_shapes=[pltpu.VMEM((tm, tn), F32)], input_output_aliases=aliases,
        compiler_params=_params(("parallel", "parallel", "arbitrary")),
    )(*ins)


class Row:
    def __init__(self, arr, w, cb=None):
        self.arr, self.w, self.cb = arr, w, (cb if cb is not None else (lambda g: g))


class Const:
    def __init__(self, arr, shape=None, idx=None):
        self.arr = arr
        self.shape = shape if shape is not None else arr.shape
        self.idx = idx if idx is not None else (lambda g: (0,) * arr.ndim)


def _row_spec(r, tm):
    return pl.BlockSpec((tm, r.w), lambda g, i, r=r: (i, r.cb(g)))


def _const_spec(c):
    return pl.BlockSpec(c.shape, lambda g, i, c=c: c.idx(g))


def _valid_rows(i, tm):
    return (i * tm + lax.broadcasted_iota(jnp.int32, (tm, 1), 0)) >= PAD


def _rw_fwd(name, f, rows, consts, outs, Lp, tm, G=1):
    nr, nc = len(rows), len(consts)

    def body(*refs):
        i = pl.program_id(1)
        rv = [r[...].astype(F32) for r in refs[:nr]]
        cv = [r[...].astype(F32) for r in refs[nr:nr + nc]]
        res = f(_valid_rows(i, tm), *rv, *cv)
        for o_ref, v in zip(refs[nr + nc:], res):
            o_ref[...] = v.astype(o_ref.dtype)

    return pl.pallas_call(
        body, name=name, grid=(G, Lp // tm),
        in_specs=[_row_spec(r, tm) for r in rows] + [_const_spec(c) for c in consts],
        out_specs=[pl.BlockSpec((tm, w), lambda g, i: (i, g)) for w, _ in outs],
        out_shape=[jax.ShapeDtypeStruct((Lp, w * G), dt) for w, dt in outs],
        compiler_params=_params(("parallel", "arbitrary")),
    )(*[r.arr for r in rows], *[c.arr for c in consts])


def _rw_bwd(name, f, rows, consts, cts, drow_dtypes, adds, Lp, tm, G=1):
    nr, nc, nt = len(rows), len(consts), len(cts)
    want = [k for k, dt in enumerate(drow_dtypes) if dt is not None]
    add_k = [k for k in want if adds[k] is not None]

    def body(*refs):
        i = pl.program_id(1)
        pos = 0
        rv = [r[...].astype(F32) for r in refs[pos:pos + nr]]
        pos += nr
        cv = [r[...].astype(F32) for r in refs[pos:pos + nc]]
        pos += nc
        tv = [r[...].astype(F32) for r in refs[pos:pos + nt]]
        pos += nt
        av = {k: refs[pos + n][...].astype(F32) for n, k in enumerate(add_k)}
        pos += len(add_k)
        drow_refs = refs[pos:pos + len(want)]
        pos += len(want)
        dconst_refs = refs[pos:pos + nc]
        valid = _valid_rows(i, tm)
        _, vjp = jax.vjp(lambda *a: tuple(f(valid, *a)), *rv, *cv)
        grads = vjp(tuple(tv))
        for o_ref, k in zip(drow_refs, want):
            gk = grads[k]
            if k in av:
                gk = gk + av[k]
            o_ref[...] = gk.astype(o_ref.dtype)
        for n, o_ref in enumerate(dconst_refs):
            gc = grads[nr + n]

            @pl.when(i == 0)
            def _(o_ref=o_ref, gc=gc):
                o_ref[...] = gc

            @pl.when(i > 0)
            def _(o_ref=o_ref, gc=gc):
                o_ref[...] += gc

    out_row = lambda w: pl.BlockSpec((tm, w), lambda g, i: (i, g))
    res = pl.pallas_call(
        body, name=name, grid=(G, Lp // tm),
        in_specs=([_row_spec(r, tm) for r in rows] + [_const_spec(c) for c in consts]
                  + [_row_spec(r, tm) for r in cts] + [out_row(rows[k].w) for k in add_k]),
        out_specs=[out_row(rows[k].w) for k in want] + [_const_spec(c) for c in consts],
        out_shape=([jax.ShapeDtypeStruct((Lp, rows[k].w * G), drow_dtypes[k]) for k in want]
                   + [jax.ShapeDtypeStruct(c.arr.shape, F32) for c in consts]),
        compiler_params=_params(("parallel", "arbitrary")),
    )(*[r.arr for r in rows], *[c.arr for c in consts], *[r.arr for r in cts], *[adds[k] for k in add_k])
    drows = [None] * nr
    for n, k in enumerate(want):
        drows[k] = res[n]
    return drows, list(res[len(want):])


def _f_rms(valid, h, g):
    r = lax.rsqrt(jnp.mean(h * h, axis=-1, keepdims=True) + EPS)
    return (jnp.where(valid, h * r * g, 0.0),)


def _rms_fwd(h, g, name, Lp):
    t = BLOCK

    def body(h_ref, g_ref, o_ref, ot_ref):
        (y,) = _f_rms(_valid_rows(pl.program_id(0), t), h_ref[...], g_ref[...])
        o_ref[...] = y.astype(o_ref.dtype)
        ot_ref[...] = y.T.astype(ot_ref.dtype)

    return pl.pallas_call(
        body, name=name, grid=(Lp // t,),
        in_specs=[pl.BlockSpec((t, D_MODEL), lambda i: (i, 0)), pl.BlockSpec((1, D_MODEL), lambda i: (0, 0))],
        out_specs=[pl.BlockSpec((t, D_MODEL), lambda i: (i, 0)), pl.BlockSpec((D_MODEL, t), lambda i: (0, i))],
        out_shape=[jax.ShapeDtypeStruct((Lp, D_MODEL), BF16), jax.ShapeDtypeStruct((D_MODEL, Lp), BF16)],
        compiler_params=_params(("parallel",)),
    )(h, g)


def _f_logg(valid, glr, w, b):
    pre = jnp.dot(glr.astype(BF16), w.astype(BF16), preferred_element_type=F32) + b
    return (jnp.where(valid, _log_sigmoid(pre) / GLA_TAU, 0.0),)


def _f_gla_post(valid, oc, rc, g):
    y = oc * lax.rsqrt(jnp.mean(oc * oc, axis=-1, keepdims=True) + EPS) * g
    return (jnp.where(valid, rc * _sigmoid(rc) * y, 0.0),)


def _f_merge(valid, ya, yb, yc, ga, gb, gc, ba, bb, bc):
    mix = _sigmoid(ga + ba) * ya + _sigmoid(gb + bb) * yb + _sigmoid(gc + bc) * yc
    return (jnp.where(valid, mix, 0.0),)


def _fox_gate_fwd(side, bf, Lp):
    t = BLOCK
    n = Lp // t

    def body(s_ref, b_ref, c_ref, carry):
        i = pl.program_id(0)

        @pl.when(i == 0)
        def _():
            carry[...] = jnp.zeros_like(carry)

        lane = lax.broadcasted_iota(jnp.int32, (t, LANES), 1)
        ok = _valid_rows(i, t) & (lane < FOX_HEADS)
        logf = jnp.where(ok, _log_sigmoid(s_ref[...] + b_ref[...]), 0.0)
        tril = (lax.broadcasted_iota(jnp.int32, (t, t), 1) <= lax.broadcasted_iota(jnp.int32, (t, t), 0)).astype(F32)
        c = jnp.dot(tril, logf, precision=HIGHEST, preferred_element_type=F32) + carry[...]
        c_ref[...] = c
        carry[...] = c[t - 1:t, :]

    return pl.pallas_call(
        body, name="fox_gate_fwd", grid=(n,),
        in_specs=[pl.BlockSpec((t, LANES), lambda i: (i, 0)), pl.BlockSpec((1, LANES), lambda i: (0, 0))],
        out_specs=pl.BlockSpec((t, LANES), lambda i: (i, 0)),
        out_shape=jax.ShapeDtypeStruct((Lp, LANES), F32),
        scratch_shapes=[pltpu.VMEM((1, LANES), F32)],
        compiler_params=_params(("arbitrary",)),
    )(side, bf)


def _fox_gate_bwd(side, bf, dc, Lp):
    t = BLOCK
    n = Lp // t

    def body(s_ref, b_ref, dc_ref, dfa_ref, db_ref, carry):
        i = pl.program_id(0)

        @pl.when(i == 0)
        def _():
            carry[...] = jnp.zeros_like(carry)

        lane = lax.broadcasted_iota(jnp.int32, (t, LANES), 1)
        ok = _valid_rows(n - 1 - i, t) & (lane < FOX_HEADS)
        triu = (lax.broadcasted_iota(jnp.int32, (t, t), 1) >= lax.broadcasted_iota(jnp.int32, (t, t), 0)).astype(F32)
        dlogf = jnp.dot(triu, dc_ref[...], precision=HIGHEST, preferred_element_type=F32) + carry[...]
        carry[...] = dlogf[0:1, :]
        dpre = jnp.where(ok, dlogf * _sigmoid(-(s_ref[...] + b_ref[...])), 0.0)
        dfa_ref[...] = dpre
        part = jnp.sum(dpre, axis=0, keepdims=True)

        @pl.when(i == 0)
        def _():
            db_ref[...] = part

        @pl.when(i > 0)
        def _():
            db_ref[...] += part

    rev = lambda i: (n - 1 - i, 0)
    return pl.pallas_call(
        body, name="fox_gate_bwd", grid=(n,),
        in_specs=[pl.BlockSpec((t, LANES), rev), pl.BlockSpec((1, LANES), lambda i: (0, 0)),
                  pl.BlockSpec((t, LANES), rev)],
        out_specs=[pl.BlockSpec((t, LANES), rev), pl.BlockSpec((1, LANES), lambda i: (0, 0))],
        out_shape=[jax.ShapeDtypeStruct((Lp, LANES), F32), jax.ShapeDtypeStruct((1, LANES), F32)],
        scratch_shapes=[pltpu.VMEM((1, LANES), F32)],
        compiler_params=_params(("arbitrary",)),
    )(side, bf, dc)


def _fox_key_bias(cq_ref, ck_ref, j, t):
    col = j * t + lax.broadcasted_iota(jnp.int32, (1, t), 1)
    return jnp.where(col >= PAD, ck_ref[...] - cq_ref[0:1, :], -NEG)


def _fox_s(q, k, bias, diagonal, t):
    s = lax.dot_general(q, k, (((1,), (1,)), ((), ())), preferred_element_type=F32) * (FOX_HD ** -0.5) - bias
    if diagonal:
        causal = lax.broadcasted_iota(jnp.int32, (t, t), 1) <= lax.broadcasted_iota(jnp.int32, (t, t), 0)
        s = jnp.where(causal, s, NEG)
    return s


def _fox_fwd(main, c_col, c_row, Lp, t):
    n = Lp // t
    qb, kb, vb = QA // FOX_HD, KA // FOX_HD, VA // FOX_HD

    def body(q_ref, k_ref, v_ref, cq_ref, ck_ref, o_ref, ox_ref, lse_ref, m_s, l_s, acc):
        i, j = pl.program_id(1), pl.program_id(2)

        @pl.when(j == 0)
        def _():
            m_s[...] = jnp.full_like(m_s, NEG)
            l_s[...] = jnp.zeros_like(l_s)
            acc[...] = jnp.zeros_like(acc)

        def update(diagonal):
            s = _fox_s(q_ref[...], k_ref[...], _fox_key_bias(cq_ref, ck_ref, j, t), diagonal, t)
            m_new = jnp.maximum(m_s[...], jnp.max(s, axis=1, keepdims=True))
            alpha = jnp.exp(m_s[...] - m_new)
            p = jnp.exp(s - m_new)
            l_s[...] = alpha * l_s[...] + jnp.sum(p, axis=1, keepdims=True)
            p_hi = p.astype(BF16)
            p_lo = (p - p_hi.astype(F32)).astype(BF16)
            pv = (jnp.dot(p_hi, v_ref[...], preferred_element_type=F32)
                  + jnp.dot(p_lo, v_ref[...], preferred_element_type=F32))
            acc[...] = alpha * acc[...] + pv
            m_s[...] = m_new

        @pl.when(j < i)
        def _():
            update(False)

        @pl.when(j == i)
        def _():
            update(True)
            o = jnp.where(_valid_rows(i, t), acc[...] / l_s[...], 0.0)
            o_ref[...] = o.astype(o_ref.dtype)
            ox_ref[...] = o
            lse_ref[...] = m_s[...] + jnp.log(l_s[...])

    kv = lambda base: pl.BlockSpec((t, FOX_HD), lambda h, i, j: (jnp.minimum(j, i), base + h))
    return pl.pallas_call(
        body, name="fox_fwd", grid=(FOX_HEADS, n, n),
        in_specs=[pl.BlockSpec((t, FOX_HD), lambda h, i, j: (i, qb + h)), kv(kb), kv(vb),
                  pl.BlockSpec((None, t, 1), lambda h, i, j: (h, i, 0)),
                  pl.BlockSpec((None, 1, t), lambda h, i, j: (h, 0, jnp.minimum(j, i)))],
        out_specs=[pl.BlockSpec((t, FOX_HD), lambda h, i, j: (i, h)), pl.BlockSpec((t, FOX_HD), lambda h, i, j: (i, h)),
                   pl.BlockSpec((None, t, 1), lambda h, i, j: (h, i, 0))],
        out_shape=[jax.ShapeDtypeStruct((Lp, FOX_WIDTH), BF16), jax.ShapeDtypeStruct((Lp, FOX_WIDTH), F32),
                   jax.ShapeDtypeStruct((FOX_HEADS, Lp, 1), F32)],
        scratch_shapes=[pltpu.VMEM((t, 1), F32), pltpu.VMEM((t, 1), F32), pltpu.VMEM((t, FOX_HD), F32)],
        compiler_params=_params(("parallel", "parallel", "arbitrary")),
    )(main, main, main, c_col, c_row)


def _fox_p_dp(q_ref, k_ref, v_ref, cq_ref, ck_ref, lse_ref, do_ref, j, diagonal, t):
    s = _fox_s(q_ref[...], k_ref[...], _fox_key_bias(cq_ref, ck_ref, j, t), diagonal, t)
    p = jnp.exp(s - lse_ref[...])
    dp = lax.dot_general(do_ref[...], v_ref[...], (((1,), (1,)), ((), ())), preferred_element_type=F32)
    return p, dp


def _fox_delta(ox, doa, Lp, t):
    n = Lp // t

    def body(o_ref, do_ref, dl_ref):
        dl_ref[...] = jnp.sum(o_ref[...] * do_ref[...].astype(F32), axis=1, keepdims=True)

    blk = pl.BlockSpec((t, FOX_HD), lambda h, i: (i, h))
    return pl.pallas_call(
        body, name="fox_delta", grid=(FOX_HEADS, n), in_specs=[blk, blk],
        out_specs=pl.BlockSpec((None, t, 1), lambda h, i: (h, i, 0)),
        out_shape=jax.ShapeDtypeStruct((FOX_HEADS, Lp, 1), F32),
        compiler_params=_params(("parallel", "parallel")),
    )(ox, doa)


def _fox_bwd(main, c_col, c_row, lse, delta, doa, Lp, t):
    n = Lp // t
    qb, kb, vb = QA // FOX_HD, KA // FOX_HD, VA // FOX_HD
    scale = FOX_HD ** -0.5

    def body(q_ref, k_ref, v_ref, cq_ref, ck_ref, lse_ref, dl_ref, do_ref, dq_ref, dk_ref, dv_ref, dck_ref,
             dq_s, dk_s, dv_s, dc_s):
        j, i = pl.program_id(1), pl.program_id(2)

        @pl.when(jnp.logical_and(j == 0, i == 0))
        def _():
            dq_s[...] = jnp.zeros_like(dq_s)

        @pl.when(i == 0)
        def _():
            dk_s[...] = jnp.zeros_like(dk_s)
            dv_s[...] = jnp.zeros_like(dv_s)
            dc_s[...] = jnp.zeros_like(dc_s)

        def sweep(diagonal):
            p, dp = _fox_p_dp(q_ref, k_ref, v_ref, cq_ref, ck_ref, lse_ref, do_ref, j, diagonal, t)
            ds = p * (dp - dl_ref[...])
            dsb = ds.astype(BF16)
            tn = (((0,), (0,)), ((), ()))
            dv_s[...] += lax.dot_general(p.astype(BF16), do_ref[...], tn, preferred_element_type=F32)
            dk_s[...] += lax.dot_general(dsb, q_ref[...], tn, preferred_element_type=F32)
            dc_s[...] -= jnp.sum(ds, axis=0, keepdims=True)
            rows = pl.ds(pl.multiple_of(i * t, t), t)
            dq_s[rows, :] += jnp.dot(dsb, k_ref[...], preferred_element_type=F32)

        @pl.when(i > j)
        def _():
            sweep(False)

        @pl.when(i == j)
        def _():
            sweep(True)

        @pl.when(i == n - 1)
        def _():
            dk_ref[...] = (dk_s[...] * scale).astype(dk_ref.dtype)
            dv_ref[...] = dv_s[...].astype(dv_ref.dtype)
            dck_ref[...] = dc_s[...]

        @pl.when(jnp.logical_and(j == n - 1, i == n - 1))
        def _():
            dq_ref[...] = (dq_s[...] * scale).astype(dq_ref.dtype)

    qrow = lambda base: pl.BlockSpec((t, FOX_HD), lambda h, j, i: (jnp.maximum(i, j), base + h))
    kv = lambda base: pl.BlockSpec((t, FOX_HD), lambda h, j, i: (j, base + h))
    col = pl.BlockSpec((None, t, 1), lambda h, j, i: (h, jnp.maximum(i, j), 0))
    row = pl.BlockSpec((None, 1, t), lambda h, j, i: (h, 0, j))
    wide = jax.ShapeDtypeStruct((Lp, FOX_WIDTH), BF16)
    return pl.pallas_call(
        body, name="fox_bwd", grid=(FOX_HEADS, n, n),
        in_specs=[qrow(qb), kv(kb), kv(vb), col, row, col, col, qrow(0)],
        out_specs=[pl.BlockSpec((Lp, FOX_HD), lambda h, j, i: (0, h)), kv(0), kv(0), row],
        out_shape=[wide, wide, wide, jax.ShapeDtypeStruct((FOX_HEADS, 1, Lp), F32)],
        scratch_shapes=[pltpu.VMEM((Lp, FOX_HD), F32), pltpu.VMEM((t, FOX_HD), F32), pltpu.VMEM((t, FOX_HD), F32),
                        pltpu.VMEM((1, t), F32)],
        compiler_params=_params(("parallel", "arbitrary", "arbitrary")),
    )(main, main, main, c_col, c_row, lse, delta, doa)


def _shift_down(x, n):
    return pltpu.roll(x, n, 0)


def _shift_up(x, n):
    return pltpu.roll(x, x.shape[0] - n, 0)


def _prev_spec(tm, ct, cb):
    return pl.BlockSpec((HALO, ct), lambda g, i: (jnp.maximum(i * (tm // HALO) - 1, 0), cb(g)))


def _next_spec(tm, ct, cb, nrows):
    last = nrows // HALO - 1
    return pl.BlockSpec((HALO, ct), lambda g, i: (jnp.minimum((i + 1) * (tm // HALO), last), cb(g)))


def _cur_spec(tm, ct, cb):
    return pl.BlockSpec((tm, ct), lambda g, i: (i, cb(g)))


def _wrow(w_ref, k):
    return w_ref[k:k + 1, :]


def _rows3(s0, s1, s2, ct):
    r = lax.broadcasted_iota(jnp.int32, (8, ct), 0)
    return jnp.where(r == 0, s0, jnp.where(r == 1, s1, jnp.where(r == 2, s2, 0.0)))


def _acc_out(ref, i, val):
    @pl.when(i == 0)
    def _():
        ref[...] = val

    @pl.when(i > 0)
    def _():
        ref[...] += val


def _sconv_fwd(main, w8, Lp, tm):
    ct = 256
    G = CONV_CH // ct
    bb, cb, hb = (lambda g: SCB // ct + g), (lambda g: SCC // ct + g), (lambda g: SCH // ct + g)

    def body(b_ref, c_ref, h_ref, cp_ref, hp_ref, w_ref, o_ref):
        i = pl.program_id(1)
        z = c_ref[...].astype(F32) * h_ref[...].astype(F32)
        zp = jnp.where(i > 0, cp_ref[...].astype(F32) * hp_ref[...].astype(F32), 0.0)
        zz = jnp.concatenate([zp, z], axis=0)
        cz = (_wrow(w_ref, 0) * _shift_down(zz, 2)[HALO:] + _wrow(w_ref, 1) * _shift_down(zz, 1)[HALO:]
              + _wrow(w_ref, 2) * z)
        o_ref[...] = (b_ref[...].astype(F32) * cz).astype(o_ref.dtype)

    return pl.pallas_call(
        body, name="sconv_fwd", grid=(G, Lp // tm),
        in_specs=[_cur_spec(tm, ct, bb), _cur_spec(tm, ct, cb), _cur_spec(tm, ct, hb),
                  _prev_spec(tm, ct, cb), _prev_spec(tm, ct, hb), pl.BlockSpec((8, ct), lambda g, i: (0, g))],
        out_specs=pl.BlockSpec((tm, ct), lambda g, i: (i, g)),
        out_shape=jax.ShapeDtypeStruct((Lp, CONV_CH), BF16),
        compiler_params=_params(("parallel", "arbitrary")),
    )(main, main, main, main, main, w8)


def _sconv_bwd(main, w8, dub, Lp, tm):
    ct = 256
    G = CONV_CH // ct
    n = Lp // tm
    bb, cb, hb, ob = (lambda g: SCB // ct + g), (lambda g: SCC // ct + g), (lambda g: SCH // ct + g), (lambda g: g)

    def body(b_ref, c_ref, h_ref, cp_ref, hp_ref, bn_ref, d_ref, dn_ref, w_ref, db_ref, dc_ref, dh_ref, dw_ref):
        i = pl.program_id(1)
        b, c, h = b_ref[...].astype(F32), c_ref[...].astype(F32), h_ref[...].astype(F32)
        z = c * h
        zp = jnp.where(i > 0, cp_ref[...].astype(F32) * hp_ref[...].astype(F32), 0.0)
        zz = jnp.concatenate([zp, z], axis=0)
        z1, z2 = _shift_down(zz, 1)[HALO:], _shift_down(zz, 2)[HALO:]
        w0, w1, w2 = _wrow(w_ref, 0), _wrow(w_ref, 1), _wrow(w_ref, 2)
        cz = w0 * z2 + w1 * z1 + w2 * z
        dub_c = d_ref[...].astype(F32)
        db_ref[...] = (dub_c * cz).astype(db_ref.dtype)
        dcz = dub_c * b
        dcz_n = jnp.where(i < n - 1, dn_ref[...].astype(F32) * bn_ref[...].astype(F32), 0.0)
        dd = jnp.concatenate([dcz, dcz_n], axis=0)
        dz = w2 * dcz + w1 * _shift_up(dd, 1)[:tm] + w0 * _shift_up(dd, 2)[:tm]
        dc_ref[...] = (dz * h).astype(dc_ref.dtype)
        dh_ref[...] = (dz * c).astype(dh_ref.dtype)
        s = lambda x: jnp.sum(dcz * x, axis=0, keepdims=True)
        _acc_out(dw_ref, i, _rows3(s(z2), s(z1), s(z), ct))

    out = pl.BlockSpec((tm, ct), lambda g, i: (i, g))
    return pl.pallas_call(
        body, name="sconv_bwd", grid=(G, n),
        in_specs=[_cur_spec(tm, ct, bb), _cur_spec(tm, ct, cb), _cur_spec(tm, ct, hb),
                  _prev_spec(tm, ct, cb), _prev_spec(tm, ct, hb), _next_spec(tm, ct, bb, Lp),
                  _cur_spec(tm, ct, ob), _next_spec(tm, ct, ob, Lp), pl.BlockSpec((8, ct), lambda g, i: (0, g))],
        out_specs=[out, out, out, pl.BlockSpec((8, ct), lambda g, i: (0, g))],
        out_shape=[jax.ShapeDtypeStruct((Lp, CONV_CH), BF16)] * 3 + [jax.ShapeDtypeStruct((8, CONV_CH), F32)],
        compiler_params=_params(("parallel", "arbitrary")),
    )(main, main, main, main, main, main, dub, dub, w8)


def _conv3(w_ref, ext):
    return _wrow(w_ref, 0) * _shift_down(ext, 2) + _wrow(w_ref, 1) * _shift_down(ext, 1) + _wrow(w_ref, 2) * ext


def _mlp_act_fwd(up, w8, Lp, tm):
    ct = 256
    G = D_FF // ct
    gb, ub = (lambda g: g), (lambda g: G + g)

    def body(g_ref, u_ref, gp_ref, up_ref, wg_ref, wu_ref, o_ref, ot_ref):
        i = pl.program_id(1)

        def conv(cur, prev, w_ref):
            ext = jnp.concatenate([jnp.where(i > 0, prev[...].astype(F32), 0.0), cur[...].astype(F32)], axis=0)
            return _conv3(w_ref, ext)[HALO:]

        ug, uu = conv(g_ref, gp_ref, wg_ref), conv(u_ref, up_ref, wu_ref)
        a = ug * _sigmoid(ug) * uu
        o_ref[...] = a.astype(o_ref.dtype)
        ot_ref[...] = a.T.astype(ot_ref.dtype)

    wspec = lambda cb: pl.BlockSpec((8, ct), lambda g, i: (0, cb(g)))
    return pl.pallas_call(
        body, name="mlp_act_fwd", grid=(G, Lp // tm),
        in_specs=[_cur_spec(tm, ct, gb), _cur_spec(tm, ct, ub), _prev_spec(tm, ct, gb), _prev_spec(tm, ct, ub),
                  wspec(gb), wspec(ub)],
        out_specs=[pl.BlockSpec((tm, ct), lambda g, i: (i, g)), pl.BlockSpec((ct, tm), lambda g, i: (g, i))],
        out_shape=[jax.ShapeDtypeStruct((Lp, D_FF), BF16), jax.ShapeDtypeStruct((D_FF, Lp), BF16)],
        compiler_params=_params(("parallel", "arbitrary")),
    )(up, up, up, up, w8, w8)


def _mlp_act_bwd(up, w8, da, Lp, tm):
    ct = 256
    G = D_FF // ct
    n = Lp // tm
    gb, ub, ob = (lambda g: g), (lambda g: G + g), (lambda g: g)

    def body(g_ref, u_ref, gp_ref, up_ref, gn_ref, un_ref, d_ref, dn_ref, wg_ref, wu_ref,
             dg_ref, du_ref, dwg_ref, dwu_ref):
        i = pl.program_id(1)

        def ext_of(prev, cur, nxt):
            return jnp.concatenate([jnp.where(i > 0, prev[...].astype(F32), 0.0), cur[...].astype(F32),
                                    jnp.where(i < n - 1, nxt[...].astype(F32), 0.0)], axis=0)

        eg, eu = ext_of(gp_ref, g_ref, gn_ref), ext_of(up_ref, u_ref, un_ref)
        da_e = jnp.concatenate([jnp.zeros((HALO, ct), F32), d_ref[...].astype(F32),
                                jnp.where(i < n - 1, dn_ref[...].astype(F32), 0.0)], axis=0)
        ug, uu = _conv3(wg_ref, eg), _conv3(wu_ref, eu)
        sg = _sigmoid(ug)
        dug = da_e * uu * (sg * (1.0 + ug * (1.0 - sg)))
        duu = da_e * (ug * sg)
        cur = slice(HALO, HALO + tm)

        def back(w_ref, dx, e, dx_ref, dw_ref):
            d_in = _wrow(w_ref, 2) * dx + _wrow(w_ref, 1) * _shift_up(dx, 1) + _wrow(w_ref, 0) * _shift_up(dx, 2)
            dx_ref[...] = d_in[cur].astype(dx_ref.dtype)
            s = lambda x: jnp.sum(dx[cur] * x[cur], axis=0, keepdims=True)
            _acc_out(dw_ref, i, _rows3(s(_shift_down(e, 2)), s(_shift_down(e, 1)), s(e), ct))

        back(wg_ref, dug, eg, dg_ref, dwg_ref)
        back(wu_ref, duu, eu, du_ref, dwu_ref)

    wspec = lambda cb: pl.BlockSpec((8, ct), lambda g, i: (0, cb(g)))
    out = pl.BlockSpec((tm, ct), lambda g, i: (i, g))
    return pl.pallas_call(
        body, name="mlp_act_bwd", grid=(G, n),
        in_specs=[_cur_spec(tm, ct, gb), _cur_spec(tm, ct, ub), _prev_spec(tm, ct, gb), _prev_spec(tm, ct, ub),
                  _next_spec(tm, ct, gb, Lp), _next_spec(tm, ct, ub, Lp), _cur_spec(tm, ct, ob),
                  _next_spec(tm, ct, ob, Lp), wspec(gb), wspec(ub)],
        out_specs=[out, out, wspec(ob), wspec(ob)],
        out_shape=[jax.ShapeDtypeStruct((Lp, D_FF), BF16)] * 2 + [jax.ShapeDtypeStruct((8, D_FF), F32)] * 2,
        compiler_params=_params(("parallel", "arbitrary")),
    )(up, up, up, up, up, up, da, da, w8, w8)


def _gla_chunk(q, k, v, g, s0):
    C = BLOCK
    r_i = lax.broadcasted_iota(jnp.int32, (C, C), 0)
    c_i = lax.broadcasted_iota(jnp.int32, (C, C), 1)
    row = lax.broadcasted_iota(jnp.int32, (C, GLA_DK), 0)
    b = jnp.dot((c_i <= r_i).astype(F32), g, precision=HIGHEST, preferred_element_type=F32)
    row_of = lambda n: jnp.sum(jnp.where(row == n, b, 0.0), axis=0, keepdims=True)
    refs = [row_of(n * GLA_SUB) for n in range(C // GLA_SUB)]
    sub = jnp.bitwise_and(row, -GLA_SUB)
    ref_all = sum(jnp.where(sub == n * GLA_SUB, refs[n], 0.0) for n in range(C // GLA_SUB))
    qs = q * (GLA_DK ** -0.5)
    qt = (qs * jnp.exp(b - ref_all)).astype(BF16)
    sub_start = jnp.bitwise_and(r_i, -GLA_SUB)
    att = jnp.zeros((C, C), F32)
    for n in range(C // GLA_SUB):
        kt = (k * jnp.exp(jnp.minimum(refs[n] - b, 60.0))).astype(BF16)
        a_n = lax.dot_general(qt, kt, (((1,), (1,)), ((), ())), preferred_element_type=F32)
        att = att + jnp.where((sub_start == n * GLA_SUB) & (c_i <= r_i), a_n, 0.0)
    o = (jnp.dot(att.astype(BF16), v.astype(BF16), preferred_element_type=F32)
         + jnp.dot((qs * jnp.exp(b)).astype(BF16), s0.astype(BF16), preferred_element_type=F32))
    kd = (k * jnp.exp(row_of(C - 1) - b)).astype(BF16)
    last_rows = (lax.broadcasted_iota(jnp.int32, (C, GLA_DV), 0) == C - 1).astype(F32)
    decay = lax.dot_general(b, last_rows, (((0,), (0,)), ((), ())), precision=HIGHEST,
                            preferred_element_type=F32)
    s1 = jnp.exp(decay) * s0 + lax.dot_general(kd, v.astype(BF16), (((0,), (0,)), ((), ())),
                                               preferred_element_type=F32)
    return o, s1


def _gla_fwd(main, logg, Lp):
    n = Lp // BLOCK
    qb, kb, vb = QC // GLA_DK, KC // GLA_DK, VC // GLA_DV

    def body(q_ref, k_ref, v_ref, g_ref, o_ref, st_ref, s_s):
        c = pl.program_id(1)

        @pl.when(c == 0)
        def _():
            s_s[...] = jnp.zeros_like(s_s)

        s0 = s_s[...]
        st_ref[...] = s0
        o, s1 = _gla_chunk(q_ref[...].astype(F32), k_ref[...].astype(F32), v_ref[...].astype(F32), g_ref[...], s0)
        o_ref[...] = o
        s_s[...] = s1

    return pl.pallas_call(
        body, name="gla_fwd", grid=(GLA_HEADS, n),
        in_specs=[pl.BlockSpec((BLOCK, GLA_DK), lambda h, c: (c, qb + h)),
                  pl.BlockSpec((BLOCK, GLA_DK), lambda h, c: (c, kb + h)),
                  pl.BlockSpec((BLOCK, GLA_DV), lambda h, c: (c, vb + h)),
                  pl.BlockSpec((BLOCK, GLA_DK), lambda h, c: (c, h))],
        out_specs=[pl.BlockSpec((BLOCK, GLA_DV), lambda h, c: (c, h)),
                   pl.BlockSpec((None, None, GLA_DK, GLA_DV), lambda h, c: (h, c, 0, 0))],
        out_shape=[jax.ShapeDtypeStruct((Lp, GLA_HEADS * GLA_DV), F32),
                   jax.ShapeDtypeStruct((GLA_HEADS, n, GLA_DK, GLA_DV), F32)],
        scratch_shapes=[pltpu.VMEM((GLA_DK, GLA_DV), F32)],
        compiler_params=_params(("parallel", "arbitrary")),
    )(main, main, main, logg)


def _gla_bwd(main, logg, states, do, Lp):
    n = Lp // BLOCK
    qb, kb, vb = QC // GLA_DK, KC // GLA_DK, VC // GLA_DV

    def body(q_ref, k_ref, v_ref, g_ref, st_ref, do_ref, dq_ref, dk_ref, dv_ref, dg_ref, ds_s):
        c = pl.program_id(1)

        @pl.when(c == 0)
        def _():
            ds_s[...] = jnp.zeros_like(ds_s)

        _, vjp = jax.vjp(_gla_chunk, q_ref[...].astype(F32), k_ref[...].astype(F32), v_ref[...].astype(F32),
                         g_ref[...], st_ref[...])
        dq, dk, dv, dg, ds0 = vjp((do_ref[...], ds_s[...]))
        dq_ref[...] = dq.astype(dq_ref.dtype)
        dk_ref[...] = dk.astype(dk_ref.dtype)
        dv_ref[...] = dv.astype(dv_ref.dtype)
        dg_ref[...] = dg
        ds_s[...] = ds0

    rk = lambda base: pl.BlockSpec((BLOCK, GLA_DK), lambda h, c: (n - 1 - c, base + h))
    rv = lambda base: pl.BlockSpec((BLOCK, GLA_DV), lambda h, c: (n - 1 - c, base + h))
    return pl.pallas_call(
        body, name="gla_bwd", grid=(GLA_HEADS, n),
        in_specs=[rk(qb), rk(kb), rv(vb), rk(0),
                  pl.BlockSpec((None, None, GLA_DK, GLA_DV), lambda h, c: (h, n - 1 - c, 0, 0)), rv(0)],
        out_specs=[rk(0), rk(0), rv(0), rk(0)],
        out_shape=[jax.ShapeDtypeStruct((Lp, GLA_HEADS * GLA_DK), BF16), jax.ShapeDtypeStruct((Lp, GLA_HEADS * GLA_DK), BF16),
                   jax.ShapeDtypeStruct((Lp, GLA_HEADS * GLA_DV), BF16), jax.ShapeDtypeStruct((Lp, GLA_HEADS * GLA_DK), F32)],
        scratch_shapes=[pltpu.VMEM((GLA_DK, GLA_DV), F32)],
        compiler_params=_params(("parallel", "arbitrary")),
    )(main, main, main, logg, states, do)


def _loss_head(h, g, target, Lp):
    t = BLOCK
    D = D_MODEL

    def body(h_ref, g_ref, t_ref, loss_ref, dh_ref, dg_ref):
        i = pl.program_id(0)
        x = h_ref[...]
        tok = (i * t + lax.broadcasted_iota(jnp.int32, (t, 1), 0)) >= BLOCK
        r = lax.rsqrt(jnp.mean(x * x, axis=-1, keepdims=True) + EPS)
        nrm = x * r
        e = jnp.where(tok, nrm * g_ref[...] - t_ref[...], 0.0)
        part = 0.5 * jnp.sum(jnp.sum(e * e, axis=1, keepdims=True), axis=0, keepdims=True) / D
        dy = e / D
        dn = dy * g_ref[...]
        dh_ref[...] = r * (dn - nrm * jnp.mean(dn * nrm, axis=-1, keepdims=True))
        _acc_out(dg_ref, i, jnp.sum(dy * nrm, axis=0, keepdims=True))
        _acc_out(loss_ref, i, jnp.broadcast_to(part, (1, LANES)))

    return pl.pallas_call(
        body, name="loss_head", grid=(Lp // t,),
        in_specs=[pl.BlockSpec((t, D), lambda i: (i, 0)), pl.BlockSpec((1, D), lambda i: (0, 0)),
                  pl.BlockSpec((t, D), lambda i: (jnp.maximum(i - 1, 0), 0))],
        out_specs=[pl.BlockSpec((1, LANES), lambda i: (0, 0)), pl.BlockSpec((t, D), lambda i: (i, 0)),
                   pl.BlockSpec((1, D), lambda i: (0, 0))],
        out_shape=[jax.ShapeDtypeStruct((1, LANES), F32), jax.ShapeDtypeStruct((Lp, D), F32),
                   jax.ShapeDtypeStruct((1, D), F32)],
        compiler_params=_params(("arbitrary",)),
    )(h, g, target)


def _adamw(w, g, m, v, name):
    if w.ndim == 1:
        outs = _adamw(*(a.reshape(1, -1) for a in (w, g, m, v)), name)
        return tuple(o.reshape(w.shape) for o in outs)
    if w.ndim == 3 and w.shape[-1] % LANES and w.shape[-2] % LANES == 0:
        outs = _adamw(*(a.transpose(2, 0, 1) for a in (w, g, m, v)), name)
        return tuple(o.transpose(1, 2, 0) for o in outs)
    rows, cols = w.shape[-2:]
    budget_rows = max(8, ADAM_BLOCK_BYTES // (4 * cols))
    tr = rows if rows <= budget_rows else _pick(rows, tuple(t for t in (512, 256, 128, 64, 32, 16, 8) if t <= budget_rows))
    lead, tc = 1, cols
    if w.ndim == 3 and tr == rows:
        lead = max(d for d in range(1, 1025) if w.shape[0] % d == 0)
        fits = [c for c in (cols, 2048, 1024, 512, 256, 128) if cols % c == 0 and 4 * lead * rows * c <= ADAM_BLOCK_BYTES]
        tc = fits[0] if fits else LANES

    def body(w_ref, g_ref, m_ref, v_ref, go_ref, d_ref, nm_ref, nv_ref):
        gg = g_ref[...]
        mm = ADAM_B1 * m_ref[...] + (1.0 - ADAM_B1) * gg
        vv = ADAM_B2 * v_ref[...] + (1.0 - ADAM_B2) * jnp.square(gg)
        m_hat = mm / (1.0 - ADAM_B1 ** ADAM_STEP)
        v_hat = vv / (1.0 - ADAM_B2 ** ADAM_STEP)
        d_ref[...] = -ADAM_LR * (m_hat / (jnp.sqrt(v_hat) + ADAM_EPS) + ADAM_WD * w_ref[...])
        go_ref[...] = gg
        nm_ref[...] = mm
        nv_ref[...] = vv

    if w.ndim == 3:
        spec = pl.BlockSpec((lead, tr, tc), lambda l, i, j: (l, i, j))
        grid = (w.shape[0] // lead, rows // tr, cols // tc)
    else:
        spec, grid = pl.BlockSpec((tr, cols), lambda i: (i, 0)), (rows // tr,)
    return pl.pallas_call(
        body, name=name, grid=grid, in_specs=[spec] * 4, out_specs=[spec] * 4,
        out_shape=[jax.ShapeDtypeStruct(w.shape, F32)] * 4,
        compiler_params=_params(("parallel",) * len(grid)),
    )(w, g, m, v)


def _place():
    x, y, c = lax.axis_index("x"), lax.axis_index("y"), lax.axis_index("c")
    chips = [(1 - x, y), (x, 1 - y), (1 - x, 1 - y)]
    return x, y, c, chips


def _rcopy(src, dst, send_sems, recv_sems, k, to):
    return pltpu.make_async_remote_copy(src_ref=src, dst_ref=dst, send_sem=send_sems.at[k], recv_sem=recv_sems.at[k],
                                        device_id=to, device_id_type=MESH)


def _any_spec():
    return pl.BlockSpec(memory_space=pl.ANY)


def _shard_ref(ref, mode, t, r, c):
    if mode == "rows":
        return ref.at[pl.ds(pl.multiple_of(t * r, 16), r), :]
    if mode == "cols":
        return ref.at[:, pl.ds(pl.multiple_of(t * c, LANES), c)]
    return ref.at[t]


def _gathered_shape(mode, r, c):
    return {"rows": (4 * r, c), "cols": (r, 4 * c), "stack": (4, r, c)}[mode]


def _place_own(shard, mode, me1, name):
    _, r, c = shard.shape
    tr = _ew_rows(r, c)
    blk = {"rows": (None, tr, c), "cols": (None, tr, c), "stack": (None, None, tr, c)}[mode]
    idx = {"rows": lambda l, i, me: (l, me[0] * (r // tr) + i, 0),
           "cols": lambda l, i, me: (l, i, me[0]),
           "stack": lambda l, i, me: (l, me[0], i, 0)}[mode]

    def body(me_ref, in_ref, out_ref):
        out_ref[...] = in_ref[...]

    return pl.pallas_call(
        body, name=name,
        grid_spec=pltpu.PrefetchScalarGridSpec(
            num_scalar_prefetch=1, grid=(DEPTH, r // tr),
            in_specs=[pl.BlockSpec((None, tr, c), lambda l, i, me: (l, i, 0))],
            out_specs=pl.BlockSpec(blk, idx)),
        out_shape=jax.ShapeDtypeStruct((DEPTH,) + _gathered_shape(mode, r, c), shard.dtype),
        compiler_params=_params(("parallel", "parallel")),
    )(me1, shard)


def _gather_weights(shards, modes):
    n = len(shards)
    dims = [s.shape[1:] for s in shards]
    me1 = jnp.reshape(2 * lax.axis_index("x") + lax.axis_index("y"), (1,)).astype(jnp.int32)
    placed = [_place_own(shards[k], modes[k], me1, f"gather_place_{k}") for k in range(n)]

    def body(*refs):
        ins, outs = refs[:n], refs[2 * n:3 * n]
        send_sems, recv_sems = refs[3 * n:]
        x, y, c, _ = _place()
        n1 = (x + (1 - c) * (1 - 2 * x), y + c * (1 - 2 * y))
        n2 = (x + c * (1 - 2 * x), y + (1 - c) * (1 - 2 * y))
        diag = (1 - x, 1 - y)
        chip = lambda ch: 2 * ch[0] + ch[1]
        me, here, sibling = (x, y), (x, y, c), (x, y, 1 - c)
        place = lambda k, l, t: _shard_ref(outs[k].at[l], modes[k], chip(t), *dims[k])

        def copy(k, m, l, t, to, src=None):
            blk = place(k, l, t)
            return _rcopy(blk if src is None else src, blk, send_sems, recv_sems, 6 * k + m, to)

        sent = [copy(k, 0, c, me, (*n1, c), ins[k].at[c]) for k in range(n)]
        sent += [copy(k, 1, c, me, (*n2, c), ins[k].at[c]) for k in range(n)]
        for cp in sent:
            cp.start()
        for k in range(n):
            copy(k, 0, c, n1, here).wait_recv()
            sent += [copy(k, 2, c, n1, (*n2, c)), copy(k, 3, c, n1, sibling)]
            sent[-2].start()
            sent[-1].start()
        for m, t in ((1, n2), (2, diag)):
            for k in range(n):
                copy(k, m, c, t, here).wait_recv()
                sent.append(copy(k, 3 + m, c, t, sibling))
                sent[-1].start()
        for m, t in ((3, n2), (4, n1), (5, diag)):
            for k in range(n):
                copy(k, m, 1 - c, t, here).wait_recv()
        for cp in sent:
            cp.wait_send()

    return pl.pallas_call(
        body, name="gather_weights", in_specs=[_any_spec()] * (2 * n), out_specs=[_any_spec()] * n,
        out_shape=[jax.ShapeDtypeStruct(a.shape, a.dtype) for a in placed],
        input_output_aliases={n + k: k for k in range(n)},
        scratch_shapes=[pltpu.SemaphoreType.DMA((6 * n,)), pltpu.SemaphoreType.DMA((6 * n,))],
    )(*shards, *placed)


def _swap_layers(gs):
    n = len(gs)

    def body(*refs):
        ins, outs = refs[:n], refs[n:2 * n]
        send_sems, recv_sems = refs[2 * n:]
        x, y, c, _ = _place()
        cps = [_rcopy(ins[k].at[1 - c], outs[k], send_sems, recv_sems, k, (x, y, 1 - c)) for k in range(n)]
        for cp in cps:
            cp.start()
        for cp in cps:
            cp.wait()

    return pl.pallas_call(
        body, name="rs_swap_layers", in_specs=[_any_spec()] * n, out_specs=[_any_spec()] * n,
        out_shape=[jax.ShapeDtypeStruct(g.shape[1:], g.dtype) for g in gs],
        scratch_shapes=[pltpu.SemaphoreType.DMA((n,)), pltpu.SemaphoreType.DMA((n,))],
    )(*gs)


def _partners(x, y, c):
    n1 = (x + (1 - c) * (1 - 2 * x), y + c * (1 - 2 * y))
    n2 = (x + c * (1 - 2 * x), y + (1 - c) * (1 - 2 * y))
    return n1, n2, (1 - x, 1 - y)


def _chip(ch):
    return 2 * ch[0] + ch[1]


def _scatter_pairs(hs, modes, dims):
    n = len(hs)

    def body(*refs):
        ins, outs = refs[:n], refs[n:2 * n]
        send_sems, recv_sems = refs[2 * n:]
        x, y, c, _ = _place()
        _, n2, diag = _partners(x, y, c)
        part = lambda k, t: _shard_ref(ins[k], modes[k], _chip(t), *dims[k])
        cps = [_rcopy(part(k, t), outs[k].at[j], send_sems, recv_sems, 2 * k + j, (*n2, c))
               for k in range(n) for j, t in enumerate((n2, diag))]
        for cp in cps:
            cp.start()
        for cp in cps:
            cp.wait()

    return pl.pallas_call(
        body, name="rs_scatter_pairs", in_specs=[_any_spec()] * n, out_specs=[_any_spec()] * n,
        out_shape=[jax.ShapeDtypeStruct((2,) + tuple(dims[k]), hs[k].dtype) for k in range(n)],
        scratch_shapes=[pltpu.SemaphoreType.DMA((2 * n,)), pltpu.SemaphoreType.DMA((2 * n,))],
    )(*hs)


def _scatter_last(ts):
    n = len(ts)

    def body(*refs):
        ins, outs = refs[:n], refs[n:2 * n]
        send_sems, recv_sems = refs[2 * n:]
        x, y, c, _ = _place()
        n1, _, _ = _partners(x, y, c)
        cps = [_rcopy(ins[k], outs[k], send_sems, recv_sems, k, (*n1, c)) for k in range(n)]
        for cp in cps:
            cp.start()
        for cp in cps:
            cp.wait()

    return pl.pallas_call(
        body, name="rs_scatter_last", in_specs=[_any_spec()] * n, out_specs=[_any_spec()] * n,
        out_shape=[jax.ShapeDtypeStruct(t.shape, t.dtype) for t in ts],
        scratch_shapes=[pltpu.SemaphoreType.DMA((n,)), pltpu.SemaphoreType.DMA((n,))],
    )(*ts)


def _add_pair(h, got, j, mode, dims, who, out_dtype, name):
    r, c = dims
    tr = _ew_rows(r, c)
    if mode == "stack":
        h_spec = pl.BlockSpec((None, tr, c), lambda i, w: (w[0], i, 0))
    elif mode == "rows":
        h_spec = pl.BlockSpec((tr, c), lambda i, w: (w[0] * (r // tr) + i, 0))
    else:
        h_spec = pl.BlockSpec((tr, c), lambda i, w: (i, w[0]))

    def body(w_ref, h_ref, g_ref, out_ref):
        out_ref[...] = (h_ref[...].astype(F32) + g_ref[...].astype(F32)).astype(out_ref.dtype)

    return pl.pallas_call(
        body, name=name,
        grid_spec=pltpu.PrefetchScalarGridSpec(
            num_scalar_prefetch=1, grid=(r // tr,),
            in_specs=[h_spec, pl.BlockSpec((None, tr, c), lambda i, w: (j, i, 0))],
            out_specs=pl.BlockSpec((tr, c), lambda i, w: (i, 0))),
        out_shape=jax.ShapeDtypeStruct((r, c), out_dtype),
        compiler_params=_params(("parallel",)),
    )(who, h, got)


def _add_last(mine, got, c1, name):
    r, c = mine.shape
    tr = _ew_rows(r, c)

    def body(c_ref, a_ref, b_ref, out_ref):
        out_ref[...] = a_ref[...] + b_ref[...].astype(F32)

    spec = pl.BlockSpec((tr, c), lambda i, cr: (i, 0))
    return pl.pallas_call(
        body, name=name,
        grid_spec=pltpu.PrefetchScalarGridSpec(
            num_scalar_prefetch=1, grid=(r // tr,), in_specs=[spec, spec],
            out_specs=pl.BlockSpec((None, tr, c), lambda i, cr: (cr[0], i, 0))),
        out_shape=jax.ShapeDtypeStruct((DEPTH, r, c), F32),
        compiler_params=_params(("parallel",)),
    )(c1, mine, got)


def _join_layers(fs):
    n = len(fs)

    def body(*refs):
        outs = refs[n:2 * n]
        send_sems, recv_sems = refs[2 * n:]
        x, y, c, _ = _place()
        cps = [_rcopy(outs[k].at[c], outs[k].at[c], send_sems, recv_sems, k, (x, y, 1 - c)) for k in range(n)]
        for cp in cps:
            cp.start()
        for k in range(n):
            blk = outs[k].at[1 - c]
            _rcopy(blk, blk, send_sems, recv_sems, k, (x, y, c)).wait_recv()
        for cp in cps:
            cp.wait_send()

    return pl.pallas_call(
        body, name="rs_join_layers", in_specs=[_any_spec()] * n, out_specs=[_any_spec()] * n,
        out_shape=[jax.ShapeDtypeStruct(f.shape, f.dtype) for f in fs],
        input_output_aliases={k: k for k in range(n)},
        scratch_shapes=[pltpu.SemaphoreType.DMA((n,)), pltpu.SemaphoreType.DMA((n,))],
    )(*fs)


def _ew_rows(M, N):
    fit = [t for t in (512, 256, 128, 64, 32, 16) if M % t == 0 and t * N * 4 <= EW_BLOCK_BYTES]
    return fit[0] if fit else M


def _add_own(g, other, c1, out_dtype, name):
    _, M, N = g.shape
    tr = _ew_rows(M, N)

    def body(c_ref, g_ref, o_ref, out_ref):
        out_ref[...] = (g_ref[...] + o_ref[...].astype(F32)).astype(out_ref.dtype)

    return pl.pallas_call(
        body, name=name,
        grid_spec=pltpu.PrefetchScalarGridSpec(
            num_scalar_prefetch=1, grid=(M // tr,),
            in_specs=[pl.BlockSpec((None, tr, N), lambda i, cr: (cr[0], i, 0)),
                      pl.BlockSpec((tr, N), lambda i, cr: (i, 0))],
            out_specs=pl.BlockSpec((tr, N), lambda i, cr: (i, 0))),
        out_shape=jax.ShapeDtypeStruct((M, N), out_dtype),
        compiler_params=_params(("parallel",)),
    )(c1, g, other)


def _reduce_scatter(gs, gs_d2d, modes, dims, wire):
    x, y, c = lax.axis_index("x"), lax.axis_index("y"), lax.axis_index("c")
    c1 = jnp.reshape(c, (1,)).astype(jnp.int32)
    n1, _, _ = _partners(x, y, c)
    me1, next1 = (jnp.reshape(_chip(ch), (1,)).astype(jnp.int32) for ch in ((x, y), n1))
    flat = lambda a, lead: a.reshape(a.shape[:lead] + (-1, a.shape[-1]))
    others = _swap_layers(gs_d2d)
    hs = [_add_own(flat(g, 1), flat(o, 0), c1, wire[k], f"rs_add_own_{k}").reshape(o.shape)
          for k, (g, o) in enumerate(zip(gs, others))]
    got = _scatter_pairs(hs, modes, dims)
    mine = [_add_pair(hs[k], got[k], 0, modes[k], dims[k], me1, F32, f"rs_add_pair_mine_{k}") for k in range(len(hs))]
    pass_on = [_add_pair(hs[k], got[k], 1, modes[k], dims[k], next1, wire[k], f"rs_add_pair_next_{k}")
               for k in range(len(hs))]
    last = _scatter_last(pass_on)
    fs = [_add_last(mine[k], last[k], c1, f"rs_add_last_{k}") for k in range(len(hs))]
    return _join_layers(fs)


def _allreduce_small(v):
    R, C = v.shape

    def body(v_ref, sum_ref, all_ref, send_sems, recv_sems):
        x, y, c, _ = _place()
        me = 4 * x + 2 * y + c
        rows = lambda d: all_ref.at[pl.ds(pl.multiple_of(d * R, 8), R), :]

        def peer(k):
            flip = lambda bit, v: (1 - v) if ((k + 1) >> bit) & 1 else v
            return flip(2, x), flip(1, y), flip(0, c)

        outs = [_rcopy(v_ref, rows(me), send_sems, recv_sems, k, peer(k)) for k in range(7)]
        for cp in outs:
            cp.start()
        all_ref[pl.ds(pl.multiple_of(me * R, 8), R), :] = v_ref[...]
        for k in range(7):
            px, py, pc = peer(k)
            blk = rows(4 * px + 2 * py + pc)
            _rcopy(blk, blk, send_sems, recv_sems, k, (x, y, c)).wait_recv()
        for cp in outs:
            cp.wait_send()
        tot = all_ref[0:R, :]
        for d in range(1, 8):
            tot = tot + all_ref[d * R:(d + 1) * R, :]
        sum_ref[...] = tot

    vm = pl.BlockSpec(memory_space=pltpu.VMEM)
    return pl.pallas_call(
        body, name="allreduce_small", in_specs=[vm], out_specs=[vm, vm],
        out_shape=[jax.ShapeDtypeStruct((R, C), F32), jax.ShapeDtypeStruct((8 * R, C), F32)],
        scratch_shapes=[pltpu.SemaphoreType.DMA((7,)), pltpu.SemaphoreType.DMA((7,))],
    )(v)[0]


def _size(shape):
    n = 1
    for d in shape:
        n *= d
    return n


def _pack(pieces, dtype):
    flat = jnp.concatenate([p.astype(dtype).reshape(-1) for p in pieces])
    rows = -(-flat.shape[0] // (PACK_COLS * 16)) * 16
    return jnp.pad(flat, (0, rows * PACK_COLS - flat.shape[0])).reshape(rows, PACK_COLS)


def _unpack(buf, shapes):
    flat = buf.reshape(-1)
    out, pos = [], 0
    for s in shapes:
        n = _size(s)
        out.append(flat[pos:pos + n].reshape(s))
        pos += n
    return out


def _small_piece(name, arr, l):
    if name == "meta_tokens":
        return arr[l * (N_META // DEPTH):(l + 1) * (N_META // DEPTH)]
    return arr[l]


def _prep_w_in(w_in4):
    w_in = jnp.concatenate([w_in4[t] for t in range(4)], axis=1)
    col = lambda a, n: w_in[:, _R[a]:_R[a] + n]
    main = jnp.concatenate([col("qa", 3072), col("scb", 3072), col("qc", 3072), col("ga", 6144)], axis=1)
    zpad = lambda n: jnp.zeros((D_MODEL, n), w_in.dtype)
    side = jnp.concatenate([col("fa", 8), zpad(LANES - 8), col("glr", GLA_RANK), zpad(LANES - GLA_RANK)], axis=1)
    return main.astype(BF16), side.astype(BF16)


def _w_in_cols(dmain, dside, lo, hi):
    segs = ((0, _R["fa"], dmain, 0), (_R["fa"], _R["scb"], dside, 0), (_R["scb"], _R["glr"], dmain, SCB),
            (_R["glr"], _R["ga"], dside, LANES), (_R["ga"], N_IN, dmain, GA))
    parts = [src[..., off + max(a, lo) - a:off + min(b, hi) - a] for a, b, src, off in segs if max(a, lo) < min(b, hi)]
    return jnp.concatenate(parts, axis=-1)


def _pad_rows(a, rows):
    return jnp.pad(a.astype(F32), ((0, rows - a.shape[0]), (0, 0)))


def _row2(v):
    return v.reshape(1, -1).astype(F32)


def _layer_fwd(h, p, rep, l, Lp, tm, ta):
    tag = lambda s: f"{s}_l{l}"
    g1, g2 = _row2(rep["norm1_g"][l]), _row2(rep["norm2_g"][l])
    bf = jnp.pad(_row2(rep["fox_b_f"][l]), ((0, 0), (0, LANES - FOX_HEADS)))
    gate_b, b_g, gnorm = _row2(rep["gate_b"][l]), _row2(rep["gla_b_g"][l]), _row2(rep["gla_norm_g"][l])
    xn, xn_t = _rms_fwd(h, g1, tag("rms1_fwd"), Lp)
    main = _mm(xn, p["main"][l], "nn", BF16, tag("proj_main"))
    side = _mm(xn, p["side"][l], "nn", F32, tag("proj_side"))
    c = _fox_gate_fwd(side, bf, Lp)
    c_t = c[:, :FOX_HEADS].T
    c_col, c_row = c_t[:, :, None], c_t[:, None, :]
    oa, ox, lse = _fox_fwd(main, c_col, c_row, Lp, ta)
    ya = _mm(oa, p["w_a_o"], "nn", BF16, tag("ya"), b_lead=l)
    ub = _sconv_fwd(main, p["conv_w"][l], Lp, tm)
    yb = _mm(ub, p["w_b_o"], "nn", BF16, tag("yb"), b_lead=l)
    glr = Row(side, LANES, lambda g: 1)
    (logg,) = _rw_fwd(tag("logg_fwd"), _f_logg, [glr], [Const(p["w_g2"][l]), Const(b_g)], [(512, F32)], Lp, tm)
    oc, states = _gla_fwd(main, logg, Lp)
    rc = Row(main, GLA_DV, lambda g: RC // GLA_DV + g)
    gn = Const(gnorm, (1, GLA_DV), lambda g: (0, g))
    (uc,) = _rw_fwd(tag("gla_post_fwd"), _f_gla_post, [Row(oc, GLA_DV), rc], [gn], [(GLA_DV, BF16)], Lp, tm,
                    G=GLA_HEADS)
    yc = _mm(uc, p["w_c_o"], "nn", BF16, tag("yc"), b_lead=l)
    cw = 512
    G = D_MODEL // cw
    mrows = [Row(ya, cw), Row(yb, cw), Row(yc, cw), Row(main, cw, lambda g: GA // cw + g),
             Row(main, cw, lambda g: GB // cw + g), Row(main, cw, lambda g: GC // cw + g)]
    mconsts = [Const(gate_b, (1, cw), lambda g, k=k: (0, k * G + g)) for k in range(3)]
    (mix,) = _rw_fwd(tag("merge_fwd"), _f_merge, mrows, mconsts, [(cw, BF16)], Lp, tm, G=G)
    h1 = _mm(mix, p["w_o"], "nn", F32, tag("h1"), add=h, b_lead=l)
    xn2, xn2_t = _rms_fwd(h1, g2, tag("rms2_fwd"), Lp)
    up = _mm(xn2, p["w_up"], "nn", BF16, tag("up"), b_lead=l)
    act, act_t = _mlp_act_fwd(up, p["mlp_conv_w"][l], Lp, tm)
    h2 = _mm(act, p["w_down"], "nn", F32, tag("h2"), add=h1, b_lead=l)
    res = dict(h=h, xn=xn, xn_t=xn_t, xn2_t=xn2_t, act_t=act_t, main=main, side=side, c_col=c_col, c_row=c_row, oa=oa, ox=ox, lse=lse, ya=ya, ub=ub, yb=yb,
               logg=logg, oc=oc, states=states, uc=uc, yc=yc, mix=mix, h1=h1, xn2=xn2, up=up, act=act,
               g1=g1, g2=g2, bf=bf, gate_b=gate_b, b_g=b_g, gnorm=gnorm)
    return h2, res


def _layer_bwd(dh2, p, r, l, Lp, tm, ta, big):
    tag = lambda s: f"{s}_l{l}"
    g = {}

    def wgrad(name, a, b, mode="tn"):
        big[name] = _mm(a, b, mode, F32, tag("d_" + name), slot=(big.get(name), l))

    wgrad("w_down", r["act_t"], dh2, "nn")
    dact = _mm(dh2, p["w_down"], "nt", BF16, tag("d_act"), b_lead=l)
    dgate, dval, dwg, dwu = _mlp_act_bwd(r["up"], p["mlp_conv_w"][l], dact, Lp, tm)
    g["mlp_conv_w"] = jnp.concatenate([dwg[:3], dwu[:3]], axis=1)
    dup = jnp.concatenate([dgate, dval], axis=1)
    wgrad("w_up", r["xn2_t"], dup, "nn")
    dxn2 = _mm(dup, p["w_up"], "nt", F32, tag("d_xn2"), b_lead=l)
    (dh1,), (dg2,) = _rw_bwd(tag("rms2_bwd"), _f_rms, [Row(r["h1"], D_MODEL)], [Const(r["g2"])],
                             [Row(dxn2, D_MODEL)], [F32], [dh2], Lp, BLOCK)
    g["norm2_g"] = dg2[0]
    wgrad("w_o", r["mix"], dh1)
    dmix = _mm(dh1, p["w_o"], "nt", BF16, tag("d_mix"), b_lead=l)
    cw = 512
    G = D_MODEL // cw
    main = r["main"]
    mrows = [Row(r["ya"], cw), Row(r["yb"], cw), Row(r["yc"], cw), Row(main, cw, lambda g_: GA // cw + g_),
             Row(main, cw, lambda g_: GB // cw + g_), Row(main, cw, lambda g_: GC // cw + g_)]
    mconsts = [Const(r["gate_b"], (1, cw), lambda g_, k=k: (0, k * G + g_)) for k in range(3)]
    (dya, dyb, dyc, dga, dgb, dgc), dbs = _rw_bwd(tag("merge_bwd"), _f_merge, mrows, mconsts, [Row(dmix, cw)],
                                                  [BF16] * 6, [None] * 6, Lp, tm, G=G)
    g["gate_b"] = jnp.concatenate([dbs[k][0, k * D_MODEL:(k + 1) * D_MODEL] for k in range(3)])
    wgrad("w_a_o", r["oa"], dya)
    doa = _mm(dya, p["w_a_o"], "nt", BF16, tag("d_oa"), b_lead=l)
    wgrad("w_b_o", r["ub"], dyb)
    dub = _mm(dyb, p["w_b_o"], "nt", BF16, tag("d_ub"), b_lead=l)
    wgrad("w_c_o", r["uc"], dyc)
    duc = _mm(dyc, p["w_c_o"], "nt", BF16, tag("d_uc"), b_lead=l)
    delta = _fox_delta(r["ox"], doa, Lp, ta)
    dq, dk, dv, dck = _fox_bwd(main, r["c_col"], r["c_row"], r["lse"], delta, doa, Lp, ta)
    dc = jnp.pad(dck[:, 0, :].T, ((0, 0), (0, LANES - FOX_HEADS)))
    dfa, dbf = _fox_gate_bwd(r["side"], r["bf"], dc, Lp)
    g["fox_b_f"] = dbf[0, :FOX_HEADS]
    dscb, dscc, dsch, dcw = _sconv_bwd(main, p["conv_w"][l], dub, Lp, tm)
    g["conv_w"] = dcw[:3]
    rc = Row(main, GLA_DV, lambda g_: RC // GLA_DV + g_)
    gn = Const(r["gnorm"], (1, GLA_DV), lambda g_: (0, g_))
    (doc, drc), (dgn,) = _rw_bwd(tag("gla_post_bwd"), _f_gla_post, [Row(r["oc"], GLA_DV), rc], [gn],
                                 [Row(duc, GLA_DV)], [F32, BF16], [None, None], Lp, tm, G=GLA_HEADS)
    g["gla_norm_g"] = dgn[0]
    dqc, dkc, dvc, dlogg = _gla_bwd(main, r["logg"], r["states"], doc, Lp)
    glr = Row(r["side"], LANES, lambda g_: 1)
    (dglr,), (dwg2, dbg) = _rw_bwd(tag("logg_bwd"), _f_logg, [glr], [Const(p["w_g2"][l]), Const(r["b_g"])],
                                   [Row(dlogg, 512)], [F32], [None], Lp, tm)
    g["gla_w_g2"] = dwg2[:GLA_RANK]
    g["gla_b_g"] = dbg[0]
    dmain = jnp.concatenate([dq, dk, dv, dscb, dscc, dsch, dqc, dkc, dvc, drc, dga, dgb, dgc], axis=1)
    dside = jnp.concatenate([dfa, dglr], axis=1)
    wgrad("main", r["xn_t"], dmain, "nn")
    wgrad("side", r["xn"], dside)
    dxn = _mm(dmain, p["main"][l], "nt", F32, tag("d_xn_main"))
    dxn = _mm(dside, p["side"][l], "nt", F32, tag("d_xn_side"), add=dxn)
    (dh,), (dg1,) = _rw_bwd(tag("rms1_bwd"), _f_rms, [Row(r["h"], D_MODEL)], [Const(r["g1"])], [Row(dxn, D_MODEL)],
                            [F32], [dh1], Lp, BLOCK)
    g["norm1_g"] = dg1[0]
    return dh, g


def _local_step(x, target, meta, p, rep):
    seq = x.shape[0]
    Lp = PAD + N_META + seq
    tm = _pick(Lp, (640, 384, 128))
    ta = tm
    h = jnp.concatenate([jnp.zeros((PAD, D_MODEL), F32), meta.astype(F32), x], axis=0)
    saved = []
    for l in range(DEPTH):
        h, res = _layer_fwd(h, p, rep, l, Lp, tm, ta)
        saved.append(res)
    loss, dh, dgf = _loss_head(h, _row2(rep["final_norm_g"]), target, Lp)
    big, small = {}, [None] * DEPTH
    for l in reversed(range(DEPTH)):
        dh, small[l] = _layer_bwd(dh, p, saved[l], l, Lp, tm, ta, big)
    return loss[0, 0], dh[BLOCK:], dh[PAD:BLOCK], big, small, dgf[0]


def kernel(x, meta_tokens, norm1_g, w_in, fox_b_f, gate_b, conv_w, gla_w_g2, gla_b_g, gla_norm_g, w_a_o, w_b_o, w_c_o, w_o, norm2_g, w_up, mlp_conv_w, w_down, final_norm_g, loss_target, m_meta_tokens, m_norm1_g, m_w_in, m_fox_b_f, m_gate_b, m_conv_w, m_gla_w_g2, m_gla_b_g, m_gla_norm_g, m_w_a_o, m_w_b_o, m_w_c_o, m_w_o, m_norm2_g, m_w_up, m_mlp_conv_w, m_w_down, m_final_norm_g, v_meta_tokens, v_norm1_g, v_w_in, v_fox_b_f, v_gate_b, v_conv_w, v_gla_w_g2, v_gla_b_g, v_gla_norm_g, v_w_a_o, v_w_b_o, v_w_c_o, v_w_o, v_norm2_g, v_w_up, v_mlp_conv_w, v_w_down, v_final_norm_g):
    given = dict(locals())
    weights = {n: given[n] for n in WEIGHT_ORDER}
    rep = {n: weights[n] for n, _ in REPLICATED}
    big_names = [n for n, _ in BIG]
    big_modes = [m for _, m in BIG] + ["stack"]
    small_shapes = [(s[0], s[1] // 4) for _, s in SMALL]
    exact = [k for k, (n, _) in enumerate(SMALL) if n in GATHER_F32]

    def small_wire(l):
        ws = [_small_piece(n, weights[n], l) for n, _ in SMALL]
        his = [w.astype(BF16) for w in ws]
        return his + [(ws[k] - his[k].astype(F32)).astype(BF16) for k in exact]

    shards = [weights[n].astype(BF16) for n in big_names] + [jnp.stack([_pack(small_wire(l), BF16) for l in range(DEPTH)])]
    gathered = _gather_weights(shards, big_modes)
    gw = dict(zip(big_names, gathered[:-1]))
    p = {n: gw[n] for n in big_names if n != "w_in"}
    p["main"], p["side"] = zip(*[_prep_w_in(gw["w_in"][l]) for l in range(DEPTH)])
    small_full = []
    for l in range(DEPTH):
        per_chip = [_unpack(gathered[-1][l, t], small_shapes + [small_shapes[k] for k in exact]) for t in range(4)]
        full = [jnp.concatenate([per_chip[t][k] for t in range(4)], axis=1).astype(F32) for k in range(len(per_chip[0]))]
        for e, k in enumerate(exact):
            full[k] = full[k] + full[len(SMALL) + e]
        small_full.append(dict(zip([n for n, _ in SMALL], full[:len(SMALL)])))
    p["conv_w"] = [_pad_rows(s["conv_w"], 8) for s in small_full]
    p["mlp_conv_w"] = [_pad_rows(s["mlp_conv_w"], 8) for s in small_full]
    p["w_g2"] = [_pad_rows(s["gla_w_g2"], LANES) for s in small_full]
    meta_full = jnp.concatenate([s["meta_tokens"] for s in small_full], axis=0)

    loss, grad_x, grad_meta, big, small, d_final = _local_step(x[0], loss_target[0], meta_full, p, rep)
    loss = lax.psum(loss, ("x", "y", "c"))

    d_w_in = jnp.stack([_w_in_cols(big["main"][0], big["side"][0], t * (N_IN // 4), (t + 1) * (N_IN // 4))
                        for t in range(4)], axis=1)
    big["w_in"] = (d_w_in, d_w_in)
    for l in range(DEPTH):
        small[l]["meta_tokens"] = grad_meta[l * (N_META // DEPTH):(l + 1) * (N_META // DEPTH)]
    shard_of = lambda a, t: lax.slice_in_dim(a, t * (a.shape[1] // 4), (t + 1) * (a.shape[1] // 4), axis=1)
    small_g = jnp.stack([jnp.stack([_pack([shard_of(small[l][n], t) for n, _ in SMALL], F32) for t in range(4)])
                         for l in range(DEPTH)])
    dims = [shards[k].shape[1:] for k in range(len(BIG))] + [small_g.shape[2:]]
    summed = _reduce_scatter([big[n][0] for n in big_names] + [small_g], [big[n][1] for n in big_names] + [small_g],
                             big_modes, dims, [BF16] * len(BIG) + [F32])
    gout = dict(zip(big_names, summed[:-1]))
    pieces = [_unpack(summed[-1][l], small_shapes) for l in range(DEPTH)]
    for k, (n, _) in enumerate(SMALL):
        per_layer = [pieces[l][k] for l in range(DEPTH)]
        gout[n] = jnp.concatenate(per_layer, axis=0) if n == "meta_tokens" else jnp.stack(per_layer)

    rep_g = {n: (d_final if n == "final_norm_g" else jnp.stack([small[l][n] for l in range(DEPTH)])) for n, _ in REPLICATED}
    flat = jnp.concatenate([rep_g[n].astype(F32).reshape(-1) for n, _ in REPLICATED])
    rrows = -(-flat.shape[0] // (PACK_COLS * 8)) * 8
    summed_small = _allreduce_small(jnp.pad(flat, (0, rrows * PACK_COLS - flat.shape[0])).reshape(rrows, PACK_COLS))
    pos = 0
    for n, shape in REPLICATED:
        gout[n] = summed_small.reshape(-1)[pos:pos + _size(shape)].reshape(shape)
        pos += _size(shape)

    deltas, new_m, new_v = {}, {}, {}
    for n in WEIGHT_ORDER:
        gout[n], deltas[n], new_m[n], new_v[n] = _adamw(weights[n], gout[n], given["m_" + n], given["v_" + n],
                                                        "adamw_" + n)
    return (loss, grad_x[None], *[gout[n] for n in WEIGHT_ORDER], *[deltas[n] for n in WEIGHT_ORDER],
            *[new_m[n] for n in WEIGHT_ORDER], *[new_v[n] for n in WEIGHT_ORDER])
```

```python
import functools

import jax
import jax.numpy as jnp
from jax import lax
from jax.experimental import pallas as pl
from jax.experimental.pallas import tpu as pltpu

F32, BF16 = jnp.float32, jnp.bfloat16
HIGHEST = lax.Precision.HIGHEST
MESH = pl.DeviceIdType.MESH

N_META = 16
BLOCK = 128
LANES = 128
PAD = BLOCK - N_META
EPS = 1e-6
NEG = -1e30
HALO = 16
VMEM_LIMIT = 56 * 1024 * 1024
ADAM_BLOCK_BYTES = 1 << 20
EW_BLOCK_BYTES = 3 << 19

D_MODEL = 2048
FOX_HEADS, FOX_HD = 8, 128
FOX_WIDTH = FOX_HEADS * FOX_HD
CONV_CH = 1024
GLA_HEADS, GLA_DK, GLA_DV, GLA_RANK, GLA_TAU = 4, 128, 256, 16, 16.0
GLA_SUB = 32
D_FF = 5632
N_IN = 15384
DEPTH = 2

_R = dict(qa=0, ka=1024, va=2048, fa=3072, scb=3080, scc=4104, sch=5128, qc=6152, kc=6664,
          vc=7176, rc=8200, glr=9224, ga=9240, gb=11288, gc=13336)
QA, KA, VA, SCB, SCC, SCH, QC, KC, VC, RC, GA, GB, GC = (
    0, 1024, 2048, 3072, 4096, 5120, 6144, 6656, 7168, 8192, 9216, 11264, 13312)
N_MAIN = 15360
N_SIDE = 256

ADAM_LR, ADAM_B1, ADAM_B2, ADAM_EPS, ADAM_WD, ADAM_STEP = 0.001, 0.9, 0.999, 1e-08, 0.01, 10

BIG = (("w_in", "stack"), ("w_a_o", "cols"), ("w_b_o", "cols"), ("w_c_o", "cols"), ("w_o", "rows"), ("w_up", "cols"),
       ("w_down", "rows"))
SMALL = (("conv_w", (3, CONV_CH)), ("mlp_conv_w", (3, 2 * D_FF)), ("gla_w_g2", (GLA_RANK, 512)),
         ("meta_tokens", (N_META // DEPTH, D_MODEL)))
REPLICATED = (("norm1_g", (2, D_MODEL)), ("fox_b_f", (2, 8)), ("gate_b", (2, 3 * D_MODEL)), ("gla_b_g", (2, 512)),
              ("gla_norm_g", (2, 1024)), ("norm2_g", (2, D_MODEL)), ("final_norm_g", (D_MODEL,)))
WEIGHT_ORDER = ("meta_tokens", "norm1_g", "w_in", "fox_b_f", "gate_b", "conv_w", "gla_w_g2", "gla_b_g",
                "gla_norm_g", "w_a_o", "w_b_o", "w_c_o", "w_o", "norm2_g", "w_up", "mlp_conv_w", "w_down",
                "final_norm_g")
PACK_COLS = 1024
GATHER_F32 = ("conv_w", "mlp_conv_w", "meta_tokens")


def _pick(n, cands):
    for c in cands:
        if n % c == 0:
            return c
    return n


def _params(sem):
    return pltpu.CompilerParams(dimension_semantics=sem, vmem_limit_bytes=VMEM_LIMIT)


def _sigmoid(x):
    return jax.nn.sigmoid(x)


def _log_sigmoid(x):
    return jnp.minimum(x, 0.0) - jnp.log(1.0 + jnp.exp(-jnp.abs(x)))


def _mm(a, b, mode, out_dtype, name, add=None, b_lead=None, slot=None):
    bshape = b.shape if b_lead is None else b.shape[1:]
    if mode == "nn":
        (M, K), (K2, N) = a.shape, bshape
    elif mode == "nt":
        (M, K), (N, K2) = a.shape, bshape
    else:
        (K, M), (K2, N) = a.shape, bshape
    assert K == K2, (name, a.shape, b.shape)
    if mode == "tn":
        tm = _pick(M, (2048, 1408, 1024, 512, 256, 128))
        tn = _pick(N, (1024, 512, 256, 128))
        tk = _pick(K, (640, 512, 384, 256, 128))
    else:
        tm = _pick(M, (1664, 2048, 1408, 1024, 640, 384, 128))
        wide = mode == "nt" and a.dtype == BF16 and slot is None and add is None
        tn = _pick(N, (1024, 512, 256, 128) if wide else (512, 256, 128))
        tk = K if K <= 2048 else _pick(K, (1664, 1408, 1024, 640, 512, 384, 256, 128))
    nk = K // tk
    dims = {"nn": (((1,), (0,)), ((), ())), "nt": (((1,), (1,)), ((), ())), "tn": (((0,), (0,)), ((), ()))}[mode]
    n_in = 2 + (add is not None) + (2 if slot is not None and slot[0] is not None else 0)

    def body(*refs):
        a_ref, b_ref = refs[:2]
        add_ref = refs[2] if add is not None else None
        o_ref, acc = refs[n_in], refs[-1]
        o16_ref = refs[n_in + 1] if slot is not None else None
        k = pl.program_id(2)

        @pl.when(k == 0)
        def _():
            acc[...] = jnp.zeros_like(acc)

        acc[...] += lax.dot_general(a_ref[...].astype(BF16), b_ref[...].astype(BF16), dims,
                                    preferred_element_type=F32)

        @pl.when(k == nk - 1)
        def _():
            r = acc[...]
            if add is not None:
                r = r + add_ref[...].astype(F32)
            o_ref[...] = r.astype(o_ref.dtype)
            if o16_ref is not None:
                o16_ref[...] = r.astype(BF16)

    a_spec = {"nn": pl.BlockSpec((tm, tk), lambda i, j, k: (i, k)),
              "nt": pl.BlockSpec((tm, tk), lambda i, j, k: (i, k)),
              "tn": pl.BlockSpec((tk, tm), lambda i, j, k: (k, i))}[mode]
    b_blk, b_idx = {"nn": ((tk, tn), lambda i, j, k: (k, j)),
                    "nt": ((tn, tk), lambda i, j, k: (j, k)),
                    "tn": ((tk, tn), lambda i, j, k: (k, j))}[mode]
    if b_lead is None:
        b_spec = pl.BlockSpec(b_blk, b_idx)
    else:
        b_spec = pl.BlockSpec((None,) + b_blk, lambda i, j, k: (b_lead,) + b_idx(i, j, k))
    o_spec = pl.BlockSpec((tm, tn), lambda i, j, k: (i, j))
    ins, specs = [a, b], [a_spec, b_spec]
    if add is not None:
        ins.append(add)
        specs.append(o_spec)
    aliases = {}
    out_shape = jax.ShapeDtypeStruct((M, N), out_dtype)
    if slot is not None:
        bufs, l = slot
        o_spec = [pl.BlockSpec((None, tm, tn), lambda i, j, k: (l, i, j))] * 2
        out_shape = [jax.ShapeDtypeStruct((DEPTH, M, N), out_dtype), jax.ShapeDtypeStruct((DEPTH, M, N), BF16)]
        if bufs is not None:
            aliases = {len(ins): 0, len(ins) + 1: 1}
            ins.extend(bufs)
            specs.extend([pl.BlockSpec(memory_space=pl.ANY)] * 2)
    return pl.pallas_call(
        body, name=name, grid=(M // tm, N // tn, nk), in_specs=specs, out_specs=o_spec, out_shape=out_shape,
        scratch_shapes=[pltpu.VMEM((tm, tn), F32)], input_output_aliases=aliases,
        compiler_params=_params(("parallel", "parallel", "arbitrary")),
    )(*ins)


class Row:
    def __init__(self, arr, w, cb=None):
        self.arr, self.w, self.cb = arr, w, (cb if cb is not None else (lambda g: g))


class Const:
    def __init__(self, arr, shape=None, idx=None):
        self.arr = arr
        self.shape = shape if shape is not None else arr.shape
        self.idx = idx if idx is not None else (lambda g: (0,) * arr.ndim)


def _row_spec(r, tm):
    return pl.BlockSpec((tm, r.w), lambda g, i, r=r: (i, r.cb(g)))


def _const_spec(c):
    return pl.BlockSpec(c.shape, lambda g, i, c=c: c.idx(g))


def _valid_rows(i, tm):
    return (i * tm + lax.broadcasted_iota(jnp.int32, (tm, 1), 0)) >= PAD


def _rw_fwd(name, f, rows, consts, outs, Lp, tm, G=1):
    nr, nc = len(rows), len(consts)

    def body(*refs):
        i = pl.program_id(1)
        rv = [r[...].astype(F32) for r in refs[:nr]]
        cv = [r[...].astype(F32) for r in refs[nr:nr + nc]]
        res = f(_valid_rows(i, tm), *rv, *cv)
        for o_ref, v in zip(refs[nr + nc:], res):
            o_ref[...] = v.astype(o_ref.dtype)

    return pl.pallas_call(
        body, name=name, grid=(G, Lp // tm),
        in_specs=[_row_spec(r, tm) for r in rows] + [_const_spec(c) for c in consts],
        out_specs=[pl.BlockSpec((tm, w), lambda g, i: (i, g)) for w, _ in outs],
        out_shape=[jax.ShapeDtypeStruct((Lp, w * G), dt) for w, dt in outs],
        compiler_params=_params(("parallel", "arbitrary")),
    )(*[r.arr for r in rows], *[c.arr for c in consts])


def _rw_bwd(name, f, rows, consts, cts, drow_dtypes, adds, Lp, tm, G=1):
    nr, nc, nt = len(rows), len(consts), len(cts)
    want = [k for k, dt in enumerate(drow_dtypes) if dt is not None]
    add_k = [k for k in want if adds[k] is not None]

    def body(*refs):
        i = pl.program_id(1)
        pos = 0
        rv = [r[...].astype(F32) for r in refs[pos:pos + nr]]
        pos += nr
        cv = [r[...].astype(F32) for r in refs[pos:pos + nc]]
        pos += nc
        tv = [r[...].astype(F32) for r in refs[pos:pos + nt]]
        pos += nt
        av = {k: refs[pos + n][...].astype(F32) for n, k in enumerate(add_k)}
        pos += len(add_k)
        drow_refs = refs[pos:pos + len(want)]
        pos += len(want)
        dconst_refs = refs[pos:pos + nc]
        valid = _valid_rows(i, tm)
        _, vjp = jax.vjp(lambda *a: tuple(f(valid, *a)), *rv, *cv)
        grads = vjp(tuple(tv))
        for o_ref, k in zip(drow_refs, want):
            gk = grads[k]
            if k in av:
                gk = gk + av[k]
            o_ref[...] = gk.astype(o_ref.dtype)
        for n, o_ref in enumerate(dconst_refs):
            gc = grads[nr + n]

            @pl.when(i == 0)
            def _(o_ref=o_ref, gc=gc):
                o_ref[...] = gc

            @pl.when(i > 0)
            def _(o_ref=o_ref, gc=gc):
                o_ref[...] += gc

    out_row = lambda w: pl.BlockSpec((tm, w), lambda g, i: (i, g))
    res = pl.pallas_call(
        body, name=name, grid=(G, Lp // tm),
        in_specs=([_row_spec(r, tm) for r in rows] + [_const_spec(c) for c in consts]
                  + [_row_spec(r, tm) for r in cts] + [out_row(rows[k].w) for k in add_k]),
        out_specs=[out_row(rows[k].w) for k in want] + [_const_spec(c) for c in consts],
        out_shape=([jax.ShapeDtypeStruct((Lp, rows[k].w * G), drow_dtypes[k]) for k in want]
                   + [jax.ShapeDtypeStruct(c.arr.shape, F32) for c in consts]),
        compiler_params=_params(("parallel", "arbitrary")),
    )(*[r.arr for r in rows], *[c.arr for c in consts], *[r.arr for r in cts], *[adds[k] for k in add_k])
    drows = [None] * nr
    for n, k in enumerate(want):
        drows[k] = res[n]
    return drows, list(res[len(want):])


def _f_rms(valid, h, g):
    r = lax.rsqrt(jnp.mean(h * h, axis=-1, keepdims=True) + EPS)
    return (jnp.where(valid, h * r * g, 0.0),)


def _rms_fwd(h, g, name, Lp):
    t = BLOCK

    def body(h_ref, g_ref, o_ref, ot_ref):
        (y,) = _f_rms(_valid_rows(pl.program_id(0), t), h_ref[...], g_ref[...])
        o_ref[...] = y.astype(o_ref.dtype)
        ot_ref[...] = y.T.astype(ot_ref.dtype)

    return pl.pallas_call(
        body, name=name, grid=(Lp // t,),
        in_specs=[pl.BlockSpec((t, D_MODEL), lambda i: (i, 0)), pl.BlockSpec((1, D_MODEL), lambda i: (0, 0))],
        out_specs=[pl.BlockSpec((t, D_MODEL), lambda i: (i, 0)), pl.BlockSpec((D_MODEL, t), lambda i: (0, i))],
        out_shape=[jax.ShapeDtypeStruct((Lp, D_MODEL), BF16), jax.ShapeDtypeStruct((D_MODEL, Lp), BF16)],
        compiler_params=_params(("parallel",)),
    )(h, g)


def _f_logg(valid, glr, w, b):
    pre = jnp.dot(glr.astype(BF16), w.astype(BF16), preferred_element_type=F32) + b
    return (jnp.where(valid, _log_sigmoid(pre) / GLA_TAU, 0.0),)


def _f_gla_post(valid, oc, rc, g):
    y = oc * lax.rsqrt(jnp.mean(oc * oc, axis=-1, keepdims=True) + EPS) * g
    return (jnp.where(valid, rc * _sigmoid(rc) * y, 0.0),)


def _f_merge(valid, ya, yb, yc, ga, gb, gc, ba, bb, bc):
    mix = _sigmoid(ga + ba) * ya + _sigmoid(gb + bb) * yb + _sigmoid(gc + bc) * yc
    return (jnp.where(valid, mix, 0.0),)


def _fox_gate_fwd(side, bf, Lp):
    t = BLOCK
    n = Lp // t

    def body(s_ref, b_ref, c_ref, carry):
        i = pl.program_id(0)

        @pl.when(i == 0)
        def _():
            carry[...] = jnp.zeros_like(carry)

        lane = lax.broadcasted_iota(jnp.int32, (t, LANES), 1)
        ok = _valid_rows(i, t) & (lane < FOX_HEADS)
        logf = jnp.where(ok, _log_sigmoid(s_ref[...] + b_ref[...]), 0.0)
        tril = (lax.broadcasted_iota(jnp.int32, (t, t), 1) <= lax.broadcasted_iota(jnp.int32, (t, t), 0)).astype(F32)
        c = jnp.dot(tril, logf, precision=HIGHEST, preferred_element_type=F32) + carry[...]
        c_ref[...] = c
        carry[...] = c[t - 1:t, :]

    return pl.pallas_call(
        body, name="fox_gate_fwd", grid=(n,),
        in_specs=[pl.BlockSpec((t, LANES), lambda i: (i, 0)), pl.BlockSpec((1, LANES), lambda i: (0, 0))],
        out_specs=pl.BlockSpec((t, LANES), lambda i: (i, 0)),
        out_shape=jax.ShapeDtypeStruct((Lp, LANES), F32),
        scratch_shapes=[pltpu.VMEM((1, LANES), F32)],
        compiler_params=_params(("arbitrary",)),
    )(side, bf)


def _fox_gate_bwd(side, bf, dc, Lp):
    t = BLOCK
    n = Lp // t

    def body(s_ref, b_ref, dc_ref, dfa_ref, db_ref, carry):
        i = pl.program_id(0)

        @pl.when(i == 0)
        def _():
            carry[...] = jnp.zeros_like(carry)

        lane = lax.broadcasted_iota(jnp.int32, (t, LANES), 1)
        ok = _valid_rows(n - 1 - i, t) & (lane < FOX_HEADS)
        triu = (lax.broadcasted_iota(jnp.int32, (t, t), 1) >= lax.broadcasted_iota(jnp.int32, (t, t), 0)).astype(F32)
        dlogf = jnp.dot(triu, dc_ref[...], precision=HIGHEST, preferred_element_type=F32) + carry[...]
        carry[...] = dlogf[0:1, :]
        dpre = jnp.where(ok, dlogf * _sigmoid(-(s_ref[...] + b_ref[...])), 0.0)
        dfa_ref[...] = dpre
        part = jnp.sum(dpre, axis=0, keepdims=True)

        @pl.when(i == 0)
        def _():
            db_ref[...] = part

        @pl.when(i > 0)
        def _():
            db_ref[...] += part

    rev = lambda i: (n - 1 - i, 0)
    return pl.pallas_call(
        body, name="fox_gate_bwd", grid=(n,),
        in_specs=[pl.BlockSpec((t, LANES), rev), pl.BlockSpec((1, LANES), lambda i: (0, 0)),
                  pl.BlockSpec((t, LANES), rev)],
        out_specs=[pl.BlockSpec((t, LANES), rev), pl.BlockSpec((1, LANES), lambda i: (0, 0))],
        out_shape=[jax.ShapeDtypeStruct((Lp, LANES), F32), jax.ShapeDtypeStruct((1, LANES), F32)],
        scratch_shapes=[pltpu.VMEM((1, LANES), F32)],
        compiler_params=_params(("arbitrary",)),
    )(side, bf, dc)


def _fox_key_bias(cq_ref, ck_ref, j, t):
    col = j * t + lax.broadcasted_iota(jnp.int32, (1, t), 1)
    return jnp.where(col >= PAD, ck_ref[...] - cq_ref[0:1, :], -NEG)


def _fox_s(q, k, bias, diagonal, t):
    s = lax.dot_general(q, k, (((1,), (1,)), ((), ())), preferred_element_type=F32) * (FOX_HD ** -0.5) - bias
    if diagonal:
        causal = lax.broadcasted_iota(jnp.int32, (t, t), 1) <= lax.broadcasted_iota(jnp.int32, (t, t), 0)
        s = jnp.where(causal, s, NEG)
    return s


def _fox_fwd(main, c_col, c_row, Lp, t):
    n = Lp // t
    qb, kb, vb = QA // FOX_HD, KA // FOX_HD, VA // FOX_HD

    def body(q_ref, k_ref, v_ref, cq_ref, ck_ref, o_ref, ox_ref, lse_ref, m_s, l_s, acc):
        i, j = pl.program_id(1), pl.program_id(2)

        @pl.when(j == 0)
        def _():
            m_s[...] = jnp.full_like(m_s, NEG)
            l_s[...] = jnp.zeros_like(l_s)
            acc[...] = jnp.zeros_like(acc)

        def update(diagonal):
            s = _fox_s(q_ref[...], k_ref[...], _fox_key_bias(cq_ref, ck_ref, j, t), diagonal, t)
            m_new = jnp.maximum(m_s[...], jnp.max(s, axis=1, keepdims=True))
            alpha = jnp.exp(m_s[...] - m_new)
            p = jnp.exp(s - m_new)
            l_s[...] = alpha * l_s[...] + jnp.sum(p, axis=1, keepdims=True)
            p_hi = p.astype(BF16)
            p_lo = (p - p_hi.astype(F32)).astype(BF16)
            pv = (jnp.dot(p_hi, v_ref[...], preferred_element_type=F32)
                  + jnp.dot(p_lo, v_ref[...], preferred_element_type=F32))
            acc[...] = alpha * acc[...] + pv
            m_s[...] = m_new

        @pl.when(j < i)
        def _():
            update(False)

        @pl.when(j == i)
        def _():
            update(True)
            o = jnp.where(_valid_rows(i, t), acc[...] / l_s[...], 0.0)
            o_ref[...] = o.astype(o_ref.dtype)
            ox_ref[...] = o
            lse_ref[...] = m_s[...] + jnp.log(l_s[...])

    kv = lambda base: pl.BlockSpec((t, FOX_HD), lambda h, i, j: (jnp.minimum(j, i), base + h))
    return pl.pallas_call(
        body, name="fox_fwd", grid=(FOX_HEADS, n, n),
        in_specs=[pl.BlockSpec((t, FOX_HD), lambda h, i, j: (i, qb + h)), kv(kb), kv(vb),
                  pl.BlockSpec((None, t, 1), lambda h, i, j: (h, i, 0)),
                  pl.BlockSpec((None, 1, t), lambda h, i, j: (h, 0, jnp.minimum(j, i)))],
        out_specs=[pl.BlockSpec((t, FOX_HD), lambda h, i, j: (i, h)), pl.BlockSpec((t, FOX_HD), lambda h, i, j: (i, h)),
                   pl.BlockSpec((None, t, 1), lambda h, i, j: (h, i, 0))],
        out_shape=[jax.ShapeDtypeStruct((Lp, FOX_WIDTH), BF16), jax.ShapeDtypeStruct((Lp, FOX_WIDTH), F32),
                   jax.ShapeDtypeStruct((FOX_HEADS, Lp, 1), F32)],
        scratch_shapes=[pltpu.VMEM((t, 1), F32), pltpu.VMEM((t, 1), F32), pltpu.VMEM((t, FOX_HD), F32)],
        compiler_params=_params(("parallel", "parallel", "arbitrary")),
    )(main, main, main, c_col, c_row)


def _fox_p_dp(q_ref, k_ref, v_ref, cq_ref, ck_ref, lse_ref, do_ref, j, diagonal, t):
    s = _fox_s(q_ref[...], k_ref[...], _fox_key_bias(cq_ref, ck_ref, j, t), diagonal, t)
    p = jnp.exp(s - lse_ref[...])
    dp = lax.dot_general(do_ref[...], v_ref[...], (((1,), (1,)), ((), ())), preferred_element_type=F32)
    return p, dp


def _fox_delta(ox, doa, Lp, t):
    n = Lp // t

    def body(o_ref, do_ref, dl_ref):
        dl_ref[...] = jnp.sum(o_ref[...] * do_ref[...].astype(F32), axis=1, keepdims=True)

    blk = pl.BlockSpec((t, FOX_HD), lambda h, i: (i, h))
    return pl.pallas_call(
        body, name="fox_delta", grid=(FOX_HEADS, n), in_specs=[blk, blk],
        out_specs=pl.BlockSpec((None, t, 1), lambda h, i: (h, i, 0)),
        out_shape=jax.ShapeDtypeStruct((FOX_HEADS, Lp, 1), F32),
        compiler_params=_params(("parallel", "parallel")),
    )(ox, doa)


def _fox_bwd(main, c_col, c_row, lse, delta, doa, Lp, t):
    n = Lp // t
    qb, kb, vb = QA // FOX_HD, KA // FOX_HD, VA // FOX_HD
    scale = FOX_HD ** -0.5

    def body(q_ref, k_ref, v_ref, cq_ref, ck_ref, lse_ref, dl_ref, do_ref, dq_ref, dk_ref, dv_ref, dck_ref,
             dq_s, dk_s, dv_s, dc_s):
        j, i = pl.program_id(1), pl.program_id(2)

        @pl.when(jnp.logical_and(j == 0, i == 0))
        def _():
            dq_s[...] = jnp.zeros_like(dq_s)

        @pl.when(i == 0)
        def _():
            dk_s[...] = jnp.zeros_like(dk_s)
            dv_s[...] = jnp.zeros_like(dv_s)
            dc_s[...] = jnp.zeros_like(dc_s)

        def sweep(diagonal):
            p, dp = _fox_p_dp(q_ref, k_ref, v_ref, cq_ref, ck_ref, lse_ref, do_ref, j, diagonal, t)
            ds = p * (dp - dl_ref[...])
            dsb = ds.astype(BF16)
            tn = (((0,), (0,)), ((), ()))
            dv_s[...] += lax.dot_general(p.astype(BF16), do_ref[...], tn, preferred_element_type=F32)
            dk_s[...] += lax.dot_general(dsb, q_ref[...], tn, preferred_element_type=F32)
            dc_s[...] -= jnp.sum(ds, axis=0, keepdims=True)
            rows = pl.ds(pl.multiple_of(i * t, t), t)
            dq_s[rows, :] += jnp.dot(dsb, k_ref[...], preferred_element_type=F32)

        @pl.when(i > j)
        def _():
            sweep(False)

        @pl.when(i == j)
        def _():
            sweep(True)

        @pl.when(i == n - 1)
        def _():
            dk_ref[...] = (dk_s[...] * scale).astype(dk_ref.dtype)
            dv_ref[...] = dv_s[...].astype(dv_ref.dtype)
            dck_ref[...] = dc_s[...]

        @pl.when(jnp.logical_and(j == n - 1, i == n - 1))
        def _():
            dq_ref[...] = (dq_s[...] * scale).astype(dq_ref.dtype)

    qrow = lambda base: pl.BlockSpec((t, FOX_HD), lambda h, j, i: (jnp.maximum(i, j), base + h))
    kv = lambda base: pl.BlockSpec((t, FOX_HD), lambda h, j, i: (j, base + h))
    col = pl.BlockSpec((None, t, 1), lambda h, j, i: (h, jnp.maximum(i, j), 0))
    row = pl.BlockSpec((None, 1, t), lambda h, j, i: (h, 0, j))
    wide = jax.ShapeDtypeStruct((Lp, FOX_WIDTH), BF16)
    return pl.pallas_call(
        body, name="fox_bwd", grid=(FOX_HEADS, n, n),
        in_specs=[qrow(qb), kv(kb), kv(vb), col, row, col, col, qrow(0)],
        out_specs=[pl.BlockSpec((Lp, FOX_HD), lambda h, j, i: (0, h)), kv(0), kv(0), row],
        out_shape=[wide, wide, wide, jax.ShapeDtypeStruct((FOX_HEADS, 1, Lp), F32)],
        scratch_shapes=[pltpu.VMEM((Lp, FOX_HD), F32), pltpu.VMEM((t, FOX_HD), F32), pltpu.VMEM((t, FOX_HD), F32),
                        pltpu.VMEM((1, t), F32)],
        compiler_params=_params(("parallel", "arbitrary", "arbitrary")),
    )(main, main, main, c_col, c_row, lse, delta, doa)


def _shift_down(x, n):
    return pltpu.roll(x, n, 0)


def _shift_up(x, n):
    return pltpu.roll(x, x.shape[0] - n, 0)


def _prev_spec(tm, ct, cb):
    return pl.BlockSpec((HALO, ct), lambda g, i: (jnp.maximum(i * (tm // HALO) - 1, 0), cb(g)))


def _next_spec(tm, ct, cb, nrows):
    last = nrows // HALO - 1
    return pl.BlockSpec((HALO, ct), lambda g, i: (jnp.minimum((i + 1) * (tm // HALO), last), cb(g)))


def _cur_spec(tm, ct, cb):
    return pl.BlockSpec((tm, ct), lambda g, i: (i, cb(g)))


def _wrow(w_ref, k):
    return w_ref[k:k + 1, :]


def _rows3(s0, s1, s2, ct):
    r = lax.broadcasted_iota(jnp.int32, (8, ct), 0)
    return jnp.where(r == 0, s0, jnp.where(r == 1, s1, jnp.where(r == 2, s2, 0.0)))


def _acc_out(ref, i, val):
    @pl.when(i == 0)
    def _():
        ref[...] = val

    @pl.when(i > 0)
    def _():
        ref[...] += val


def _sconv_fwd(main, w8, Lp, tm):
    ct = 256
    G = CONV_CH // ct
    bb, cb, hb = (lambda g: SCB // ct + g), (lambda g: SCC // ct + g), (lambda g: SCH // ct + g)

    def body(b_ref, c_ref, h_ref, cp_ref, hp_ref, w_ref, o_ref):
        i = pl.program_id(1)
        z = c_ref[...].astype(F32) * h_ref[...].astype(F32)
        zp = jnp.where(i > 0, cp_ref[...].astype(F32) * hp_ref[...].astype(F32), 0.0)
        zz = jnp.concatenate([zp, z], axis=0)
        cz = (_wrow(w_ref, 0) * _shift_down(zz, 2)[HALO:] + _wrow(w_ref, 1) * _shift_down(zz, 1)[HALO:]
              + _wrow(w_ref, 2) * z)
        o_ref[...] = (b_ref[...].astype(F32) * cz).astype(o_ref.dtype)

    return pl.pallas_call(
        body, name="sconv_fwd", grid=(G, Lp // tm),
        in_specs=[_cur_spec(tm, ct, bb), _cur_spec(tm, ct, cb), _cur_spec(tm, ct, hb),
                  _prev_spec(tm, ct, cb), _prev_spec(tm, ct, hb), pl.BlockSpec((8, ct), lambda g, i: (0, g))],
        out_specs=pl.BlockSpec((tm, ct), lambda g, i: (i, g)),
        out_shape=jax.ShapeDtypeStruct((Lp, CONV_CH), BF16),
        compiler_params=_params(("parallel", "arbitrary")),
    )(main, main, main, main, main, w8)


def _sconv_bwd(main, w8, dub, Lp, tm):
    ct = 256
    G = CONV_CH // ct
    n = Lp // tm
    bb, cb, hb, ob = (lambda g: SCB // ct + g), (lambda g: SCC // ct + g), (lambda g: SCH // ct + g), (lambda g: g)

    def body(b_ref, c_ref, h_ref, cp_ref, hp_ref, bn_ref, d_ref, dn_ref, w_ref, db_ref, dc_ref, dh_ref, dw_ref):
        i = pl.program_id(1)
        b, c, h = b_ref[...].astype(F32), c_ref[...].astype(F32), h_ref[...].astype(F32)
        z = c * h
        zp = jnp.where(i > 0, cp_ref[...].astype(F32) * hp_ref[...].astype(F32), 0.0)
        zz = jnp.concatenate([zp, z], axis=0)
        z1, z2 = _shift_down(zz, 1)[HALO:], _shift_down(zz, 2)[HALO:]
        w0, w1, w2 = _wrow(w_ref, 0), _wrow(w_ref, 1), _wrow(w_ref, 2)
        cz = w0 * z2 + w1 * z1 + w2 * z
        dub_c = d_ref[...].astype(F32)
        db_ref[...] = (dub_c * cz).astype(db_ref.dtype)
        dcz = dub_c * b
        dcz_n = jnp.where(i < n - 1, dn_ref[...].astype(F32) * bn_ref[...].astype(F32), 0.0)
        dd = jnp.concatenate([dcz, dcz_n], axis=0)
        dz = w2 * dcz + w1 * _shift_up(dd, 1)[:tm] + w0 * _shift_up(dd, 2)[:tm]
        dc_ref[...] = (dz * h).astype(dc_ref.dtype)
        dh_ref[...] = (dz * c).astype(dh_ref.dtype)
        s = lambda x: jnp.sum(dcz * x, axis=0, keepdims=True)
        _acc_out(dw_ref, i, _rows3(s(z2), s(z1), s(z), ct))

    out = pl.BlockSpec((tm, ct), lambda g, i: (i, g))
    return pl.pallas_call(
        body, name="sconv_bwd", grid=(G, n),
        in_specs=[_cur_spec(tm, ct, bb), _cur_spec(tm, ct, cb), _cur_spec(tm, ct, hb),
                  _prev_spec(tm, ct, cb), _prev_spec(tm, ct, hb), _next_spec(tm, ct, bb, Lp),
                  _cur_spec(tm, ct, ob), _next_spec(tm, ct, ob, Lp), pl.BlockSpec((8, ct), lambda g, i: (0, g))],
        out_specs=[out, out, out, pl.BlockSpec((8, ct), lambda g, i: (0, g))],
        out_shape=[jax.ShapeDtypeStruct((Lp, CONV_CH), BF16)] * 3 + [jax.ShapeDtypeStruct((8, CONV_CH), F32)],
        compiler_params=_params(("parallel", "arbitrary")),
    )(main, main, main, main, main, main, dub, dub, w8)


def _conv3(w_ref, ext):
    return _wrow(w_ref, 0) * _shift_down(ext, 2) + _wrow(w_ref, 1) * _shift_down(ext, 1) + _wrow(w_ref, 2) * ext


def _mlp_act_fwd(up, w8, Lp, tm):
    ct = 256
    G = D_FF // ct
    gb, ub = (lambda g: g), (lambda g: G + g)

    def body(g_ref, u_ref, gp_ref, up_ref, wg_ref, wu_ref, o_ref, ot_ref):
        i = pl.program_id(1)

        def conv(cur, prev, w_ref):
            ext = jnp.concatenate([jnp.where(i > 0, prev[...].astype(F32), 0.0), cur[...].astype(F32)], axis=0)
            return _conv3(w_ref, ext)[HALO:]

        ug, uu = conv(g_ref, gp_ref, wg_ref), conv(u_ref, up_ref, wu_ref)
        a = ug * _sigmoid(ug) * uu
        o_ref[...] = a.astype(o_ref.dtype)
        ot_ref[...] = a.T.astype(ot_ref.dtype)

    wspec = lambda cb: pl.BlockSpec((8, ct), lambda g, i: (0, cb(g)))
    return pl.pallas_call(
        body, name="mlp_act_fwd", grid=(G, Lp // tm),
        in_specs=[_cur_spec(tm, ct, gb), _cur_spec(tm, ct, ub), _prev_spec(tm, ct, gb), _prev_spec(tm, ct, ub),
                  wspec(gb), wspec(ub)],
        out_specs=[pl.BlockSpec((tm, ct), lambda g, i: (i, g)), pl.BlockSpec((ct, tm), lambda g, i: (g, i))],
        out_shape=[jax.ShapeDtypeStruct((Lp, D_FF), BF16), jax.ShapeDtypeStruct((D_FF, Lp), BF16)],
        compiler_params=_params(("parallel", "arbitrary")),
    )(up, up, up, up, w8, w8)


def _mlp_act_bwd(up, w8, da, Lp, tm):
    ct = 256
    G = D_FF // ct
    n = Lp // tm
    gb, ub, ob = (lambda g: g), (lambda g: G + g), (lambda g: g)

    def body(g_ref, u_ref, gp_ref, up_ref, gn_ref, un_ref, d_ref, dn_ref, wg_ref, wu_ref,
             dg_ref, du_ref, dwg_ref, dwu_ref):
        i = pl.program_id(1)

        def ext_of(prev, cur, nxt):
            return jnp.concatenate([jnp.where(i > 0, prev[...].astype(F32), 0.0), cur[...].astype(F32),
                                    jnp.where(i < n - 1, nxt[...].astype(F32), 0.0)], axis=0)

        eg, eu = ext_of(gp_ref, g_ref, gn_ref), ext_of(up_ref, u_ref, un_ref)
        da_e = jnp.concatenate([jnp.zeros((HALO, ct), F32), d_ref[...].astype(F32),
                                jnp.where(i < n - 1, dn_ref[...].astype(F32), 0.0)], axis=0)
        ug, uu = _conv3(wg_ref, eg), _conv3(wu_ref, eu)
        sg = _sigmoid(ug)
        dug = da_e * uu * (sg * (1.0 + ug * (1.0 - sg)))
        duu = da_e * (ug * sg)
        cur = slice(HALO, HALO + tm)

        def back(w_ref, dx, e, dx_ref, dw_ref):
            d_in = _wrow(w_ref, 2) * dx + _wrow(w_ref, 1) * _shift_up(dx, 1) + _wrow(w_ref, 0) * _shift_up(dx, 2)
            dx_ref[...] = d_in[cur].astype(dx_ref.dtype)
            s = lambda x: jnp.sum(dx[cur] * x[cur], axis=0, keepdims=True)
            _acc_out(dw_ref, i, _rows3(s(_shift_down(e, 2)), s(_shift_down(e, 1)), s(e), ct))

        back(wg_ref, dug, eg, dg_ref, dwg_ref)
        back(wu_ref, duu, eu, du_ref, dwu_ref)

    wspec = lambda cb: pl.BlockSpec((8, ct), lambda g, i: (0, cb(g)))
    out = pl.BlockSpec((tm, ct), lambda g, i: (i, g))
    return pl.pallas_call(
        body, name="mlp_act_bwd", grid=(G, n),
        in_specs=[_cur_spec(tm, ct, gb), _cur_spec(tm, ct, ub), _prev_spec(tm, ct, gb), _prev_spec(tm, ct, ub),
                  _next_spec(tm, ct, gb, Lp), _next_spec(tm, ct, ub, Lp), _cur_spec(tm, ct, ob),
                  _next_spec(tm, ct, ob, Lp), wspec(gb), wspec(ub)],
        out_specs=[out, out, wspec(ob), wspec(ob)],
        out_shape=[jax.ShapeDtypeStruct((Lp, D_FF), BF16)] * 2 + [jax.ShapeDtypeStruct((8, D_FF), F32)] * 2,
        compiler_params=_params(("parallel", "arbitrary")),
    )(up, up, up, up, up, up, da, da, w8, w8)


def _gla_chunk(q, k, v, g, s0):
    C = BLOCK
    r_i = lax.broadcasted_iota(jnp.int32, (C, C), 0)
    c_i = lax.broadcasted_iota(jnp.int32, (C, C), 1)
    row = lax.broadcasted_iota(jnp.int32, (C, GLA_DK), 0)
    b = jnp.dot((c_i <= r_i).astype(F32), g, precision=HIGHEST, preferred_element_type=F32)
    row_of = lambda n: jnp.sum(jnp.where(row == n, b, 0.0), axis=0, keepdims=True)
    refs = [row_of(n * GLA_SUB) for n in range(C // GLA_SUB)]
    sub = jnp.bitwise_and(row, -GLA_SUB)
    ref_all = sum(jnp.where(sub == n * GLA_SUB, refs[n], 0.0) for n in range(C // GLA_SUB))
    qs = q * (GLA_DK ** -0.5)
    qt = (qs * jnp.exp(b - ref_all)).astype(BF16)
    sub_start = jnp.bitwise_and(r_i, -GLA_SUB)
    att = jnp.zeros((C, C), F32)
    for n in range(C // GLA_SUB):
        kt = (k * jnp.exp(jnp.minimum(refs[n] - b, 60.0))).astype(BF16)
        a_n = lax.dot_general(qt, kt, (((1,), (1,)), ((), ())), preferred_element_type=F32)
        att = att + jnp.where((sub_start == n * GLA_SUB) & (c_i <= r_i), a_n, 0.0)
    o = (jnp.dot(att.astype(BF16), v.astype(BF16), preferred_element_type=F32)
         + jnp.dot((qs * jnp.exp(b)).astype(BF16), s0.astype(BF16), preferred_element_type=F32))
    kd = (k * jnp.exp(row_of(C - 1) - b)).astype(BF16)
    last_rows = (lax.broadcasted_iota(jnp.int32, (C, GLA_DV), 0) == C - 1).astype(F32)
    decay = lax.dot_general(b, last_rows, (((0,), (0,)), ((), ())), precision=HIGHEST,
                            preferred_element_type=F32)
    s1 = jnp.exp(decay) * s0 + lax.dot_general(kd, v.astype(BF16), (((0,), (0,)), ((), ())),
                                               preferred_element_type=F32)
    return o, s1


def _gla_fwd(main, logg, Lp):
    n = Lp // BLOCK
    qb, kb, vb = QC // GLA_DK, KC // GLA_DK, VC // GLA_DV

    def body(q_ref, k_ref, v_ref, g_ref, o_ref, st_ref, s_s):
        c = pl.program_id(1)

        @pl.when(c == 0)
        def _():
            s_s[...] = jnp.zeros_like(s_s)

        s0 = s_s[...]
        st_ref[...] = s0
        o, s1 = _gla_chunk(q_ref[...].astype(F32), k_ref[...].astype(F32), v_ref[...].astype(F32), g_ref[...], s0)
        o_ref[...] = o
        s_s[...] = s1

    return pl.pallas_call(
        body, name="gla_fwd", grid=(GLA_HEADS, n),
        in_specs=[pl.BlockSpec((BLOCK, GLA_DK), lambda h, c: (c, qb + h)),
                  pl.BlockSpec((BLOCK, GLA_DK), lambda h, c: (c, kb + h)),
                  pl.BlockSpec((BLOCK, GLA_DV), lambda h, c: (c, vb + h)),
                  pl.BlockSpec((BLOCK, GLA_DK), lambda h, c: (c, h))],
        out_specs=[pl.BlockSpec((BLOCK, GLA_DV), lambda h, c: (c, h)),
                   pl.BlockSpec((None, None, GLA_DK, GLA_DV), lambda h, c: (h, c, 0, 0))],
        out_shape=[jax.ShapeDtypeStruct((Lp, GLA_HEADS * GLA_DV), F32),
                   jax.ShapeDtypeStruct((GLA_HEADS, n, GLA_DK, GLA_DV), F32)],
        scratch_shapes=[pltpu.VMEM((GLA_DK, GLA_DV), F32)],
        compiler_params=_params(("parallel", "arbitrary")),
    )(main, main, main, logg)


def _gla_bwd(main, logg, states, do, Lp):
    n = Lp // BLOCK
    qb, kb, vb = QC // GLA_DK, KC // GLA_DK, VC // GLA_DV

    def body(q_ref, k_ref, v_ref, g_ref, st_ref, do_ref, dq_ref, dk_ref, dv_ref, dg_ref, ds_s):
        c = pl.program_id(1)

        @pl.when(c == 0)
        def _():
            ds_s[...] = jnp.zeros_like(ds_s)

        _, vjp = jax.vjp(_gla_chunk, q_ref[...].astype(F32), k_ref[...].astype(F32), v_ref[...].astype(F32),
                         g_ref[...], st_ref[...])
        dq, dk, dv, dg, ds0 = vjp((do_ref[...], ds_s[...]))
        dq_ref[...] = dq.astype(dq_ref.dtype)
        dk_ref[...] = dk.astype(dk_ref.dtype)
        dv_ref[...] = dv.astype(dv_ref.dtype)
        dg_ref[...] = dg
        ds_s[...] = ds0

    rk = lambda base: pl.BlockSpec((BLOCK, GLA_DK), lambda h, c: (n - 1 - c, base + h))
    rv = lambda base: pl.BlockSpec((BLOCK, GLA_DV), lambda h, c: (n - 1 - c, base + h))
    return pl.pallas_call(
        body, name="gla_bwd", grid=(GLA_HEADS, n),
        in_specs=[rk(qb), rk(kb), rv(vb), rk(0),
                  pl.BlockSpec((None, None, GLA_DK, GLA_DV), lambda h, c: (h, n - 1 - c, 0, 0)), rv(0)],
        out_specs=[rk(0), rk(0), rv(0), rk(0)],
        out_shape=[jax.ShapeDtypeStruct((Lp, GLA_HEADS * GLA_DK), BF16), jax.ShapeDtypeStruct((Lp, GLA_HEADS * GLA_DK), BF16),
                   jax.ShapeDtypeStruct((Lp, GLA_HEADS * GLA_DV), BF16), jax.ShapeDtypeStruct((Lp, GLA_HEADS * GLA_DK), F32)],
        scratch_shapes=[pltpu.VMEM((GLA_DK, GLA_DV), F32)],
        compiler_params=_params(("parallel", "arbitrary")),
    )(main, main, main, logg, states, do)


def _loss_head(h, g, target, Lp):
    t = BLOCK
    D = D_MODEL

    def body(h_ref, g_ref, t_ref, loss_ref, dh_ref, dg_ref):
        i = pl.program_id(0)
        x = h_ref[...]
        tok = (i * t + lax.broadcasted_iota(jnp.int32, (t, 1), 0)) >= BLOCK
        r = lax.rsqrt(jnp.mean(x * x, axis=-1, keepdims=True) + EPS)
        nrm = x * r
        e = jnp.where(tok, nrm * g_ref[...] - t_ref[...], 0.0)
        part = 0.5 * jnp.sum(jnp.sum(e * e, axis=1, keepdims=True), axis=0, keepdims=True) / D
        dy = e / D
        dn = dy * g_ref[...]
        dh_ref[...] = r * (dn - nrm * jnp.mean(dn * nrm, axis=-1, keepdims=True))
        _acc_out(dg_ref, i, jnp.sum(dy * nrm, axis=0, keepdims=True))
        _acc_out(loss_ref, i, jnp.broadcast_to(part, (1, LANES)))

    return pl.pallas_call(
        body, name="loss_head", grid=(Lp // t,),
        in_specs=[pl.BlockSpec((t, D), lambda i: (i, 0)), pl.BlockSpec((1, D), lambda i: (0, 0)),
                  pl.BlockSpec((t, D), lambda i: (jnp.maximum(i - 1, 0), 0))],
        out_specs=[pl.BlockSpec((1, LANES), lambda i: (0, 0)), pl.BlockSpec((t, D), lambda i: (i, 0)),
                   pl.BlockSpec((1, D), lambda i: (0, 0))],
        out_shape=[jax.ShapeDtypeStruct((1, LANES), F32), jax.ShapeDtypeStruct((Lp, D), F32),
                   jax.ShapeDtypeStruct((1, D), F32)],
        compiler_params=_params(("arbitrary",)),
    )(h, g, target)


def _adamw(w, g, m, v, name):
    if w.ndim == 1:
        outs = _adamw(*(a.reshape(1, -1) for a in (w, g, m, v)), name)
        return tuple(o.reshape(w.shape) for o in outs)
    if w.ndim == 3 and w.shape[-1] % LANES and w.shape[-2] % LANES == 0:
        outs = _adamw(*(a.transpose(2, 0, 1) for a in (w, g, m, v)), name)
        return tuple(o.transpose(1, 2, 0) for o in outs)
    rows, cols = w.shape[-2:]
    budget_rows = max(8, ADAM_BLOCK_BYTES // (4 * cols))
    tr = rows if rows <= budget_rows else _pick(rows, tuple(t for t in (512, 256, 128, 64, 32, 16, 8) if t <= budget_rows))
    lead, tc = 1, cols
    if w.ndim == 3 and tr == rows:
        lead = max(d for d in range(1, 1025) if w.shape[0] % d == 0)
        fits = [c for c in (cols, 2048, 1024, 512, 256, 128) if cols % c == 0 and 4 * lead * rows * c <= ADAM_BLOCK_BYTES]
        tc = fits[0] if fits else LANES

    def body(w_ref, g_ref, m_ref, v_ref, go_ref, d_ref, nm_ref, nv_ref):
        gg = g_ref[...]
        mm = ADAM_B1 * m_ref[...] + (1.0 - ADAM_B1) * gg
        vv = ADAM_B2 * v_ref[...] + (1.0 - ADAM_B2) * jnp.square(gg)
        m_hat = mm / (1.0 - ADAM_B1 ** ADAM_STEP)
        v_hat = vv / (1.0 - ADAM_B2 ** ADAM_STEP)
        d_ref[...] = -ADAM_LR * (m_hat / (jnp.sqrt(v_hat) + ADAM_EPS) + ADAM_WD * w_ref[...])
        go_ref[...] = gg
        nm_ref[...] = mm
        nv_ref[...] = vv

    if w.ndim == 3:
        spec = pl.BlockSpec((lead, tr, tc), lambda l, i, j: (l, i, j))
        grid = (w.shape[0] // lead, rows // tr, cols // tc)
    else:
        spec, grid = pl.BlockSpec((tr, cols), lambda i: (i, 0)), (rows // tr,)
    return pl.pallas_call(
        body, name=name, grid=grid, in_specs=[spec] * 4, out_specs=[spec] * 4,
        out_shape=[jax.ShapeDtypeStruct(w.shape, F32)] * 4,
        compiler_params=_params(("parallel",) * len(grid)),
    )(w, g, m, v)


def _place():
    x, y, c = lax.axis_index("x"), lax.axis_index("y"), lax.axis_index("c")
    chips = [(1 - x, y), (x, 1 - y), (1 - x, 1 - y)]
    return x, y, c, chips


def _rcopy(src, dst, send_sems, recv_sems, k, to):
    return pltpu.make_async_remote_copy(src_ref=src, dst_ref=dst, send_sem=send_sems.at[k], recv_sem=recv_sems.at[k],
                                        device_id=to, device_id_type=MESH)


def _any_spec():
    return pl.BlockSpec(memory_space=pl.ANY)


def _shard_ref(ref, mode, t, r, c):
    if mode == "rows":
        return ref.at[pl.ds(pl.multiple_of(t * r, 16), r), :]
    if mode == "cols":
        return ref.at[:, pl.ds(pl.multiple_of(t * c, LANES), c)]
    return ref.at[t]


def _gathered_shape(mode, r, c):
    return {"rows": (4 * r, c), "cols": (r, 4 * c), "stack": (4, r, c)}[mode]


def _place_own(shard, mode, me1, name):
    _, r, c = shard.shape
    tr = _ew_rows(r, c)
    blk = {"rows": (None, tr, c), "cols": (None, tr, c), "stack": (None, None, tr, c)}[mode]
    idx = {"rows": lambda l, i, me: (l, me[0] * (r // tr) + i, 0),
           "cols": lambda l, i, me: (l, i, me[0]),
           "stack": lambda l, i, me: (l, me[0], i, 0)}[mode]

    def body(me_ref, in_ref, out_ref):
        out_ref[...] = in_ref[...]

    return pl.pallas_call(
        body, name=name,
        grid_spec=pltpu.PrefetchScalarGridSpec(
            num_scalar_prefetch=1, grid=(DEPTH, r // tr),
            in_specs=[pl.BlockSpec((None, tr, c), lambda l, i, me: (l, i, 0))],
            out_specs=pl.BlockSpec(blk, idx)),
        out_shape=jax.ShapeDtypeStruct((DEPTH,) + _gathered_shape(mode, r, c), shard.dtype),
        compiler_params=_params(("parallel", "parallel")),
    )(me1, shard)


def _gather_weights(shards, modes):
    n = len(shards)
    dims = [s.shape[1:] for s in shards]
    me1 = jnp.reshape(2 * lax.axis_index("x") + lax.axis_index("y"), (1,)).astype(jnp.int32)
    placed = [_place_own(shards[k], modes[k], me1, f"gather_place_{k}") for k in range(n)]

    def body(*refs):
        ins, outs = refs[:n], refs[2 * n:3 * n]
        send_sems, recv_sems = refs[3 * n:]
        x, y, c, _ = _place()
        n1 = (x + (1 - c) * (1 - 2 * x), y + c * (1 - 2 * y))
        n2 = (x + c * (1 - 2 * x), y + (1 - c) * (1 - 2 * y))
        diag = (1 - x, 1 - y)
        chip = lambda ch: 2 * ch[0] + ch[1]
        me, here, sibling = (x, y), (x, y, c), (x, y, 1 - c)
        place = lambda k, l, t: _shard_ref(outs[k].at[l], modes[k], chip(t), *dims[k])

        def copy(k, m, l, t, to, src=None):
            blk = place(k, l, t)
            return _rcopy(blk if src is None else src, blk, send_sems, recv_sems, 6 * k + m, to)

        sent = [copy(k, 0, c, me, (*n1, c), ins[k].at[c]) for k in range(n)]
        sent += [copy(k, 1, c, me, (*n2, c), ins[k].at[c]) for k in range(n)]
        for cp in sent:
            cp.start()
        for k in range(n):
            copy(k, 0, c, n1, here).wait_recv()
            sent += [copy(k, 2, c, n1, (*n2, c)), copy(k, 3, c, n1, sibling)]
            sent[-2].start()
            sent[-1].start()
        for m, t in ((1, n2), (2, diag)):
            for k in range(n):
                copy(k, m, c, t, here).wait_recv()
                sent.append(copy(k, 3 + m, c, t, sibling))
                sent[-1].start()
        for m, t in ((3, n2), (4, n1), (5, diag)):
            for k in range(n):
                copy(k, m, 1 - c, t, here).wait_recv()
        for cp in sent:
            cp.wait_send()

    return pl.pallas_call(
        body, name="gather_weights", in_specs=[_any_spec()] * (2 * n), out_specs=[_any_spec()] * n,
        out_shape=[jax.ShapeDtypeStruct(a.shape, a.dtype) for a in placed],
        input_output_aliases={n + k: k for k in range(n)},
        scratch_shapes=[pltpu.SemaphoreType.DMA((6 * n,)), pltpu.SemaphoreType.DMA((6 * n,))],
    )(*shards, *placed)


def _swap_layers(gs):
    n = len(gs)

    def body(*refs):
        ins, outs = refs[:n], refs[n:2 * n]
        send_sems, recv_sems = refs[2 * n:]
        x, y, c, _ = _place()
        cps = [_rcopy(ins[k].at[1 - c], outs[k], send_sems, recv_sems, k, (x, y, 1 - c)) for k in range(n)]
        for cp in cps:
            cp.start()
        for cp in cps:
            cp.wait()

    return pl.pallas_call(
        body, name="rs_swap_layers", in_specs=[_any_spec()] * n, out_specs=[_any_spec()] * n,
        out_shape=[jax.ShapeDtypeStruct(g.shape[1:], g.dtype) for g in gs],
        scratch_shapes=[pltpu.SemaphoreType.DMA((n,)), pltpu.SemaphoreType.DMA((n,))],
    )(*gs)


def _partners(x, y, c):
    n1 = (x + (1 - c) * (1 - 2 * x), y + c * (1 - 2 * y))
    n2 = (x + c * (1 - 2 * x), y + (1 - c) * (1 - 2 * y))
    return n1, n2, (1 - x, 1 - y)


def _chip(ch):
    return 2 * ch[0] + ch[1]


def _scatter_pairs(hs, modes, dims):
    n = len(hs)

    def body(*refs):
        ins, outs = refs[:n], refs[n:2 * n]
        send_sems, recv_sems = refs[2 * n:]
        x, y, c, _ = _place()
        _, n2, diag = _partners(x, y, c)
        part = lambda k, t: _shard_ref(ins[k], modes[k], _chip(t), *dims[k])
        cps = [_rcopy(part(k, t), outs[k].at[j], send_sems, recv_sems, 2 * k + j, (*n2, c))
               for k in range(n) for j, t in enumerate((n2, diag))]
        for cp in cps:
            cp.start()
        for cp in cps:
            cp.wait()

    return pl.pallas_call(
        body, name="rs_scatter_pairs", in_specs=[_any_spec()] * n, out_specs=[_any_spec()] * n,
        out_shape=[jax.ShapeDtypeStruct((2,) + tuple(dims[k]), hs[k].dtype) for k in range(n)],
        scratch_shapes=[pltpu.SemaphoreType.DMA((2 * n,)), pltpu.SemaphoreType.DMA((2 * n,))],
    )(*hs)


def _scatter_last(ts):
    n = len(ts)

    def body(*refs):
        ins, outs = refs[:n], refs[n:2 * n]
        send_sems, recv_sems = refs[2 * n:]
        x, y, c, _ = _place()
        n1, _, _ = _partners(x, y, c)
        cps = [_rcopy(ins[k], outs[k], send_sems, recv_sems, k, (*n1, c)) for k in range(n)]
        for cp in cps:
            cp.start()
        for cp in cps:
            cp.wait()

    return pl.pallas_call(
        body, name="rs_scatter_last", in_specs=[_any_spec()] * n, out_specs=[_any_spec()] * n,
        out_shape=[jax.ShapeDtypeStruct(t.shape, t.dtype) for t in ts],
        scratch_shapes=[pltpu.SemaphoreType.DMA((n,)), pltpu.SemaphoreType.DMA((n,))],
    )(*ts)


def _add_pair(h, got, j, mode, dims, who, out_dtype, name):
    r, c = dims
    tr = _ew_rows(r, c)
    if mode == "stack":
        h_spec = pl.BlockSpec((None, tr, c), lambda i, w: (w[0], i, 0))
    elif mode == "rows":
        h_spec = pl.BlockSpec((tr, c), lambda i, w: (w[0] * (r // tr) + i, 0))
    else:
        h_spec = pl.BlockSpec((tr, c), lambda i, w: (i, w[0]))

    def body(w_ref, h_ref, g_ref, out_ref):
        out_ref[...] = (h_ref[...].astype(F32) + g_ref[...].astype(F32)).astype(out_ref.dtype)

    return pl.pallas_call(
        body, name=name,
        grid_spec=pltpu.PrefetchScalarGridSpec(
            num_scalar_prefetch=1, grid=(r // tr,),
            in_specs=[h_spec, pl.BlockSpec((None, tr, c), lambda i, w: (j, i, 0))],
            out_specs=pl.BlockSpec((tr, c), lambda i, w: (i, 0))),
        out_shape=jax.ShapeDtypeStruct((r, c), out_dtype),
        compiler_params=_params(("parallel",)),
    )(who, h, got)


def _add_last(mine, got, c1, name):
    r, c = mine.shape
    tr = _ew_rows(r, c)

    def body(c_ref, a_ref, b_ref, out_ref):
        out_ref[...] = a_ref[...] + b_ref[...].astype(F32)

    spec = pl.BlockSpec((tr, c), lambda i, cr: (i, 0))
    return pl.pallas_call(
        body, name=name,
        grid_spec=pltpu.PrefetchScalarGridSpec(
            num_scalar_prefetch=1, grid=(r // tr,), in_specs=[spec, spec],
            out_specs=pl.BlockSpec((None, tr, c), lambda i, cr: (cr[0], i, 0))),
        out_shape=jax.ShapeDtypeStruct((DEPTH, r, c), F32),
        compiler_params=_params(("parallel",)),
    )(c1, mine, got)


def _join_layers(fs):
    n = len(fs)

    def body(*refs):
        outs = refs[n:2 * n]
        send_sems, recv_sems = refs[2 * n:]
        x, y, c, _ = _place()
        cps = [_rcopy(outs[k].at[c], outs[k].at[c], send_sems, recv_sems, k, (x, y, 1 - c)) for k in range(n)]
        for cp in cps:
            cp.start()
        for k in range(n):
            blk = outs[k].at[1 - c]
            _rcopy(blk, blk, send_sems, recv_sems, k, (x, y, c)).wait_recv()
        for cp in cps:
            cp.wait_send()

    return pl.pallas_call(
        body, name="rs_join_layers", in_specs=[_any_spec()] * n, out_specs=[_any_spec()] * n,
        out_shape=[jax.ShapeDtypeStruct(f.shape, f.dtype) for f in fs],
        input_output_aliases={k: k for k in range(n)},
        scratch_shapes=[pltpu.SemaphoreType.DMA((n,)), pltpu.SemaphoreType.DMA((n,))],
    )(*fs)


def _ew_rows(M, N):
    fit = [t for t in (512, 256, 128, 64, 32, 16) if M % t == 0 and t * N * 4 <= EW_BLOCK_BYTES]
    return fit[0] if fit else M


def _add_own(g, other, c1, out_dtype, name):
    _, M, N = g.shape
    tr = _ew_rows(M, N)

    def body(c_ref, g_ref, o_ref, out_ref):
        out_ref[...] = (g_ref[...] + o_ref[...].astype(F32)).astype(out_ref.dtype)

    return pl.pallas_call(
        body, name=name,
        grid_spec=pltpu.PrefetchScalarGridSpec(
            num_scalar_prefetch=1, grid=(M // tr,),
            in_specs=[pl.BlockSpec((None, tr, N), lambda i, cr: (cr[0], i, 0)),
                      pl.BlockSpec((tr, N), lambda i, cr: (i, 0))],
            out_specs=pl.BlockSpec((tr, N), lambda i, cr: (i, 0))),
        out_shape=jax.ShapeDtypeStruct((M, N), out_dtype),
        compiler_params=_params(("parallel",)),
    )(c1, g, other)


def _reduce_scatter(gs, gs_d2d, modes, dims, wire):
    x, y, c = lax.axis_index("x"), lax.axis_index("y"), lax.axis_index("c")
    c1 = jnp.reshape(c, (1,)).astype(jnp.int32)
    n1, _, _ = _partners(x, y, c)
    me1, next1 = (jnp.reshape(_chip(ch), (1,)).astype(jnp.int32) for ch in ((x, y), n1))
    flat = lambda a, lead: a.reshape(a.shape[:lead] + (-1, a.shape[-1]))
    others = _swap_layers(gs_d2d)
    hs = [_add_own(flat(g, 1), flat(o, 0), c1, wire[k], f"rs_add_own_{k}").reshape(o.shape)
          for k, (g, o) in enumerate(zip(gs, others))]
    got = _scatter_pairs(hs, modes, dims)
    mine = [_add_pair(hs[k], got[k], 0, modes[k], dims[k], me1, F32, f"rs_add_pair_mine_{k}") for k in range(len(hs))]
    pass_on = [_add_pair(hs[k], got[k], 1, modes[k], dims[k], next1, wire[k], f"rs_add_pair_next_{k}")
               for k in range(len(hs))]
    last = _scatter_last(pass_on)
    fs = [_add_last(mine[k], last[k], c1, f"rs_add_last_{k}") for k in range(len(hs))]
    return _join_layers(fs)


def _allreduce_small(v):
    R, C = v.shape

    def body(v_ref, sum_ref, all_ref, send_sems, recv_sems):
        x, y, c, _ = _place()
        me = 4 * x + 2 * y + c
        rows = lambda d: all_ref.at[pl.ds(pl.multiple_of(d * R, 8), R), :]

        def peer(k):
            flip = lambda bit, v: (1 - v) if ((k + 1) >> bit) & 1 else v
            return flip(2, x), flip(1, y), flip(0, c)

        outs = [_rcopy(v_ref, rows(me), send_sems, recv_sems, k, peer(k)) for k in range(7)]
        for cp in outs:
            cp.start()
        all_ref[pl.ds(pl.multiple_of(me * R, 8), R), :] = v_ref[...]
        for k in range(7):
            px, py, pc = peer(k)
            blk = rows(4 * px + 2 * py + pc)
            _rcopy(blk, blk, send_sems, recv_sems, k, (x, y, c)).wait_recv()
        for cp in outs:
            cp.wait_send()
        tot = all_ref[0:R, :]
        for d in range(1, 8):
            tot = tot + all_ref[d * R:(d + 1) * R, :]
        sum_ref[...] = tot

    vm = pl.BlockSpec(memory_space=pltpu.VMEM)
    return pl.pallas_call(
        body, name="allreduce_small", in_specs=[vm], out_specs=[vm, vm],
        out_shape=[jax.ShapeDtypeStruct((R, C), F32), jax.ShapeDtypeStruct((8 * R, C), F32)],
        scratch_shapes=[pltpu.SemaphoreType.DMA((7,)), pltpu.SemaphoreType.DMA((7,))],
    )(v)[0]


def _size(shape):
    n = 1
    for d in shape:
        n *= d
    return n


def _pack(pieces, dtype):
    flat = jnp.concatenate([p.astype(dtype).reshape(-1) for p in pieces])
    rows = -(-flat.shape[0] // (PACK_COLS * 16)) * 16
    return jnp.pad(flat, (0, rows * PACK_COLS - flat.shape[0])).reshape(rows, PACK_COLS)


def _unpack(buf, shapes):
    flat = buf.reshape(-1)
    out, pos = [], 0
    for s in shapes:
        n = _size(s)
        out.append(flat[pos:pos + n].reshape(s))
        pos += n
    return out


def _small_piece(name, arr, l):
    if name == "meta_tokens":
        return arr[l * (N_META // DEPTH):(l + 1) * (N_META // DEPTH)]
    return arr[l]


def _prep_w_in(w_in4):
    w_in = jnp.concatenate([w_in4[t] for t in range(4)], axis=1)
    col = lambda a, n: w_in[:, _R[a]:_R[a] + n]
    main = jnp.concatenate([col("qa", 3072), col("scb", 3072), col("qc", 3072), col("ga", 6144)], axis=1)
    zpad = lambda n: jnp.zeros((D_MODEL, n), w_in.dtype)
    side = jnp.concatenate([col("fa", 8), zpad(LANES - 8), col("glr", GLA_RANK), zpad(LANES - GLA_RANK)], axis=1)
    return main.astype(BF16), side.astype(BF16)


def _w_in_cols(dmain, dside, lo, hi):
    segs = ((0, _R["fa"], dmain, 0), (_R["fa"], _R["scb"], dside, 0), (_R["scb"], _R["glr"], dmain, SCB),
            (_R["glr"], _R["ga"], dside, LANES), (_R["ga"], N_IN, dmain, GA))
    parts = [src[..., off + max(a, lo) - a:off + min(b, hi) - a] for a, b, src, off in segs if max(a, lo) < min(b, hi)]
    return jnp.concatenate(parts, axis=-1)


def _pad_rows(a, rows):
    return jnp.pad(a.astype(F32), ((0, rows - a.shape[0]), (0, 0)))


def _row2(v):
    return v.reshape(1, -1).astype(F32)


def _layer_fwd(h, p, rep, l, Lp, tm, ta):
    tag = lambda s: f"{s}_l{l}"
    g1, g2 = _row2(rep["norm1_g"][l]), _row2(rep["norm2_g"][l])
    bf = jnp.pad(_row2(rep["fox_b_f"][l]), ((0, 0), (0, LANES - FOX_HEADS)))
    gate_b, b_g, gnorm = _row2(rep["gate_b"][l]), _row2(rep["gla_b_g"][l]), _row2(rep["gla_norm_g"][l])
    xn, xn_t = _rms_fwd(h, g1, tag("rms1_fwd"), Lp)
    main = _mm(xn, p["main"][l], "nn", BF16, tag("proj_main"))
    side = _mm(xn, p["side"][l], "nn", F32, tag("proj_side"))
    c = _fox_gate_fwd(side, bf, Lp)
    c_t = c[:, :FOX_HEADS].T
    c_col, c_row = c_t[:, :, None], c_t[:, None, :]
    oa, ox, lse = _fox_fwd(main, c_col, c_row, Lp, ta)
    ya = _mm(oa, p["w_a_o"], "nn", BF16, tag("ya"), b_lead=l)
    ub = _sconv_fwd(main, p["conv_w"][l], Lp, tm)
    yb = _mm(ub, p["w_b_o"], "nn", BF16, tag("yb"), b_lead=l)
    glr = Row(side, LANES, lambda g: 1)
    (logg,) = _rw_fwd(tag("logg_fwd"), _f_logg, [glr], [Const(p["w_g2"][l]), Const(b_g)], [(512, F32)], Lp, tm)
    oc, states = _gla_fwd(main, logg, Lp)
    rc = Row(main, GLA_DV, lambda g: RC // GLA_DV + g)
    gn = Const(gnorm, (1, GLA_DV), lambda g: (0, g))
    (uc,) = _rw_fwd(tag("gla_post_fwd"), _f_gla_post, [Row(oc, GLA_DV), rc], [gn], [(GLA_DV, BF16)], Lp, tm,
                    G=GLA_HEADS)
    yc = _mm(uc, p["w_c_o"], "nn", BF16, tag("yc"), b_lead=l)
    cw = 512
    G = D_MODEL // cw
    mrows = [Row(ya, cw), Row(yb, cw), Row(yc, cw), Row(main, cw, lambda g: GA // cw + g),
             Row(main, cw, lambda g: GB // cw + g), Row(main, cw, lambda g: GC // cw + g)]
    mconsts = [Const(gate_b, (1, cw), lambda g, k=k: (0, k * G + g)) for k in range(3)]
    (mix,) = _rw_fwd(tag("merge_fwd"), _f_merge, mrows, mconsts, [(cw, BF16)], Lp, tm, G=G)
    h1 = _mm(mix, p["w_o"], "nn", F32, tag("h1"), add=h, b_lead=l)
    xn2, xn2_t = _rms_fwd(h1, g2, tag("rms2_fwd"), Lp)
    up = _mm(xn2, p["w_up"], "nn", BF16, tag("up"), b_lead=l)
    act, act_t = _mlp_act_fwd(up, p["mlp_conv_w"][l], Lp, tm)
    h2 = _mm(act, p["w_down"], "nn", F32, tag("h2"), add=h1, b_lead=l)
    res = dict(h=h, xn=xn, xn_t=xn_t, xn2_t=xn2_t, act_t=act_t, main=main, side=side, c_col=c_col, c_row=c_row, oa=oa, ox=ox, lse=lse, ya=ya, ub=ub, yb=yb,
               logg=logg, oc=oc, states=states, uc=uc, yc=yc, mix=mix, h1=h1, xn2=xn2, up=up, act=act,
               g1=g1, g2=g2, bf=bf, gate_b=gate_b, b_g=b_g, gnorm=gnorm)
    return h2, res


def _layer_bwd(dh2, p, r, l, Lp, tm, ta, big):
    tag = lambda s: f"{s}_l{l}"
    g = {}

    def wgrad(name, a, b, mode="tn"):
        big[name] = _mm(a, b, mode, F32, tag("d_" + name), slot=(big.get(name), l))

    wgrad("w_down", r["act_t"], dh2, "nn")
    dact = _mm(dh2, p["w_down"], "nt", BF16, tag("d_act"), b_lead=l)
    dgate, dval, dwg, dwu = _mlp_act_bwd(r["up"], p["mlp_conv_w"][l], dact, Lp, tm)
    g["mlp_conv_w"] = jnp.concatenate([dwg[:3], dwu[:3]], axis=1)
    dup = jnp.concatenate([dgate, dval], axis=1)
    wgrad("w_up", r["xn2_t"], dup, "nn")
    dxn2 = _mm(dup, p["w_up"], "nt", F32, tag("d_xn2"), b_lead=l)
    (dh1,), (dg2,) = _rw_bwd(tag("rms2_bwd"), _f_rms, [Row(r["h1"], D_MODEL)], [Const(r["g2"])],
                             [Row(dxn2, D_MODEL)], [F32], [dh2], Lp, BLOCK)
    g["norm2_g"] = dg2[0]
    wgrad("w_o", r["mix"], dh1)
    dmix = _mm(dh1, p["w_o"], "nt", BF16, tag("d_mix"), b_lead=l)
    cw = 512
    G = D_MODEL // cw
    main = r["main"]
    mrows = [Row(r["ya"], cw), Row(r["yb"], cw), Row(r["yc"], cw), Row(main, cw, lambda g_: GA // cw + g_),
             Row(main, cw, lambda g_: GB // cw + g_), Row(main, cw, lambda g_: GC // cw + g_)]
    mconsts = [Const(r["gate_b"], (1, cw), lambda g_, k=k: (0, k * G + g_)) for k in range(3)]
    (dya, dyb, dyc, dga, dgb, dgc), dbs = _rw_bwd(tag("merge_bwd"), _f_merge, mrows, mconsts, [Row(dmix, cw)],
                                                  [BF16] * 6, [None] * 6, Lp, tm, G=G)
    g["gate_b"] = jnp.concatenate([dbs[k][0, k * D_MODEL:(k + 1) * D_MODEL] for k in range(3)])
    wgrad("w_a_o", r["oa"], dya)
    doa = _mm(dya, p["w_a_o"], "nt", BF16, tag("d_oa"), b_lead=l)
    wgrad("w_b_o", r["ub"], dyb)
    dub = _mm(dyb, p["w_b_o"], "nt", BF16, tag("d_ub"), b_lead=l)
    wgrad("w_c_o", r["uc"], dyc)
    duc = _mm(dyc, p["w_c_o"], "nt", BF16, tag("d_uc"), b_lead=l)
    delta = _fox_delta(r["ox"], doa, Lp, ta)
    dq, dk, dv, dck = _fox_bwd(main, r["c_col"], r["c_row"], r["lse"], delta, doa, Lp, ta)
    dc = jnp.pad(dck[:, 0, :].T, ((0, 0), (0, LANES - FOX_HEADS)))
    dfa, dbf = _fox_gate_bwd(r["side"], r["bf"], dc, Lp)
    g["fox_b_f"] = dbf[0, :FOX_HEADS]
    dscb, dscc, dsch, dcw = _sconv_bwd(main, p["conv_w"][l], dub, Lp, tm)
    g["conv_w"] = dcw[:3]
    rc = Row(main, GLA_DV, lambda g_: RC // GLA_DV + g_)
    gn = Const(r["gnorm"], (1, GLA_DV), lambda g_: (0, g_))
    (doc, drc), (dgn,) = _rw_bwd(tag("gla_post_bwd"), _f_gla_post, [Row(r["oc"], GLA_DV), rc], [gn],
                                 [Row(duc, GLA_DV)], [F32, BF16], [None, None], Lp, tm, G=GLA_HEADS)
    g["gla_norm_g"] = dgn[0]
    dqc, dkc, dvc, dlogg = _gla_bwd(main, r["logg"], r["states"], doc, Lp)
    glr = Row(r["side"], LANES, lambda g_: 1)
    (dglr,), (dwg2, dbg) = _rw_bwd(tag("logg_bwd"), _f_logg, [glr], [Const(p["w_g2"][l]), Const(r["b_g"])],
                                   [Row(dlogg, 512)], [F32], [None], Lp, tm)
    g["gla_w_g2"] = dwg2[:GLA_RANK]
    g["gla_b_g"] = dbg[0]
    dmain = jnp.concatenate([dq, dk, dv, dscb, dscc, dsch, dqc, dkc, dvc, drc, dga, dgb, dgc], axis=1)
    dside = jnp.concatenate([dfa, dglr], axis=1)
    wgrad("main", r["xn_t"], dmain, "nn")
    wgrad("side", r["xn"], dside)
    dxn = _mm(dmain, p["main"][l], "nt", F32, tag("d_xn_main"))
    dxn = _mm(dside, p["side"][l], "nt", F32, tag("d_xn_side"), add=dxn)
    (dh,), (dg1,) = _rw_bwd(tag("rms1_bwd"), _f_rms, [Row(r["h"], D_MODEL)], [Const(r["g1"])], [Row(dxn, D_MODEL)],
                            [F32], [dh1], Lp, BLOCK)
    g["norm1_g"] = dg1[0]
    return dh, g


def _local_step(x, target, meta, p, rep):
    seq = x.shape[0]
    Lp = PAD + N_META + seq
    tm = _pick(Lp, (640, 384, 128))
    ta = tm
    h = jnp.concatenate([jnp.zeros((PAD, D_MODEL), F32), meta.astype(F32), x], axis=0)
    saved = []
    for l in range(DEPTH):
        h, res = _layer_fwd(h, p, rep, l, Lp, tm, ta)
        saved.append(res)
    loss, dh, dgf = _loss_head(h, _row2(rep["final_norm_g"]), target, Lp)
    big, small = {}, [None] * DEPTH
    for l in reversed(range(DEPTH)):
        dh, small[l] = _layer_bwd(dh, p, saved[l], l, Lp, tm, ta, big)
    return loss[0, 0], dh[BLOCK:], dh[PAD:BLOCK], big, small, dgf[0]


def kernel(x, meta_tokens, norm1_g, w_in, fox_b_f, gate_b, conv_w, gla_w_g2, gla_b_g, gla_norm_g, w_a_o, w_b_o, w_c_o, w_o, norm2_g, w_up, mlp_conv_w, w_down, final_norm_g, loss_target, m_meta_tokens, m_norm1_g, m_w_in, m_fox_b_f, m_gate_b, m_conv_w, m_gla_w_g2, m_gla_b_g, m_gla_norm_g, m_w_a_o, m_w_b_o, m_w_c_o, m_w_o, m_norm2_g, m_w_up, m_mlp_conv_w, m_w_down, m_final_norm_g, v_meta_tokens, v_norm1_g, v_w_in, v_fox_b_f, v_gate_b, v_conv_w, v_gla_w_g2, v_gla_b_g, v_gla_norm_g, v_w_a_o, v_w_b_o, v_w_c_o, v_w_o, v_norm2_g, v_w_up, v_mlp_conv_w, v_w_down, v_final_norm_g):
    given = dict(locals())
    weights = {n: given[n] for n in WEIGHT_ORDER}
    rep = {n: weights[n] for n, _ in REPLICATED}
    big_names = [n for n, _ in BIG]
    big_modes = [m for _, m in BIG] + ["stack"]
    small_shapes = [(s[0], s[1] // 4) for _, s in SMALL]
    exact = [k for k, (n, _) in enumerate(SMALL) if n in GATHER_F32]

    def small_wire(l):
        ws = [_small_piece(n, weights[n], l) for n, _ in SMALL]
        his = [w.astype(BF16) for w in ws]
        return his + [(ws[k] - his[k].astype(F32)).astype(BF16) for k in exact]

    shards = [weights[n].astype(BF16) for n in big_names] + [jnp.stack([_pack(small_wire(l), BF16) for l in range(DEPTH)])]
    gathered = _gather_weights(shards, big_modes)
    gw = dict(zip(big_names, gathered[:-1]))
    p = {n: gw[n] for n in big_names if n != "w_in"}
    p["main"], p["side"] = zip(*[_prep_w_in(gw["w_in"][l]) for l in range(DEPTH)])
    small_full = []
    for l in range(DEPTH):
        per_chip = [_unpack(gathered[-1][l, t], small_shapes + [small_shapes[k] for k in exact]) for t in range(4)]
        full = [jnp.concatenate([per_chip[t][k] for t in range(4)], axis=1).astype(F32) for k in range(len(per_chip[0]))]
        for e, k in enumerate(exact):
            full[k] = full[k] + full[len(SMALL) + e]
        small_full.append(dict(zip([n for n, _ in SMALL], full[:len(SMALL)])))
    p["conv_w"] = [_pad_rows(s["conv_w"], 8) for s in small_full]
    p["mlp_conv_w"] = [_pad_rows(s["mlp_conv_w"], 8) for s in small_full]
    p["w_g2"] = [_pad_rows(s["gla_w_g2"], LANES) for s in small_full]
    meta_full = jnp.concatenate([s["meta_tokens"] for s in small_full], axis=0)

    loss, grad_x, grad_meta, big, small, d_final = _local_step(x[0], loss_target[0], meta_full, p, rep)
    loss = lax.psum(loss, ("x", "y", "c"))

    d_w_in = jnp.stack([_w_in_cols(big["main"][0], big["side"][0], t * (N_IN // 4), (t + 1) * (N_IN // 4))
                        for t in range(4)], axis=1)
    big["w_in"] = (d_w_in, d_w_in)
    for l in range(DEPTH):
        small[l]["meta_tokens"] = grad_meta[l * (N_META // DEPTH):(l + 1) * (N_META // DEPTH)]
    shard_of = lambda a, t: lax.slice_in_dim(a, t * (a.shape[1] // 4), (t + 1) * (a.shape[1] // 4), axis=1)
    small_g = jnp.stack([jnp.stack([_pack([shard_of(small[l][n], t) for n, _ in SMALL], F32) for t in range(4)])
                         for l in range(DEPTH)])
    dims = [shards[k].shape[1:] for k in range(len(BIG))] + [small_g.shape[2:]]
    summed = _reduce_scatter([big[n][0] for n in big_names] + [small_g], [big[n][1] for n in big_names] + [small_g],
                             big_modes, dims, [BF16] * len(BIG) + [F32])
    gout = dict(zip(big_names, summed[:-1]))
    pieces = [_unpack(summed[-1][l], small_shapes) for l in range(DEPTH)]
    for k, (n, _) in enumerate(SMALL):
        per_layer = [pieces[l][k] for l in range(DEPTH)]
        gout[n] = jnp.concatenate(per_layer, axis=0) if n == "meta_tokens" else jnp.stack(per_layer)

    rep_g = {n: (d_final if n == "final_norm_g" else jnp.stack([small[l][n] for l in range(DEPTH)])) for n, _ in REPLICATED}
    flat = jnp.concatenate([rep_g[n].astype(F32).reshape(-1) for n, _ in REPLICATED])
    rrows = -(-flat.shape[0] // (PACK_COLS * 8)) * 8
    summed_small = _allreduce_small(jnp.pad(flat, (0, rrows * PACK_COLS - flat.shape[0])).reshape(rrows, PACK_COLS))
    pos = 0
    for n, shape in REPLICATED:
        gout[n] = summed_small.reshape(-1)[pos:pos + _size(shape)].reshape(shape)
        pos += _size(shape)

    deltas, new_m, new_v = {}, {}, {}
    for n in WEIGHT_ORDER:
        gout[n], deltas[n], new_m[n], new_v[n] = _adamw(weights[n], gout[n], given["m_" + n], given["v_" + n],
                                                        "adamw_" + n)
    return (loss, grad_x[None], *[gout[n] for n in WEIGHT_ORDER], *[deltas[n] for n in WEIGHT_ORDER],
            *[new_m[n] for n in WEIGHT_ORDER], *[new_v[n] for n in WEIGHT_ORDER])
```

```python
import functools

import jax
import jax.numpy as jnp
from jax import lax
from jax.experimental import pallas as pl
from jax.experimental.pallas import tpu as pltpu

F32, BF16 = jnp.float32, jnp.bfloat16
HIGHEST = lax.Precision.HIGHEST
MESH = pl.DeviceIdType.MESH

N_META = 16
BLOCK = 128
LANES = 128
PAD = BLOCK - N_META
EPS = 1e-6
NEG = -1e30
HALO = 16
VMEM_LIMIT = 56 * 1024 * 1024
ADAM_BLOCK_BYTES = 1 << 20
EW_BLOCK_BYTES = 3 << 19

D_MODEL = 2048
FOX_HEADS, FOX_HD = 8, 128
FOX_WIDTH = FOX_HEADS * FOX_HD
CONV_CH = 1024
GLA_HEADS, GLA_DK, GLA_DV, GLA_RANK, GLA_TAU = 4, 128, 256, 16, 16.0
GLA_SUB = 32
D_FF = 5632
N_IN = 15384
DEPTH = 2

_R = dict(qa=0, ka=1024, va=2048, fa=3072, scb=3080, scc=4104, sch=5128, qc=6152, kc=6664,
          vc=7176, rc=8200, glr=9224, ga=9240, gb=11288, gc=13336)
QA, KA, VA, SCB, SCC, SCH, QC, KC, VC, RC, GA, GB, GC = (
    0, 1024, 2048, 3072, 4096, 5120, 6144, 6656, 7168, 8192, 9216, 11264, 13312)
N_MAIN = 15360
N_SIDE = 256

ADAM_LR, ADAM_B1, ADAM_B2, ADAM_EPS, ADAM_WD, ADAM_STEP = 0.001, 0.9, 0.999, 1e-08, 0.01, 10

BIG = (("w_in", "stack"), ("w_a_o", "cols"), ("w_b_o", "cols"), ("w_c_o", "cols"), ("w_o", "rows"), ("w_up", "cols"),
       ("w_down", "rows"))
SMALL = (("conv_w", (3, CONV_CH)), ("mlp_conv_w", (3, 2 * D_FF)), ("gla_w_g2", (GLA_RANK, 512)),
         ("meta_tokens", (N_META // DEPTH, D_MODEL)))
REPLICATED = (("norm1_g", (2, D_MODEL)), ("fox_b_f", (2, 8)), ("gate_b", (2, 3 * D_MODEL)), ("gla_b_g", (2, 512)),
              ("gla_norm_g", (2, 1024)), ("norm2_g", (2, D_MODEL)), ("final_norm_g", (D_MODEL,)))
WEIGHT_ORDER = ("meta_tokens", "norm1_g", "w_in", "fox_b_f", "gate_b", "conv_w", "gla_w_g2", "gla_b_g",
                "gla_norm_g", "w_a_o", "w_b_o", "w_c_o", "w_o", "norm2_g", "w_up", "mlp_conv_w", "w_down",
                "final_norm_g")
PACK_COLS = 1024
GATHER_F32 = ("conv_w", "mlp_conv_w", "meta_tokens")


def _pick(n, cands):
    for c in cands:
        if n % c == 0:
            return c
    return n


def _params(sem):
    return pltpu.CompilerParams(dimension_semantics=sem, vmem_limit_bytes=VMEM_LIMIT)


def _sigmoid(x):
    return jax.nn.sigmoid(x)


def _log_sigmoid(x):
    return jnp.minimum(x, 0.0) - jnp.log(1.0 + jnp.exp(-jnp.abs(x)))


def _mm(a, b, mode, out_dtype, name, add=None, b_lead=None, slot=None):
    bshape = b.shape if b_lead is None else b.shape[1:]
    if mode == "nn":
        (M, K), (K2, N) = a.shape, bshape
    elif mode == "nt":
        (M, K), (N, K2) = a.shape, bshape
    else:
        (K, M), (K2, N) = a.shape, bshape
    assert K == K2, (name, a.shape, b.shape)
    if mode == "tn":
        tm = _pick(M, (2048, 1408, 1024, 512, 256, 128))
        tn = _pick(N, (1024, 512, 256, 128))
        tk = _pick(K, (640, 512, 384, 256, 128))
    else:
        tm = _pick(M, (1664, 2048, 1408, 1024, 640, 384, 128))
        wide = mode == "nt" and a.dtype == BF16 and slot is None and add is None
        tn = _pick(N, (1024, 512, 256, 128) if wide else (512, 256, 128))
        tk = K if K <= 2048 else _pick(K, (1664, 1408, 1024, 640, 512, 384, 256, 128))
    nk = K // tk
    dims = {"nn": (((1,), (0,)), ((), ())), "nt": (((1,), (1,)), ((), ())), "tn": (((0,), (0,)), ((), ()))}[mode]
    n_in = 2 + (add is not None) + (2 if slot is not None and slot[0] is not None else 0)

    def body(*refs):
        a_ref, b_ref = refs[:2]
        add_ref = refs[2] if add is not None else None
        o_ref, acc = refs[n_in], refs[-1]
        o16_ref = refs[n_in + 1] if slot is not None else None
        k = pl.program_id(2)

        @pl.when(k == 0)
        def _():
            acc[...] = jnp.zeros_like(acc)

        acc[...] += lax.dot_general(a_ref[...].astype(BF16), b_ref[...].astype(BF16), dims,
                                    preferred_element_type=F32)

        @pl.when(k == nk - 1)
        def _():
            r = acc[...]
            if add is not None:
                r = r + add_ref[...].astype(F32)
            o_ref[...] = r.astype(o_ref.dtype)
            if o16_ref is not None:
                o16_ref[...] = r.astype(BF16)

    a_spec = {"nn": pl.BlockSpec((tm, tk), lambda i, j, k: (i, k)),
              "nt": pl.BlockSpec((tm, tk), lambda i, j, k: (i, k)),
              "tn": pl.BlockSpec((tk, tm), lambda i, j, k: (k, i))}[mode]
    b_blk, b_idx = {"nn": ((tk, tn), lambda i, j, k: (k, j)),
                    "nt": ((tn, tk), lambda i, j, k: (j, k)),
                    "tn": ((tk, tn), lambda i, j, k: (k, j))}[mode]
    if b_lead is None:
        b_spec = pl.BlockSpec(b_blk, b_idx)
    else:
        b_spec = pl.BlockSpec((None,) + b_blk, lambda i, j, k: (b_lead,) + b_idx(i, j, k))
    o_spec = pl.BlockSpec((tm, tn), lambda i, j, k: (i, j))
    ins, specs = [a, b], [a_spec, b_spec]
    if add is not None:
        ins.append(add)
        specs.append(o_spec)
    aliases = {}
    out_shape = jax.ShapeDtypeStruct((M, N), out_dtype)
    if slot is not None:
        bufs, l = slot
        o_spec = [pl.BlockSpec((None, tm, tn), lambda i, j, k: (l, i, j))] * 2
        out_shape = [jax.ShapeDtypeStruct((DEPTH, M, N), out_dtype), jax.ShapeDtypeStruct((DEPTH, M, N), BF16)]
        if bufs is not None:
            aliases = {len(ins): 0, len(ins) + 1: 1}
            ins.extend(bufs)
            specs.extend([pl.BlockSpec(memory_space=pl.ANY)] * 2)
    return pl.pallas_call(
        body, name=name, grid=(M // tm, N // tn, nk), in_specs=specs, out_specs=o_spec, out_shape=out_shape,
        scratch_shapes=[pltpu.VMEM((tm, tn), F32)], input_output_aliases=aliases,
        compiler_params=_params(("parallel", "parallel", "arbitrary")),
    )(*ins)


class Row:
    def __init__(self, arr, w, cb=None):
        self.arr, self.w, self.cb = arr, w, (cb if cb is not None else (lambda g: g))


class Const:
    def __init__(self, arr, shape=None, idx=None):
        self.arr = arr
        self.shape = shape if shape is not None else arr.shape
        self.idx = idx if idx is not None else (lambda g: (0,) * arr.ndim)


def _row_spec(r, tm):
    return pl.BlockSpec((tm, r.w), lambda g, i, r=r: (i, r.cb(g)))


def _const_spec(c):
    return pl.BlockSpec(c.shape, lambda g, i, c=c: c.idx(g))


def _valid_rows(i, tm):
    return (i * tm + lax.broadcasted_iota(jnp.int32, (tm, 1), 0)) >= PAD


def _rw_fwd(name, f, rows, consts, outs, Lp, tm, G=1):
    nr, nc = len(rows), len(consts)

    def body(*refs):
        i = pl.program_id(1)
        rv = [r[...].astype(F32) for r in refs[:nr]]
        cv = [r[...].astype(F32) for r in refs[nr:nr + nc]]
        res = f(_valid_rows(i, tm), *rv, *cv)
        for o_ref, v in zip(refs[nr + nc:], res):
            o_ref[...] = v.astype(o_ref.dtype)

    return pl.pallas_call(
        body, name=name, grid=(G, Lp // tm),
        in_specs=[_row_spec(r, tm) for r in rows] + [_const_spec(c) for c in consts],
        out_specs=[pl.BlockSpec((tm, w), lambda g, i: (i, g)) for w, _ in outs],
        out_shape=[jax.ShapeDtypeStruct((Lp, w * G), dt) for w, dt in outs],
        compiler_params=_params(("parallel", "arbitrary")),
    )(*[r.arr for r in rows], *[c.arr for c in consts])


def _rw_bwd(name, f, rows, consts, cts, drow_dtypes, adds, Lp, tm, G=1):
    nr, nc, nt = len(rows), len(consts), len(cts)
    want = [k for k, dt in enumerate(drow_dtypes) if dt is not None]
    add_k = [k for k in want if adds[k] is not None]

    def body(*refs):
        i = pl.program_id(1)
        pos = 0
        rv = [r[...].astype(F32) for r in refs[pos:pos + nr]]
        pos += nr
        cv = [r[...].astype(F32) for r in refs[pos:pos + nc]]
        pos += nc
        tv = [r[...].astype(F32) for r in refs[pos:pos + nt]]
        pos += nt
        av = {k: refs[pos + n][...].astype(F32) for n, k in enumerate(add_k)}
        pos += len(add_k)
        drow_refs = refs[pos:pos + len(want)]
        pos += len(want)
        dconst_refs = refs[pos:pos + nc]
        valid = _valid_rows(i, tm)
        _, vjp = jax.vjp(lambda *a: tuple(f(valid, *a)), *rv, *cv)
        grads = vjp(tuple(tv))
        for o_ref, k in zip(drow_refs, want):
            gk = grads[k]
            if k in av:
                gk = gk + av[k]
            o_ref[...] = gk.astype(o_ref.dtype)
        for n, o_ref in enumerate(dconst_refs):
            gc = grads[nr + n]

            @pl.when(i == 0)
            def _(o_ref=o_ref, gc=gc):
                o_ref[...] = gc

            @pl.when(i > 0)
            def _(o_ref=o_ref, gc=gc):
                o_ref[...] += gc

    out_row = lambda w: pl.BlockSpec((tm, w), lambda g, i: (i, g))
    res = pl.pallas_call(
        body, name=name, grid=(G, Lp // tm),
        in_specs=([_row_spec(r, tm) for r in rows] + [_const_spec(c) for c in consts]
                  + [_row_spec(r, tm) for r in cts] + [out_row(rows[k].w) for k in add_k]),
        out_specs=[out_row(rows[k].w) for k in want] + [_const_spec(c) for c in consts],
        out_shape=([jax.ShapeDtypeStruct((Lp, rows[k].w * G), drow_dtypes[k]) for k in want]
                   + [jax.ShapeDtypeStruct(c.arr.shape, F32) for c in consts]),
        compiler_params=_params(("parallel", "arbitrary")),
    )(*[r.arr for r in rows], *[c.arr for c in consts], *[r.arr for r in cts], *[adds[k] for k in add_k])
    drows = [None] * nr
    for n, k in enumerate(want):
        drows[k] = res[n]
    return drows, list(res[len(want):])


def _f_rms(valid, h, g):
    r = lax.rsqrt(jnp.mean(h * h, axis=-1, keepdims=True) + EPS)
    return (jnp.where(valid, h * r * g, 0.0),)


def _rms_fwd(h, g, name, Lp):
    t = BLOCK

    def body(h_ref, g_ref, o_ref, ot_ref):
        (y,) = _f_rms(_valid_rows(pl.program_id(0), t), h_ref[...], g_ref[...])
        o_ref[...] = y.astype(o_ref.dtype)
        ot_ref[...] = y.T.astype(ot_ref.dtype)

    return pl.pallas_call(
        body, name=name, grid=(Lp // t,),
        in_specs=[pl.BlockSpec((t, D_MODEL), lambda i: (i, 0)), pl.BlockSpec((1, D_MODEL), lambda i: (0, 0))],
        out_specs=[pl.BlockSpec((t, D_MODEL), lambda i: (i, 0)), pl.BlockSpec((D_MODEL, t), lambda i: (0, i))],
        out_shape=[jax.ShapeDtypeStruct((Lp, D_MODEL), BF16), jax.ShapeDtypeStruct((D_MODEL, Lp), BF16)],
        compiler_params=_params(("parallel",)),
    )(h, g)


def _f_logg(valid, glr, w, b):
    pre = jnp.dot(glr.astype(BF16), w.astype(BF16), preferred_element_type=F32) + b
    return (jnp.where(valid, _log_sigmoid(pre) / GLA_TAU, 0.0),)


def _f_gla_post(valid, oc, rc, g):
    y = oc * lax.rsqrt(jnp.mean(oc * oc, axis=-1, keepdims=True) + EPS) * g
    return (jnp.where(valid, rc * _sigmoid(rc) * y, 0.0),)


def _f_merge(valid, ya, yb, yc, ga, gb, gc, ba, bb, bc):
    mix = _sigmoid(ga + ba) * ya + _sigmoid(gb + bb) * yb + _sigmoid(gc + bc) * yc
    return (jnp.where(valid, mix, 0.0),)


def _fox_gate_fwd(side, bf, Lp):
    t = BLOCK
    n = Lp // t

    def body(s_ref, b_ref, c_ref, carry):
        i = pl.program_id(0)

        @pl.when(i == 0)
        def _():
            carry[...] = jnp.zeros_like(carry)

        lane = lax.broadcasted_iota(jnp.int32, (t, LANES), 1)
        ok = _valid_rows(i, t) & (lane < FOX_HEADS)
        logf = jnp.where(ok, _log_sigmoid(s_ref[...] + b_ref[...]), 0.0)
        tril = (lax.broadcasted_iota(jnp.int32, (t, t), 1) <= lax.broadcasted_iota(jnp.int32, (t, t), 0)).astype(F32)
        c = jnp.dot(tril, logf, precision=HIGHEST, preferred_element_type=F32) + carry[...]
        c_ref[...] = c
        carry[...] = c[t - 1:t, :]

    return pl.pallas_call(
        body, name="fox_gate_fwd", grid=(n,),
        in_specs=[pl.BlockSpec((t, LANES), lambda i: (i, 0)), pl.BlockSpec((1, LANES), lambda i: (0, 0))],
        out_specs=pl.BlockSpec((t, LANES), lambda i: (i, 0)),
        out_shape=jax.ShapeDtypeStruct((Lp, LANES), F32),
        scratch_shapes=[pltpu.VMEM((1, LANES), F32)],
        compiler_params=_params(("arbitrary",)),
    )(side, bf)


def _fox_gate_bwd(side, bf, dc, Lp):
    t = BLOCK
    n = Lp // t

    def body(s_ref, b_ref, dc_ref, dfa_ref, db_ref, carry):
        i = pl.program_id(0)

        @pl.when(i == 0)
        def _():
            carry[...] = jnp.zeros_like(carry)

        lane = lax.broadcasted_iota(jnp.int32, (t, LANES), 1)
        ok = _valid_rows(n - 1 - i, t) & (lane < FOX_HEADS)
        triu = (lax.broadcasted_iota(jnp.int32, (t, t), 1) >= lax.broadcasted_iota(jnp.int32, (t, t), 0)).astype(F32)
        dlogf = jnp.dot(triu, dc_ref[...], precision=HIGHEST, preferred_element_type=F32) + carry[...]
        carry[...] = dlogf[0:1, :]
        dpre = jnp.where(ok, dlogf * _sigmoid(-(s_ref[...] + b_ref[...])), 0.0)
        dfa_ref[...] = dpre
        part = jnp.sum(dpre, axis=0, keepdims=True)

        @pl.when(i == 0)
        def _():
            db_ref[...] = part

        @pl.when(i > 0)
        def _():
            db_ref[...] += part

    rev = lambda i: (n - 1 - i, 0)
    return pl.pallas_call(
        body, name="fox_gate_bwd", grid=(n,),
        in_specs=[pl.BlockSpec((t, LANES), rev), pl.BlockSpec((1, LANES), lambda i: (0, 0)),
                  pl.BlockSpec((t, LANES), rev)],
        out_specs=[pl.BlockSpec((t, LANES), rev), pl.BlockSpec((1, LANES), lambda i: (0, 0))],
        out_shape=[jax.ShapeDtypeStruct((Lp, LANES), F32), jax.ShapeDtypeStruct((1, LANES), F32)],
        scratch_shapes=[pltpu.VMEM((1, LANES), F32)],
        compiler_params=_params(("arbitrary",)),
    )(side, bf, dc)


def _fox_key_bias(cq_ref, ck_ref, j, t):
    col = j * t + lax.broadcasted_iota(jnp.int32, (1, t), 1)
    return jnp.where(col >= PAD, ck_ref[...] - cq_ref[0:1, :], -NEG)


def _fox_s(q, k, bias, diagonal, t):
    s = lax.dot_general(q, k, (((1,), (1,)), ((), ())), preferred_element_type=F32) * (FOX_HD ** -0.5) - bias
    if diagonal:
        causal = lax.broadcasted_iota(jnp.int32, (t, t), 1) <= lax.broadcasted_iota(jnp.int32, (t, t), 0)
        s = jnp.where(causal, s, NEG)
    return s


def _fox_fwd(main, c_col, c_row, Lp, t):
    n = Lp // t
    qb, kb, vb = QA // FOX_HD, KA // FOX_HD, VA // FOX_HD
    first = (t // LANES + 1) // 2 * LANES
    halves = ((0, first), (first, t - first)) if t > LANES else ((0, t),)

    def body(q_ref, k_ref, v_ref, cq_ref, ck_ref, o_ref, ox_ref, lse_ref, m_s, l_s, acc):
        i, j = pl.program_id(1), pl.program_id(2)

        @pl.when(j == 0)
        def _():
            m_s[...] = jnp.full_like(m_s, NEG)
            l_s[...] = jnp.zeros_like(l_s)
            acc[...] = jnp.zeros_like(acc)

        def update(diagonal):
            bias = _fox_key_bias(cq_ref, ck_ref, j, t)
            for lo, w in halves:
                kk, vv = k_ref[lo:lo + w, :], v_ref[lo:lo + w, :]
                s = lax.dot_general(q_ref[...], kk, (((1,), (1,)), ((), ())),
                                    preferred_element_type=F32) * (FOX_HD ** -0.5) - bias[:, lo:lo + w]
                if diagonal:
                    causal = (lo + lax.broadcasted_iota(jnp.int32, (t, w), 1)) <= lax.broadcasted_iota(jnp.int32, (t, w), 0)
                    s = jnp.where(causal, s, NEG)
                m_new = jnp.maximum(m_s[...], jnp.max(s, axis=1, keepdims=True))
                alpha = jnp.exp(m_s[...] - m_new)
                p = jnp.exp(s - m_new)
                l_s[...] = alpha * l_s[...] + jnp.sum(p, axis=1, keepdims=True)
                p_hi = p.astype(BF16)
                p_lo = (p - p_hi.astype(F32)).astype(BF16)
                pv = jnp.dot(p_hi, vv, preferred_element_type=F32) + jnp.dot(p_lo, vv, preferred_element_type=F32)
                acc[...] = alpha * acc[...] + pv
                m_s[...] = m_new

        @pl.when(j < i)
        def _():
            update(False)

        @pl.when(j == i)
        def _():
            update(True)
            o = jnp.where(_valid_rows(i, t), acc[...] / l_s[...], 0.0)
            o_ref[...] = o.astype(o_ref.dtype)
            ox_ref[...] = o
            lse_ref[...] = m_s[...] + jnp.log(l_s[...])

    kv = lambda base: pl.BlockSpec((t, FOX_HD), lambda h, i, j: (jnp.minimum(j, i), base + h))
    return pl.pallas_call(
        body, name="fox_fwd", grid=(FOX_HEADS, n, n),
        in_specs=[pl.BlockSpec((t, FOX_HD), lambda h, i, j: (i, qb + h)), kv(kb), kv(vb),
                  pl.BlockSpec((None, t, 1), lambda h, i, j: (h, i, 0)),
                  pl.BlockSpec((None, 1, t), lambda h, i, j: (h, 0, jnp.minimum(j, i)))],
        out_specs=[pl.BlockSpec((t, FOX_HD), lambda h, i, j: (i, h)), pl.BlockSpec((t, FOX_HD), lambda h, i, j: (i, h)),
                   pl.BlockSpec((None, t, 1), lambda h, i, j: (h, i, 0))],
        out_shape=[jax.ShapeDtypeStruct((Lp, FOX_WIDTH), BF16), jax.ShapeDtypeStruct((Lp, FOX_WIDTH), F32),
                   jax.ShapeDtypeStruct((FOX_HEADS, Lp, 1), F32)],
        scratch_shapes=[pltpu.VMEM((t, 1), F32), pltpu.VMEM((t, 1), F32), pltpu.VMEM((t, FOX_HD), F32)],
        compiler_params=_params(("parallel", "parallel", "arbitrary")),
    )(main, main, main, c_col, c_row)


def _fox_p_dp(q_ref, k_ref, v_ref, cq_ref, ck_ref, lse_ref, do_ref, j, diagonal, t):
    s = _fox_s(q_ref[...], k_ref[...], _fox_key_bias(cq_ref, ck_ref, j, t), diagonal, t)
    p = jnp.exp(s - lse_ref[...])
    dp = lax.dot_general(do_ref[...], v_ref[...], (((1,), (1,)), ((), ())), preferred_element_type=F32)
    return p, dp


def _fox_delta(ox, doa, Lp, t):
    n = Lp // t

    def body(o_ref, do_ref, dl_ref):
        dl_ref[...] = jnp.sum(o_ref[...] * do_ref[...].astype(F32), axis=1, keepdims=True)

    blk = pl.BlockSpec((t, FOX_HD), lambda h, i: (i, h))
    return pl.pallas_call(
        body, name="fox_delta", grid=(FOX_HEADS, n), in_specs=[blk, blk],
        out_specs=pl.BlockSpec((None, t, 1), lambda h, i: (h, i, 0)),
        out_shape=jax.ShapeDtypeStruct((FOX_HEADS, Lp, 1), F32),
        compiler_params=_params(("parallel", "parallel")),
    )(ox, doa)


def _fox_bwd(main, c_col, c_row, lse, delta, doa, Lp, t):
    n = Lp // t
    qb, kb, vb = QA // FOX_HD, KA // FOX_HD, VA // FOX_HD
    scale = FOX_HD ** -0.5

    def body(q_ref, k_ref, v_ref, cq_ref, ck_ref, lse_ref, dl_ref, do_ref, dq_ref, dk_ref, dv_ref, dck_ref,
             dq_s, dk_s, dv_s, dc_s):
        j, i = pl.program_id(1), pl.program_id(2)

        @pl.when(jnp.logical_and(j == 0, i == 0))
        def _():
            dq_s[...] = jnp.zeros_like(dq_s)

        @pl.when(i == 0)
        def _():
            dk_s[...] = jnp.zeros_like(dk_s)
            dv_s[...] = jnp.zeros_like(dv_s)
            dc_s[...] = jnp.zeros_like(dc_s)

        def sweep(diagonal):
            p, dp = _fox_p_dp(q_ref, k_ref, v_ref, cq_ref, ck_ref, lse_ref, do_ref, j, diagonal, t)
            ds = p * (dp - dl_ref[...])
            dsb = ds.astype(BF16)
            tn = (((0,), (0,)), ((), ()))
            dv_s[...] += lax.dot_general(p.astype(BF16), do_ref[...], tn, preferred_element_type=F32)
            dk_s[...] += lax.dot_general(dsb, q_ref[...], tn, preferred_element_type=F32)
            dc_s[...] -= jnp.sum(ds, axis=0, keepdims=True)
            rows = pl.ds(pl.multiple_of(i * t, t), t)
            dq_s[rows, :] += jnp.dot(dsb, k_ref[...], preferred_element_type=F32)

        @pl.when(i > j)
        def _():
            sweep(False)

        @pl.when(i == j)
        def _():
            sweep(True)

        @pl.when(i == n - 1)
        def _():
            dk_ref[...] = (dk_s[...] * scale).astype(dk_ref.dtype)
            dv_ref[...] = dv_s[...].astype(dv_ref.dtype)
            dck_ref[...] = dc_s[...]

        @pl.when(jnp.logical_and(j == n - 1, i == n - 1))
        def _():
            dq_ref[...] = (dq_s[...] * scale).astype(dq_ref.dtype)

    qrow = lambda base: pl.BlockSpec((t, FOX_HD), lambda h, j, i: (jnp.maximum(i, j), base + h))
    kv = lambda base: pl.BlockSpec((t, FOX_HD), lambda h, j, i: (j, base + h))
    col = pl.BlockSpec((None, t, 1), lambda h, j, i: (h, jnp.maximum(i, j), 0))
    row = pl.BlockSpec((None, 1, t), lambda h, j, i: (h, 0, j))
    wide = jax.ShapeDtypeStruct((Lp, FOX_WIDTH), BF16)
    return pl.pallas_call(
        body, name="fox_bwd", grid=(FOX_HEADS, n, n),
        in_specs=[qrow(qb), kv(kb), kv(vb), col, row, col, col, qrow(0)],
        out_specs=[pl.BlockSpec((Lp, FOX_HD), lambda h, j, i: (0, h)), kv(0), kv(0), row],
        out_shape=[wide, wide, wide, jax.ShapeDtypeStruct((FOX_HEADS, 1, Lp), F32)],
        scratch_shapes=[pltpu.VMEM((Lp, FOX_HD), F32), pltpu.VMEM((t, FOX_HD), F32), pltpu.VMEM((t, FOX_HD), F32),
                        pltpu.VMEM((1, t), F32)],
        compiler_params=_params(("parallel", "arbitrary", "arbitrary")),
    )(main, main, main, c_col, c_row, lse, delta, doa)


def _shift_down(x, n):
    return pltpu.roll(x, n, 0)


def _shift_up(x, n):
    return pltpu.roll(x, x.shape[0] - n, 0)


def _prev_spec(tm, ct, cb):
    return pl.BlockSpec((HALO, ct), lambda g, i: (jnp.maximum(i * (tm // HALO) - 1, 0), cb(g)))


def _next_spec(tm, ct, cb, nrows):
    last = nrows // HALO - 1
    return pl.BlockSpec((HALO, ct), lambda g, i: (jnp.minimum((i + 1) * (tm // HALO), last), cb(g)))


def _cur_spec(tm, ct, cb):
    return pl.BlockSpec((tm, ct), lambda g, i: (i, cb(g)))


def _wrow(w_ref, k):
    return w_ref[k:k + 1, :]


def _rows3(s0, s1, s2, ct):
    r = lax.broadcasted_iota(jnp.int32, (8, ct), 0)
    return jnp.where(r == 0, s0, jnp.where(r == 1, s1, jnp.where(r == 2, s2, 0.0)))


def _acc_out(ref, i, val):
    @pl.when(i == 0)
    def _():
        ref[...] = val

    @pl.when(i > 0)
    def _():
        ref[...] += val


def _sconv_fwd(main, w8, Lp, tm):
    ct = 256
    G = CONV_CH // ct
    bb, cb, hb = (lambda g: SCB // ct + g), (lambda g: SCC // ct + g), (lambda g: SCH // ct + g)

    def body(b_ref, c_ref, h_ref, cp_ref, hp_ref, w_ref, o_ref):
        i = pl.program_id(1)
        z = c_ref[...].astype(F32) * h_ref[...].astype(F32)
        zp = jnp.where(i > 0, cp_ref[...].astype(F32) * hp_ref[...].astype(F32), 0.0)
        zz = jnp.concatenate([zp, z], axis=0)
        cz = (_wrow(w_ref, 0) * _shift_down(zz, 2)[HALO:] + _wrow(w_ref, 1) * _shift_down(zz, 1)[HALO:]
              + _wrow(w_ref, 2) * z)
        o_ref[...] = (b_ref[...].astype(F32) * cz).astype(o_ref.dtype)

    return pl.pallas_call(
        body, name="sconv_fwd", grid=(G, Lp // tm),
        in_specs=[_cur_spec(tm, ct, bb), _cur_spec(tm, ct, cb), _cur_spec(tm, ct, hb),
                  _prev_spec(tm, ct, cb), _prev_spec(tm, ct, hb), pl.BlockSpec((8, ct), lambda g, i: (0, g))],
        out_specs=pl.BlockSpec((tm, ct), lambda g, i: (i, g)),
        out_shape=jax.ShapeDtypeStruct((Lp, CONV_CH), BF16),
        compiler_params=_params(("parallel", "arbitrary")),
    )(main, main, main, main, main, w8)


def _sconv_bwd(main, w8, dub, Lp, tm):
    ct = 256
    G = CONV_CH // ct
    n = Lp // tm
    bb, cb, hb, ob = (lambda g: SCB // ct + g), (lambda g: SCC // ct + g), (lambda g: SCH // ct + g), (lambda g: g)

    def body(b_ref, c_ref, h_ref, cp_ref, hp_ref, bn_ref, d_ref, dn_ref, w_ref, db_ref, dc_ref, dh_ref, dw_ref):
        i = pl.program_id(1)
        b, c, h = b_ref[...].astype(F32), c_ref[...].astype(F32), h_ref[...].astype(F32)
        z = c * h
        zp = jnp.where(i > 0, cp_ref[...].astype(F32) * hp_ref[...].astype(F32), 0.0)
        zz = jnp.concatenate([zp, z], axis=0)
        z1, z2 = _shift_down(zz, 1)[HALO:], _shift_down(zz, 2)[HALO:]
        w0, w1, w2 = _wrow(w_ref, 0), _wrow(w_ref, 1), _wrow(w_ref, 2)
        cz = w0 * z2 + w1 * z1 + w2 * z
        dub_c = d_ref[...].astype(F32)
        db_ref[...] = (dub_c * cz).astype(db_ref.dtype)
        dcz = dub_c * b
        dcz_n = jnp.where(i < n - 1, dn_ref[...].astype(F32) * bn_ref[...].astype(F32), 0.0)
        dd = jnp.concatenate([dcz, dcz_n], axis=0)
        dz = w2 * dcz + w1 * _shift_up(dd, 1)[:tm] + w0 * _shift_up(dd, 2)[:tm]
        dc_ref[...] = (dz * h).astype(dc_ref.dtype)
        dh_ref[...] = (dz * c).astype(dh_ref.dtype)
        s = lambda x: jnp.sum(dcz * x, axis=0, keepdims=True)
        _acc_out(dw_ref, i, _rows3(s(z2), s(z1), s(z), ct))

    out = pl.BlockSpec((tm, ct), lambda g, i: (i, g))
    return pl.pallas_call(
        body, name="sconv_bwd", grid=(G, n),
        in_specs=[_cur_spec(tm, ct, bb), _cur_spec(tm, ct, cb), _cur_spec(tm, ct, hb),
                  _prev_spec(tm, ct, cb), _prev_spec(tm, ct, hb), _next_spec(tm, ct, bb, Lp),
                  _cur_spec(tm, ct, ob), _next_spec(tm, ct, ob, Lp), pl.BlockSpec((8, ct), lambda g, i: (0, g))],
        out_specs=[out, out, out, pl.BlockSpec((8, ct), lambda g, i: (0, g))],
        out_shape=[jax.ShapeDtypeStruct((Lp, CONV_CH), BF16)] * 3 + [jax.ShapeDtypeStruct((8, CONV_CH), F32)],
        compiler_params=_params(("parallel", "arbitrary")),
    )(main, main, main, main, main, main, dub, dub, w8)


def _conv3(w_ref, ext):
    return _wrow(w_ref, 0) * _shift_down(ext, 2) + _wrow(w_ref, 1) * _shift_down(ext, 1) + _wrow(w_ref, 2) * ext


def _mlp_act_fwd(up, w8, Lp, tm):
    ct = 256
    G = D_FF // ct
    gb, ub = (lambda g: g), (lambda g: G + g)

    def body(g_ref, u_ref, gp_ref, up_ref, wg_ref, wu_ref, o_ref, ot_ref):
        i = pl.program_id(1)

        def conv(cur, prev, w_ref):
            ext = jnp.concatenate([jnp.where(i > 0, prev[...].astype(F32), 0.0), cur[...].astype(F32)], axis=0)
            return _conv3(w_ref, ext)[HALO:]

        ug, uu = conv(g_ref, gp_ref, wg_ref), conv(u_ref, up_ref, wu_ref)
        a = ug * _sigmoid(ug) * uu
        o_ref[...] = a.astype(o_ref.dtype)
        ot_ref[...] = a.T.astype(ot_ref.dtype)

    wspec = lambda cb: pl.BlockSpec((8, ct), lambda g, i: (0, cb(g)))
    return pl.pallas_call(
        body, name="mlp_act_fwd", grid=(G, Lp // tm),
        in_specs=[_cur_spec(tm, ct, gb), _cur_spec(tm, ct, ub), _prev_spec(tm, ct, gb), _prev_spec(tm, ct, ub),
                  wspec(gb), wspec(ub)],
        out_specs=[pl.BlockSpec((tm, ct), lambda g, i: (i, g)), pl.BlockSpec((ct, tm), lambda g, i: (g, i))],
        out_shape=[jax.ShapeDtypeStruct((Lp, D_FF), BF16), jax.ShapeDtypeStruct((D_FF, Lp), BF16)],
        compiler_params=_params(("parallel", "arbitrary")),
    )(up, up, up, up, w8, w8)


def _mlp_act_bwd(up, w8, da, Lp, tm):
    ct = 256
    G = D_FF // ct
    n = Lp // tm
    gb, ub, ob = (lambda g: g), (lambda g: G + g), (lambda g: g)

    def body(g_ref, u_ref, gp_ref, up_ref, gn_ref, un_ref, d_ref, dn_ref, wg_ref, wu_ref,
             dg_ref, du_ref, dwg_ref, dwu_ref):
        i = pl.program_id(1)

        def ext_of(prev, cur, nxt):
            return jnp.concatenate([jnp.where(i > 0, prev[...].astype(F32), 0.0), cur[...].astype(F32),
                                    jnp.where(i < n - 1, nxt[...].astype(F32), 0.0)], axis=0)

        eg, eu = ext_of(gp_ref, g_ref, gn_ref), ext_of(up_ref, u_ref, un_ref)
        da_e = jnp.concatenate([jnp.zeros((HALO, ct), F32), d_ref[...].astype(F32),
                                jnp.where(i < n - 1, dn_ref[...].astype(F32), 0.0)], axis=0)
        ug, uu = _conv3(wg_ref, eg), _conv3(wu_ref, eu)
        sg = _sigmoid(ug)
        dug = da_e * uu * (sg * (1.0 + ug * (1.0 - sg)))
        duu = da_e * (ug * sg)
        cur = slice(HALO, HALO + tm)

        def back(w_ref, dx, e, dx_ref, dw_ref):
            d_in = _wrow(w_ref, 2) * dx + _wrow(w_ref, 1) * _shift_up(dx, 1) + _wrow(w_ref, 0) * _shift_up(dx, 2)
            dx_ref[...] = d_in[cur].astype(dx_ref.dtype)
            s = lambda x: jnp.sum(dx[cur] * x[cur], axis=0, keepdims=True)
            _acc_out(dw_ref, i, _rows3(s(_shift_down(e, 2)), s(_shift_down(e, 1)), s(e), ct))

        back(wg_ref, dug, eg, dg_ref, dwg_ref)
        back(wu_ref, duu, eu, du_ref, dwu_ref)

    wspec = lambda cb: pl.BlockSpec((8, ct), lambda g, i: (0, cb(g)))
    out = pl.BlockSpec((tm, ct), lambda g, i: (i, g))
    return pl.pallas_call(
        body, name="mlp_act_bwd", grid=(G, n),
        in_specs=[_cur_spec(tm, ct, gb), _cur_spec(tm, ct, ub), _prev_spec(tm, ct, gb), _prev_spec(tm, ct, ub),
                  _next_spec(tm, ct, gb, Lp), _next_spec(tm, ct, ub, Lp), _cur_spec(tm, ct, ob),
                  _next_spec(tm, ct, ob, Lp), wspec(gb), wspec(ub)],
        out_specs=[out, out, wspec(ob), wspec(ob)],
        out_shape=[jax.ShapeDtypeStruct((Lp, D_FF), BF16)] * 2 + [jax.ShapeDtypeStruct((8, D_FF), F32)] * 2,
        compiler_params=_params(("parallel", "arbitrary")),
    )(up, up, up, up, up, up, da, da, w8, w8)


def _gla_chunk(q, k, v, g, s0):
    C = BLOCK
    r_i = lax.broadcasted_iota(jnp.int32, (C, C), 0)
    c_i = lax.broadcasted_iota(jnp.int32, (C, C), 1)
    row = lax.broadcasted_iota(jnp.int32, (C, GLA_DK), 0)
    b = jnp.dot((c_i <= r_i).astype(F32), g, precision=HIGHEST, preferred_element_type=F32)
    row_of = lambda n: jnp.sum(jnp.where(row == n, b, 0.0), axis=0, keepdims=True)
    refs = [row_of(n * GLA_SUB) for n in range(C // GLA_SUB)]
    sub = jnp.bitwise_and(row, -GLA_SUB)
    ref_all = sum(jnp.where(sub == n * GLA_SUB, refs[n], 0.0) for n in range(C // GLA_SUB))
    qs = q * (GLA_DK ** -0.5)
    qt = (qs * jnp.exp(b - ref_all)).astype(BF16)
    sub_start = jnp.bitwise_and(r_i, -GLA_SUB)
    att = jnp.zeros((C, C), F32)
    for n in range(C // GLA_SUB):
        kt = (k * jnp.exp(jnp.minimum(refs[n] - b, 60.0))).astype(BF16)
        a_n = lax.dot_general(qt, kt, (((1,), (1,)), ((), ())), preferred_element_type=F32)
        att = att + jnp.where((sub_start == n * GLA_SUB) & (c_i <= r_i), a_n, 0.0)
    o = (jnp.dot(att.astype(BF16), v.astype(BF16), preferred_element_type=F32)
         + jnp.dot((qs * jnp.exp(b)).astype(BF16), s0.astype(BF16), preferred_element_type=F32))
    kd = (k * jnp.exp(row_of(C - 1) - b)).astype(BF16)
    last_rows = (lax.broadcasted_iota(jnp.int32, (C, GLA_DV), 0) == C - 1).astype(F32)
    decay = lax.dot_general(b, last_rows, (((0,), (0,)), ((), ())), precision=HIGHEST,
                            preferred_element_type=F32)
    s1 = jnp.exp(decay) * s0 + lax.dot_general(kd, v.astype(BF16), (((0,), (0,)), ((), ())),
                                               preferred_element_type=F32)
    return o, s1


def _gla_fwd(main, logg, Lp):
    n = Lp // BLOCK
    qb, kb, vb = QC // GLA_DK, KC // GLA_DK, VC // GLA_DV

    def body(q_ref, k_ref, v_ref, g_ref, o_ref, st_ref, s_s):
        c = pl.program_id(1)

        @pl.when(c == 0)
        def _():
            s_s[...] = jnp.zeros_like(s_s)

        s0 = s_s[...]
        st_ref[...] = s0
        o, s1 = _gla_chunk(q_ref[...].astype(F32), k_ref[...].astype(F32), v_ref[...].astype(F32), g_ref[...], s0)
        o_ref[...] = o
        s_s[...] = s1

    return pl.pallas_call(
        body, name="gla_fwd", grid=(GLA_HEADS, n),
        in_specs=[pl.BlockSpec((BLOCK, GLA_DK), lambda h, c: (c, qb + h)),
                  pl.BlockSpec((BLOCK, GLA_DK), lambda h, c: (c, kb + h)),
                  pl.BlockSpec((BLOCK, GLA_DV), lambda h, c: (c, vb + h)),
                  pl.BlockSpec((BLOCK, GLA_DK), lambda h, c: (c, h))],
        out_specs=[pl.BlockSpec((BLOCK, GLA_DV), lambda h, c: (c, h)),
                   pl.BlockSpec((None, None, GLA_DK, GLA_DV), lambda h, c: (h, c, 0, 0))],
        out_shape=[jax.ShapeDtypeStruct((Lp, GLA_HEADS * GLA_DV), F32),
                   jax.ShapeDtypeStruct((GLA_HEADS, n, GLA_DK, GLA_DV), F32)],
        scratch_shapes=[pltpu.VMEM((GLA_DK, GLA_DV), F32)],
        compiler_params=_params(("parallel", "arbitrary")),
    )(main, main, main, logg)


def _gla_bwd(main, logg, states, do, Lp):
    n = Lp // BLOCK
    qb, kb, vb = QC // GLA_DK, KC // GLA_DK, VC // GLA_DV

    def body(q_ref, k_ref, v_ref, g_ref, st_ref, do_ref, dq_ref, dk_ref, dv_ref, dg_ref, ds_s):
        c = pl.program_id(1)

        @pl.when(c == 0)
        def _():
            ds_s[...] = jnp.zeros_like(ds_s)

        _, vjp = jax.vjp(_gla_chunk, q_ref[...].astype(F32), k_ref[...].astype(F32), v_ref[...].astype(F32),
                         g_ref[...], st_ref[...])
        dq, dk, dv, dg, ds0 = vjp((do_ref[...], ds_s[...]))
        dq_ref[...] = dq.astype(dq_ref.dtype)
        dk_ref[...] = dk.astype(dk_ref.dtype)
        dv_ref[...] = dv.astype(dv_ref.dtype)
        dg_ref[...] = dg
        ds_s[...] = ds0

    rk = lambda base: pl.BlockSpec((BLOCK, GLA_DK), lambda h, c: (n - 1 - c, base + h))
    rv = lambda base: pl.BlockSpec((BLOCK, GLA_DV), lambda h, c: (n - 1 - c, base + h))
    return pl.pallas_call(
        body, name="gla_bwd", grid=(GLA_HEADS, n),
        in_specs=[rk(qb), rk(kb), rv(vb), rk(0),
                  pl.BlockSpec((None, None, GLA_DK, GLA_DV), lambda h, c: (h, n - 1 - c, 0, 0)), rv(0)],
        out_specs=[rk(0), rk(0), rv(0), rk(0)],
        out_shape=[jax.ShapeDtypeStruct((Lp, GLA_HEADS * GLA_DK), BF16), jax.ShapeDtypeStruct((Lp, GLA_HEADS * GLA_DK), BF16),
                   jax.ShapeDtypeStruct((Lp, GLA_HEADS * GLA_DV), BF16), jax.ShapeDtypeStruct((Lp, GLA_HEADS * GLA_DK), F32)],
        scratch_shapes=[pltpu.VMEM((GLA_DK, GLA_DV), F32)],
        compiler_params=_params(("parallel", "arbitrary")),
    )(main, main, main, logg, states, do)


def _loss_head(h, g, target, Lp):
    t = BLOCK
    D = D_MODEL

    def body(h_ref, g_ref, t_ref, loss_ref, dh_ref, dg_ref):
        i = pl.program_id(0)
        x = h_ref[...]
        tok = (i * t + lax.broadcasted_iota(jnp.int32, (t, 1), 0)) >= BLOCK
        r = lax.rsqrt(jnp.mean(x * x, axis=-1, keepdims=True) + EPS)
        nrm = x * r
        e = jnp.where(tok, nrm * g_ref[...] - t_ref[...], 0.0)
        part = 0.5 * jnp.sum(jnp.sum(e * e, axis=1, keepdims=True), axis=0, keepdims=True) / D
        dy = e / D
        dn = dy * g_ref[...]
        dh_ref[...] = r * (dn - nrm * jnp.mean(dn * nrm, axis=-1, keepdims=True))
        _acc_out(dg_ref, i, jnp.sum(dy * nrm, axis=0, keepdims=True))
        _acc_out(loss_ref, i, jnp.broadcast_to(part, (1, LANES)))

    return pl.pallas_call(
        body, name="loss_head", grid=(Lp // t,),
        in_specs=[pl.BlockSpec((t, D), lambda i: (i, 0)), pl.BlockSpec((1, D), lambda i: (0, 0)),
                  pl.BlockSpec((t, D), lambda i: (jnp.maximum(i - 1, 0), 0))],
        out_specs=[pl.BlockSpec((1, LANES), lambda i: (0, 0)), pl.BlockSpec((t, D), lambda i: (i, 0)),
                   pl.BlockSpec((1, D), lambda i: (0, 0))],
        out_shape=[jax.ShapeDtypeStruct((1, LANES), F32), jax.ShapeDtypeStruct((Lp, D), F32),
                   jax.ShapeDtypeStruct((1, D), F32)],
        compiler_params=_params(("arbitrary",)),
    )(h, g, target)


def _adamw(w, g, m, v, name):
    if w.ndim == 1:
        outs = _adamw(*(a.reshape(1, -1) for a in (w, g, m, v)), name)
        return tuple(o.reshape(w.shape) for o in outs)
    if w.ndim == 3 and w.shape[-1] % LANES and w.shape[-2] % LANES == 0:
        outs = _adamw(*(a.transpose(2, 0, 1) for a in (w, g, m, v)), name)
        return tuple(o.transpose(1, 2, 0) for o in outs)
    rows, cols = w.shape[-2:]
    budget_rows = max(8, ADAM_BLOCK_BYTES // (4 * cols))
    tr = rows if rows <= budget_rows else _pick(rows, tuple(t for t in (512, 256, 128, 64, 32, 16, 8) if t <= budget_rows))
    lead, tc = 1, cols
    if w.ndim == 3 and tr == rows:
        lead = max(d for d in range(1, 1025) if w.shape[0] % d == 0)
        fits = [c for c in (cols, 2048, 1024, 512, 256, 128) if cols % c == 0 and 4 * lead * rows * c <= ADAM_BLOCK_BYTES]
        tc = fits[0] if fits else LANES

    def body(w_ref, g_ref, m_ref, v_ref, go_ref, d_ref, nm_ref, nv_ref):
        gg = g_ref[...]
        mm = ADAM_B1 * m_ref[...] + (1.0 - ADAM_B1) * gg
        vv = ADAM_B2 * v_ref[...] + (1.0 - ADAM_B2) * jnp.square(gg)
        m_hat = mm / (1.0 - ADAM_B1 ** ADAM_STEP)
        v_hat = vv / (1.0 - ADAM_B2 ** ADAM_STEP)
        d_ref[...] = -ADAM_LR * (m_hat / (jnp.sqrt(v_hat) + ADAM_EPS) + ADAM_WD * w_ref[...])
        go_ref[...] = gg
        nm_ref[...] = mm
        nv_ref[...] = vv

    if w.ndim == 3:
        spec = pl.BlockSpec((lead, tr, tc), lambda l, i, j: (l, i, j))
        grid = (w.shape[0] // lead, rows // tr, cols // tc)
    else:
        spec, grid = pl.BlockSpec((tr, cols), lambda i: (i, 0)), (rows // tr,)
    return pl.pallas_call(
        body, name=name, grid=grid, in_specs=[spec] * 4, out_specs=[spec] * 4,
        out_shape=[jax.ShapeDtypeStruct(w.shape, F32)] * 4,
        compiler_params=_params(("parallel",) * len(grid)),
    )(w, g, m, v)


def _place():
    x, y, c = lax.axis_index("x"), lax.axis_index("y"), lax.axis_index("c")
    chips = [(1 - x, y), (x, 1 - y), (1 - x, 1 - y)]
    return x, y, c, chips


def _rcopy(src, dst, send_sems, recv_sems, k, to):
    return pltpu.make_async_remote_copy(src_ref=src, dst_ref=dst, send_sem=send_sems.at[k], recv_sem=recv_sems.at[k],
                                        device_id=to, device_id_type=MESH)


def _any_spec():
    return pl.BlockSpec(memory_space=pl.ANY)


def _shard_ref(ref, mode, t, r, c):
    if mode == "rows":
        return ref.at[pl.ds(pl.multiple_of(t * r, 16), r), :]
    if mode == "cols":
        return ref.at[:, pl.ds(pl.multiple_of(t * c, LANES), c)]
    return ref.at[t]


def _gathered_shape(mode, r, c):
    return {"rows": (4 * r, c), "cols": (r, 4 * c), "stack": (4, r, c)}[mode]


def _place_own(shard, mode, me1, name):
    _, r, c = shard.shape
    tr = _ew_rows(r, c)
    blk = {"rows": (None, tr, c), "cols": (None, tr, c), "stack": (None, None, tr, c)}[mode]
    idx = {"rows": lambda l, i, me: (l, me[0] * (r // tr) + i, 0),
           "cols": lambda l, i, me: (l, i, me[0]),
           "stack": lambda l, i, me: (l, me[0], i, 0)}[mode]

    def body(me_ref, in_ref, out_ref):
        out_ref[...] = in_ref[...]

    return pl.pallas_call(
        body, name=name,
        grid_spec=pltpu.PrefetchScalarGridSpec(
            num_scalar_prefetch=1, grid=(DEPTH, r // tr),
            in_specs=[pl.BlockSpec((None, tr, c), lambda l, i, me: (l, i, 0))],
            out_specs=pl.BlockSpec(blk, idx)),
        out_shape=jax.ShapeDtypeStruct((DEPTH,) + _gathered_shape(mode, r, c), shard.dtype),
        compiler_params=_params(("parallel", "parallel")),
    )(me1, shard)


def _gather_weights(shards, modes):
    n = len(shards)
    dims = [s.shape[1:] for s in shards]
    me1 = jnp.reshape(2 * lax.axis_index("x") + lax.axis_index("y"), (1,)).astype(jnp.int32)
    placed = [_place_own(shards[k], modes[k], me1, f"gather_place_{k}") for k in range(n)]

    def body(*refs):
        ins, outs = refs[:n], refs[2 * n:3 * n]
        send_sems, recv_sems = refs[3 * n:]
        x, y, c, _ = _place()
        n1 = (x + (1 - c) * (1 - 2 * x), y + c * (1 - 2 * y))
        n2 = (x + c * (1 - 2 * x), y + (1 - c) * (1 - 2 * y))
        diag = (1 - x, 1 - y)
        chip = lambda ch: 2 * ch[0] + ch[1]
        me, here, sibling = (x, y), (x, y, c), (x, y, 1 - c)
        place = lambda k, l, t: _shard_ref(outs[k].at[l], modes[k], chip(t), *dims[k])

        def copy(k, m, l, t, to, src=None):
            blk = place(k, l, t)
            return _rcopy(blk if src is None else src, blk, send_sems, recv_sems, 6 * k + m, to)

        sent = [copy(k, 0, c, me, (*n1, c), ins[k].at[c]) for k in range(n)]
        sent += [copy(k, 1, c, me, (*n2, c), ins[k].at[c]) for k in range(n)]
        for cp in sent:
            cp.start()
        for k in range(n):
            copy(k, 0, c, n1, here).wait_recv()
            sent += [copy(k, 2, c, n1, (*n2, c)), copy(k, 3, c, n1, sibling)]
            sent[-2].start()
            sent[-1].start()
        for m, t in ((1, n2), (2, diag)):
            for k in range(n):
                copy(k, m, c, t, here).wait_recv()
                sent.append(copy(k, 3 + m, c, t, sibling))
                sent[-1].start()
        for m, t in ((3, n2), (4, n1), (5, diag)):
            for k in range(n):
                copy(k, m, 1 - c, t, here).wait_recv()
        for cp in sent:
            cp.wait_send()

    return pl.pallas_call(
        body, name="gather_weights", in_specs=[_any_spec()] * (2 * n), out_specs=[_any_spec()] * n,
        out_shape=[jax.ShapeDtypeStruct(a.shape, a.dtype) for a in placed],
        input_output_aliases={n + k: k for k in range(n)},
        scratch_shapes=[pltpu.SemaphoreType.DMA((6 * n,)), pltpu.SemaphoreType.DMA((6 * n,))],
    )(*shards, *placed)


def _swap_layers(gs):
    n = len(gs)

    def body(*refs):
        ins, outs = refs[:n], refs[n:2 * n]
        send_sems, recv_sems = refs[2 * n:]
        x, y, c, _ = _place()
        cps = [_rcopy(ins[k].at[1 - c], outs[k], send_sems, recv_sems, k, (x, y, 1 - c)) for k in range(n)]
        for cp in cps:
            cp.start()
        for cp in cps:
            cp.wait()

    return pl.pallas_call(
        body, name="rs_swap_layers", in_specs=[_any_spec()] * n, out_specs=[_any_spec()] * n,
        out_shape=[jax.ShapeDtypeStruct(g.shape[1:], g.dtype) for g in gs],
        scratch_shapes=[pltpu.SemaphoreType.DMA((n,)), pltpu.SemaphoreType.DMA((n,))],
    )(*gs)


def _partners(x, y, c):
    n1 = (x + (1 - c) * (1 - 2 * x), y + c * (1 - 2 * y))
    n2 = (x + c * (1 - 2 * x), y + (1 - c) * (1 - 2 * y))
    return n1, n2, (1 - x, 1 - y)


def _chip(ch):
    return 2 * ch[0] + ch[1]


def _scatter_pairs(hs, modes, dims):
    n = len(hs)

    def body(*refs):
        ins, outs = refs[:n], refs[n:2 * n]
        send_sems, recv_sems = refs[2 * n:]
        x, y, c, _ = _place()
        _, n2, diag = _partners(x, y, c)
        part = lambda k, t: _shard_ref(ins[k], modes[k], _chip(t), *dims[k])
        cps = [_rcopy(part(k, t), outs[k].at[j], send_sems, recv_sems, 2 * k + j, (*n2, c))
               for k in range(n) for j, t in enumerate((n2, diag))]
        for cp in cps:
            cp.start()
        for cp in cps:
            cp.wait()

    return pl.pallas_call(
        body, name="rs_scatter_pairs", in_specs=[_any_spec()] * n, out_specs=[_any_spec()] * n,
        out_shape=[jax.ShapeDtypeStruct((2,) + tuple(dims[k]), hs[k].dtype) for k in range(n)],
        scratch_shapes=[pltpu.SemaphoreType.DMA((2 * n,)), pltpu.SemaphoreType.DMA((2 * n,))],
    )(*hs)


def _scatter_last(ts):
    n = len(ts)

    def body(*refs):
        ins, outs = refs[:n], refs[n:2 * n]
        send_sems, recv_sems = refs[2 * n:]
        x, y, c, _ = _place()
        n1, _, _ = _partners(x, y, c)
        cps = [_rcopy(ins[k], outs[k], send_sems, recv_sems, k, (*n1, c)) for k in range(n)]
        for cp in cps:
            cp.start()
        for cp in cps:
            cp.wait()

    return pl.pallas_call(
        body, name="rs_scatter_last", in_specs=[_any_spec()] * n, out_specs=[_any_spec()] * n,
        out_shape=[jax.ShapeDtypeStruct(t.shape, t.dtype) for t in ts],
        scratch_shapes=[pltpu.SemaphoreType.DMA((n,)), pltpu.SemaphoreType.DMA((n,))],
    )(*ts)


def _add_pair(h, got, j, mode, dims, who, out_dtype, name):
    r, c = dims
    tr = _ew_rows(r, c)
    if mode == "stack":
        h_spec = pl.BlockSpec((None, tr, c), lambda i, w: (w[0], i, 0))
    elif mode == "rows":
        h_spec = pl.BlockSpec((tr, c), lambda i, w: (w[0] * (r // tr) + i, 0))
    else:
        h_spec = pl.BlockSpec((tr, c), lambda i, w: (i, w[0]))

    def body(w_ref, h_ref, g_ref, out_ref):
        out_ref[...] = (h_ref[...].astype(F32) + g_ref[...].astype(F32)).astype(out_ref.dtype)

    return pl.pallas_call(
        body, name=name,
        grid_spec=pltpu.PrefetchScalarGridSpec(
            num_scalar_prefetch=1, grid=(r // tr,),
            in_specs=[h_spec, pl.BlockSpec((None, tr, c), lambda i, w: (j, i, 0))],
            out_specs=pl.BlockSpec((tr, c), lambda i, w: (i, 0))),
        out_shape=jax.ShapeDtypeStruct((r, c), out_dtype),
        compiler_params=_params(("parallel",)),
    )(who, h, got)


def _add_last(mine, got, c1, name):
    r, c = mine.shape
    tr = _ew_rows(r, c)

    def body(c_ref, a_ref, b_ref, out_ref):
        out_ref[...] = a_ref[...] + b_ref[...].astype(F32)

    spec = pl.BlockSpec((tr, c), lambda i, cr: (i, 0))
    return pl.pallas_call(
        body, name=name,
        grid_spec=pltpu.PrefetchScalarGridSpec(
            num_scalar_prefetch=1, grid=(r // tr,), in_specs=[spec, spec],
            out_specs=pl.BlockSpec((None, tr, c), lambda i, cr: (cr[0], i, 0))),
        out_shape=jax.ShapeDtypeStruct((DEPTH, r, c), F32),
        compiler_params=_params(("parallel",)),
    )(c1, mine, got)


def _join_layers(fs):
    n = len(fs)

    def body(*refs):
        outs = refs[n:2 * n]
        send_sems, recv_sems = refs[2 * n:]
        x, y, c, _ = _place()
        cps = [_rcopy(outs[k].at[c], outs[k].at[c], send_sems, recv_sems, k, (x, y, 1 - c)) for k in range(n)]
        for cp in cps:
            cp.start()
        for k in range(n):
            blk = outs[k].at[1 - c]
            _rcopy(blk, blk, send_sems, recv_sems, k, (x, y, c)).wait_recv()
        for cp in cps:
            cp.wait_send()

    return pl.pallas_call(
        body, name="rs_join_layers", in_specs=[_any_spec()] * n, out_specs=[_any_spec()] * n,
        out_shape=[jax.ShapeDtypeStruct(f.shape, f.dtype) for f in fs],
        input_output_aliases={k: k for k in range(n)},
        scratch_shapes=[pltpu.SemaphoreType.DMA((n,)), pltpu.SemaphoreType.DMA((n,))],
    )(*fs)


def _ew_rows(M, N):
    fit = [t for t in (512, 256, 128, 64, 32, 16) if M % t == 0 and t * N * 4 <= EW_BLOCK_BYTES]
    return fit[0] if fit else M


def _add_own(g, other, c1, out_dtype, name):
    _, M, N = g.shape
    tr = _ew_rows(M, N)

    def body(c_ref, g_ref, o_ref, out_ref):
        out_ref[...] = (g_ref[...] + o_ref[...].astype(F32)).astype(out_ref.dtype)

    return pl.pallas_call(
        body, name=name,
        grid_spec=pltpu.PrefetchScalarGridSpec(
            num_scalar_prefetch=1, grid=(M // tr,),
            in_specs=[pl.BlockSpec((None, tr, N), lambda i, cr: (cr[0], i, 0)),
                      pl.BlockSpec((tr, N), lambda i, cr: (i, 0))],
            out_specs=pl.BlockSpec((tr, N), lambda i, cr: (i, 0))),
        out_shape=jax.ShapeDtypeStruct((M, N), out_dtype),
        compiler_params=_params(("parallel",)),
    )(c1, g, other)


def _reduce_scatter(gs, gs_d2d, modes, dims, wire):
    x, y, c = lax.axis_index("x"), lax.axis_index("y"), lax.axis_index("c")
    c1 = jnp.reshape(c, (1,)).astype(jnp.int32)
    n1, _, _ = _partners(x, y, c)
    me1, next1 = (jnp.reshape(_chip(ch), (1,)).astype(jnp.int32) for ch in ((x, y), n1))
    flat = lambda a, lead: a.reshape(a.shape[:lead] + (-1, a.shape[-1]))
    others = _swap_layers(gs_d2d)
    hs = [_add_own(flat(g, 1), flat(o, 0), c1, wire[k], f"rs_add_own_{k}").reshape(o.shape)
          for k, (g, o) in enumerate(zip(gs, others))]
    got = _scatter_pairs(hs, modes, dims)
    mine = [_add_pair(hs[k], got[k], 0, modes[k], dims[k], me1, F32, f"rs_add_pair_mine_{k}") for k in range(len(hs))]
    pass_on = [_add_pair(hs[k], got[k], 1, modes[k], dims[k], next1, wire[k], f"rs_add_pair_next_{k}")
               for k in range(len(hs))]
    last = _scatter_last(pass_on)
    fs = [_add_last(mine[k], last[k], c1, f"rs_add_last_{k}") for k in range(len(hs))]
    return _join_layers(fs)


def _allreduce_small(v):
    R, C = v.shape

    def body(v_ref, sum_ref, all_ref, send_sems, recv_sems):
        x, y, c, _ = _place()
        me = 4 * x + 2 * y + c
        rows = lambda d: all_ref.at[pl.ds(pl.multiple_of(d * R, 8), R), :]

        def peer(k):
            flip = lambda bit, v: (1 - v) if ((k + 1) >> bit) & 1 else v
            return flip(2, x), flip(1, y), flip(0, c)

        outs = [_rcopy(v_ref, rows(me), send_sems, recv_sems, k, peer(k)) for k in range(7)]
        for cp in outs:
            cp.start()
        all_ref[pl.ds(pl.multiple_of(me * R, 8), R), :] = v_ref[...]
        for k in range(7):
            px, py, pc = peer(k)
            blk = rows(4 * px + 2 * py + pc)
            _rcopy(blk, blk, send_sems, recv_sems, k, (x, y, c)).wait_recv()
        for cp in outs:
            cp.wait_send()
        tot = all_ref[0:R, :]
        for d in range(1, 8):
            tot = tot + all_ref[d * R:(d + 1) * R, :]
        sum_ref[...] = tot

    vm = pl.BlockSpec(memory_space=pltpu.VMEM)
    return pl.pallas_call(
        body, name="allreduce_small", in_specs=[vm], out_specs=[vm, vm],
        out_shape=[jax.ShapeDtypeStruct((R, C), F32), jax.ShapeDtypeStruct((8 * R, C), F32)],
        scratch_shapes=[pltpu.SemaphoreType.DMA((7,)), pltpu.SemaphoreType.DMA((7,))],
    )(v)[0]


def _size(shape):
    n = 1
    for d in shape:
        n *= d
    return n


def _pack(pieces, dtype):
    flat = jnp.concatenate([p.astype(dtype).reshape(-1) for p in pieces])
    rows = -(-flat.shape[0] // (PACK_COLS * 16)) * 16
    return jnp.pad(flat, (0, rows * PACK_COLS - flat.shape[0])).reshape(rows, PACK_COLS)


def _unpack(buf, shapes):
    flat = buf.reshape(-1)
    out, pos = [], 0
    for s in shapes:
        n = _size(s)
        out.append(flat[pos:pos + n].reshape(s))
        pos += n
    return out


def _small_piece(name, arr, l):
    if name == "meta_tokens":
        return arr[l * (N_META // DEPTH):(l + 1) * (N_META // DEPTH)]
    return arr[l]


def _prep_w_in(w_in4):
    w_in = jnp.concatenate([w_in4[t] for t in range(4)], axis=1)
    col = lambda a, n: w_in[:, _R[a]:_R[a] + n]
    main = jnp.concatenate([col("qa", 3072), col("scb", 3072), col("qc", 3072), col("ga", 6144)], axis=1)
    zpad = lambda n: jnp.zeros((D_MODEL, n), w_in.dtype)
    side = jnp.concatenate([col("fa", 8), zpad(LANES - 8), col("glr", GLA_RANK), zpad(LANES - GLA_RANK)], axis=1)
    return main.astype(BF16), side.astype(BF16)


def _w_in_cols(dmain, dside, lo, hi):
    segs = ((0, _R["fa"], dmain, 0), (_R["fa"], _R["scb"], dside, 0), (_R["scb"], _R["glr"], dmain, SCB),
            (_R["glr"], _R["ga"], dside, LANES), (_R["ga"], N_IN, dmain, GA))
    parts = [src[..., off + max(a, lo) - a:off + min(b, hi) - a] for a, b, src, off in segs if max(a, lo) < min(b, hi)]
    return jnp.concatenate(parts, axis=-1)


def _pad_rows(a, rows):
    return jnp.pad(a.astype(F32), ((0, rows - a.shape[0]), (0, 0)))


def _row2(v):
    return v.reshape(1, -1).astype(F32)


def _layer_fwd(h, p, rep, l, Lp, tm, ta):
    tag = lambda s: f"{s}_l{l}"
    g1, g2 = _row2(rep["norm1_g"][l]), _row2(rep["norm2_g"][l])
    bf = jnp.pad(_row2(rep["fox_b_f"][l]), ((0, 0), (0, LANES - FOX_HEADS)))
    gate_b, b_g, gnorm = _row2(rep["gate_b"][l]), _row2(rep["gla_b_g"][l]), _row2(rep["gla_norm_g"][l])
    xn, xn_t = _rms_fwd(h, g1, tag("rms1_fwd"), Lp)
    main = _mm(xn, p["main"][l], "nn", BF16, tag("proj_main"))
    side = _mm(xn, p["side"][l], "nn", F32, tag("proj_side"))
    c = _fox_gate_fwd(side, bf, Lp)
    c_t = c[:, :FOX_HEADS].T
    c_col, c_row = c_t[:, :, None], c_t[:, None, :]
    oa, ox, lse = _fox_fwd(main, c_col, c_row, Lp, ta)
    ya = _mm(oa, p["w_a_o"], "nn", BF16, tag("ya"), b_lead=l)
    ub = _sconv_fwd(main, p["conv_w"][l], Lp, tm)
    yb = _mm(ub, p["w_b_o"], "nn", BF16, tag("yb"), b_lead=l)
    glr = Row(side, LANES, lambda g: 1)
    (logg,) = _rw_fwd(tag("logg_fwd"), _f_logg, [glr], [Const(p["w_g2"][l]), Const(b_g)], [(512, F32)], Lp, tm)
    oc, states = _gla_fwd(main, logg, Lp)
    rc = Row(main, GLA_DV, lambda g: RC // GLA_DV + g)
    gn = Const(gnorm, (1, GLA_DV), lambda g: (0, g))
    (uc,) = _rw_fwd(tag("gla_post_fwd"), _f_gla_post, [Row(oc, GLA_DV), rc], [gn], [(GLA_DV, BF16)], Lp, tm,
                    G=GLA_HEADS)
    yc = _mm(uc, p["w_c_o"], "nn", BF16, tag("yc"), b_lead=l)
    cw = 512
    G = D_MODEL // cw
    mrows = [Row(ya, cw), Row(yb, cw), Row(yc, cw), Row(main, cw, lambda g: GA // cw + g),
             Row(main, cw, lambda g: GB // cw + g), Row(main, cw, lambda g: GC // cw + g)]
    mconsts = [Const(gate_b, (1, cw), lambda g, k=k: (0, k * G + g)) for k in range(3)]
    (mix,) = _rw_fwd(tag("merge_fwd"), _f_merge, mrows, mconsts, [(cw, BF16)], Lp, tm, G=G)
    h1 = _mm(mix, p["w_o"], "nn", F32, tag("h1"), add=h, b_lead=l)
    xn2, xn2_t = _rms_fwd(h1, g2, tag("rms2_fwd"), Lp)
    up = _mm(xn2, p["w_up"], "nn", BF16, tag("up"), b_lead=l)
    act, act_t = _mlp_act_fwd(up, p["mlp_conv_w"][l], Lp, tm)
    h2 = _mm(act, p["w_down"], "nn", F32, tag("h2"), add=h1, b_lead=l)
    res = dict(h=h, xn=xn, xn_t=xn_t, xn2_t=xn2_t, act_t=act_t, main=main, side=side, c_col=c_col, c_row=c_row, oa=oa, ox=ox, lse=lse, ya=ya, ub=ub, yb=yb,
               logg=logg, oc=oc, states=states, uc=uc, yc=yc, mix=mix, h1=h1, xn2=xn2, up=up, act=act,
               g1=g1, g2=g2, bf=bf, gate_b=gate_b, b_g=b_g, gnorm=gnorm)
    return h2, res


def _layer_bwd(dh2, p, r, l, Lp, tm, ta, big):
    tag = lambda s: f"{s}_l{l}"
    g = {}

    def wgrad(name, a, b, mode="tn"):
        big[name] = _mm(a, b, mode, F32, tag("d_" + name), slot=(big.get(name), l))

    wgrad("w_down", r["act_t"], dh2, "nn")
    dact = _mm(dh2, p["w_down"], "nt", BF16, tag("d_act"), b_lead=l)
    dgate, dval, dwg, dwu = _mlp_act_bwd(r["up"], p["mlp_conv_w"][l], dact, Lp, tm)
    g["mlp_conv_w"] = jnp.concatenate([dwg[:3], dwu[:3]], axis=1)
    dup = jnp.concatenate([dgate, dval], axis=1)
    wgrad("w_up", r["xn2_t"], dup, "nn")
    dxn2 = _mm(dup, p["w_up"], "nt", F32, tag("d_xn2"), b_lead=l)
    (dh1,), (dg2,) = _rw_bwd(tag("rms2_bwd"), _f_rms, [Row(r["h1"], D_MODEL)], [Const(r["g2"])],
                             [Row(dxn2, D_MODEL)], [F32], [dh2], Lp, BLOCK)
    g["norm2_g"] = dg2[0]
    wgrad("w_o", r["mix"], dh1)
    dmix = _mm(dh1, p["w_o"], "nt", BF16, tag("d_mix"), b_lead=l)
    cw = 512
    G = D_MODEL // cw
    main = r["main"]
    mrows = [Row(r["ya"], cw), Row(r["yb"], cw), Row(r["yc"], cw), Row(main, cw, lambda g_: GA // cw + g_),
             Row(main, cw, lambda g_: GB // cw + g_), Row(main, cw, lambda g_: GC // cw + g_)]
    mconsts = [Const(r["gate_b"], (1, cw), lambda g_, k=k: (0, k * G + g_)) for k in range(3)]
    (dya, dyb, dyc, dga, dgb, dgc), dbs = _rw_bwd(tag("merge_bwd"), _f_merge, mrows, mconsts, [Row(dmix, cw)],
                                                  [BF16] * 6, [None] * 6, Lp, tm, G=G)
    g["gate_b"] = jnp.concatenate([dbs[k][0, k * D_MODEL:(k + 1) * D_MODEL] for k in range(3)])
    wgrad("w_a_o", r["oa"], dya)
    doa = _mm(dya, p["w_a_o"], "nt", BF16, tag("d_oa"), b_lead=l)
    wgrad("w_b_o", r["ub"], dyb)
    dub = _mm(dyb, p["w_b_o"], "nt", BF16, tag("d_ub"), b_lead=l)
    wgrad("w_c_o", r["uc"], dyc)
    duc = _mm(dyc, p["w_c_o"], "nt", BF16, tag("d_uc"), b_lead=l)
    delta = _fox_delta(r["ox"], doa, Lp, ta)
    dq, dk, dv, dck = _fox_bwd(main, r["c_col"], r["c_row"], r["lse"], delta, doa, Lp, ta)
    dc = jnp.pad(dck[:, 0, :].T, ((0, 0), (0, LANES - FOX_HEADS)))
    dfa, dbf = _fox_gate_bwd(r["side"], r["bf"], dc, Lp)
    g["fox_b_f"] = dbf[0, :FOX_HEADS]
    dscb, dscc, dsch, dcw = _sconv_bwd(main, p["conv_w"][l], dub, Lp, tm)
    g["conv_w"] = dcw[:3]
    rc = Row(main, GLA_DV, lambda g_: RC // GLA_DV + g_)
    gn = Const(r["gnorm"], (1, GLA_DV), lambda g_: (0, g_))
    (doc, drc), (dgn,) = _rw_bwd(tag("gla_post_bwd"), _f_gla_post, [Row(r["oc"], GLA_DV), rc], [gn],
                                 [Row(duc, GLA_DV)], [F32, BF16], [None, None], Lp, tm, G=GLA_HEADS)
    g["gla_norm_g"] = dgn[0]
    dqc, dkc, dvc, dlogg = _gla_bwd(main, r["logg"], r["states"], doc, Lp)
    glr = Row(r["side"], LANES, lambda g_: 1)
    (dglr,), (dwg2, dbg) = _rw_bwd(tag("logg_bwd"), _f_logg, [glr], [Const(p["w_g2"][l]), Const(r["b_g"])],
                                   [Row(dlogg, 512)], [F32], [None], Lp, tm)
    g["gla_w_g2"] = dwg2[:GLA_RANK]
    g["gla_b_g"] = dbg[0]
    dmain = jnp.concatenate([dq, dk, dv, dscb, dscc, dsch, dqc, dkc, dvc, drc, dga, dgb, dgc], axis=1)
    dside = jnp.concatenate([dfa, dglr], axis=1)
    wgrad("main", r["xn_t"], dmain, "nn")
    wgrad("side", r["xn"], dside)
    dxn = _mm(dmain, p["main"][l], "nt", F32, tag("d_xn_main"))
    dxn = _mm(dside, p["side"][l], "nt", F32, tag("d_xn_side"), add=dxn)
    (dh,), (dg1,) = _rw_bwd(tag("rms1_bwd"), _f_rms, [Row(r["h"], D_MODEL)], [Const(r["g1"])], [Row(dxn, D_MODEL)],
                            [F32], [dh1], Lp, BLOCK)
    g["norm1_g"] = dg1[0]
    return dh, g


def _local_step(x, target, meta, p, rep):
    seq = x.shape[0]
    Lp = PAD + N_META + seq
    tm = _pick(Lp, (640, 384, 128))
    ta = tm
    h = jnp.concatenate([jnp.zeros((PAD, D_MODEL), F32), meta.astype(F32), x], axis=0)
    saved = []
    for l in range(DEPTH):
        h, res = _layer_fwd(h, p, rep, l, Lp, tm, ta)
        saved.append(res)
    loss, dh, dgf = _loss_head(h, _row2(rep["final_norm_g"]), target, Lp)
    big, small = {}, [None] * DEPTH
    for l in reversed(range(DEPTH)):
        dh, small[l] = _layer_bwd(dh, p, saved[l], l, Lp, tm, ta, big)
    return loss[0, 0], dh[BLOCK:], dh[PAD:BLOCK], big, small, dgf[0]


def kernel(x, meta_tokens, norm1_g, w_in, fox_b_f, gate_b, conv_w, gla_w_g2, gla_b_g, gla_norm_g, w_a_o, w_b_o, w_c_o, w_o, norm2_g, w_up, mlp_conv_w, w_down, final_norm_g, loss_target, m_meta_tokens, m_norm1_g, m_w_in, m_fox_b_f, m_gate_b, m_conv_w, m_gla_w_g2, m_gla_b_g, m_gla_norm_g, m_w_a_o, m_w_b_o, m_w_c_o, m_w_o, m_norm2_g, m_w_up, m_mlp_conv_w, m_w_down, m_final_norm_g, v_meta_tokens, v_norm1_g, v_w_in, v_fox_b_f, v_gate_b, v_conv_w, v_gla_w_g2, v_gla_b_g, v_gla_norm_g, v_w_a_o, v_w_b_o, v_w_c_o, v_w_o, v_norm2_g, v_w_up, v_mlp_conv_w, v_w_down, v_final_norm_g):
    given = dict(locals())
    weights = {n: given[n] for n in WEIGHT_ORDER}
    rep = {n: weights[n] for n, _ in REPLICATED}
    big_names = [n for n, _ in BIG]
    big_modes = [m for _, m in BIG] + ["stack"]
    small_shapes = [(s[0], s[1] // 4) for _, s in SMALL]
    exact = [k for k, (n, _) in enumerate(SMALL) if n in GATHER_F32]

    def small_wire(l):
        ws = [_small_piece(n, weights[n], l) for n, _ in SMALL]
        his = [w.astype(BF16) for w in ws]
        return his + [(ws[k] - his[k].astype(F32)).astype(BF16) for k in exact]

    shards = [weights[n].astype(BF16) for n in big_names] + [jnp.stack([_pack(small_wire(l), BF16) for l in range(DEPTH)])]
    gathered = _gather_weights(shards, big_modes)
    gw = dict(zip(big_names, gathered[:-1]))
    p = {n: gw[n] for n in big_names if n != "w_in"}
    p["main"], p["side"] = zip(*[_prep_w_in(gw["w_in"][l]) for l in range(DEPTH)])
    small_full = []
    for l in range(DEPTH):
        per_chip = [_unpack(gathered[-1][l, t], small_shapes + [small_shapes[k] for k in exact]) for t in range(4)]
        full = [jnp.concatenate([per_chip[t][k] for t in range(4)], axis=1).astype(F32) for k in range(len(per_chip[0]))]
        for e, k in enumerate(exact):
            full[k] = full[k] + full[len(SMALL) + e]
        small_full.append(dict(zip([n for n, _ in SMALL], full[:len(SMALL)])))
    p["conv_w"] = [_pad_rows(s["conv_w"], 8) for s in small_full]
    p["mlp_conv_w"] = [_pad_rows(s["mlp_conv_w"], 8) for s in small_full]
    p["w_g2"] = [_pad_rows(s["gla_w_g2"], LANES) for s in small_full]
    meta_full = jnp.concatenate([s["meta_tokens"] for s in small_full], axis=0)

    loss, grad_x, grad_meta, big, small, d_final = _local_step(x[0], loss_target[0], meta_full, p, rep)
    loss = lax.psum(loss, ("x", "y", "c"))

    d_w_in = jnp.stack([_w_in_cols(big["main"][0], big["side"][0], t * (N_IN // 4), (t + 1) * (N_IN // 4))
                        for t in range(4)], axis=1)
    big["w_in"] = (d_w_in, d_w_in)
    for l in range(DEPTH):
        small[l]["meta_tokens"] = grad_meta[l * (N_META // DEPTH):(l + 1) * (N_META // DEPTH)]
    shard_of = lambda a, t: lax.slice_in_dim(a, t * (a.shape[1] // 4), (t + 1) * (a.shape[1] // 4), axis=1)
    small_g = jnp.stack([jnp.stack([_pack([shard_of(small[l][n], t) for n, _ in SMALL], F32) for t in range(4)])
                         for l in range(DEPTH)])
    dims = [shards[k].shape[1:] for k in range(len(BIG))] + [small_g.shape[2:]]
    summed = _reduce_scatter([big[n][0] for n in big_names] + [small_g], [big[n][1] for n in big_names] + [small_g],
                             big_modes, dims, [BF16] * len(BIG) + [F32])
    gout = dict(zip(big_names, summed[:-1]))
    pieces = [_unpack(summed[-1][l], small_shapes) for l in range(DEPTH)]
    for k, (n, _) in enumerate(SMALL):
        per_layer = [pieces[l][k] for l in range(DEPTH)]
        gout[n] = jnp.concatenate(per_layer, axis=0) if n == "meta_tokens" else jnp.stack(per_layer)

    rep_g = {n: (d_final if n == "final_norm_g" else jnp.stack([small[l][n] for l in range(DEPTH)])) for n, _ in REPLICATED}
    flat = jnp.concatenate([rep_g[n].astype(F32).reshape(-1) for n, _ in REPLICATED])
    rrows = -(-flat.shape[0] // (PACK_COLS * 8)) * 8
    summed_small = _allreduce_small(jnp.pad(flat, (0, rrows * PACK_COLS - flat.shape[0])).reshape(rrows, PACK_COLS))
    pos = 0
    for n, shape in REPLICATED:
        gout[n] = summed_small.reshape(-1)[pos:pos + _size(shape)].reshape(shape)
        pos += _size(shape)

    deltas, new_m, new_v = {}, {}, {}
    for n in WEIGHT_ORDER:
        gout[n], deltas[n], new_m[n], new_v[n] = _adamw(weights[n], gout[n], given["m_" + n], given["v_" + n],
                                                        "adamw_" + n)
    return (loss, grad_x[None], *[gout[n] for n in WEIGHT_ORDER], *[deltas[n] for n in WEIGHT_ORDER],
            *[new_m[n] for n in WEIGHT_ORDER], *[new_v[n] for n in WEIGHT_ORDER])
```
